```python
import math
import jax, jax.numpy as jnp
from jax import lax
import numpy as np

D_MODEL = 2048
BATCH = 2
SEQ = 8192
DEPTH = 4

HEAD_DIM = 64
Q_CHUNK = 128
A_CONFIGS = ((128, 1), (512, 4), (2048, 16))
N_A_GROUPS = len(A_CONFIGS)
A_HEADS_PER_GROUP = 4
A_HEADS = N_A_GROUPS * A_HEADS_PER_GROUP
A_BLOCK = 128
B_HEADS = 8
MOBA_BLOCK = 256
MOBA_TOPK = 3
C_KV_HEADS = 3
C_GROUP = 4
C_HEADS = C_KV_HEADS * C_GROUP
CMP_STRIDE = 16
CMP_LEN = 2 * CMP_STRIDE
CMP_HIDDEN = 128
SLC_BLOCK = 64
SLC_TOPN = 16
WIN = 512
FORCE_SCORE = 1e9
N_HEADS = A_HEADS + B_HEADS + C_HEADS
MIX_WIDTH = N_HEADS * HEAD_DIM
A_WIDTH = 3 * A_HEADS * HEAD_DIM
B_WIDTH = 3 * B_HEADS * HEAD_DIM
C_Q_WIDTH = C_HEADS * HEAD_DIM
C_KV_WIDTH = 6 * C_KV_HEADS * HEAD_DIM
C_GATE_WIDTH = 3 * C_HEADS
IN_WIDTH = A_WIDTH + B_WIDTH + C_Q_WIDTH + C_KV_WIDTH + C_GATE_WIDTH
D_FF = 5632
CONV_WIDTH = 3
N_BUCKETS = 32
T5_MAX_DIST = 2048
EPS = 1e-6
NEG = -1e30
Q_SCALE = HEAD_DIM ** -0.5

kernel_name = "hybrid_dilated_moba_nsa_convffn"


def rmsnorm(x, g):
    x32 = x.astype(jnp.float32)
    y = x32 * lax.rsqrt(jnp.mean(x32 * x32, axis=-1, keepdims=True) + EPS)
    return (y * g.astype(jnp.float32)).astype(x.dtype)


def t5_bucket(dist):
    n = jnp.maximum(dist, 0)
    max_exact = N_BUCKETS // 2
    nf = jnp.maximum(n, 1).astype(jnp.float32)
    large = max_exact + (jnp.log(nf / max_exact) / math.log(T5_MAX_DIST / max_exact)
                         * (N_BUCKETS - max_exact)).astype(jnp.int32)
    large = jnp.minimum(large, N_BUCKETS - 1)
    return jnp.where(n < max_exact, n, large)


def masked_softmax(s, mask):
    s32 = jnp.where(mask, s.astype(jnp.float32), NEG)
    m = jnp.max(s32, axis=-1, keepdims=True)
    e = jnp.where(mask, jnp.exp(s32 - m), 0.0)
    den = jnp.sum(e, axis=-1, keepdims=True)
    p = e / jnp.maximum(den, 1e-30)
    lse = m[..., 0] + jnp.log(jnp.maximum(den[..., 0], 1e-30))
    return p, lse


def dilated_group(q, k, v, window, dilation, tbl):
    B, S, H, dh = q.shape
    L = S // dilation
    nq = -(-L // A_BLOCK)
    Lp = nq * A_BLOCK
    nback = window // dilation

    def to_class(t):
        t = t.reshape(B, L, dilation, H, dh).transpose(0, 2, 3, 1, 4)
        return jnp.pad(t, ((0, 0), (0, 0), (0, 0), (0, Lp - L), (0, 0)))

    def band(t):
        tp = jnp.pad(t, ((0, 0), (0, 0), (0, 0), (A_BLOCK, 0), (0, 0)))
        prev = tp[:, :, :, :Lp].reshape(B, dilation, H, nq, A_BLOCK, dh)
        cur = t.reshape(B, dilation, H, nq, A_BLOCK, dh)
        return jnp.concatenate([prev, cur], axis=4)

    qb = to_class(q).reshape(B, dilation, H, nq, A_BLOCK, dh)
    kb = band(to_class(k))
    vb = band(to_class(v))
    a = jnp.arange(A_BLOCK)[:, None]
    bk = jnp.arange(2 * A_BLOCK)[None, :]
    rel = a + A_BLOCK - bk
    key_idx = jnp.arange(nq)[:, None, None] * A_BLOCK - A_BLOCK + bk[None]
    mask = (rel >= 0)[None] & (rel <= nback)[None] & (key_idx >= 0)
    bias = tbl[:, t5_bucket(rel * dilation)]
    s = jnp.einsum('bdhnqe,bdhnke->bdhnqk', qb, kb) + bias[:, None]
    p, lse = masked_softmax(s, mask)
    o = jnp.einsum('bdhnqk,bdhnke->bdhnqe', p.astype(v.dtype), vb)
    o = o.reshape(B, dilation, H, Lp, dh)[:, :, :, :L].transpose(0, 3, 1, 2, 4).reshape(B, S, H, dh)
    lse = lse.reshape(B, dilation, H, Lp)[:, :, :, :L].transpose(0, 3, 1, 2).reshape(B, S, H)
    return o, lse


def dilated_mixer(pA, tbl_a):
    B, S = pA.shape[:2]
    outs, lses = [], []
    for g, (w, d) in enumerate(A_CONFIGS):
        o, lse = dilated_group(pA[:, :, g, 0] * Q_SCALE, pA[:, :, g, 1], pA[:, :, g, 2], w, d,
                               tbl_a[g * A_HEADS_PER_GROUP:(g + 1) * A_HEADS_PER_GROUP])
        outs.append(o)
        lses.append(lse)
    alpha = jax.nn.softmax(jnp.stack(lses, axis=2), axis=2)
    o = jnp.stack(outs, axis=2) * alpha[..., None].astype(outs[0].dtype)
    return o.reshape(B, S, A_HEADS * HEAD_DIM)


def moba_mixer(q, k, v, tbl):
    B, S, H, dh = q.shape
    q, k, v = (t.transpose(0, 2, 1, 3) for t in (q, k, v))
    nb = -(-S // MOBA_BLOCK)
    Sp = nb * MOBA_BLOCK
    kp = jnp.pad(k, ((0, 0), (0, 0), (0, Sp - S), (0, 0)))
    vp = jnp.pad(v, ((0, 0), (0, 0), (0, Sp - S), (0, 0)))
    kb = kp.reshape(B, H, nb, MOBA_BLOCK, dh)
    vb = vp.reshape(B, H, nb, MOBA_BLOCK, dh)
    kmean = jnp.mean(kb.astype(jnp.float32), axis=3).astype(k.dtype)
    topk = min(MOBA_TOPK, nb)
    ar_b = jnp.arange(B)[:, None, None, None]
    ar_h = jnp.arange(H)[None, :, None, None]
    blk_off = jnp.arange(MOBA_BLOCK)

    def chunk(c):
        start = c * Q_CHUNK
        t = start + jnp.arange(Q_CHUNK)
        ob = start // MOBA_BLOCK
        qc = lax.dynamic_slice_in_dim(q, start, Q_CHUNK, axis=2)
        gate = jnp.einsum('bhqe,bhne->bhqn', qc, kmean).astype(jnp.float32)
        past = jnp.arange(nb) < ob
        gate = jnp.where(past, gate, NEG)
        _, idx = lax.top_k(gate, topk)
        sel_ok = idx < ob
        ks = kb[ar_b, ar_h, idx].reshape(B, H, Q_CHUNK, topk * MOBA_BLOCK, dh)
        vs = vb[ar_b, ar_h, idx].reshape(B, H, Q_CHUNK, topk * MOBA_BLOCK, dh)
        pos_s = (idx[..., None] * MOBA_BLOCK + blk_off).reshape(B, H, Q_CHUNK, topk * MOBA_BLOCK)
        mask_s = jnp.repeat(sel_ok, MOBA_BLOCK, axis=-1)
        ko = lax.dynamic_slice_in_dim(kp, ob * MOBA_BLOCK, MOBA_BLOCK, axis=2)
        vo = lax.dynamic_slice_in_dim(vp, ob * MOBA_BLOCK, MOBA_BLOCK, axis=2)
        pos_o = ob * MOBA_BLOCK + blk_off
        mask_o = jnp.broadcast_to(pos_o[None, :] <= t[:, None], (B, H, Q_CHUNK, MOBA_BLOCK))
        s_s = (jnp.einsum('bhqe,bhqke->bhqk', qc, ks)
               + tbl[ar_h, t5_bucket(t[None, None, :, None] - pos_s)])
        s_o = (jnp.einsum('bhqe,bhke->bhqk', qc, ko)
               + tbl[:, t5_bucket(t[:, None] - pos_o[None, :])][None])
        p, _ = masked_softmax(jnp.concatenate([s_s, s_o], -1), jnp.concatenate([mask_s, mask_o], -1))
        p = p.astype(v.dtype)
        n_sel = topk * MOBA_BLOCK
        return (jnp.einsum('bhqk,bhqke->bhqe', p[..., :n_sel], vs)
                + jnp.einsum('bhqk,bhke->bhqe', p[..., n_sel:], vo))

    outs = lax.map(chunk, jnp.arange(S // Q_CHUNK))
    return outs.transpose(1, 0, 3, 2, 4).reshape(B, S, H * dh)


def nsa_mixer(q, kvC, gates, cmp_w1, cmp_w2, cmp_pe, tbl):
    B, S, _, dh = q.shape
    KV, G = C_KV_HEADS, C_GROUP
    q = q.reshape(B, S, KV, G, dh).transpose(0, 2, 3, 1, 4)
    k_cmp, v_cmp, k_slc, v_slc, k_win, v_win = (kvC[:, :, i].transpose(0, 2, 1, 3) for i in range(6))
    gates = gates.reshape(B, S, KV, G, 3).transpose(0, 2, 3, 1, 4)
    n_cmp = S // CMP_STRIDE - 1
    n_slc = S // SLC_BLOCK
    n_sel = min(SLC_TOPN, n_slc)

    def compress(t, w1, w2, pe):
        sub = t.reshape(B, KV, S // CMP_STRIDE, CMP_STRIDE, dh)
        blocks = jnp.concatenate([sub[:, :, :-1], sub[:, :, 1:]], axis=3) + pe
        return jax.nn.gelu(blocks.reshape(B, KV, n_cmp, CMP_LEN * dh) @ w1) @ w2

    kc = compress(k_cmp, cmp_w1[0], cmp_w2[0], cmp_pe[0])
    vc = compress(v_cmp, cmp_w1[1], cmp_w2[1], cmp_pe[1])
    cmp_end = jnp.arange(n_cmp) * CMP_STRIDE + CMP_LEN - 1
    ci = jnp.arange(n_cmp)[:, None] * CMP_STRIDE
    sj = jnp.arange(n_slc)[None, :] * SLC_BLOCK
    overlap = ((ci < sj + SLC_BLOCK) & (ci + CMP_LEN > sj)).astype(jnp.float32)
    ksb = k_slc.reshape(B, KV, n_slc, SLC_BLOCK, dh)
    vsb = v_slc.reshape(B, KV, n_slc, SLC_BLOCK, dh)
    kwp = jnp.pad(k_win, ((0, 0), (0, 0), (WIN, 0), (0, 0)))
    vwp = jnp.pad(v_win, ((0, 0), (0, 0), (WIN, 0), (0, 0)))
    tbl_kgn = tbl.reshape(KV, G, N_BUCKETS)
    tbl_kng = tbl_kgn.transpose(0, 2, 1)
    ar_b = jnp.arange(B)[:, None, None, None]
    ar_kv = jnp.arange(KV)[None, :, None, None]
    slc_off = jnp.arange(SLC_BLOCK)
    jj = jnp.arange(n_slc)[None, :]

    def chunk(c):
        start = c * Q_CHUNK
        t = start + jnp.arange(Q_CHUNK)
        qc = lax.dynamic_slice_in_dim(q, start, Q_CHUNK, axis=3)
        s_c = jnp.einsum('bkgqe,bkne->bkgqn', qc, kc)
        p_c, _ = masked_softmax(s_c, cmp_end[None, :] <= t[:, None])
        o_c = jnp.einsum('bkgqn,bkne->bkgqe', p_c.astype(v_cmp.dtype), vc)
        imp = jnp.einsum('bkgqn,nj->bkqj', p_c, overlap)
        own = (t // SLC_BLOCK)[:, None]
        valid = jj <= own
        forced = (jj == 0) | (jj == own) | (jj == own - 1)
        score = jnp.where(valid, jnp.where(forced, FORCE_SCORE, imp), -1.0)
        _, idx = lax.top_k(score, n_sel)
        ks = ksb[ar_b, ar_kv, idx].reshape(B, KV, Q_CHUNK, n_sel * SLC_BLOCK, dh)
        vs = vsb[ar_b, ar_kv, idx].reshape(B, KV, Q_CHUNK, n_sel * SLC_BLOCK, dh)
        pos = (idx[..., None] * SLC_BLOCK + slc_off).reshape(B, KV, Q_CHUNK, n_sel * SLC_BLOCK)
        dist = t[None, None, :, None] - pos
        bias_s = tbl_kng[ar_kv, t5_bucket(dist)].transpose(0, 1, 4, 2, 3)
        s_s = jnp.einsum('bkgqe,bkqse->bkgqs', qc, ks) + bias_s
        p_s, _ = masked_softmax(s_s, (dist >= 0)[:, :, None])
        o_s = jnp.einsum('bkgqs,bkqse->bkgqe', p_s.astype(v_slc.dtype), vs)
        kw = lax.dynamic_slice_in_dim(kwp, start, Q_CHUNK + WIN, axis=2)
        vw = lax.dynamic_slice_in_dim(vwp, start, Q_CHUNK + WIN, axis=2)
        pos_w = start - WIN + jnp.arange(Q_CHUNK + WIN)
        dist_w = t[:, None] - pos_w[None, :]
        mask_w = (dist_w >= 0) & (dist_w < WIN) & (pos_w[None, :] >= 0)
        s_w = jnp.einsum('bkgqe,bkse->bkgqs', qc, kw) + tbl_kgn[:, :, t5_bucket(dist_w)]
        p_w, _ = masked_softmax(s_w, mask_w)
        o_w = jnp.einsum('bkgqs,bkse->bkgqe', p_w.astype(v_win.dtype), vw)
        g = lax.dynamic_slice_in_dim(gates, start, Q_CHUNK, axis=3).astype(o_c.dtype)
        return g[..., 0:1] * o_c + g[..., 1:2] * o_s + g[..., 2:3] * o_w

    outs = lax.map(chunk, jnp.arange(S // Q_CHUNK))
    return outs.transpose(1, 0, 4, 2, 3, 5).reshape(B, S, C_HEADS * dh)


def conv_ffn(h, w_up, conv_w, conv_b, w_down):
    S = h.shape[1]
    u = h @ w_up
    up = jnp.pad(u, ((0, 0), (CONV_WIDTH - 1, 0), (0, 0)))
    acc = conv_b
    for j in range(CONV_WIDTH):
        acc = acc + conv_w[j] * up[:, CONV_WIDTH - 1 - j:CONV_WIDTH - 1 - j + S]
    a, g = jnp.split(acc, 2, axis=-1)
    return (jax.nn.silu(g) * a) @ w_down


def setup_inputs(seed: int = 0) -> dict:
    key = jax.random.key(seed)
    ks = jax.random.split(key, 14)
    nrm = jax.random.normal
    f32 = jnp.float32
    return {
        "x": nrm(ks[0], (BATCH, SEQ, D_MODEL), f32),
        "rel_table": 0.5 * nrm(ks[1], (N_HEADS, N_BUCKETS), f32),
        "w_in": nrm(ks[2], (DEPTH, D_MODEL, IN_WIDTH), f32) * D_MODEL ** -0.5,
        "w_out": nrm(ks[3], (DEPTH, MIX_WIDTH, D_MODEL), f32) * MIX_WIDTH ** -0.5,
        "cmp_w1": nrm(ks[4], (DEPTH, 2, CMP_LEN * HEAD_DIM, CMP_HIDDEN), f32) * (CMP_LEN * HEAD_DIM) ** -0.5,
        "cmp_w2": nrm(ks[5], (DEPTH, 2, CMP_HIDDEN, HEAD_DIM), f32) * CMP_HIDDEN ** -0.5,
        "cmp_pe": 0.5 * nrm(ks[6], (DEPTH, 2, CMP_LEN, HEAD_DIM), f32),
        "norm_attn": 1.0 + 0.05 * nrm(ks[7], (DEPTH, D_MODEL), f32),
        "norm_mlp": 1.0 + 0.05 * nrm(ks[8], (DEPTH, D_MODEL), f32),
        "w_up": nrm(ks[9], (DEPTH, D_MODEL, 2 * D_FF), f32) * D_MODEL ** -0.5,
        "conv_w": nrm(ks[10], (DEPTH, CONV_WIDTH, 2 * D_FF), f32) * CONV_WIDTH ** -0.5,
        "conv_b": 0.02 * nrm(ks[11], (DEPTH, 2 * D_FF), f32),
        "w_down": nrm(ks[12], (DEPTH, D_FF, D_MODEL), f32) * D_FF ** -0.5,
        "norm_final": 1.0 + 0.05 * nrm(ks[13], (D_MODEL,), f32),
    }


def reference(x, rel_table, w_in, w_out, cmp_w1, cmp_w2, cmp_pe, norm_attn, norm_mlp,
              w_up, conv_w, conv_b, w_down, norm_final):
    B, S, _ = x.shape
    tbl_a = rel_table[:A_HEADS]
    tbl_b = rel_table[A_HEADS:A_HEADS + B_HEADS]
    tbl_c = rel_table[A_HEADS + B_HEADS:]
    for l in range(DEPTH):
        h = rmsnorm(x, norm_attn[l])
        proj = h @ w_in[l]
        pA = proj[..., :A_WIDTH].reshape(B, S, N_A_GROUPS, 3, A_HEADS_PER_GROUP, HEAD_DIM)
        pB = proj[..., A_WIDTH:A_WIDTH + B_WIDTH].reshape(B, S, 3, B_HEADS, HEAD_DIM)
        pC = proj[..., A_WIDTH + B_WIDTH:]
        o_a = dilated_mixer(pA, tbl_a)
        o_b = moba_mixer(pB[:, :, 0] * Q_SCALE, pB[:, :, 1], pB[:, :, 2], tbl_b)
        q_c = pC[..., :C_Q_WIDTH].reshape(B, S, C_HEADS, HEAD_DIM) * Q_SCALE
        kv_c = pC[..., C_Q_WIDTH:C_Q_WIDTH + C_KV_WIDTH].reshape(B, S, 6, C_KV_HEADS, HEAD_DIM)
        g_c = jax.nn.sigmoid(pC[..., C_Q_WIDTH + C_KV_WIDTH:].reshape(B, S, C_HEADS, 3))
        o_c = nsa_mixer(q_c, kv_c, g_c, cmp_w1[l], cmp_w2[l], cmp_pe[l], tbl_c)
        x = x + jnp.concatenate([o_a, o_b, o_c], axis=-1) @ w_out[l]
        h = rmsnorm(x, norm_mlp[l])
        x = x + conv_ffn(h, w_up[l], conv_w[l], conv_b[l], w_down[l])
    return rmsnorm(x, norm_final)
```

```python
import functools
import math

import numpy as np
import jax
import jax.numpy as jnp
from jax import lax
from jax.experimental import pallas as pl
from jax.experimental.pallas import tpu as pltpu

F32 = jnp.float32
BF16 = jnp.bfloat16

HEAD_DIM = 64
LANES = 128
A_CONFIGS = ((128, 1), (512, 4), (2048, 16))
A_HEADS_PER_GROUP = 4
A_HEADS = 12
A_BLOCK = 128
B_HEADS = 8
MOBA_BLOCK = 256
MOBA_TOPK = 3
C_KV_HEADS = 3
C_GROUP = 4
C_HEADS = 12
CMP_STRIDE = 16
CMP_LEN = 32
CMP_HIDDEN = 128
SLC_BLOCK = 64
SLC_TOPN = 16
WIN = 512
FORCE_SCORE = 1e9
N_BUCKETS = 32
T5_MAX_DIST = 2048
EPS = 1e-6
NEG = -1e30
Q_SCALE = HEAD_DIM ** -0.5
C_CHUNK = 128

A_WIDTH = 3 * A_HEADS * HEAD_DIM
B_WIDTH = 3 * B_HEADS * HEAD_DIM
C_Q_WIDTH = C_HEADS * HEAD_DIM
C_KV_WIDTH = 6 * C_KV_HEADS * HEAD_DIM
C_GATE_WIDTH = 3 * C_HEADS
IN_WIDTH = A_WIDTH + B_WIDTH + C_Q_WIDTH + C_KV_WIDTH + C_GATE_WIDTH
BLK_A = 0
BLK_B = A_WIDTH // LANES
BLK_CQ = BLK_B + B_WIDTH // LANES
BLK_CKV = BLK_CQ + C_Q_WIDTH // LANES
BLK_GATE = BLK_CKV + 3 * C_KV_HEADS
N_BLK = BLK_GATE + 1
P_WIDTH = N_BLK * LANES

VMEM_LIMIT = 56 * 1024 * 1024


def _cparams(*sem):
    return pltpu.CompilerParams(dimension_semantics=sem, vmem_limit_bytes=VMEM_LIMIT)


def _dot_nt(a, b):
    return lax.dot_general(a, b, (((1,), (1,)), ((), ())), preferred_element_type=F32)


def _dot(a, b, **kw):
    return jnp.dot(a, b, preferred_element_type=F32, **kw)


def _iota(shape, dim):
    return lax.broadcasted_iota(jnp.int32, shape, dim)


def _t5_bucket(dist):
    n = jnp.maximum(dist, 0)
    max_exact = N_BUCKETS // 2
    nf = jnp.maximum(n, 1).astype(F32)
    large = max_exact + (jnp.log(nf / max_exact) / math.log(T5_MAX_DIST / max_exact)
                         * (N_BUCKETS - max_exact)).astype(jnp.int32)
    large = jnp.minimum(large, N_BUCKETS - 1)
    return jnp.where(n < max_exact, n, large)


def _bias_lookup_kernel(tbl_ref, idx_ref, o_ref):
    h = pl.program_id(1)
    idx = idx_ref[0]
    acc = jnp.zeros(idx.shape, F32)
    for b in range(N_BUCKETS):
        acc = jnp.where(idx == b, tbl_ref[h, b], acc)
    o_ref[0, 0] = acc


def _bias_tiles(tbl, idx):
    n, R, C = idx.shape
    H = tbl.shape[0]
    return pl.pallas_call(
        _bias_lookup_kernel,
        grid=(n, H),
        in_specs=[pl.BlockSpec(memory_space=pltpu.SMEM),
                  pl.BlockSpec((1, R, C), lambda t, h: (t, 0, 0))],
        out_specs=pl.BlockSpec((1, 1, R, C), lambda t, h: (t, h, 0, 0)),
        out_shape=jax.ShapeDtypeStruct((n, H, R, C), F32),
        compiler_params=_cparams("arbitrary", "arbitrary"),
        name="bias_tiles",
    )(tbl, idx)


def _toeplitz_bucket_tiles(n_tiles, size):
    d = jnp.arange(n_tiles)[:, None, None] * size
    a = jnp.arange(size)[None, :, None]
    b = jnp.arange(size)[None, None, :]
    return _t5_bucket(d + a - b).astype(jnp.int32)


def _n_far_tiles(n_chunks, size):
    dmax = -(-(T5_MAX_DIST + size - 1) // size)
    dmax = min(dmax, n_chunks - 1)
    return dmax


def _norm_rows(x, g):
    ms = jnp.mean(x * x, axis=-1, keepdims=True)
    return x * lax.rsqrt(ms + EPS) * g


def _inproj_kernel(x_ref, g_ref, w_ref, o_ref, h_scr):
    @pl.when(pl.program_id(1) == 0)
    def _():
        h_scr[...] = _norm_rows(x_ref[...], g_ref[...]).astype(BF16)

    o_ref[...] = _dot(h_scr[...], w_ref[...]).astype(o_ref.dtype)


def _inproj(x2, gain, w, tm=1024, tn=256):
    T, D = x2.shape
    N = w.shape[1]
    return pl.pallas_call(
        _inproj_kernel,
        grid=(T // tm, N // tn),
        in_specs=[pl.BlockSpec((tm, D), lambda i, j: (i, 0)),
                  pl.BlockSpec((1, D), lambda i, j: (0, 0)),
                  pl.BlockSpec((D, tn), lambda i, j: (0, j))],
        out_specs=pl.BlockSpec((tm, tn), lambda i, j: (i, j)),
        out_shape=jax.ShapeDtypeStruct((T, N), BF16),
        scratch_shapes=[pltpu.VMEM((tm, D), BF16)],
        compiler_params=_cparams("arbitrary", "arbitrary"),
        name="inproj",
    )(x2, gain.reshape(1, D), w)


def _outproj_kernel(x_ref, a_ref, b_ref, c_ref, wa_ref, wb_ref, wc_ref, o_ref):
    acc = _dot(a_ref[...], wa_ref[...])
    acc += _dot(b_ref[...], wb_ref[...])
    acc += _dot(c_ref[...], wc_ref[...])
    o_ref[...] = x_ref[...] + acc


def _outproj(x2, oa, ob, oc, wa, wb, wc, tm=1024, tn=512):
    T, D = x2.shape
    return pl.pallas_call(
        _outproj_kernel,
        grid=(T // tm, D // tn),
        in_specs=[pl.BlockSpec((tm, tn), lambda i, j: (i, j)),
                  pl.BlockSpec((tm, oa.shape[1]), lambda i, j: (i, 0)),
                  pl.BlockSpec((tm, ob.shape[1]), lambda i, j: (i, 0)),
                  pl.BlockSpec((tm, oc.shape[1]), lambda i, j: (i, 0)),
                  pl.BlockSpec((wa.shape[0], tn), lambda i, j: (0, j)),
                  pl.BlockSpec((wb.shape[0], tn), lambda i, j: (0, j)),
                  pl.BlockSpec((wc.shape[0], tn), lambda i, j: (0, j))],
        out_specs=pl.BlockSpec((tm, tn), lambda i, j: (i, j)),
        out_shape=jax.ShapeDtypeStruct((T, D), F32),
        compiler_params=_cparams("arbitrary", "arbitrary"),
        name="outproj",
    )(x2, oa, ob, oc, wa, wb, wc)


FFN_HALO = 8


def _ffn_kernel(x_ref, halo_ref, g_ref, wa_ref, wg_ref, cwa_ref, cwg_ref, cba_ref, cbg_ref, wd_ref,
                o_ref, h_scr, ua_scr, ug_scr, *, tm, blocks_per_seq):
    i = pl.program_id(0)
    f = pl.program_id(1)

    @pl.when(f == 0)
    def _():
        x = x_ref[...]
        h_scr[pl.ds(FFN_HALO, tm), :] = _norm_rows(x, g_ref[...]).astype(BF16)
        keep = jnp.where(i % blocks_per_seq == 0, 0.0, 1.0)
        h_scr[pl.ds(0, FFN_HALO), :] = (_norm_rows(halo_ref[0], g_ref[...]) * keep).astype(BF16)
        o_ref[...] = x

    h = h_scr[...]
    ua_scr[...] = _dot(h, wa_ref[...])
    ug_scr[...] = _dot(h, wg_ref[...])

    def conv(u_scr, cw_ref, cb_ref):
        acc = cb_ref[...]
        for j in range(3):
            acc = acc + cw_ref[pl.ds(j, 1), :] * u_scr[pl.ds(FFN_HALO - j, tm), :]
        return acc

    ya = conv(ua_scr, cwa_ref, cba_ref)
    yg = conv(ug_scr, cwg_ref, cbg_ref)
    act = (yg * jax.nn.sigmoid(yg) * ya).astype(BF16)
    o_ref[...] += _dot(act, wd_ref[...])


def _ffn(x2, gain, w_up, conv_w, conv_b, w_down, seq, tm=512, tf=512):
    T, D = x2.shape
    Fh = w_down.shape[0]
    nf = Fh // tf
    halo_view = x2.reshape(T // FFN_HALO, FFN_HALO, D)
    rb = tm // FFN_HALO
    kern = functools.partial(_ffn_kernel, tm=tm, blocks_per_seq=seq // tm)
    return pl.pallas_call(
        kern,
        grid=(T // tm, nf),
        in_specs=[pl.BlockSpec((tm, D), lambda i, f: (i, 0)),
                  pl.BlockSpec((1, FFN_HALO, D), lambda i, f: (jnp.maximum(i * rb - 1, 0), 0, 0)),
                  pl.BlockSpec((1, D), lambda i, f: (0, 0)),
                  pl.BlockSpec((D, tf), lambda i, f: (0, f)),
                  pl.BlockSpec((D, tf), lambda i, f: (0, f + nf)),
                  pl.BlockSpec((3, tf), lambda i, f: (0, f)),
                  pl.BlockSpec((3, tf), lambda i, f: (0, f + nf)),
                  pl.BlockSpec((1, tf), lambda i, f: (0, f)),
                  pl.BlockSpec((1, tf), lambda i, f: (0, f + nf)),
                  pl.BlockSpec((tf, D), lambda i, f: (f, 0))],
        out_specs=pl.BlockSpec((tm, D), lambda i, f: (i, 0)),
        out_shape=jax.ShapeDtypeStruct((T, D), F32),
        scratch_shapes=[pltpu.VMEM((tm + FFN_HALO, D), BF16),
                        pltpu.VMEM((tm + FFN_HALO, tf), F32),
                        pltpu.VMEM((tm + FFN_HALO, tf), F32)],
        compiler_params=_cparams("arbitrary", "arbitrary"),
        name="convffn",
    )(x2, halo_view, gain.reshape(1, D), w_up, w_up, conv_w, conv_w,
      conv_b.reshape(1, -1), conv_b.reshape(1, -1), w_down)


def _final_norm_kernel(x_ref, g_ref, o_ref):
    o_ref[...] = _norm_rows(x_ref[...], g_ref[...])


def _final_norm(x2, gain, tm=1024):
    T, D = x2.shape
    return pl.pallas_call(
        _final_norm_kernel,
        grid=(T // tm,),
        in_specs=[pl.BlockSpec((tm, D), lambda i: (i, 0)),
                  pl.BlockSpec((1, D), lambda i: (0, 0))],
        out_specs=pl.BlockSpec((tm, D), lambda i: (i, 0)),
        out_shape=jax.ShapeDtypeStruct((T, D), F32),
        compiler_params=_cparams("arbitrary"),
        name="final_norm",
    )(x2, gain.reshape(1, D))


def _stack_pair(qp):
    lane = _iota(qp.shape, 1)
    zero = jnp.zeros_like(qp)
    return jnp.concatenate([jnp.where(lane < HEAD_DIM, qp, zero),
                            jnp.where(lane >= HEAD_DIM, qp, zero)], axis=0)


def _unstack_pair(o):
    R = o.shape[0] // 2
    lane = _iota((R, LANES), 1)
    return jnp.where(lane < HEAD_DIM, o[:R], o[R:])


def _dilated_kernel(q_ref, kc_ref, kp_ref, vc_ref, vp_ref, bias_ref, o_ref, lse_ref):
    i = pl.program_id(3)
    qs = _stack_pair(q_ref[0])
    k = jnp.concatenate([kp_ref[0], kc_ref[0]], axis=0)
    v = jnp.concatenate([vp_ref[0], vc_ref[0]], axis=0)
    R = 2 * A_BLOCK
    s = _dot_nt(qs, k) + bias_ref[0].reshape(R, 2 * A_BLOCK)
    a = _iota((R, 2 * A_BLOCK), 0) % A_BLOCK
    bk = _iota((R, 2 * A_BLOCK), 1)
    rel = a + A_BLOCK - bk
    mask = (rel >= 0) & (rel <= A_BLOCK) & ((bk >= A_BLOCK) | (i > 0))
    s = jnp.where(mask, s, NEG)
    m = jnp.max(s, axis=1, keepdims=True)
    e = jnp.where(mask, jnp.exp(s - m), 0.0)
    den = jnp.sum(e, axis=1, keepdims=True)
    o = _dot(e.astype(BF16), v) / den
    lse = m + jnp.log(den)
    o_ref[0] = _unstack_pair(o)
    lse_ref[0] = _unstack_pair(jnp.broadcast_to(lse, (R, LANES)))


def _dilated_group(proj, bias_a, g, dilation):
    B, S, _ = proj.shape
    L = S // dilation
    nq = L // A_BLOCK
    view = proj.reshape(B, L, dilation * P_WIDTH)
    qb, kb, vb = BLK_A + 6 * g, BLK_A + 6 * g + 2, BLK_A + 6 * g + 4

    def cur(base):
        return pl.BlockSpec((1, A_BLOCK, LANES), lambda b, r, p, i: (b, i, r * N_BLK + base + p))

    def prev(base):
        return pl.BlockSpec((1, A_BLOCK, LANES),
                            lambda b, r, p, i: (b, jnp.maximum(i - 1, 0), r * N_BLK + base + p))

    out_spec = pl.BlockSpec((1, A_BLOCK, LANES), lambda b, r, p, i: (b, i, r * 2 + p))
    o, lse = pl.pallas_call(
        _dilated_kernel,
        grid=(B, dilation, 2, nq),
        in_specs=[cur(qb), cur(kb), prev(kb), cur(vb), prev(vb),
                  pl.BlockSpec((1, 2, A_BLOCK, 2 * A_BLOCK), lambda b, r, p, i: (g, 2 * g + p, 0, 0))],
        out_specs=[out_spec, out_spec],
        out_shape=[jax.ShapeDtypeStruct((B, L, dilation * 2 * LANES), F32)] * 2,
        compiler_params=_cparams("arbitrary", "arbitrary", "arbitrary", "arbitrary"),
        name=f"dilated_g{g}",
    )(view, view, view, view, view, bias_a)
    return o.reshape(B * S, 2 * LANES), lse.reshape(B * S, 2 * LANES)


def _combine_kernel(o0, o1, o2, l0, l1, l2, out_ref):
    ls = [l0[...], l1[...], l2[...]]
    m = jnp.maximum(jnp.maximum(ls[0], ls[1]), ls[2])
    es = [jnp.exp(l - m) for l in ls]
    den = es[0] + es[1] + es[2]
    W = o0.shape[1]
    for g, o in enumerate((o0, o1, o2)):
        out_ref[:, g * W:(g + 1) * W] = (o[...] * (es[g] / den)).astype(out_ref.dtype)


def _combine_groups(os_, lses, tm=1024):
    T, W = os_[0].shape
    spec = pl.BlockSpec((tm, W), lambda i: (i, 0))
    return pl.pallas_call(
        _combine_kernel,
        grid=(T // tm,),
        in_specs=[spec] * 6,
        out_specs=pl.BlockSpec((tm, 3 * W), lambda i: (i, 0)),
        out_shape=jax.ShapeDtypeStruct((T, 3 * W), BF16),
        compiler_params=_cparams("arbitrary"),
        name="dilated_combine",
    )(*os_, *lses)


def _flash_init(m_scr, l_scr, acc_scr):
    m_scr[...] = jnp.full(m_scr.shape, NEG, F32)
    l_scr[...] = jnp.zeros(l_scr.shape, F32)
    acc_scr[...] = jnp.zeros(acc_scr.shape, F32)


def _flash_step(qs, k, v, bias, mask, m_scr, l_scr, acc_scr):
    s = _dot_nt(qs, k) + bias
    s = jnp.where(mask, s, NEG)
    m_prev = m_scr[...]
    m_new = jnp.maximum(m_prev, jnp.max(s, axis=1, keepdims=True))
    alpha = jnp.exp(m_prev - m_new)
    p = jnp.where(mask, jnp.exp(s - m_new), 0.0)
    l_scr[...] = alpha * l_scr[...] + jnp.sum(p, axis=1, keepdims=True)
    acc_scr[...] = alpha * acc_scr[...] + _dot(p.astype(BF16), v)
    m_scr[...] = m_new


def _top_picks(score, n_pick):
    idx = _iota(score.shape, 1).astype(F32)

    def pick(_, carry):
        g, sel = carry
        mx = jnp.max(g, axis=1, keepdims=True)
        first = jnp.min(jnp.where(g == mx, idx, 1e9), axis=1, keepdims=True)
        pk = idx == first
        return jnp.where(pk, -3e38, g), jnp.where(pk, 1.0, sel)

    _, sel = lax.fori_loop(0, n_pick, pick, (score, jnp.zeros(score.shape, F32)))
    return sel


def _moba_kernel(q_ref, k_ref, v_ref, bias_ref, o_ref, km_scr, m_scr, l_scr, acc_scr, *, dmax, n_blocks):
    c = pl.program_id(2)
    NB = km_scr.shape[0]
    BLK = MOBA_BLOCK
    R = 2 * BLK

    @pl.when(c == 0)
    def _():
        km_scr[...] = jnp.zeros(km_scr.shape, F32)

        def body(j, _):
            kb = k_ref[0, pl.ds(pl.multiple_of(j * BLK, BLK), BLK), :].astype(F32)
            km_scr[pl.ds(j, 1), :] = jnp.mean(kb, axis=0, keepdims=True)
            return 0
        lax.fori_loop(0, n_blocks, body, 0)

    qs = _stack_pair(q_ref[0])
    gate = _dot_nt(qs, km_scr[...].astype(BF16))
    blk = _iota((R, NB), 1)
    gate = jnp.where(blk < c, gate, NEG)
    sel = _top_picks(gate, min(MOBA_TOPK, n_blocks))
    sel = jnp.where(blk < c, sel, 0.0)

    _flash_init(m_scr, l_scr, acc_scr)

    def body(j, _):
        start = pl.multiple_of(j * BLK, BLK)
        kj = k_ref[0, pl.ds(start, BLK), :]
        vj = v_ref[0, pl.ds(start, BLK), :]
        bias = bias_ref[jnp.minimum(c - j, dmax)].reshape(R, BLK)
        selj = jnp.sum(jnp.where(blk == j, sel, 0.0), axis=1, keepdims=True) > 0.5
        mask = jnp.broadcast_to(selj, (R, BLK))
        _flash_step(qs, kj, vj, bias, mask, m_scr, l_scr, acc_scr)
        return 0

    lax.fori_loop(0, c, body, 0)

    start = pl.multiple_of(c * BLK, BLK)
    ko = k_ref[0, pl.ds(start, BLK), :]
    vo = v_ref[0, pl.ds(start, BLK), :]
    row = _iota((R, BLK), 0) % BLK
    col = _iota((R, BLK), 1)
    _flash_step(qs, ko, vo, bias_ref[0].reshape(R, BLK), col <= row, m_scr, l_scr, acc_scr)
    o_ref[0] = _unstack_pair(acc_scr[...] / l_scr[...]).astype(o_ref.dtype)


def _moba(proj, bias_b, dmax):
    B, S, _ = proj.shape
    NB = S // MOBA_BLOCK
    n_tiles = bias_b.shape[0]
    qb, kb, vb = BLK_B, BLK_B + 4, BLK_B + 8
    kern = functools.partial(_moba_kernel, dmax=dmax, n_blocks=NB)
    R = 2 * MOBA_BLOCK
    nb_pad = -(-NB // LANES) * LANES
    return pl.pallas_call(
        kern,
        grid=(B, B_HEADS // 2, NB),
        in_specs=[pl.BlockSpec((1, MOBA_BLOCK, LANES), lambda b, p, c: (b, c, qb + p)),
                  pl.BlockSpec((1, S, LANES), lambda b, p, c: (b, 0, kb + p)),
                  pl.BlockSpec((1, S, LANES), lambda b, p, c: (b, 0, vb + p)),
                  pl.BlockSpec((n_tiles, 2, MOBA_BLOCK, MOBA_BLOCK), lambda b, p, c: (0, p, 0, 0))],
        out_specs=pl.BlockSpec((1, MOBA_BLOCK, LANES), lambda b, p, c: (b, c, p)),
        out_shape=jax.ShapeDtypeStruct((B, S, B_HEADS * HEAD_DIM), BF16),
        scratch_shapes=[pltpu.VMEM((nb_pad, LANES), F32),
                        pltpu.VMEM((R, 1), F32), pltpu.VMEM((R, 1), F32), pltpu.VMEM((R, LANES), F32)],
        compiler_params=_cparams("arbitrary", "arbitrary", "arbitrary"),
        name="moba",
    )(proj, proj, proj, bias_b)


def _gelu_tanh(x):
    return 0.5 * x * (1.0 + jnp.tanh(math.sqrt(2.0 / math.pi) * (x + 0.044715 * (x * x * x))))


def _compress_kernel(sub_ref, pe_ref, w1_ref, w2_ref, o_ref, v_scr):
    n16 = sub_ref.shape[2]
    sub = sub_ref[0, 0].astype(F32)
    top = (sub + pe_ref[0:1, :]).astype(BF16)
    bot = (sub + pe_ref[1:2, :]).astype(BF16)
    u = _dot(top, w1_ref[0])
    v_scr[pl.ds(0, n16), :] = _dot(bot, w1_ref[1])
    v_scr[pl.ds(n16, 8), :] = jnp.zeros((8, v_scr.shape[1]), F32)
    hidden = u + v_scr[pl.ds(1, n16), :]
    o_ref[0, 0] = _dot(_gelu_tanh(hidden).astype(BF16), w2_ref[...]).astype(o_ref.dtype)


def _compress(sub, pe_rows, w1_blk, w2_blk):
    B, KV, n16, W = sub.shape
    return pl.pallas_call(
        _compress_kernel,
        grid=(B, KV),
        in_specs=[pl.BlockSpec((1, 1, n16, W), lambda b, n: (b, n, 0, 0)),
                  pl.BlockSpec((2, W), lambda b, n: (0, 0)),
                  pl.BlockSpec((2, W, 2 * CMP_HIDDEN), lambda b, n: (0, 0, 0)),
                  pl.BlockSpec((2 * CMP_HIDDEN, LANES), lambda b, n: (0, 0))],
        out_specs=pl.BlockSpec((1, 1, n16, LANES), lambda b, n: (b, n, 0, 0)),
        out_shape=jax.ShapeDtypeStruct((B, KV, n16, LANES), BF16),
        scratch_shapes=[pltpu.VMEM((n16 + 8, 2 * CMP_HIDDEN), F32)],
        compiler_params=_cparams("arbitrary", "arbitrary"),
        name="nsa_compress",
    )(sub, pe_rows, w1_blk, w2_blk)


def _nsa_kernel(q_ref, kc_ref, slc_ref, win_ref, gate_ref, bias_ref, ov_ref, o_ref,
                m_scr, l_scr, acc_scr, *, dmax, n_sel):
    n = pl.program_id(1)
    c = pl.program_id(2)
    CQ = C_CHUNK
    R = C_GROUP * CQ
    lane = _iota((CQ, LANES), 1)
    q = q_ref[0].astype(F32)
    heads = []
    for pr in range(2):
        qp = q[:, pr * LANES:(pr + 1) * LANES]
        heads.append(jnp.where(lane < HEAD_DIM, qp, 0.0))
        heads.append(jnp.where(lane < HEAD_DIM, pltpu.roll(qp, HEAD_DIM, 1), 0.0))
    qs = jnp.concatenate(heads, axis=0).astype(BF16)

    kc = kc_ref[0, 0]
    NC = kc.shape[0]
    s = _dot_nt(qs, kc)
    tq = c * CQ + _iota((R, NC), 0) % CQ
    mask = (_iota((R, NC), 1) * CMP_STRIDE + (CMP_LEN - 1)) <= tq
    s = jnp.where(mask, s, NEG)
    m = jnp.max(s, axis=1, keepdims=True)
    e = jnp.where(mask, jnp.exp(s - m), 0.0)
    den = jnp.sum(e, axis=1, keepdims=True)
    p = e / jnp.maximum(den, 1e-30)
    o_cmp = _dot(p.astype(BF16), kc)

    psum = p[0:CQ] + p[CQ:2 * CQ] + p[2 * CQ:3 * CQ] + p[3 * CQ:4 * CQ]
    imp = _dot(psum, ov_ref[...], precision=lax.Precision.HIGHEST)
    NS = imp.shape[1]
    jj = _iota((CQ, NS), 1)
    own = (c * CQ + _iota((CQ, NS), 0)) // SLC_BLOCK
    forced = (jj == 0) | (jj == own) | (jj == own - 1)
    score = jnp.where(jj <= own, jnp.where(forced, FORCE_SCORE, imp), -1.0)
    selb = _top_picks(score, n_sel).astype(BF16)

    row_i = _iota((CQ, LANES), 0)
    per_tile = LANES // SLC_BLOCK

    _flash_init(m_scr, l_scr, acc_scr)

    def slc_body(kt, _):
        kv = slc_ref[0, pl.ds(pl.multiple_of(kt * LANES, LANES), LANES), :]
        expand = jnp.where(_iota((NS, LANES), 0) == per_tile * kt + _iota((NS, LANES), 1) // SLC_BLOCK,
                           1.0, 0.0).astype(BF16)
        chosen = _dot(selb, expand)
        ok = (chosen > 0.5) & (kt * LANES + lane <= c * CQ + row_i)
        okf = jnp.where(ok, 1.0, 0.0)
        mask4 = jnp.concatenate([okf] * C_GROUP, axis=0) > 0.5
        bias = bias_ref[jnp.minimum(c - kt, dmax)].reshape(R, LANES)
        _flash_step(qs, kv, kv, bias, mask4, m_scr, l_scr, acc_scr)
        return 0

    lax.fori_loop(0, c + 1, slc_body, 0)
    o_slc = acc_scr[...] / l_scr[...]

    _flash_init(m_scr, l_scr, acc_scr)
    for dd in range(WIN // LANES + 1):
        kt = c - dd
        kv = win_ref[0, pl.ds(pl.multiple_of(jnp.maximum(kt, 0) * LANES, LANES), LANES), :]
        dist = dd * LANES + row_i - lane
        okf = jnp.where((dist >= 0) & (dist < WIN) & (kt >= 0), 1.0, 0.0)
        mask4 = jnp.concatenate([okf] * C_GROUP, axis=0) > 0.5
        _flash_step(qs, kv, kv, bias_ref[dd].reshape(R, LANES), mask4, m_scr, l_scr, acc_scr)
    o_win = acc_scr[...] / l_scr[...]

    gs = jax.nn.sigmoid(gate_ref[0].astype(F32))
    outs = []
    for gi in range(C_GROUP):
        rows = slice(gi * CQ, (gi + 1) * CQ)
        col0 = (n * C_GROUP + gi) * 3
        acc = jnp.zeros((CQ, LANES), F32)
        for br, ob in enumerate((o_cmp, o_slc, o_win)):
            gcol = jnp.sum(jnp.where(lane == col0 + br, gs, 0.0), axis=1, keepdims=True)
            acc = acc + gcol * ob[rows]
        outs.append(acc)
    for pr in range(2):
        pair = jnp.where(lane < HEAD_DIM, pltpu.roll(outs[2 * pr], HEAD_DIM, 1), outs[2 * pr + 1])
        o_ref[0, :, pr * LANES:(pr + 1) * LANES] = pair.astype(o_ref.dtype)


def _nsa(proj, kcvc, bias_c, overlap, dmax):
    B, S, _ = proj.shape
    n_chunks = S // C_CHUNK
    NC = kcvc.shape[2]
    NS = S // SLC_BLOCK
    n_tiles = bias_c.shape[0]
    R = C_GROUP * C_CHUNK
    kern = functools.partial(_nsa_kernel, dmax=dmax, n_sel=min(SLC_TOPN, NS))
    return pl.pallas_call(
        kern,
        grid=(B, C_KV_HEADS, n_chunks),
        in_specs=[pl.BlockSpec((1, C_CHUNK, 2 * LANES), lambda b, n, c: (b, c, BLK_CQ // 2 + n)),
                  pl.BlockSpec((1, 1, NC, LANES), lambda b, n, c: (b, n, 0, 0)),
                  pl.BlockSpec((1, S, LANES), lambda b, n, c: (b, 0, BLK_CKV + 3 * n + 1)),
                  pl.BlockSpec((1, S, LANES), lambda b, n, c: (b, 0, BLK_CKV + 3 * n + 2)),
                  pl.BlockSpec((1, C_CHUNK, LANES), lambda b, n, c: (b, c, BLK_GATE)),
                  pl.BlockSpec((n_tiles, C_GROUP, C_CHUNK, LANES), lambda b, n, c: (0, n, 0, 0)),
                  pl.BlockSpec((NC, NS), lambda b, n, c: (0, 0))],
        out_specs=pl.BlockSpec((1, C_CHUNK, 2 * LANES), lambda b, n, c: (b, c, n)),
        out_shape=jax.ShapeDtypeStruct((B, S, C_HEADS * HEAD_DIM), BF16),
        scratch_shapes=[pltpu.VMEM((R, 1), F32), pltpu.VMEM((R, 1), F32), pltpu.VMEM((R, LANES), F32)],
        compiler_params=_cparams("arbitrary", "arbitrary", "arbitrary"),
        name="nsa",
    )(proj, kcvc, proj, proj, proj, bias_c, overlap)


def _inproj_perm():
    perm = list(range(A_WIDTH + B_WIDTH + C_Q_WIDTH))
    kv0 = A_WIDTH + B_WIDTH + C_Q_WIDTH
    for n in range(C_KV_HEADS):
        for br in range(3):
            for t in (2 * br, 2 * br + 1):
                base = kv0 + t * C_KV_HEADS * HEAD_DIM + n * HEAD_DIM
                perm.extend(range(base, base + HEAD_DIM))
    g0 = kv0 + C_KV_WIDTH
    perm.extend(range(g0, g0 + C_GATE_WIDTH))
    perm.extend([IN_WIDTH] * (P_WIDTH - len(perm)))
    scale = np.ones((P_WIDTH,), np.float32)
    for g in range(len(A_CONFIGS)):
        scale[g * 768:g * 768 + 256] = Q_SCALE
    scale[A_WIDTH:A_WIDTH + B_HEADS * HEAD_DIM] = Q_SCALE
    scale[A_WIDTH + B_WIDTH:A_WIDTH + B_WIDTH + C_Q_WIDTH] = Q_SCALE
    return np.asarray(perm, np.int32), scale


def _compress_weights(cmp_w1, cmp_w2, cmp_pe):
    half = CMP_LEN // 2
    w1 = cmp_w1.reshape(2, 2, half, HEAD_DIM, CMP_HIDDEN)
    w1_blk = jnp.zeros((2, half, 2, HEAD_DIM, 2, CMP_HIDDEN), F32)
    w1_blk = w1_blk.at[:, :, 0, :, 0, :].set(w1[0])
    w1_blk = w1_blk.at[:, :, 1, :, 1, :].set(w1[1])
    w1_blk = w1_blk.reshape(2, half * 2 * HEAD_DIM, 2 * CMP_HIDDEN).astype(BF16)
    w2_blk = jnp.zeros((2, CMP_HIDDEN, 2, HEAD_DIM), F32)
    w2_blk = w2_blk.at[0, :, 0, :].set(cmp_w2[0])
    w2_blk = w2_blk.at[1, :, 1, :].set(cmp_w2[1])
    w2_blk = w2_blk.reshape(2 * CMP_HIDDEN, 2 * HEAD_DIM).astype(BF16)
    pe = cmp_pe.reshape(2, 2, half, HEAD_DIM)
    pe_rows = pe.transpose(1, 2, 0, 3).reshape(2, half * 2 * HEAD_DIM)
    return w1_blk, w2_blk, pe_rows


def kernel(x, rel_table, w_in, w_out, cmp_w1, cmp_w2, cmp_pe, norm_attn, norm_mlp,
           w_up, conv_w, conv_b, w_down, norm_final):
    B, S, D = x.shape
    depth = w_in.shape[0]
    T = B * S

    a = jnp.arange(A_BLOCK)[:, None]
    bk = jnp.arange(2 * A_BLOCK)[None, :]
    rel = a + A_BLOCK - bk
    idx_a = jnp.stack([_t5_bucket(rel * d) for _, d in A_CONFIGS]).astype(jnp.int32)
    bias_a = _bias_tiles(rel_table[:A_HEADS], idx_a)
    dmax_b = _n_far_tiles(S // MOBA_BLOCK, MOBA_BLOCK)
    bias_b = _bias_tiles(rel_table[A_HEADS:A_HEADS + B_HEADS],
                         _toeplitz_bucket_tiles(dmax_b + 1, MOBA_BLOCK))
    dmax_c = _n_far_tiles(S // C_CHUNK, C_CHUNK)
    bias_c = _bias_tiles(rel_table[A_HEADS + B_HEADS:],
                         _toeplitz_bucket_tiles(dmax_c + 1, C_CHUNK))

    n16 = S // CMP_STRIDE
    ci = jnp.arange(n16)[:, None] * CMP_STRIDE
    sj = jnp.arange(S // SLC_BLOCK)[None, :] * SLC_BLOCK
    overlap = ((ci < sj + SLC_BLOCK) & (ci + CMP_LEN > sj)).astype(F32)

    perm, scale = _inproj_perm()
    w_in_z = jnp.concatenate([w_in, jnp.zeros((depth, D, 1), w_in.dtype)], axis=2)
    w_in_p = (jnp.take(w_in_z, perm, axis=2) * scale).astype(BF16)
    w_out_b = w_out.astype(BF16)
    w_up_b = w_up.astype(BF16)
    w_down_b = w_down.astype(BF16)
    wa_rows = A_HEADS * HEAD_DIM
    wb_rows = wa_rows + B_HEADS * HEAD_DIM

    x2 = x.reshape(T, D)
    for l in range(depth):
        proj = _inproj(x2, norm_attn[l], w_in_p[l]).reshape(B, S, P_WIDTH)
        os_, lses = [], []
        for g, (_, d) in enumerate(A_CONFIGS):
            o, lse = _dilated_group(proj, bias_a, g, d)
            os_.append(o)
            lses.append(lse)
        o_a = _combine_groups(os_, lses)
        o_b = _moba(proj, bias_b, dmax_b).reshape(T, B_HEADS * HEAD_DIM)
        w1_blk, w2_blk, pe_rows = _compress_weights(cmp_w1[l], cmp_w2[l], cmp_pe[l])
        cmp_cols = jnp.stack([proj[:, :, (BLK_CKV + 3 * n) * LANES:(BLK_CKV + 3 * n + 1) * LANES]
                              for n in range(C_KV_HEADS)], axis=1)
        sub = cmp_cols.reshape(B, C_KV_HEADS, n16, CMP_STRIDE * LANES)
        kcvc = _compress(sub, pe_rows, w1_blk, w2_blk)
        o_c = _nsa(proj, kcvc, bias_c, overlap, dmax_c).reshape(T, C_HEADS * HEAD_DIM)
        x2 = _outproj(x2, o_a, o_b, o_c, w_out_b[l, :wa_rows], w_out_b[l, wa_rows:wb_rows],
                      w_out_b[l, wb_rows:])
        x2 = _ffn(x2, norm_mlp[l], w_up_b[l], conv_w[l], conv_b[l], w_down_b[l], S)
    return _final_norm(x2, norm_final).reshape(B, S, D)
```

```python
import functools
import math

import numpy as np
import jax
import jax.numpy as jnp
from jax import lax
from jax.experimental import pallas as pl
from jax.experimental.pallas import tpu as pltpu

F32 = jnp.float32
BF16 = jnp.bfloat16

HEAD_DIM = 64
LANES = 128
A_CONFIGS = ((128, 1), (512, 4), (2048, 16))
A_HEADS_PER_GROUP = 4
A_HEADS = 12
A_BLOCK = 128
B_HEADS = 8
MOBA_BLOCK = 256
MOBA_TOPK = 3
MOBA_UNROLL = 2
C_KV_HEADS = 3
C_GROUP = 4
C_HEADS = 12
CMP_STRIDE = 16
CMP_LEN = 32
CMP_HIDDEN = 128
SLC_BLOCK = 64
SLC_TOPN = 16
WIN = 512
FORCE_SCORE = 1e9
N_BUCKETS = 32
T5_MAX_DIST = 2048
EPS = 1e-6
NEG = -1e30
M_FLOOR = -1e20
Q_SCALE = HEAD_DIM ** -0.5
C_CHUNK = 128
NSA_UNROLL = 2

A_WIDTH = 3 * A_HEADS * HEAD_DIM
A_GROUP_WIDTH = 3 * A_HEADS_PER_GROUP * HEAD_DIM
B_WIDTH = 3 * B_HEADS * HEAD_DIM
C_Q_WIDTH = C_HEADS * HEAD_DIM
C_KV_WIDTH = 6 * C_KV_HEADS * HEAD_DIM
C_GATE_WIDTH = 3 * C_HEADS
IN_WIDTH = A_WIDTH + B_WIDTH + C_Q_WIDTH + C_KV_WIDTH + C_GATE_WIDTH
BLK_A = 0
BLK_B = A_WIDTH // LANES
BLK_CQ = BLK_B + B_WIDTH // LANES
BLK_CKV = BLK_CQ + C_Q_WIDTH // LANES
BLK_GATE = BLK_CKV + 3 * C_KV_HEADS
N_BLK = BLK_GATE + 1
P_WIDTH = N_BLK * LANES

VMEM_LIMIT = 56 * 1024 * 1024


def _cparams(*sem):
    return pltpu.CompilerParams(dimension_semantics=sem, vmem_limit_bytes=VMEM_LIMIT)


def _dot_nt(a, b):
    return lax.dot_general(a, b, (((1,), (1,)), ((), ())), preferred_element_type=F32)


def _dot(a, b, **kw):
    return jnp.dot(a, b, preferred_element_type=F32, **kw)


def _iota(shape, dim):
    return lax.broadcasted_iota(jnp.int32, shape, dim)


def _t5_bucket(dist):
    n = jnp.maximum(dist, 0)
    max_exact = N_BUCKETS // 2
    nf = jnp.maximum(n, 1).astype(F32)
    large = max_exact + (jnp.log(nf / max_exact) / math.log(T5_MAX_DIST / max_exact)
                         * (N_BUCKETS - max_exact)).astype(jnp.int32)
    large = jnp.minimum(large, N_BUCKETS - 1)
    return jnp.where(n < max_exact, n, large)


def _bias_lookup_kernel(tbl_ref, idx_ref, o_ref):
    h = pl.program_id(1)
    idx = idx_ref[0]
    acc = jnp.zeros(idx.shape, F32)
    for b in range(N_BUCKETS):
        acc = jnp.where(idx == b, tbl_ref[h, b], acc)
    o_ref[0, 0] = acc


def _bias_tiles(tbl, idx, heads_per_row=1):
    n, R, C = idx.shape
    H = tbl.shape[0]
    hpr = heads_per_row
    return pl.pallas_call(
        _bias_lookup_kernel,
        grid=(n, H),
        in_specs=[pl.BlockSpec(memory_space=pltpu.SMEM),
                  pl.BlockSpec((1, R, C), lambda t, h: (t, 0, 0))],
        out_specs=pl.BlockSpec((1, 1, R, C), lambda t, h: (t, h // hpr, 0, h % hpr)),
        out_shape=jax.ShapeDtypeStruct((n, H // hpr, R, hpr * C), F32),
        compiler_params=_cparams("arbitrary", "arbitrary"),
        name="bias_tiles",
    )(tbl, idx)


def _toeplitz_bucket_tiles_t(n_tiles, size):
    d = jnp.arange(n_tiles)[:, None, None] * size
    key = jnp.arange(size)[None, :, None]
    qry = jnp.arange(size)[None, None, :]
    return _t5_bucket(d + qry - key).astype(jnp.int32)


def _n_far_tiles(n_chunks, size):
    dmax = -(-(T5_MAX_DIST + size - 1) // size)
    dmax = min(dmax, n_chunks - 1)
    return dmax


def _norm_rows(x, g):
    ms = jnp.mean(x * x, axis=-1, keepdims=True)
    return x * lax.rsqrt(ms + EPS) * g


def _inproj_kernel(x_ref, g_ref, w_ref, o_ref, h_scr):
    @pl.when(pl.program_id(1) == 0)
    def _():
        h_scr[...] = _norm_rows(x_ref[...], g_ref[...]).astype(BF16)

    o_ref[...] = _dot(h_scr[...], w_ref[...]).astype(o_ref.dtype)


def _inproj(x2, gain, w, tm=1024, tn=256):
    T, D = x2.shape
    N = w.shape[1]
    return pl.pallas_call(
        _inproj_kernel,
        grid=(T // tm, N // tn),
        in_specs=[pl.BlockSpec((tm, D), lambda i, j: (i, 0)),
                  pl.BlockSpec((1, D), lambda i, j: (0, 0)),
                  pl.BlockSpec((D, tn), lambda i, j: (0, j))],
        out_specs=pl.BlockSpec((tm, tn), lambda i, j: (i, j)),
        out_shape=jax.ShapeDtypeStruct((T, N), BF16),
        scratch_shapes=[pltpu.VMEM((tm, D), BF16)],
        compiler_params=_cparams("arbitrary", "arbitrary"),
        name="inproj",
    )(x2, gain.reshape(1, D), w)


def _outproj_kernel(x_ref, a_ref, b_ref, c_ref, wa_ref, wb_ref, wc_ref, o_ref):
    acc = _dot(a_ref[...], wa_ref[...])
    acc += _dot(b_ref[...], wb_ref[...])
    acc += _dot(c_ref[...], wc_ref[...])
    o_ref[...] = x_ref[...] + acc


def _outproj(x2, oa, ob, oc, wa, wb, wc, tm=1024, tn=512):
    T, D = x2.shape
    return pl.pallas_call(
        _outproj_kernel,
        grid=(T // tm, D // tn),
        in_specs=[pl.BlockSpec((tm, tn), lambda i, j: (i, j)),
                  pl.BlockSpec((tm, oa.shape[1]), lambda i, j: (i, 0)),
                  pl.BlockSpec((tm, ob.shape[1]), lambda i, j: (i, 0)),
                  pl.BlockSpec((tm, oc.shape[1]), lambda i, j: (i, 0)),
                  pl.BlockSpec((wa.shape[0], tn), lambda i, j: (0, j)),
                  pl.BlockSpec((wb.shape[0], tn), lambda i, j: (0, j)),
                  pl.BlockSpec((wc.shape[0], tn), lambda i, j: (0, j))],
        out_specs=pl.BlockSpec((tm, tn), lambda i, j: (i, j)),
        out_shape=jax.ShapeDtypeStruct((T, D), F32),
        compiler_params=_cparams("arbitrary", "arbitrary"),
        name="outproj",
    )(x2, oa, ob, oc, wa, wb, wc)


FFN_HALO = 8


def _ffn_kernel(x_ref, halo_ref, g_ref, wa_ref, wg_ref, cwa_ref, cwg_ref, cba_ref, cbg_ref, wd_ref,
                o_ref, h_scr, ua_scr, ug_scr, *, tm, blocks_per_seq):
    i = pl.program_id(0)
    f = pl.program_id(1)

    @pl.when(f == 0)
    def _():
        x = x_ref[...]
        h_scr[pl.ds(FFN_HALO, tm), :] = _norm_rows(x, g_ref[...]).astype(BF16)
        keep = jnp.where(i % blocks_per_seq == 0, 0.0, 1.0)
        h_scr[pl.ds(0, FFN_HALO), :] = (_norm_rows(halo_ref[0], g_ref[...]) * keep).astype(BF16)
        o_ref[...] = x

    h = h_scr[...]
    ua_scr[...] = _dot(h, wa_ref[...])
    ug_scr[...] = _dot(h, wg_ref[...])

    def conv(u_scr, cw_ref, cb_ref):
        acc = cb_ref[...]
        for j in range(3):
            acc = acc + cw_ref[pl.ds(j, 1), :] * u_scr[pl.ds(FFN_HALO - j, tm), :]
        return acc

    ya = conv(ua_scr, cwa_ref, cba_ref)
    yg = conv(ug_scr, cwg_ref, cbg_ref)
    act = (yg * jax.nn.sigmoid(yg) * ya).astype(BF16)
    o_ref[...] += _dot(act, wd_ref[...])


def _ffn(x2, gain, w_up, conv_w, conv_b, w_down, seq, tm=512, tf=512):
    T, D = x2.shape
    Fh = w_down.shape[0]
    nf = Fh // tf
    halo_view = x2.reshape(T // FFN_HALO, FFN_HALO, D)
    rb = tm // FFN_HALO
    kern = functools.partial(_ffn_kernel, tm=tm, blocks_per_seq=seq // tm)
    return pl.pallas_call(
        kern,
        grid=(T // tm, nf),
        in_specs=[pl.BlockSpec((tm, D), lambda i, f: (i, 0)),
                  pl.BlockSpec((1, FFN_HALO, D), lambda i, f: (jnp.maximum(i * rb - 1, 0), 0, 0)),
                  pl.BlockSpec((1, D), lambda i, f: (0, 0)),
                  pl.BlockSpec((D, tf), lambda i, f: (0, f)),
                  pl.BlockSpec((D, tf), lambda i, f: (0, f + nf)),
                  pl.BlockSpec((3, tf), lambda i, f: (0, f)),
                  pl.BlockSpec((3, tf), lambda i, f: (0, f + nf)),
                  pl.BlockSpec((1, tf), lambda i, f: (0, f)),
                  pl.BlockSpec((1, tf), lambda i, f: (0, f + nf)),
                  pl.BlockSpec((tf, D), lambda i, f: (f, 0))],
        out_specs=pl.BlockSpec((tm, D), lambda i, f: (i, 0)),
        out_shape=jax.ShapeDtypeStruct((T, D), F32),
        scratch_shapes=[pltpu.VMEM((tm + FFN_HALO, D), BF16),
                        pltpu.VMEM((tm + FFN_HALO, tf), F32),
                        pltpu.VMEM((tm + FFN_HALO, tf), F32)],
        compiler_params=_cparams("arbitrary", "arbitrary"),
        name="convffn",
    )(x2, halo_view, gain.reshape(1, D), w_up, w_up, conv_w, conv_w,
      conv_b.reshape(1, -1), conv_b.reshape(1, -1), w_down)


def _final_norm_kernel(x_ref, g_ref, o_ref):
    o_ref[...] = _norm_rows(x_ref[...], g_ref[...])


def _final_norm(x2, gain, tm=1024):
    T, D = x2.shape
    return pl.pallas_call(
        _final_norm_kernel,
        grid=(T // tm,),
        in_specs=[pl.BlockSpec((tm, D), lambda i: (i, 0)),
                  pl.BlockSpec((1, D), lambda i: (0, 0))],
        out_specs=pl.BlockSpec((tm, D), lambda i: (i, 0)),
        out_shape=jax.ShapeDtypeStruct((T, D), F32),
        compiler_params=_cparams("arbitrary"),
        name="final_norm",
    )(x2, gain.reshape(1, D))


def _stack_pair(qp):
    lane = _iota(qp.shape, 1)
    zero = jnp.zeros_like(qp)
    return jnp.concatenate([jnp.where(lane < HEAD_DIM, qp, zero),
                            jnp.where(lane >= HEAD_DIM, qp, zero)], axis=0)


def _unstack_pair(o):
    R = o.shape[0] // 2
    lane = _iota((R, LANES), 1)
    return jnp.where(lane < HEAD_DIM, o[:R], o[R:])


def _dilated_kernel(q_ref, kc_ref, kp_ref, vc_ref, vp_ref, bias_ref, o_ref, lse_ref):
    i = pl.program_id(3)
    qs = _stack_pair(q_ref[0])
    k = jnp.concatenate([kp_ref[0], kc_ref[0]], axis=0)
    v = jnp.concatenate([vp_ref[0], vc_ref[0]], axis=0)
    R = 2 * A_BLOCK
    s = _dot_nt(qs, k) + bias_ref[0].reshape(R, 2 * A_BLOCK)
    a = _iota((R, 2 * A_BLOCK), 0) % A_BLOCK
    bk = _iota((R, 2 * A_BLOCK), 1)
    rel = a + A_BLOCK - bk
    mask = (rel >= 0) & (rel <= A_BLOCK) & ((bk >= A_BLOCK) | (i > 0))
    s = jnp.where(mask, s, NEG)
    m = jnp.max(s, axis=1, keepdims=True)
    e = jnp.where(mask, jnp.exp(s - m), 0.0)
    den = jnp.sum(e, axis=1, keepdims=True)
    o = _dot(e.astype(BF16), v) / den
    lse = m + jnp.log(den)
    o_ref[0] = _unstack_pair(o)
    lse_ref[0] = _unstack_pair(jnp.broadcast_to(lse, (R, LANES)))


def _dilated_group(proj, bias_a, g, dilation):
    B, S, _ = proj.shape
    L = S // dilation
    nq = L // A_BLOCK
    nblk = A_GROUP_WIDTH // LANES
    cols = proj[:, :, g * A_GROUP_WIDTH:(g + 1) * A_GROUP_WIDTH]
    view = cols.reshape(B, L, dilation * A_GROUP_WIDTH)

    def cur(base):
        return pl.BlockSpec((1, A_BLOCK, LANES), lambda b, r, p, i: (b, i, r * nblk + base + p))

    def prev(base):
        return pl.BlockSpec((1, A_BLOCK, LANES),
                            lambda b, r, p, i: (b, jnp.maximum(i - 1, 0), r * nblk + base + p))

    out_spec = pl.BlockSpec((1, A_BLOCK, LANES), lambda b, r, p, i: (b, i, r * 2 + p))
    o, lse = pl.pallas_call(
        _dilated_kernel,
        grid=(B, dilation, 2, nq),
        in_specs=[cur(0), cur(2), prev(2), cur(4), prev(4),
                  pl.BlockSpec((1, 2, A_BLOCK, 2 * A_BLOCK), lambda b, r, p, i: (g, 2 * g + p, 0, 0))],
        out_specs=[out_spec, out_spec],
        out_shape=[jax.ShapeDtypeStruct((B, L, dilation * 2 * LANES), F32)] * 2,
        compiler_params=_cparams("arbitrary", "arbitrary", "arbitrary", "arbitrary"),
        name=f"dilated_g{g}",
    )(view, view, view, view, view, bias_a)
    return o.reshape(B * S, 2 * LANES), lse.reshape(B * S, 2 * LANES)


def _combine_kernel(o0, o1, o2, l0, l1, l2, out_ref):
    ls = [l0[...], l1[...], l2[...]]
    m = jnp.maximum(jnp.maximum(ls[0], ls[1]), ls[2])
    es = [jnp.exp(l - m) for l in ls]
    den = es[0] + es[1] + es[2]
    W = o0.shape[1]
    for g, o in enumerate((o0, o1, o2)):
        out_ref[:, g * W:(g + 1) * W] = (o[...] * (es[g] / den)).astype(out_ref.dtype)


def _combine_groups(os_, lses, tm=1024):
    T, W = os_[0].shape
    spec = pl.BlockSpec((tm, W), lambda i: (i, 0))
    return pl.pallas_call(
        _combine_kernel,
        grid=(T // tm,),
        in_specs=[spec] * 6,
        out_specs=pl.BlockSpec((tm, 3 * W), lambda i: (i, 0)),
        out_shape=jax.ShapeDtypeStruct((T, 3 * W), BF16),
        compiler_params=_cparams("arbitrary"),
        name="dilated_combine",
    )(*os_, *lses)


def _flash_init(m_scr, l_scr, acc_scrs):
    m_scr[...] = jnp.full(m_scr.shape, M_FLOOR, F32)
    l_scr[...] = jnp.zeros(l_scr.shape, F32)
    for acc in acc_scrs:
        acc[...] = jnp.zeros(acc.shape, F32)


def _flash_tiles(tiles, q_groups, m_scr, l_scr, acc_scrs):
    G = len(q_groups)
    m = [m_scr[:, g * LANES:(g + 1) * LANES] for g in range(G)]
    l = [l_scr[:, g * LANES:(g + 1) * LANES] for g in range(G)]
    units = [(t, g) for t in range(len(tiles)) for g in range(G)]

    def scores(t, g):
        k, _, terms_of = tiles[t]
        s = _dot_nt(k, q_groups[g])
        for term in terms_of(g):
            s = s + term
        return s

    def apply(g, alpha, pv):
        acc_scrs[g][...] = alpha * acc_scrs[g][...] + pv

    s_next = scores(*units[0])
    pending = None
    for n, (t, g) in enumerate(units):
        s = s_next
        if n + 1 < len(units):
            s_next = scores(*units[n + 1])
        m_new = jnp.maximum(m[g], jnp.max(s, axis=0, keepdims=True))
        alpha = jnp.exp(m[g] - m_new)
        p = jnp.exp(s - m_new)
        l[g] = alpha * l[g] + jnp.sum(p, axis=0, keepdims=True)
        m[g] = m_new
        pv = _dot(tiles[t][1], p.astype(BF16))
        if pending is not None:
            apply(*pending)
        pending = (g, alpha, pv)
    apply(*pending)
    m_scr[...] = jnp.concatenate(m, axis=1)
    l_scr[...] = jnp.concatenate(l, axis=1)


def _top_picks(score, n_pick):
    idx = _iota(score.shape, 0).astype(F32)

    def pick(_, carry):
        g, sel = carry
        mx = jnp.max(g, axis=0, keepdims=True)
        first = jnp.min(jnp.where(g == mx, idx, 1e9), axis=0, keepdims=True)
        pk = idx == first
        return jnp.where(pk, -3e38, g), jnp.where(pk, 1.0, sel)

    _, sel = lax.fori_loop(0, n_pick, pick, (score, jnp.zeros(score.shape, F32)))
    return sel


def _moba_kernel(q_ref, k_ref, vt_ref, bias_ref, o_ref, km_scr, sel_scr, m_scr, l_scr, *acc_scrs,
                 dmax, n_blocks):
    c = pl.program_id(2)
    BLK = MOBA_BLOCK
    Q = 2 * BLK

    @pl.when(c == 0)
    def _():
        def body(j, _):
            kb = k_ref[0, pl.ds(pl.multiple_of(j * BLK, BLK), BLK), :].astype(F32)
            km_scr[pl.ds(j, 1), :] = jnp.mean(kb, axis=0, keepdims=True)
            return 0
        lax.fori_loop(0, n_blocks, body, 0)

    qs = _stack_pair(q_ref[0])
    gate = _dot_nt(km_scr[...].astype(BF16), qs)
    blk = _iota(gate.shape, 0)
    gate = jnp.where(blk < c, gate, NEG)
    sel = _top_picks(gate, min(MOBA_TOPK, n_blocks))
    sel_scr[...] = jnp.where((blk < c) & (sel > 0.5), 0.0, NEG)

    _flash_init(m_scr, l_scr, acc_scrs)
    q_groups = [qs[g * LANES:(g + 1) * LANES] for g in range(Q // LANES)]

    def past_tile(j):
        jc = jnp.minimum(j, n_blocks - 1)
        kj = k_ref[0, pl.ds(pl.multiple_of(jc * BLK, BLK), BLK), :]
        tile = jnp.clip(c - j, 0, dmax)
        selrow = sel_scr[pl.ds(jc, 1), :]

        def terms(g):
            return bias_ref[tile, 0, :, pl.ds(g * LANES, LANES)], selrow[:, g * LANES:(g + 1) * LANES]

        return kj, vt_ref[0, jc], terms

    def body(it, _):
        _flash_tiles([past_tile(MOBA_UNROLL * it + u) for u in range(MOBA_UNROLL)],
                     q_groups, m_scr, l_scr, acc_scrs)
        return 0

    lax.fori_loop(0, (c + MOBA_UNROLL - 1) // MOBA_UNROLL, body, 0)

    ko = k_ref[0, pl.ds(pl.multiple_of(c * BLK, BLK), BLK), :]

    def own_terms(g):
        qpos = (g * LANES) % BLK + _iota((BLK, LANES), 1)
        return (bias_ref[0, 0, :, pl.ds(g * LANES, LANES)],
                jnp.where(_iota((BLK, LANES), 0) <= qpos, 0.0, NEG))

    _flash_tiles([(ko, vt_ref[0, c], own_terms)], q_groups, m_scr, l_scr, acc_scrs)
    ot = jnp.concatenate([acc[...] for acc in acc_scrs], axis=1) / l_scr[...]
    o2 = jnp.concatenate([ot[:HEAD_DIM, :BLK], ot[HEAD_DIM:, BLK:]], axis=0)
    o_ref[0] = o2.T.astype(o_ref.dtype)


def _moba(proj, vt, bias_b, dmax):
    B, S, _ = proj.shape
    NB = S // MOBA_BLOCK
    n_tiles = bias_b.shape[0]
    qb, kb = BLK_B, BLK_B + 4
    kern = functools.partial(_moba_kernel, dmax=dmax, n_blocks=NB)
    Q = 2 * MOBA_BLOCK
    return pl.pallas_call(
        kern,
        grid=(B, B_HEADS // 2, NB),
        in_specs=[pl.BlockSpec((1, MOBA_BLOCK, LANES), lambda b, p, c: (b, c, qb + p)),
                  pl.BlockSpec((1, S, LANES), lambda b, p, c: (b, 0, kb + p)),
                  pl.BlockSpec((1, NB, LANES, MOBA_BLOCK), lambda b, p, c: (b, 0, p, 0)),
                  pl.BlockSpec((n_tiles, 1, MOBA_BLOCK, Q), lambda b, p, c: (0, p, 0, 0))],
        out_specs=pl.BlockSpec((1, MOBA_BLOCK, LANES), lambda b, p, c: (b, c, p)),
        out_shape=jax.ShapeDtypeStruct((B, S, B_HEADS * HEAD_DIM), BF16),
        scratch_shapes=[pltpu.VMEM((NB, LANES), F32), pltpu.VMEM((NB, Q), F32),
                        pltpu.VMEM((1, Q), F32), pltpu.VMEM((1, Q), F32)]
        + [pltpu.VMEM((LANES, LANES), F32)] * (Q // LANES),
        compiler_params=_cparams("arbitrary", "arbitrary", "arbitrary"),
        name="moba",
    )(proj, proj, vt, bias_b.reshape(n_tiles, B_HEADS // 2, MOBA_BLOCK, Q))


def _gelu_tanh(x):
    return 0.5 * x * (1.0 + jnp.tanh(math.sqrt(2.0 / math.pi) * (x + 0.044715 * (x * x * x))))


def _compress_kernel(sub_ref, pe_ref, w1_ref, w2_ref, o_ref, ot_ref, v_scr):
    n16 = sub_ref.shape[2]
    sub = sub_ref[0, 0].astype(F32)
    top = (sub + pe_ref[0:1, :]).astype(BF16)
    bot = (sub + pe_ref[1:2, :]).astype(BF16)
    u = _dot(top, w1_ref[0])
    v_scr[pl.ds(0, n16), :] = _dot(bot, w1_ref[1])
    v_scr[pl.ds(n16, 8), :] = jnp.zeros((8, v_scr.shape[1]), F32)
    hidden = u + v_scr[pl.ds(1, n16), :]
    out = _dot(_gelu_tanh(hidden).astype(BF16), w2_ref[...])
    o_ref[0, 0] = out.astype(o_ref.dtype)
    ot_ref[0, 0] = out.T.astype(ot_ref.dtype)


def _compress(sub, pe_rows, w1_blk, w2_blk):
    B, KV, n16, W = sub.shape
    return pl.pallas_call(
        _compress_kernel,
        grid=(B, KV),
        in_specs=[pl.BlockSpec((1, 1, n16, W), lambda b, n: (b, n, 0, 0)),
                  pl.BlockSpec((2, W), lambda b, n: (0, 0)),
                  pl.BlockSpec((2, W, 2 * CMP_HIDDEN), lambda b, n: (0, 0, 0)),
                  pl.BlockSpec((2 * CMP_HIDDEN, LANES), lambda b, n: (0, 0))],
        out_specs=[pl.BlockSpec((1, 1, n16, LANES), lambda b, n: (b, n, 0, 0)),
                   pl.BlockSpec((1, 1, LANES, n16), lambda b, n: (b, n, 0, 0))],
        out_shape=[jax.ShapeDtypeStruct((B, KV, n16, LANES), BF16),
                   jax.ShapeDtypeStruct((B, KV, LANES, n16), BF16)],
        scratch_shapes=[pltpu.VMEM((n16 + 8, 2 * CMP_HIDDEN), F32)],
        compiler_params=_cparams("arbitrary", "arbitrary"),
        name="nsa_compress",
    )(sub, pe_rows, w1_blk, w2_blk)


def _nsa_kernel(q_ref, kc_ref, kct_ref, slc_ref, slct_ref, win_ref, wint_ref, gate_ref, bias_ref, ovt_ref,
                o_ref, sel_scr, g_scr, m_slc, l_slc, m_win, l_win, *acc_scrs, dmax, n_sel):
    n = pl.program_id(1)
    c = pl.program_id(2)
    CQ = C_CHUNK
    Q = C_GROUP * CQ
    lane = _iota((CQ, LANES), 1)
    q = q_ref[0].astype(F32)
    heads = []
    for pr in range(2):
        qp = q[:, pr * LANES:(pr + 1) * LANES]
        heads.append(jnp.where(lane < HEAD_DIM, qp, 0.0))
        heads.append(jnp.where(lane < HEAD_DIM, pltpu.roll(qp, HEAD_DIM, 1), 0.0))
    q_groups = [h.astype(BF16) for h in heads]
    qs = jnp.concatenate(q_groups, axis=0)

    kc = kc_ref[0, 0]
    NC = kc.shape[0]
    s = _dot_nt(kc, qs)
    tq = c * CQ + _iota((NC, Q), 1) % CQ
    mask = (_iota((NC, Q), 0) * CMP_STRIDE + (CMP_LEN - 1)) <= tq
    s = jnp.where(mask, s, NEG)
    m = jnp.max(s, axis=0, keepdims=True)
    e = jnp.where(mask, jnp.exp(s - m), 0.0)
    den = jnp.sum(e, axis=0, keepdims=True)
    p = e / jnp.maximum(den, 1e-30)
    o_cmp = _dot(kct_ref[0, 0], p.astype(BF16))

    psum = p[:, 0:CQ] + p[:, CQ:2 * CQ] + p[:, 2 * CQ:3 * CQ] + p[:, 3 * CQ:4 * CQ]
    imp = _dot(ovt_ref[...], psum, precision=lax.Precision.HIGHEST)
    NS = imp.shape[0]
    jj = _iota((NS, CQ), 0)
    own = (c * CQ + _iota((NS, CQ), 1)) // SLC_BLOCK
    forced = (jj == 0) | (jj == own) | (jj == own - 1)
    score = jnp.where(jj <= own, jnp.where(forced, FORCE_SCORE, imp), -1.0)
    sel_scr[...] = _top_picks(score, n_sel)

    key_row = _iota((LANES, CQ), 0)
    q_col = _iota((LANES, CQ), 1)
    per_tile = LANES // SLC_BLOCK
    n_kt = slct_ref.shape[2]
    acc_slc = acc_scrs[:C_GROUP]
    acc_win = acc_scrs[C_GROUP:]

    _flash_init(m_slc, l_slc, acc_slc)

    def slc_tile(kt):
        ktc = jnp.minimum(kt, n_kt - 1)
        k = slc_ref[0, pl.ds(pl.multiple_of(ktc * LANES, LANES), LANES), :]
        r = sel_scr[pl.ds(pl.multiple_of(per_tile * ktc, per_tile), per_tile), :]
        chosen = jnp.where(key_row < SLC_BLOCK, r[0:1, :], r[1:2, :])
        ok = (chosen > 0.5) & (kt * LANES + key_row <= c * CQ + q_col)
        negm = jnp.where(ok, 0.0, NEG)
        tile = jnp.clip(c - kt, 0, dmax)
        return k, slct_ref[0, 0, ktc], lambda g: (bias_ref[tile, 0, :, pl.ds(g * CQ, CQ)], negm)

    def slc_body(it, _):
        _flash_tiles([slc_tile(NSA_UNROLL * it + u) for u in range(NSA_UNROLL)], q_groups, m_slc, l_slc, acc_slc)
        return 0

    lax.fori_loop(0, (c + NSA_UNROLL) // NSA_UNROLL, slc_body, 0)

    _flash_init(m_win, l_win, acc_win)

    def win_tile(dd):
        kt = c - dd
        ktc = jnp.maximum(kt, 0)
        k = win_ref[0, pl.ds(pl.multiple_of(ktc * LANES, LANES), LANES), :]
        dist = dd * LANES + q_col - key_row
        negm = jnp.where((dist >= 0) & (dist < WIN) & (kt >= 0), 0.0, NEG)
        return k, wint_ref[0, 0, ktc], lambda g: (bias_ref[dd, 0, :, pl.ds(g * CQ, CQ)], negm)

    _flash_tiles([win_tile(dd) for dd in range(WIN // LANES + 1)], q_groups, m_win, l_win, acc_win)

    g_scr[...] = jax.nn.sigmoid(gate_ref[0].astype(F32)).T
    l_s = l_slc[...]
    l_w = l_win[...]
    res = []
    for gi in range(C_GROUP):
        cols = slice(gi * CQ, (gi + 1) * CQ)
        col0 = (n * C_GROUP + gi) * 3
        tot = (g_scr[pl.ds(col0, 1), :] * o_cmp[HEAD_DIM:, cols]
               + g_scr[pl.ds(col0 + 1, 1), :] * (acc_slc[gi][pl.ds(HEAD_DIM, HEAD_DIM), :] / l_s[:, cols])
               + g_scr[pl.ds(col0 + 2, 1), :] * (acc_win[gi][pl.ds(HEAD_DIM, HEAD_DIM), :] / l_w[:, cols]))
        res.append(tot)
    o_ref[0] = jnp.concatenate(res, axis=0).T.astype(o_ref.dtype)


def _nsa(proj, kcvc, kcvct, slct, wint, bias_c, overlap_t, dmax):
    B, S, _ = proj.shape
    n_chunks = S // C_CHUNK
    NC = kcvc.shape[2]
    NS = S // SLC_BLOCK
    n_tiles = bias_c.shape[0]
    Q = C_GROUP * C_CHUNK
    kern = functools.partial(_nsa_kernel, dmax=dmax, n_sel=min(SLC_TOPN, NS))
    tile_spec = pl.BlockSpec((1, 1, n_chunks, LANES, LANES), lambda b, n, c: (b, n, 0, 0, 0))
    return pl.pallas_call(
        kern,
        grid=(B, C_KV_HEADS, n_chunks),
        in_specs=[pl.BlockSpec((1, C_CHUNK, 2 * LANES), lambda b, n, c: (b, c, BLK_CQ // 2 + n)),
                  pl.BlockSpec((1, 1, NC, LANES), lambda b, n, c: (b, n, 0, 0)),
                  pl.BlockSpec((1, 1, LANES, NC), lambda b, n, c: (b, n, 0, 0)),
                  pl.BlockSpec((1, S, LANES), lambda b, n, c: (b, 0, BLK_CKV + 3 * n + 1)),
                  tile_spec,
                  pl.BlockSpec((1, S, LANES), lambda b, n, c: (b, 0, BLK_CKV + 3 * n + 2)),
                  tile_spec,
                  pl.BlockSpec((1, C_CHUNK, LANES), lambda b, n, c: (b, c, BLK_GATE)),
                  pl.BlockSpec((n_tiles, 1, C_CHUNK, Q), lambda b, n, c: (0, n, 0, 0)),
                  pl.BlockSpec((NS, NC), lambda b, n, c: (0, 0))],
        out_specs=pl.BlockSpec((1, C_CHUNK, 2 * LANES), lambda b, n, c: (b, c, n)),
        out_shape=jax.ShapeDtypeStruct((B, S, C_HEADS * HEAD_DIM), BF16),
        scratch_shapes=[pltpu.VMEM((NS, C_CHUNK), F32), pltpu.VMEM((LANES, C_CHUNK), F32),
                        pltpu.VMEM((1, Q), F32), pltpu.VMEM((1, Q), F32),
                        pltpu.VMEM((1, Q), F32), pltpu.VMEM((1, Q), F32)]
        + [pltpu.VMEM((LANES, C_CHUNK), F32)] * (2 * C_GROUP),
        compiler_params=_cparams("arbitrary", "arbitrary", "arbitrary"),
        name="nsa",
    )(proj, kcvc, kcvct, proj, slct, proj, wint, proj, bias_c, overlap_t)


def _inproj_perm():
    perm = list(range(A_WIDTH + B_WIDTH + C_Q_WIDTH))
    kv0 = A_WIDTH + B_WIDTH + C_Q_WIDTH
    for n in range(C_KV_HEADS):
        for br in range(3):
            for t in (2 * br, 2 * br + 1):
                base = kv0 + t * C_KV_HEADS * HEAD_DIM + n * HEAD_DIM
                perm.extend(range(base, base + HEAD_DIM))
    g0 = kv0 + C_KV_WIDTH
    perm.extend(range(g0, g0 + C_GATE_WIDTH))
    perm.extend([IN_WIDTH] * (P_WIDTH - len(perm)))
    scale = np.ones((P_WIDTH,), np.float32)
    for g in range(len(A_CONFIGS)):
        scale[g * A_GROUP_WIDTH:g * A_GROUP_WIDTH + A_HEADS_PER_GROUP * HEAD_DIM] = Q_SCALE
    scale[A_WIDTH:A_WIDTH + B_HEADS * HEAD_DIM] = Q_SCALE
    scale[A_WIDTH + B_WIDTH:A_WIDTH + B_WIDTH + C_Q_WIDTH] = Q_SCALE
    return np.asarray(perm, np.int32), scale


def _compress_weights(cmp_w1, cmp_w2, cmp_pe):
    half = CMP_LEN // 2
    w1 = cmp_w1.reshape(2, 2, half, HEAD_DIM, CMP_HIDDEN)
    w1_blk = jnp.zeros((2, half, 2, HEAD_DIM, 2, CMP_HIDDEN), F32)
    w1_blk = w1_blk.at[:, :, 0, :, 0, :].set(w1[0])
    w1_blk = w1_blk.at[:, :, 1, :, 1, :].set(w1[1])
    w1_blk = w1_blk.reshape(2, half * 2 * HEAD_DIM, 2 * CMP_HIDDEN).astype(BF16)
    w2_blk = jnp.zeros((2, CMP_HIDDEN, 2, HEAD_DIM), F32)
    w2_blk = w2_blk.at[0, :, 0, :].set(cmp_w2[0])
    w2_blk = w2_blk.at[1, :, 1, :].set(cmp_w2[1])
    w2_blk = w2_blk.reshape(2 * CMP_HIDDEN, 2 * HEAD_DIM).astype(BF16)
    pe = cmp_pe.reshape(2, 2, half, HEAD_DIM)
    pe_rows = pe.transpose(1, 2, 0, 3).reshape(2, half * 2 * HEAD_DIM)
    return w1_blk, w2_blk, pe_rows


def kernel(x, rel_table, w_in, w_out, cmp_w1, cmp_w2, cmp_pe, norm_attn, norm_mlp,
           w_up, conv_w, conv_b, w_down, norm_final):
    B, S, D = x.shape
    depth = w_in.shape[0]
    T = B * S
    NB = S // MOBA_BLOCK
    n_kt = S // LANES

    a = jnp.arange(A_BLOCK)[:, None]
    bk = jnp.arange(2 * A_BLOCK)[None, :]
    rel = a + A_BLOCK - bk
    idx_a = jnp.stack([_t5_bucket(rel * d) for _, d in A_CONFIGS]).astype(jnp.int32)
    bias_a = _bias_tiles(rel_table[:A_HEADS], idx_a)
    dmax_b = _n_far_tiles(NB, MOBA_BLOCK)
    bias_b = _bias_tiles(rel_table[A_HEADS:A_HEADS + B_HEADS],
                         _toeplitz_bucket_tiles_t(dmax_b + 1, MOBA_BLOCK), 2)
    dmax_c = _n_far_tiles(S // C_CHUNK, C_CHUNK)
    bias_c = _bias_tiles(rel_table[A_HEADS + B_HEADS:],
                         _toeplitz_bucket_tiles_t(dmax_c + 1, C_CHUNK), C_GROUP)

    n16 = S // CMP_STRIDE
    ci = jnp.arange(n16)[None, :] * CMP_STRIDE
    sj = jnp.arange(S // SLC_BLOCK)[:, None] * SLC_BLOCK
    overlap_t = ((ci < sj + SLC_BLOCK) & (ci + CMP_LEN > sj)).astype(F32)

    perm, scale = _inproj_perm()
    w_in_z = jnp.concatenate([w_in, jnp.zeros((depth, D, 1), w_in.dtype)], axis=2)
    w_in_p = (jnp.take(w_in_z, perm, axis=2) * scale).astype(BF16)
    w_out_b = w_out.astype(BF16)
    w_up_b = w_up.astype(BF16)
    w_down_b = w_down.astype(BF16)
    wa_rows = A_HEADS * HEAD_DIM
    wb_rows = wa_rows + B_HEADS * HEAD_DIM

    x2 = x.reshape(T, D)
    for l in range(depth):
        proj = _inproj(x2, norm_attn[l], w_in_p[l]).reshape(B, S, P_WIDTH)
        os_, lses = [], []
        for g, (_, d) in enumerate(A_CONFIGS):
            o, lse = _dilated_group(proj, bias_a, g, d)
            os_.append(o)
            lses.append(lse)
        o_a = _combine_groups(os_, lses)
        vb = proj[:, :, (BLK_B + 8) * LANES:(BLK_B + 12) * LANES]
        vt_b = vb.reshape(B, NB, MOBA_BLOCK, B_HEADS * HEAD_DIM).transpose(0, 1, 3, 2)
        o_b = _moba(proj, vt_b, bias_b, dmax_b).reshape(T, B_HEADS * HEAD_DIM)
        w1_blk, w2_blk, pe_rows = _compress_weights(cmp_w1[l], cmp_w2[l], cmp_pe[l])
        ckv = proj[:, :, BLK_CKV * LANES:BLK_GATE * LANES]
        ckv5 = ckv.reshape(B, S, C_KV_HEADS, 3, LANES)
        sub = ckv5[:, :, :, 0].transpose(0, 2, 1, 3).reshape(B, C_KV_HEADS, n16, CMP_STRIDE * LANES)
        kcvc, kcvct = _compress(sub, pe_rows, w1_blk, w2_blk)
        ckv6 = ckv.reshape(B, n_kt, LANES, C_KV_HEADS, 3, LANES)
        slct = ckv6[:, :, :, :, 1].transpose(0, 3, 1, 4, 2)
        wint = ckv6[:, :, :, :, 2].transpose(0, 3, 1, 4, 2)
        o_c = _nsa(proj, kcvc, kcvct, slct, wint, bias_c, overlap_t, dmax_c).reshape(T, C_HEADS * HEAD_DIM)
        x2 = _outproj(x2, o_a, o_b, o_c, w_out_b[l, :wa_rows], w_out_b[l, wa_rows:wb_rows],
                      w_out_b[l, wb_rows:])
        x2 = _ffn(x2, norm_mlp[l], w_up_b[l], conv_w[l], conv_b[l], w_down_b[l], S)
    return _final_norm(x2, norm_final).reshape(B, S, D)
```

```python
import functools
import math

import numpy as np
import jax
import jax.numpy as jnp
from jax import lax
from jax.experimental import pallas as pl
from jax.experimental.pallas import tpu as pltpu

F32 = jnp.float32
BF16 = jnp.bfloat16

HEAD_DIM = 64
LANES = 128
A_CONFIGS = ((128, 1), (512, 4), (2048, 16))
A_HEADS_PER_GROUP = 4
A_HEADS = 12
A_BLOCK = 128
B_HEADS = 8
MOBA_BLOCK = 256
MOBA_TOPK = 3
MOBA_UNROLL = 2
C_KV_HEADS = 3
C_GROUP = 4
C_HEADS = 12
CMP_STRIDE = 16
CMP_LEN = 32
CMP_HIDDEN = 128
SLC_BLOCK = 64
SLC_TOPN = 16
N_FORCED = 3
IMP_SPLITS = 3
WIN = 512
FORCE_SCORE = 1e9
N_BUCKETS = 32
T5_MAX_DIST = 2048
EPS = 1e-6
NEG = -1e30
M_FLOOR = -1e20
Q_SCALE = HEAD_DIM ** -0.5
C_CHUNK = 128
NSA_KT = 256
NSA_UNROLL = 2

A_WIDTH = 3 * A_HEADS * HEAD_DIM
A_GROUP_WIDTH = 3 * A_HEADS_PER_GROUP * HEAD_DIM
B_WIDTH = 3 * B_HEADS * HEAD_DIM
C_Q_WIDTH = C_HEADS * HEAD_DIM
C_KV_WIDTH = 6 * C_KV_HEADS * HEAD_DIM
C_GATE_WIDTH = 3 * C_HEADS
IN_WIDTH = A_WIDTH + B_WIDTH + C_Q_WIDTH + C_KV_WIDTH + C_GATE_WIDTH
BLK_A = 0
BLK_B = A_WIDTH // LANES
BLK_CQ = BLK_B + B_WIDTH // LANES
BLK_CKV = BLK_CQ + C_Q_WIDTH // LANES
BLK_GATE = BLK_CKV + 3 * C_KV_HEADS
N_BLK = BLK_GATE + 1
P_WIDTH = N_BLK * LANES

VMEM_LIMIT = 56 * 1024 * 1024


def _cparams(*sem):
    return pltpu.CompilerParams(dimension_semantics=sem, vmem_limit_bytes=VMEM_LIMIT)


def _dot_nt(a, b):
    return lax.dot_general(a, b, (((1,), (1,)), ((), ())), preferred_element_type=F32)


def _dot(a, b, **kw):
    return jnp.dot(a, b, preferred_element_type=F32, **kw)


def _iota(shape, dim):
    return lax.broadcasted_iota(jnp.int32, shape, dim)


def _t5_bucket(dist):
    n = jnp.maximum(dist, 0)
    max_exact = N_BUCKETS // 2
    nf = jnp.maximum(n, 1).astype(F32)
    large = max_exact + (jnp.log(nf / max_exact) / math.log(T5_MAX_DIST / max_exact)
                         * (N_BUCKETS - max_exact)).astype(jnp.int32)
    large = jnp.minimum(large, N_BUCKETS - 1)
    return jnp.where(n < max_exact, n, large)


def _bias_lookup_kernel(tbl_ref, idx_ref, o_ref):
    h = pl.program_id(1)
    idx = idx_ref[0]
    acc = jnp.zeros(idx.shape, F32)
    for b in range(N_BUCKETS):
        acc = jnp.where(idx == b, tbl_ref[h, b], acc)
    o_ref[0, 0] = acc


def _bias_tiles(tbl, idx, heads_per_row=1):
    n, R, C = idx.shape
    H = tbl.shape[0]
    hpr = heads_per_row
    return pl.pallas_call(
        _bias_lookup_kernel,
        grid=(n, H),
        in_specs=[pl.BlockSpec(memory_space=pltpu.SMEM),
                  pl.BlockSpec((1, R, C), lambda t, h: (t, 0, 0))],
        out_specs=pl.BlockSpec((1, 1, R, C), lambda t, h: (t, h // hpr, 0, h % hpr)),
        out_shape=jax.ShapeDtypeStruct((n, H // hpr, R, hpr * C), F32),
        compiler_params=_cparams("arbitrary", "arbitrary"),
        name="bias_tiles",
    )(tbl, idx)


def _toeplitz_bucket_tiles_t(n_tiles, size):
    d = jnp.arange(n_tiles)[:, None, None] * size
    key = jnp.arange(size)[None, :, None]
    qry = jnp.arange(size)[None, None, :]
    return _t5_bucket(d + qry - key).astype(jnp.int32)


def _n_far_tiles(n_chunks, size):
    dmax = -(-(T5_MAX_DIST + size - 1) // size)
    dmax = min(dmax, n_chunks - 1)
    return dmax


def _norm_rows(x, g):
    ms = jnp.mean(x * x, axis=-1, keepdims=True)
    return x * lax.rsqrt(ms + EPS) * g


def _inproj_kernel(x_ref, g_ref, w_ref, o_ref, h_scr):
    @pl.when(pl.program_id(1) == 0)
    def _():
        h_scr[...] = _norm_rows(x_ref[...], g_ref[...]).astype(BF16)

    o_ref[...] = _dot(h_scr[...], w_ref[...]).astype(o_ref.dtype)


def _inproj(x2, gain, w, tm=1024, tn=256):
    T, D = x2.shape
    N = w.shape[1]
    return pl.pallas_call(
        _inproj_kernel,
        grid=(T // tm, N // tn),
        in_specs=[pl.BlockSpec((tm, D), lambda i, j: (i, 0)),
                  pl.BlockSpec((1, D), lambda i, j: (0, 0)),
                  pl.BlockSpec((D, tn), lambda i, j: (0, j))],
        out_specs=pl.BlockSpec((tm, tn), lambda i, j: (i, j)),
        out_shape=jax.ShapeDtypeStruct((T, N), BF16),
        scratch_shapes=[pltpu.VMEM((tm, D), BF16)],
        compiler_params=_cparams("arbitrary", "arbitrary"),
        name="inproj",
    )(x2, gain.reshape(1, D), w)


def _outproj_kernel(x_ref, a_ref, b_ref, c_ref, wa_ref, wb_ref, wc_ref, o_ref):
    acc = _dot(a_ref[...], wa_ref[...])
    acc += _dot(b_ref[...], wb_ref[...])
    acc += _dot(c_ref[...], wc_ref[...])
    o_ref[...] = x_ref[...] + acc


def _outproj(x2, oa, ob, oc, wa, wb, wc, tm=1024, tn=512):
    T, D = x2.shape
    return pl.pallas_call(
        _outproj_kernel,
        grid=(T // tm, D // tn),
        in_specs=[pl.BlockSpec((tm, tn), lambda i, j: (i, j)),
                  pl.BlockSpec((tm, oa.shape[1]), lambda i, j: (i, 0)),
                  pl.BlockSpec((tm, ob.shape[1]), lambda i, j: (i, 0)),
                  pl.BlockSpec((tm, oc.shape[1]), lambda i, j: (i, 0)),
                  pl.BlockSpec((wa.shape[0], tn), lambda i, j: (0, j)),
                  pl.BlockSpec((wb.shape[0], tn), lambda i, j: (0, j)),
                  pl.BlockSpec((wc.shape[0], tn), lambda i, j: (0, j))],
        out_specs=pl.BlockSpec((tm, tn), lambda i, j: (i, j)),
        out_shape=jax.ShapeDtypeStruct((T, D), F32),
        compiler_params=_cparams("arbitrary", "arbitrary"),
        name="outproj",
    )(x2, oa, ob, oc, wa, wb, wc)


FFN_HALO = 8


def _ffn_kernel(x_ref, halo_ref, g_ref, wa_ref, wg_ref, cwa_ref, cwg_ref, cba_ref, cbg_ref, wd_ref,
                o_ref, h_scr, ua_scr, ug_scr, *, tm, blocks_per_seq):
    i = pl.program_id(0)
    f = pl.program_id(1)

    @pl.when(f == 0)
    def _():
        x = x_ref[...]
        h_scr[pl.ds(FFN_HALO, tm), :] = _norm_rows(x, g_ref[...]).astype(BF16)
        keep = jnp.where(i % blocks_per_seq == 0, 0.0, 1.0)
        h_scr[pl.ds(0, FFN_HALO), :] = (_norm_rows(halo_ref[0], g_ref[...]) * keep).astype(BF16)
        o_ref[...] = x

    h = h_scr[...]
    ua_scr[...] = _dot(h, wa_ref[...])
    ug_scr[...] = _dot(h, wg_ref[...])

    def conv(u_scr, cw_ref, cb_ref):
        acc = cb_ref[...]
        for j in range(3):
            acc = acc + cw_ref[pl.ds(j, 1), :] * u_scr[pl.ds(FFN_HALO - j, tm), :]
        return acc

    ya = conv(ua_scr, cwa_ref, cba_ref)
    yg = conv(ug_scr, cwg_ref, cbg_ref)
    act = (yg * jax.nn.sigmoid(yg) * ya).astype(BF16)
    o_ref[...] += _dot(act, wd_ref[...])


def _ffn(x2, gain, w_up, conv_w, conv_b, w_down, seq, tm=512, tf=512):
    T, D = x2.shape
    Fh = w_down.shape[0]
    nf = Fh // tf
    halo_view = x2.reshape(T // FFN_HALO, FFN_HALO, D)
    rb = tm // FFN_HALO
    kern = functools.partial(_ffn_kernel, tm=tm, blocks_per_seq=seq // tm)
    return pl.pallas_call(
        kern,
        grid=(T // tm, nf),
        in_specs=[pl.BlockSpec((tm, D), lambda i, f: (i, 0)),
                  pl.BlockSpec((1, FFN_HALO, D), lambda i, f: (jnp.maximum(i * rb - 1, 0), 0, 0)),
                  pl.BlockSpec((1, D), lambda i, f: (0, 0)),
                  pl.BlockSpec((D, tf), lambda i, f: (0, f)),
                  pl.BlockSpec((D, tf), lambda i, f: (0, f + nf)),
                  pl.BlockSpec((3, tf), lambda i, f: (0, f)),
                  pl.BlockSpec((3, tf), lambda i, f: (0, f + nf)),
                  pl.BlockSpec((1, tf), lambda i, f: (0, f)),
                  pl.BlockSpec((1, tf), lambda i, f: (0, f + nf)),
                  pl.BlockSpec((tf, D), lambda i, f: (f, 0))],
        out_specs=pl.BlockSpec((tm, D), lambda i, f: (i, 0)),
        out_shape=jax.ShapeDtypeStruct((T, D), F32),
        scratch_shapes=[pltpu.VMEM((tm + FFN_HALO, D), BF16),
                        pltpu.VMEM((tm + FFN_HALO, tf), F32),
                        pltpu.VMEM((tm + FFN_HALO, tf), F32)],
        compiler_params=_cparams("arbitrary", "arbitrary"),
        name="convffn",
    )(x2, halo_view, gain.reshape(1, D), w_up, w_up, conv_w, conv_w,
      conv_b.reshape(1, -1), conv_b.reshape(1, -1), w_down)


def _final_norm_kernel(x_ref, g_ref, o_ref):
    o_ref[...] = _norm_rows(x_ref[...], g_ref[...])


def _final_norm(x2, gain, tm=1024):
    T, D = x2.shape
    return pl.pallas_call(
        _final_norm_kernel,
        grid=(T // tm,),
        in_specs=[pl.BlockSpec((tm, D), lambda i: (i, 0)),
                  pl.BlockSpec((1, D), lambda i: (0, 0))],
        out_specs=pl.BlockSpec((tm, D), lambda i: (i, 0)),
        out_shape=jax.ShapeDtypeStruct((T, D), F32),
        compiler_params=_cparams("arbitrary"),
        name="final_norm",
    )(x2, gain.reshape(1, D))


def _stack_pair(qp):
    lane = _iota(qp.shape, 1)
    zero = jnp.zeros_like(qp)
    return jnp.concatenate([jnp.where(lane < HEAD_DIM, qp, zero),
                            jnp.where(lane >= HEAD_DIM, qp, zero)], axis=0)


def _unstack_pair(o):
    R = o.shape[0] // 2
    lane = _iota((R, LANES), 1)
    return jnp.where(lane < HEAD_DIM, o[:R], o[R:])


def _dilated_kernel(q_ref, kc_ref, kp_ref, vc_ref, vp_ref, bias_ref, o_ref, lse_ref):
    i = pl.program_id(3)
    qs = _stack_pair(q_ref[0])
    k = jnp.concatenate([kp_ref[0], kc_ref[0]], axis=0)
    v = jnp.concatenate([vp_ref[0], vc_ref[0]], axis=0)
    R = 2 * A_BLOCK
    s = _dot_nt(qs, k) + bias_ref[0].reshape(R, 2 * A_BLOCK)
    a = _iota((R, 2 * A_BLOCK), 0) % A_BLOCK
    bk = _iota((R, 2 * A_BLOCK), 1)
    rel = a + A_BLOCK - bk
    mask = (rel >= 0) & (rel <= A_BLOCK) & ((bk >= A_BLOCK) | (i > 0))
    s = jnp.where(mask, s, NEG)
    m = jnp.max(s, axis=1, keepdims=True)
    e = jnp.where(mask, jnp.exp(s - m), 0.0)
    den = jnp.sum(e, axis=1, keepdims=True)
    o = _dot(e.astype(BF16), v) / den
    lse = m + jnp.log(den)
    o_ref[0] = _unstack_pair(o)
    lse_ref[0] = _unstack_pair(jnp.broadcast_to(lse, (R, LANES)))


def _dilated_group(proj, bias_a, g, dilation):
    B, S, _ = proj.shape
    L = S // dilation
    nq = L // A_BLOCK
    nblk = A_GROUP_WIDTH // LANES
    cols = proj[:, :, g * A_GROUP_WIDTH:(g + 1) * A_GROUP_WIDTH]
    view = cols.reshape(B, L, dilation * A_GROUP_WIDTH)

    def cur(base):
        return pl.BlockSpec((1, A_BLOCK, LANES), lambda b, r, p, i: (b, i, r * nblk + base + p))

    def prev(base):
        return pl.BlockSpec((1, A_BLOCK, LANES),
                            lambda b, r, p, i: (b, jnp.maximum(i - 1, 0), r * nblk + base + p))

    out_spec = pl.BlockSpec((1, A_BLOCK, LANES), lambda b, r, p, i: (b, i, r * 2 + p))
    o, lse = pl.pallas_call(
        _dilated_kernel,
        grid=(B, dilation, 2, nq),
        in_specs=[cur(0), cur(2), prev(2), cur(4), prev(4),
                  pl.BlockSpec((1, 2, A_BLOCK, 2 * A_BLOCK), lambda b, r, p, i: (g, 2 * g + p, 0, 0))],
        out_specs=[out_spec, out_spec],
        out_shape=[jax.ShapeDtypeStruct((B, L, dilation * 2 * LANES), F32)] * 2,
        compiler_params=_cparams("arbitrary", "arbitrary", "arbitrary", "arbitrary"),
        name=f"dilated_g{g}",
    )(view, view, view, view, view, bias_a)
    return o.reshape(B * S, 2 * LANES), lse.reshape(B * S, 2 * LANES)


def _combine_kernel(o0, o1, o2, l0, l1, l2, out_ref):
    ls = [l0[...], l1[...], l2[...]]
    m = jnp.maximum(jnp.maximum(ls[0], ls[1]), ls[2])
    es = [jnp.exp(l - m) for l in ls]
    den = es[0] + es[1] + es[2]
    W = o0.shape[1]
    for g, o in enumerate((o0, o1, o2)):
        out_ref[:, g * W:(g + 1) * W] = (o[...] * (es[g] / den)).astype(out_ref.dtype)


def _combine_groups(os_, lses, tm=1024):
    T, W = os_[0].shape
    spec = pl.BlockSpec((tm, W), lambda i: (i, 0))
    return pl.pallas_call(
        _combine_kernel,
        grid=(T // tm,),
        in_specs=[spec] * 6,
        out_specs=pl.BlockSpec((tm, 3 * W), lambda i: (i, 0)),
        out_shape=jax.ShapeDtypeStruct((T, 3 * W), BF16),
        compiler_params=_cparams("arbitrary"),
        name="dilated_combine",
    )(*os_, *lses)


def _flash_init(m_scr, l_scr, acc_scrs):
    m_scr[...] = jnp.full(m_scr.shape, M_FLOOR, F32)
    l_scr[...] = jnp.zeros(l_scr.shape, F32)
    for acc in acc_scrs:
        acc[...] = jnp.zeros(acc.shape, F32)


def _flash_tiles(tiles, q_groups, m_scr, l_scr, acc_scrs):
    G = len(q_groups)
    m = [m_scr[:, g * LANES:(g + 1) * LANES] for g in range(G)]
    l = [l_scr[:, g * LANES:(g + 1) * LANES] for g in range(G)]
    units = [(t, g) for t in range(len(tiles)) for g in range(G)]

    def scores(t, g):
        k, _, terms_of = tiles[t]
        s = _dot_nt(k, q_groups[g])
        for term in terms_of(g):
            s = s + term
        return s

    def apply(g, alpha, pv):
        acc_scrs[g][...] = alpha * acc_scrs[g][...] + pv

    s_next = scores(*units[0])
    pending = None
    for n, (t, g) in enumerate(units):
        s = s_next
        if n + 1 < len(units):
            s_next = scores(*units[n + 1])
        m_new = jnp.maximum(m[g], jnp.max(s, axis=0, keepdims=True))
        alpha = jnp.exp(m[g] - m_new)
        p = jnp.exp(s - m_new)
        l[g] = alpha * l[g] + jnp.sum(p, axis=0, keepdims=True)
        m[g] = m_new
        pv = _dot(tiles[t][1], p.astype(BF16))
        if pending is not None:
            apply(*pending)
        pending = (g, alpha, pv)
    apply(*pending)
    m_scr[...] = jnp.concatenate(m, axis=1)
    l_scr[...] = jnp.concatenate(l, axis=1)


def _top_picks(score, n_pick):
    idx = _iota(score.shape, 0).astype(F32)

    def pick(_, carry):
        g, sel = carry
        mx = jnp.max(g, axis=0, keepdims=True)
        first = jnp.min(jnp.where(g == mx, idx, 1e9), axis=0, keepdims=True)
        pk = idx == first
        return jnp.where(pk, -3e38, g), jnp.where(pk, 1.0, sel)

    _, sel = lax.fori_loop(0, n_pick, pick, (score, jnp.zeros(score.shape, F32)))
    return sel


def _moba_kernel(q_ref, k_ref, vt_ref, bias_ref, o_ref, km_scr, sel_scr, m_scr, l_scr, *acc_scrs,
                 dmax, n_blocks):
    c = pl.program_id(2)
    BLK = MOBA_BLOCK
    Q = 2 * BLK

    @pl.when(c == 0)
    def _():
        def body(j, _):
            kb = k_ref[0, pl.ds(pl.multiple_of(j * BLK, BLK), BLK), :].astype(F32)
            km_scr[pl.ds(j, 1), :] = jnp.mean(kb, axis=0, keepdims=True)
            return 0
        lax.fori_loop(0, n_blocks, body, 0)

    qs = _stack_pair(q_ref[0])
    gate = _dot_nt(km_scr[...].astype(BF16), qs)
    blk = _iota(gate.shape, 0)
    gate = jnp.where(blk < c, gate, NEG)
    sel = _top_picks(gate, min(MOBA_TOPK, n_blocks))
    sel_scr[...] = jnp.where((blk < c) & (sel > 0.5), 0.0, NEG)

    _flash_init(m_scr, l_scr, acc_scrs)
    q_groups = [qs[g * LANES:(g + 1) * LANES] for g in range(Q // LANES)]

    def past_tile(j):
        jc = jnp.minimum(j, n_blocks - 1)
        kj = k_ref[0, pl.ds(pl.multiple_of(jc * BLK, BLK), BLK), :]
        tile = jnp.clip(c - j, 0, dmax)
        selrow = sel_scr[pl.ds(jc, 1), :]

        def terms(g):
            return bias_ref[tile, 0, :, pl.ds(g * LANES, LANES)], selrow[:, g * LANES:(g + 1) * LANES]

        return kj, vt_ref[0, jc], terms

    def body(it, _):
        _flash_tiles([past_tile(MOBA_UNROLL * it + u) for u in range(MOBA_UNROLL)],
                     q_groups, m_scr, l_scr, acc_scrs)
        return 0

    lax.fori_loop(0, (c + MOBA_UNROLL - 1) // MOBA_UNROLL, body, 0)

    ko = k_ref[0, pl.ds(pl.multiple_of(c * BLK, BLK), BLK), :]

    def own_terms(g):
        qpos = (g * LANES) % BLK + _iota((BLK, LANES), 1)
        return (bias_ref[0, 0, :, pl.ds(g * LANES, LANES)],
                jnp.where(_iota((BLK, LANES), 0) <= qpos, 0.0, NEG))

    _flash_tiles([(ko, vt_ref[0, c], own_terms)], q_groups, m_scr, l_scr, acc_scrs)
    ot = jnp.concatenate([acc[...] for acc in acc_scrs], axis=1) / l_scr[...]
    o2 = jnp.concatenate([ot[:HEAD_DIM, :BLK], ot[HEAD_DIM:, BLK:]], axis=0)
    o_ref[0] = o2.T.astype(o_ref.dtype)


def _moba(proj, vt, bias_b, dmax):
    B, S, _ = proj.shape
    NB = S // MOBA_BLOCK
    n_tiles = bias_b.shape[0]
    qb, kb = BLK_B, BLK_B + 4
    kern = functools.partial(_moba_kernel, dmax=dmax, n_blocks=NB)
    Q = 2 * MOBA_BLOCK
    return pl.pallas_call(
        kern,
        grid=(B, B_HEADS // 2, NB),
        in_specs=[pl.BlockSpec((1, MOBA_BLOCK, LANES), lambda b, p, c: (b, c, qb + p)),
                  pl.BlockSpec((1, S, LANES), lambda b, p, c: (b, 0, kb + p)),
                  pl.BlockSpec((1, NB, LANES, MOBA_BLOCK), lambda b, p, c: (b, 0, p, 0)),
                  pl.BlockSpec((n_tiles, 1, MOBA_BLOCK, Q), lambda b, p, c: (0, p, 0, 0))],
        out_specs=pl.BlockSpec((1, MOBA_BLOCK, LANES), lambda b, p, c: (b, c, p)),
        out_shape=jax.ShapeDtypeStruct((B, S, B_HEADS * HEAD_DIM), BF16),
        scratch_shapes=[pltpu.VMEM((NB, LANES), F32), pltpu.VMEM((NB, Q), F32),
                        pltpu.VMEM((1, Q), F32), pltpu.VMEM((1, Q), F32)]
        + [pltpu.VMEM((LANES, LANES), F32)] * (Q // LANES),
        compiler_params=_cparams("arbitrary", "arbitrary", "arbitrary"),
        name="moba",
    )(proj, proj, vt, bias_b.reshape(n_tiles, B_HEADS // 2, MOBA_BLOCK, Q))


def _gelu_tanh(x):
    return 0.5 * x * (1.0 + jnp.tanh(math.sqrt(2.0 / math.pi) * (x + 0.044715 * (x * x * x))))


def _compress_kernel(sub_ref, pe_ref, w1_ref, w2_ref, o_ref, ot_ref, v_scr):
    n16 = sub_ref.shape[2]
    sub = sub_ref[0, 0].astype(F32)
    top = (sub + pe_ref[0:1, :]).astype(BF16)
    bot = (sub + pe_ref[1:2, :]).astype(BF16)
    u = _dot(top, w1_ref[0])
    v_scr[pl.ds(0, n16), :] = _dot(bot, w1_ref[1])
    v_scr[pl.ds(n16, 8), :] = jnp.zeros((8, v_scr.shape[1]), F32)
    hidden = u + v_scr[pl.ds(1, n16), :]
    out = _dot(_gelu_tanh(hidden).astype(BF16), w2_ref[...])
    o_ref[0, 0] = out.astype(o_ref.dtype)
    ot_ref[0, 0] = out.T.astype(ot_ref.dtype)


def _compress(sub, pe_rows, w1_blk, w2_blk):
    B, KV, n16, W = sub.shape
    return pl.pallas_call(
        _compress_kernel,
        grid=(B, KV),
        in_specs=[pl.BlockSpec((1, 1, n16, W), lambda b, n: (b, n, 0, 0)),
                  pl.BlockSpec((2, W), lambda b, n: (0, 0)),
                  pl.BlockSpec((2, W, 2 * CMP_HIDDEN), lambda b, n: (0, 0, 0)),
                  pl.BlockSpec((2 * CMP_HIDDEN, LANES), lambda b, n: (0, 0))],
        out_specs=[pl.BlockSpec((1, 1, n16, LANES), lambda b, n: (b, n, 0, 0)),
                   pl.BlockSpec((1, 1, LANES, n16), lambda b, n: (b, n, 0, 0))],
        out_shape=[jax.ShapeDtypeStruct((B, KV, n16, LANES), BF16),
                   jax.ShapeDtypeStruct((B, KV, LANES, n16), BF16)],
        scratch_shapes=[pltpu.VMEM((n16 + 8, 2 * CMP_HIDDEN), F32)],
        compiler_params=_cparams("arbitrary", "arbitrary"),
        name="nsa_compress",
    )(sub, pe_rows, w1_blk, w2_blk)


def _nsa_kernel(q_ref, kc_ref, kct_ref, slc_ref, slct_ref, win_ref, wint_ref, gate_ref, bias_ref, ovt_ref,
                o_ref, sel_scr, g_scr, m_slc, l_slc, m_win, l_win, *acc_scrs, dmax, n_sel):
    n = pl.program_id(1)
    c = pl.program_id(2)
    CQ = C_CHUNK
    Q = C_GROUP * CQ
    lane = _iota((CQ, LANES), 1)
    q = q_ref[0].astype(F32)
    heads = []
    for pr in range(2):
        qp = q[:, pr * LANES:(pr + 1) * LANES]
        heads.append(jnp.where(lane < HEAD_DIM, qp, 0.0))
        heads.append(jnp.where(lane < HEAD_DIM, pltpu.roll(qp, HEAD_DIM, 1), 0.0))
    q_groups = [h.astype(BF16) for h in heads]

    kc = kc_ref[0, 0]
    kct = kct_ref[0, 0]
    NC = kc.shape[0]
    visible = (_iota((NC, CQ), 0) * CMP_STRIDE + (CMP_LEN - 1)) <= c * CQ + _iota((NC, CQ), 1)
    o_cmp = []
    psum = jnp.zeros((NC, CQ), F32)
    s_next = _dot_nt(kc, q_groups[0])
    for gi in range(C_GROUP):
        s = jnp.where(visible, s_next, NEG)
        if gi + 1 < C_GROUP:
            s_next = _dot_nt(kc, q_groups[gi + 1])
        m = jnp.max(s, axis=0, keepdims=True)
        e = jnp.where(visible, jnp.exp(s - m), 0.0)
        den = jnp.sum(e, axis=0, keepdims=True)
        p = e / jnp.maximum(den, 1e-30)
        o_cmp.append(_dot(kct, p.astype(BF16)))
        psum = psum + p

    ovt = ovt_ref[...]
    imp = jnp.zeros((ovt.shape[0], CQ), F32)
    rest = psum
    for _ in range(IMP_SPLITS):
        piece = rest.astype(BF16)
        imp = imp + _dot(ovt, piece)
        rest = rest - piece.astype(F32)
    NS = imp.shape[0]
    jj = _iota((NS, CQ), 0)
    own = (c * CQ + _iota((NS, CQ), 1)) // SLC_BLOCK
    forced = (jj == 0) | (jj == own) | (jj == own - 1)
    score = jnp.where(forced, -3e38, jnp.where(jj <= own, imp, -1.0))
    picked = _top_picks(score, max(n_sel - N_FORCED, 0))
    sel_scr[...] = jnp.where(forced, 1.0, picked)

    key_row = _iota((LANES, CQ), 0)
    q_col = _iota((LANES, CQ), 1)
    per_tile = NSA_KT // SLC_BLOCK
    n_kt = slct_ref.shape[2]
    acc_slc = acc_scrs[:C_GROUP]
    acc_win = acc_scrs[C_GROUP:]

    _flash_init(m_slc, l_slc, acc_slc)

    key_row2 = _iota((NSA_KT, CQ), 0)
    q_col2 = _iota((NSA_KT, CQ), 1)
    halves = NSA_KT // LANES

    def slc_tile(kt):
        ktc = jnp.minimum(kt, n_kt - 1)
        k = slc_ref[0, pl.ds(pl.multiple_of(ktc * NSA_KT, NSA_KT), NSA_KT), :]
        r = sel_scr[pl.ds(pl.multiple_of(per_tile * ktc, per_tile), per_tile), :]
        chosen = jnp.concatenate([jnp.broadcast_to(r[i:i + 1, :], (SLC_BLOCK, CQ)) for i in range(per_tile)], axis=0)
        ok = (chosen > 0.5) & (kt * NSA_KT + key_row2 <= c * CQ + q_col2)
        negm = jnp.where(ok, 0.0, NEG)
        delta = c - halves * kt

        def terms(g):
            cols = pl.ds(g * CQ, CQ)
            bias = jnp.concatenate([bias_ref[jnp.clip(delta - h, 0, dmax), 0, :, cols] for h in range(halves)],
                                   axis=0)
            return bias, negm

        return k, slct_ref[0, 0, ktc], terms

    def slc_body(it, _):
        _flash_tiles([slc_tile(NSA_UNROLL * it + u) for u in range(NSA_UNROLL)], q_groups, m_slc, l_slc, acc_slc)
        return 0

    keys_per_iter = NSA_UNROLL * NSA_KT
    lax.fori_loop(0, ((c + 1) * CQ + keys_per_iter - 1) // keys_per_iter, slc_body, 0)

    _flash_init(m_win, l_win, acc_win)
    n_wt = wint_ref.shape[2]
    first = WIN // LANES

    def win_tile(w):
        ks, vts, negs, dds = [], [], [], []
        for h in range(halves):
            dd = first - halves * w - h
            kt = c - dd
            ktc = jnp.clip(kt, 0, n_wt - 1)
            ks.append(win_ref[0, pl.ds(pl.multiple_of(ktc * LANES, LANES), LANES), :])
            vts.append(wint_ref[0, 0, ktc])
            dist = dd * LANES + q_col - key_row
            negs.append(jnp.where((dist >= 0) & (dist < WIN) & (kt >= 0), 0.0, NEG))
            dds.append(max(dd, 0))
        negm = jnp.concatenate(negs, axis=0)

        def terms(g):
            cols = pl.ds(g * CQ, CQ)
            return jnp.concatenate([bias_ref[dd, 0, :, cols] for dd in dds], axis=0), negm

        return jnp.concatenate(ks, axis=0), jnp.concatenate(vts, axis=1), terms

    n_win_tiles = -(-(WIN + CQ) // NSA_KT)
    _flash_tiles([win_tile(w) for w in range(n_win_tiles)], q_groups, m_win, l_win, acc_win)

    g_scr[...] = jax.nn.sigmoid(gate_ref[0].astype(F32)).T
    l_s = l_slc[...]
    l_w = l_win[...]
    res = []
    for gi in range(C_GROUP):
        cols = slice(gi * CQ, (gi + 1) * CQ)
        col0 = (n * C_GROUP + gi) * 3
        tot = (g_scr[pl.ds(col0, 1), :] * o_cmp[gi][HEAD_DIM:, :]
               + g_scr[pl.ds(col0 + 1, 1), :] * (acc_slc[gi][pl.ds(HEAD_DIM, HEAD_DIM), :] / l_s[:, cols])
               + g_scr[pl.ds(col0 + 2, 1), :] * (acc_win[gi][pl.ds(HEAD_DIM, HEAD_DIM), :] / l_w[:, cols]))
        res.append(tot)
    o_ref[0] = jnp.concatenate(res, axis=0).T.astype(o_ref.dtype)


def _nsa(proj, kcvc, kcvct, slct, wint, bias_c, overlap_t, dmax):
    B, S, _ = proj.shape
    n_chunks = S // C_CHUNK
    NC = kcvc.shape[2]
    NS = S // SLC_BLOCK
    n_tiles = bias_c.shape[0]
    Q = C_GROUP * C_CHUNK
    kern = functools.partial(_nsa_kernel, dmax=dmax, n_sel=min(SLC_TOPN, NS))
    slct_spec = pl.BlockSpec((1, 1, S // NSA_KT, LANES, NSA_KT), lambda b, n, c: (b, n, 0, 0, 0))
    wint_spec = pl.BlockSpec((1, 1, n_chunks, LANES, LANES), lambda b, n, c: (b, n, 0, 0, 0))
    return pl.pallas_call(
        kern,
        grid=(B, C_KV_HEADS, n_chunks),
        in_specs=[pl.BlockSpec((1, C_CHUNK, 2 * LANES), lambda b, n, c: (b, c, BLK_CQ // 2 + n)),
                  pl.BlockSpec((1, 1, NC, LANES), lambda b, n, c: (b, n, 0, 0)),
                  pl.BlockSpec((1, 1, LANES, NC), lambda b, n, c: (b, n, 0, 0)),
                  pl.BlockSpec((1, S, LANES), lambda b, n, c: (b, 0, BLK_CKV + 3 * n + 1)),
                  slct_spec,
                  pl.BlockSpec((1, S, LANES), lambda b, n, c: (b, 0, BLK_CKV + 3 * n + 2)),
                  wint_spec,
                  pl.BlockSpec((1, C_CHUNK, LANES), lambda b, n, c: (b, c, BLK_GATE)),
                  pl.BlockSpec((n_tiles, 1, C_CHUNK, Q), lambda b, n, c: (0, n, 0, 0)),
                  pl.BlockSpec((NS, NC), lambda b, n, c: (0, 0))],
        out_specs=pl.BlockSpec((1, C_CHUNK, 2 * LANES), lambda b, n, c: (b, c, n)),
        out_shape=jax.ShapeDtypeStruct((B, S, C_HEADS * HEAD_DIM), BF16),
        scratch_shapes=[pltpu.VMEM((NS, C_CHUNK), F32), pltpu.VMEM((LANES, C_CHUNK), F32),
                        pltpu.VMEM((1, Q), F32), pltpu.VMEM((1, Q), F32),
                        pltpu.VMEM((1, Q), F32), pltpu.VMEM((1, Q), F32)]
        + [pltpu.VMEM((LANES, C_CHUNK), F32)] * (2 * C_GROUP),
        compiler_params=_cparams("arbitrary", "arbitrary", "arbitrary"),
        name="nsa",
    )(proj, kcvc, kcvct, proj, slct, proj, wint, proj, bias_c, overlap_t)


def _inproj_perm():
    perm = list(range(A_WIDTH + B_WIDTH + C_Q_WIDTH))
    kv0 = A_WIDTH + B_WIDTH + C_Q_WIDTH
    for n in range(C_KV_HEADS):
        for br in range(3):
            for t in (2 * br, 2 * br + 1):
                base = kv0 + t * C_KV_HEADS * HEAD_DIM + n * HEAD_DIM
                perm.extend(range(base, base + HEAD_DIM))
    g0 = kv0 + C_KV_WIDTH
    perm.extend(range(g0, g0 + C_GATE_WIDTH))
    perm.extend([IN_WIDTH] * (P_WIDTH - len(perm)))
    scale = np.ones((P_WIDTH,), np.float32)
    for g in range(len(A_CONFIGS)):
        scale[g * A_GROUP_WIDTH:g * A_GROUP_WIDTH + A_HEADS_PER_GROUP * HEAD_DIM] = Q_SCALE
    scale[A_WIDTH:A_WIDTH + B_HEADS * HEAD_DIM] = Q_SCALE
    scale[A_WIDTH + B_WIDTH:A_WIDTH + B_WIDTH + C_Q_WIDTH] = Q_SCALE
    return np.asarray(perm, np.int32), scale


def _compress_weights(cmp_w1, cmp_w2, cmp_pe):
    half = CMP_LEN // 2
    w1 = cmp_w1.reshape(2, 2, half, HEAD_DIM, CMP_HIDDEN)
    w1_blk = jnp.zeros((2, half, 2, HEAD_DIM, 2, CMP_HIDDEN), F32)
    w1_blk = w1_blk.at[:, :, 0, :, 0, :].set(w1[0])
    w1_blk = w1_blk.at[:, :, 1, :, 1, :].set(w1[1])
    w1_blk = w1_blk.reshape(2, half * 2 * HEAD_DIM, 2 * CMP_HIDDEN).astype(BF16)
    w2_blk = jnp.zeros((2, CMP_HIDDEN, 2, HEAD_DIM), F32)
    w2_blk = w2_blk.at[0, :, 0, :].set(cmp_w2[0])
    w2_blk = w2_blk.at[1, :, 1, :].set(cmp_w2[1])
    w2_blk = w2_blk.reshape(2 * CMP_HIDDEN, 2 * HEAD_DIM).astype(BF16)
    pe = cmp_pe.reshape(2, 2, half, HEAD_DIM)
    pe_rows = pe.transpose(1, 2, 0, 3).reshape(2, half * 2 * HEAD_DIM)
    return w1_blk, w2_blk, pe_rows


def kernel(x, rel_table, w_in, w_out, cmp_w1, cmp_w2, cmp_pe, norm_attn, norm_mlp,
           w_up, conv_w, conv_b, w_down, norm_final):
    B, S, D = x.shape
    depth = w_in.shape[0]
    T = B * S
    NB = S // MOBA_BLOCK
    n_kt = S // LANES

    a = jnp.arange(A_BLOCK)[:, None]
    bk = jnp.arange(2 * A_BLOCK)[None, :]
    rel = a + A_BLOCK - bk
    idx_a = jnp.stack([_t5_bucket(rel * d) for _, d in A_CONFIGS]).astype(jnp.int32)
    bias_a = _bias_tiles(rel_table[:A_HEADS], idx_a)
    dmax_b = _n_far_tiles(NB, MOBA_BLOCK)
    bias_b = _bias_tiles(rel_table[A_HEADS:A_HEADS + B_HEADS],
                         _toeplitz_bucket_tiles_t(dmax_b + 1, MOBA_BLOCK), 2)
    dmax_c = _n_far_tiles(S // C_CHUNK, C_CHUNK)
    bias_c = _bias_tiles(rel_table[A_HEADS + B_HEADS:],
                         _toeplitz_bucket_tiles_t(dmax_c + 1, C_CHUNK), C_GROUP)

    n16 = S // CMP_STRIDE
    ci = jnp.arange(n16)[None, :] * CMP_STRIDE
    sj = jnp.arange(S // SLC_BLOCK)[:, None] * SLC_BLOCK
    overlap_t = ((ci < sj + SLC_BLOCK) & (ci + CMP_LEN > sj)).astype(BF16)

    perm, scale = _inproj_perm()
    w_in_z = jnp.concatenate([w_in, jnp.zeros((depth, D, 1), w_in.dtype)], axis=2)
    w_in_p = (jnp.take(w_in_z, perm, axis=2) * scale).astype(BF16)
    w_out_b = w_out.astype(BF16)
    w_up_b = w_up.astype(BF16)
    w_down_b = w_down.astype(BF16)
    wa_rows = A_HEADS * HEAD_DIM
    wb_rows = wa_rows + B_HEADS * HEAD_DIM

    x2 = x.reshape(T, D)
    for l in range(depth):
        proj = _inproj(x2, norm_attn[l], w_in_p[l]).reshape(B, S, P_WIDTH)
        os_, lses = [], []
        for g, (_, d) in enumerate(A_CONFIGS):
            o, lse = _dilated_group(proj, bias_a, g, d)
            os_.append(o)
            lses.append(lse)
        o_a = _combine_groups(os_, lses)
        vb = proj[:, :, (BLK_B + 8) * LANES:(BLK_B + 12) * LANES]
        vt_b = vb.reshape(B, NB, MOBA_BLOCK, B_HEADS * HEAD_DIM).transpose(0, 1, 3, 2)
        o_b = _moba(proj, vt_b, bias_b, dmax_b).reshape(T, B_HEADS * HEAD_DIM)
        w1_blk, w2_blk, pe_rows = _compress_weights(cmp_w1[l], cmp_w2[l], cmp_pe[l])
        ckv = proj[:, :, BLK_CKV * LANES:BLK_GATE * LANES]
        ckv5 = ckv.reshape(B, S, C_KV_HEADS, 3, LANES)
        sub = ckv5[:, :, :, 0].transpose(0, 2, 1, 3).reshape(B, C_KV_HEADS, n16, CMP_STRIDE * LANES)
        kcvc, kcvct = _compress(sub, pe_rows, w1_blk, w2_blk)
        slct = ckv.reshape(B, S // NSA_KT, NSA_KT, C_KV_HEADS, 3, LANES)[:, :, :, :, 1].transpose(0, 3, 1, 4, 2)
        wint = ckv.reshape(B, n_kt, LANES, C_KV_HEADS, 3, LANES)[:, :, :, :, 2].transpose(0, 3, 1, 4, 2)
        o_c = _nsa(proj, kcvc, kcvct, slct, wint, bias_c, overlap_t, dmax_c).reshape(T, C_HEADS * HEAD_DIM)
        x2 = _outproj(x2, o_a, o_b, o_c, w_out_b[l, :wa_rows], w_out_b[l, wa_rows:wb_rows],
                      w_out_b[l, wb_rows:])
        x2 = _ffn(x2, norm_mlp[l], w_up_b[l], conv_w[l], conv_b[l], w_down_b[l], S)
    return _final_norm(x2, norm_final).reshape(B, S, D)
```

```python
import functools
import math

import numpy as np
import jax
import jax.numpy as jnp
from jax import lax
from jax.experimental import pallas as pl
from jax.experimental.pallas import tpu as pltpu

F32 = jnp.float32
BF16 = jnp.bfloat16

HEAD_DIM = 64
LANES = 128
A_CONFIGS = ((128, 1), (512, 4), (2048, 16))
A_HEADS_PER_GROUP = 4
A_HEADS = 12
A_BLOCK = 128
B_HEADS = 8
MOBA_BLOCK = 256
MOBA_TOPK = 3
MOBA_UNROLL = 4
C_KV_HEADS = 3
C_GROUP = 4
C_HEADS = 12
CMP_STRIDE = 16
CMP_LEN = 32
CMP_HIDDEN = 128
SLC_BLOCK = 64
SLC_TOPN = 16
N_FORCED = 3
IMP_SPLITS = 3
WIN = 512
FORCE_SCORE = 1e9
N_BUCKETS = 32
T5_MAX_DIST = 2048
EPS = 1e-6
NEG = -1e30
M_FLOOR = -1e20
Q_SCALE = HEAD_DIM ** -0.5
LOG2E = math.log2(math.e)
C_CHUNK = 128
ACC_ROWS = 128
GROUP_W = 128
NSA_KT = 256
NSA_UNROLL = 4

A_WIDTH = 3 * A_HEADS * HEAD_DIM
A_GROUP_WIDTH = 3 * A_HEADS_PER_GROUP * HEAD_DIM
B_WIDTH = 3 * B_HEADS * HEAD_DIM
C_Q_WIDTH = C_HEADS * HEAD_DIM
C_KV_WIDTH = 6 * C_KV_HEADS * HEAD_DIM
C_GATE_WIDTH = 3 * C_HEADS
IN_WIDTH = A_WIDTH + B_WIDTH + C_Q_WIDTH + C_KV_WIDTH + C_GATE_WIDTH
BLK_A = 0
BLK_B = A_WIDTH // LANES
BLK_CQ = BLK_B + B_WIDTH // LANES
BLK_CKV = BLK_CQ + C_Q_WIDTH // LANES
BLK_GATE = BLK_CKV + 3 * C_KV_HEADS
N_BLK = BLK_GATE + 1
P_WIDTH = N_BLK * LANES

VMEM_LIMIT = 56 * 1024 * 1024


def _cparams(*sem):
    return pltpu.CompilerParams(dimension_semantics=sem, vmem_limit_bytes=VMEM_LIMIT)


def _dot_nt(a, b):
    return lax.dot_general(a, b, (((1,), (1,)), ((), ())), preferred_element_type=F32)


def _dot(a, b, **kw):
    return jnp.dot(a, b, preferred_element_type=F32, **kw)


def _transpose_bf16(x):
    return x.astype(F32).T.astype(BF16)


def _iota(shape, dim):
    return lax.broadcasted_iota(jnp.int32, shape, dim)


def _t5_bucket(dist):
    n = jnp.maximum(dist, 0)
    max_exact = N_BUCKETS // 2
    nf = jnp.maximum(n, 1).astype(F32)
    large = max_exact + (jnp.log(nf / max_exact) / math.log(T5_MAX_DIST / max_exact)
                         * (N_BUCKETS - max_exact)).astype(jnp.int32)
    large = jnp.minimum(large, N_BUCKETS - 1)
    return jnp.where(n < max_exact, n, large)


def _bias_lookup_kernel(tbl_ref, idx_ref, o_ref):
    h = pl.program_id(1)
    idx = idx_ref[0]
    acc = jnp.zeros(idx.shape, F32)
    for b in range(N_BUCKETS):
        acc = jnp.where(idx == b, tbl_ref[h, b], acc)
    o_ref[0, 0] = acc


def _bias_tiles(tbl, idx, heads_per_row=1):
    n, R, C = idx.shape
    H = tbl.shape[0]
    hpr = heads_per_row
    return pl.pallas_call(
        _bias_lookup_kernel,
        grid=(n, H),
        in_specs=[pl.BlockSpec(memory_space=pltpu.SMEM),
                  pl.BlockSpec((1, R, C), lambda t, h: (t, 0, 0))],
        out_specs=pl.BlockSpec((1, 1, R, C), lambda t, h: (t, h // hpr, 0, h % hpr)),
        out_shape=jax.ShapeDtypeStruct((n, H // hpr, R, hpr * C), F32),
        compiler_params=_cparams("arbitrary", "arbitrary"),
        name="bias_tiles",
    )(tbl, idx)


def _toeplitz_bucket_tiles_t(n_tiles, size):
    d = jnp.arange(n_tiles)[:, None, None] * size
    key = jnp.arange(size)[None, :, None]
    qry = jnp.arange(size)[None, None, :]
    return _t5_bucket(d + qry - key).astype(jnp.int32)


def _n_far_tiles(n_chunks, size):
    dmax = -(-(T5_MAX_DIST + size - 1) // size)
    dmax = min(dmax, n_chunks - 1)
    return dmax


def _norm_rows(x, g):
    ms = jnp.mean(x * x, axis=-1, keepdims=True)
    return x * lax.rsqrt(ms + EPS) * g


def _inproj_kernel(x_ref, g_ref, w_ref, o_ref, h_scr):
    @pl.when(pl.program_id(1) == 0)
    def _():
        h_scr[...] = _norm_rows(x_ref[...], g_ref[...]).astype(BF16)

    o_ref[...] = _dot(h_scr[...], w_ref[...]).astype(o_ref.dtype)


def _inproj(x2, gain, w, tm=1024, tn=256):
    T, D = x2.shape
    N = w.shape[1]
    return pl.pallas_call(
        _inproj_kernel,
        grid=(T // tm, N // tn),
        in_specs=[pl.BlockSpec((tm, D), lambda i, j: (i, 0)),
                  pl.BlockSpec((1, D), lambda i, j: (0, 0)),
                  pl.BlockSpec((D, tn), lambda i, j: (0, j))],
        out_specs=pl.BlockSpec((tm, tn), lambda i, j: (i, j)),
        out_shape=jax.ShapeDtypeStruct((T, N), BF16),
        scratch_shapes=[pltpu.VMEM((tm, D), BF16)],
        compiler_params=_cparams("arbitrary", "arbitrary"),
        name="inproj",
    )(x2, gain.reshape(1, D), w)


def _outproj_kernel(x_ref, a_ref, b_ref, c_ref, wa_ref, wb_ref, wc_ref, o_ref):
    acc = _dot(a_ref[...], wa_ref[...])
    acc += _dot(b_ref[...], wb_ref[...])
    acc += _dot(c_ref[...], wc_ref[...])
    o_ref[...] = x_ref[...] + acc


def _outproj(x2, oa, ob, oc, wa, wb, wc, tm=1024, tn=512):
    T, D = x2.shape
    return pl.pallas_call(
        _outproj_kernel,
        grid=(T // tm, D // tn),
        in_specs=[pl.BlockSpec((tm, tn), lambda i, j: (i, j)),
                  pl.BlockSpec((tm, oa.shape[1]), lambda i, j: (i, 0)),
                  pl.BlockSpec((tm, ob.shape[1]), lambda i, j: (i, 0)),
                  pl.BlockSpec((tm, oc.shape[1]), lambda i, j: (i, 0)),
                  pl.BlockSpec((wa.shape[0], tn), lambda i, j: (0, j)),
                  pl.BlockSpec((wb.shape[0], tn), lambda i, j: (0, j)),
                  pl.BlockSpec((wc.shape[0], tn), lambda i, j: (0, j))],
        out_specs=pl.BlockSpec((tm, tn), lambda i, j: (i, j)),
        out_shape=jax.ShapeDtypeStruct((T, D), F32),
        compiler_params=_cparams("arbitrary", "arbitrary"),
        name="outproj",
    )(x2, oa, ob, oc, wa, wb, wc)


FFN_HALO = 8


def _ffn_kernel(x_ref, halo_ref, g_ref, wa_ref, wg_ref, cwa_ref, cwg_ref, cba_ref, cbg_ref, wd_ref,
                o_ref, h_scr, ua_scr, ug_scr, *, tm, blocks_per_seq):
    i = pl.program_id(0)
    f = pl.program_id(1)

    @pl.when(f == 0)
    def _():
        x = x_ref[...]
        h_scr[pl.ds(FFN_HALO, tm), :] = _norm_rows(x, g_ref[...]).astype(BF16)
        keep = jnp.where(i % blocks_per_seq == 0, 0.0, 1.0)
        h_scr[pl.ds(0, FFN_HALO), :] = (_norm_rows(halo_ref[0], g_ref[...]) * keep).astype(BF16)
        o_ref[...] = x

    h = h_scr[...]
    ua_scr[...] = _dot(h, wa_ref[...])
    ug_scr[...] = _dot(h, wg_ref[...])

    def conv(u_scr, cw_ref, cb_ref):
        acc = cb_ref[...]
        for j in range(3):
            acc = acc + cw_ref[pl.ds(j, 1), :] * u_scr[pl.ds(FFN_HALO - j, tm), :]
        return acc

    ya = conv(ua_scr, cwa_ref, cba_ref)
    yg = conv(ug_scr, cwg_ref, cbg_ref)
    act = (yg * jax.nn.sigmoid(yg) * ya).astype(BF16)
    o_ref[...] += _dot(act, wd_ref[...])


def _ffn(x2, gain, w_up, conv_w, conv_b, w_down, seq, tm=512, tf=512):
    T, D = x2.shape
    Fh = w_down.shape[0]
    nf = Fh // tf
    halo_view = x2.reshape(T // FFN_HALO, FFN_HALO, D)
    rb = tm // FFN_HALO
    kern = functools.partial(_ffn_kernel, tm=tm, blocks_per_seq=seq // tm)
    return pl.pallas_call(
        kern,
        grid=(T // tm, nf),
        in_specs=[pl.BlockSpec((tm, D), lambda i, f: (i, 0)),
                  pl.BlockSpec((1, FFN_HALO, D), lambda i, f: (jnp.maximum(i * rb - 1, 0), 0, 0)),
                  pl.BlockSpec((1, D), lambda i, f: (0, 0)),
                  pl.BlockSpec((D, tf), lambda i, f: (0, f)),
                  pl.BlockSpec((D, tf), lambda i, f: (0, f + nf)),
                  pl.BlockSpec((3, tf), lambda i, f: (0, f)),
                  pl.BlockSpec((3, tf), lambda i, f: (0, f + nf)),
                  pl.BlockSpec((1, tf), lambda i, f: (0, f)),
                  pl.BlockSpec((1, tf), lambda i, f: (0, f + nf)),
                  pl.BlockSpec((tf, D), lambda i, f: (f, 0))],
        out_specs=pl.BlockSpec((tm, D), lambda i, f: (i, 0)),
        out_shape=jax.ShapeDtypeStruct((T, D), F32),
        scratch_shapes=[pltpu.VMEM((tm + FFN_HALO, D), BF16),
                        pltpu.VMEM((tm + FFN_HALO, tf), F32),
                        pltpu.VMEM((tm + FFN_HALO, tf), F32)],
        compiler_params=_cparams("arbitrary", "arbitrary"),
        name="convffn",
    )(x2, halo_view, gain.reshape(1, D), w_up, w_up, conv_w, conv_w,
      conv_b.reshape(1, -1), conv_b.reshape(1, -1), w_down)


def _final_norm_kernel(x_ref, g_ref, o_ref):
    o_ref[...] = _norm_rows(x_ref[...], g_ref[...])


def _final_norm(x2, gain, tm=1024):
    T, D = x2.shape
    return pl.pallas_call(
        _final_norm_kernel,
        grid=(T // tm,),
        in_specs=[pl.BlockSpec((tm, D), lambda i: (i, 0)),
                  pl.BlockSpec((1, D), lambda i: (0, 0))],
        out_specs=pl.BlockSpec((tm, D), lambda i: (i, 0)),
        out_shape=jax.ShapeDtypeStruct((T, D), F32),
        compiler_params=_cparams("arbitrary"),
        name="final_norm",
    )(x2, gain.reshape(1, D))


def _stack_pair(qp):
    lane = _iota(qp.shape, 1)
    zero = jnp.zeros_like(qp)
    return jnp.concatenate([jnp.where(lane < HEAD_DIM, qp, zero),
                            jnp.where(lane >= HEAD_DIM, qp, zero)], axis=0)


def _unstack_pair(o):
    R = o.shape[0] // 2
    lane = _iota((R, LANES), 1)
    return jnp.where(lane < HEAD_DIM, o[:R], o[R:])


def _dilated_kernel(q_ref, kc_ref, kp_ref, vc_ref, vp_ref, bias_ref, o_ref, lse_ref):
    i = pl.program_id(2)
    R = 2 * A_BLOCK
    a = _iota((R, 2 * A_BLOCK), 0) % A_BLOCK
    bk = _iota((R, 2 * A_BLOCK), 1)
    rel = a + A_BLOCK - bk
    mask = (rel >= 0) & (rel <= A_BLOCK) & ((bk >= A_BLOCK) | (i > 0))
    n_pairs = q_ref.shape[2] // LANES

    def scores(p):
        cols = slice(p * LANES, (p + 1) * LANES)
        qs = _stack_pair(q_ref[0, :, cols])
        k = jnp.concatenate([kp_ref[0, :, cols], kc_ref[0, :, cols]], axis=0)
        s = _dot_nt(qs, k) + bias_ref[0, 2 * p:2 * p + 2].reshape(R, 2 * A_BLOCK)
        return jnp.where(mask, s, NEG)

    s_all = [scores(p) for p in range(n_pairs)]
    for p, s in enumerate(s_all):
        cols = slice(p * LANES, (p + 1) * LANES)
        v = jnp.concatenate([vp_ref[0, :, cols], vc_ref[0, :, cols]], axis=0)
        m = jnp.max(s, axis=1, keepdims=True)
        e = jnp.where(mask, jnp.exp(s - m), 0.0)
        den = jnp.sum(e, axis=1, keepdims=True)
        o = _dot(e.astype(BF16), v) / den
        lse = m + jnp.log(den)
        o_ref[0, :, cols] = _unstack_pair(o)
        lse_ref[0, :, cols] = _unstack_pair(jnp.broadcast_to(lse, (R, LANES)))


def _dilated_group(proj, bias_a, g, dilation):
    B, S, _ = proj.shape
    L = S // dilation
    nq = L // A_BLOCK
    W = A_HEADS_PER_GROUP * HEAD_DIM
    cols = proj[:, :, g * A_GROUP_WIDTH:(g + 1) * A_GROUP_WIDTH]
    view = cols.reshape(B, L, dilation * A_GROUP_WIDTH)

    def cur(part):
        return pl.BlockSpec((1, A_BLOCK, W), lambda b, r, i: (b, i, r * 3 + part))

    def prev(part):
        return pl.BlockSpec((1, A_BLOCK, W), lambda b, r, i: (b, jnp.maximum(i - 1, 0), r * 3 + part))

    out_spec = pl.BlockSpec((1, A_BLOCK, W), lambda b, r, i: (b, i, r))
    o, lse = pl.pallas_call(
        _dilated_kernel,
        grid=(B, dilation, nq),
        in_specs=[cur(0), cur(1), prev(1), cur(2), prev(2),
                  pl.BlockSpec((1, A_HEADS_PER_GROUP, A_BLOCK, 2 * A_BLOCK), lambda b, r, i: (g, g, 0, 0))],
        out_specs=[out_spec, out_spec],
        out_shape=[jax.ShapeDtypeStruct((B, L, dilation * W), F32)] * 2,
        compiler_params=_cparams("arbitrary", "arbitrary", "arbitrary"),
        name=f"dilated_g{g}",
    )(view, view, view, view, view, bias_a)
    return o.reshape(B * S, W), lse.reshape(B * S, W)


def _combine_kernel(o0, o1, o2, l0, l1, l2, out_ref):
    ls = [l0[...], l1[...], l2[...]]
    m = jnp.maximum(jnp.maximum(ls[0], ls[1]), ls[2])
    es = [jnp.exp(l - m) for l in ls]
    den = es[0] + es[1] + es[2]
    W = o0.shape[1]
    for g, o in enumerate((o0, o1, o2)):
        out_ref[:, g * W:(g + 1) * W] = (o[...] * (es[g] / den)).astype(out_ref.dtype)


def _combine_groups(os_, lses, tm=1024):
    T, W = os_[0].shape
    spec = pl.BlockSpec((tm, W), lambda i: (i, 0))
    return pl.pallas_call(
        _combine_kernel,
        grid=(T // tm,),
        in_specs=[spec] * 6,
        out_specs=pl.BlockSpec((tm, 3 * W), lambda i: (i, 0)),
        out_shape=jax.ShapeDtypeStruct((T, 3 * W), BF16),
        compiler_params=_cparams("arbitrary"),
        name="dilated_combine",
    )(*os_, *lses)


def _flash_init(m_scr, acc_scrs):
    m_scr[...] = jnp.full(m_scr.shape, M_FLOOR, F32)
    for acc in acc_scrs:
        acc[...] = jnp.zeros(acc.shape, F32)


def _with_ones(vt, row):
    return jnp.where(_iota(vt.shape, 0) == row, jnp.ones_like(vt), vt)


def _flash_tiles(tiles, q_groups, m_scr, acc_scrs):
    G = len(q_groups)
    W = q_groups[0].shape[0]
    m = [m_scr[:, g * W:(g + 1) * W] for g in range(G)]
    units = [(t, g) for t in range(len(tiles)) for g in range(G)]

    def scores(t, g):
        k, _, terms_of = tiles[t]
        s = _dot_nt(k, q_groups[g])
        for term in terms_of(g):
            s = s + term
        return s

    def apply(g, alpha, pv):
        acc_scrs[g][...] = alpha * acc_scrs[g][...] + pv

    s_next = scores(*units[0])
    pending = None
    for n, (t, g) in enumerate(units):
        s = s_next
        if n + 1 < len(units):
            s_next = scores(*units[n + 1])
        m_new = jnp.maximum(m[g], jnp.max(s, axis=0, keepdims=True))
        alpha = jnp.exp2(m[g] - m_new)
        p = jnp.exp2(s - m_new)
        m[g] = m_new
        pv = _dot(tiles[t][1](g), p.astype(BF16))
        if pending is not None:
            apply(*pending)
        pending = (g, alpha, pv)
    apply(*pending)
    m_scr[...] = jnp.concatenate(m, axis=1)


def _top_picks(score, n_pick):
    idx = _iota(score.shape, 0).astype(F32)

    def pick(_, carry):
        g, sel = carry
        mx = jnp.max(g, axis=0, keepdims=True)
        first = jnp.min(jnp.where(g == mx, idx, 1e9), axis=0, keepdims=True)
        pk = idx == first
        return jnp.where(pk, -3e38, g), jnp.where(pk, 1.0, sel)

    _, sel = lax.fori_loop(0, n_pick, pick, (score, jnp.zeros(score.shape, F32)))
    return sel


def _moba_kernel(q_ref, k_ref, v_ref, bias_ref, o_ref, km_scr, vt_scr, sel_scr, m_scr, *acc_scrs,
                 dmax, n_blocks):
    c = pl.program_id(2)
    BLK = MOBA_BLOCK
    Q = 2 * BLK

    @pl.when(c == 0)
    def _():
        def body(j, _):
            rows = pl.ds(pl.multiple_of(j * BLK, BLK), BLK)
            km_scr[pl.ds(j, 1), :] = jnp.mean(k_ref[0, rows, :].astype(F32), axis=0, keepdims=True)
            vt_scr[j] = _transpose_bf16(v_ref[0, rows, :])
            return 0
        lax.fori_loop(0, n_blocks, body, 0)

    qs = _stack_pair(q_ref[0])
    gate = _dot_nt(km_scr[...].astype(BF16), qs)
    blk = _iota(gate.shape, 0)
    gate = jnp.where(blk < c, gate, NEG)
    sel = _top_picks(gate, min(MOBA_TOPK, n_blocks))
    sel_scr[...] = jnp.where((blk < c) & (sel > 0.5), 0.0, NEG)

    _flash_init(m_scr, acc_scrs)
    per_head = BLK // GROUP_W
    q_groups = [qs[g * GROUP_W:(g + 1) * GROUP_W] for g in range(Q // GROUP_W)]

    def head_values(vt):
        both = [_with_ones(vt, (1 - h) * HEAD_DIM) for h in range(2)]
        return lambda g: both[g // per_head]

    def past_tile(j):
        jc = jnp.minimum(j, n_blocks - 1)
        kj = k_ref[0, pl.ds(pl.multiple_of(jc * BLK, BLK), BLK), :]
        tile = jnp.clip(c - j, 0, dmax)
        selrow = sel_scr[pl.ds(jc, 1), :]

        def terms(g):
            return bias_ref[tile, 0, :, pl.ds(g * GROUP_W, GROUP_W)], selrow[:, g * GROUP_W:(g + 1) * GROUP_W]

        return kj, head_values(vt_scr[jc]), terms

    def body(it, _):
        _flash_tiles([past_tile(MOBA_UNROLL * it + u) for u in range(MOBA_UNROLL)], q_groups, m_scr, acc_scrs)
        return 0

    lax.fori_loop(0, (c + MOBA_UNROLL - 1) // MOBA_UNROLL, body, 0)

    ko = k_ref[0, pl.ds(pl.multiple_of(c * BLK, BLK), BLK), :]

    def own_terms(g):
        qpos = (g * GROUP_W + _iota((BLK, GROUP_W), 1)) % BLK
        return (bias_ref[0, 0, :, pl.ds(g * GROUP_W, GROUP_W)],
                jnp.where(_iota((BLK, GROUP_W), 0) <= qpos, 0.0, NEG))

    _flash_tiles([(ko, head_values(vt_scr[c]), own_terms)], q_groups, m_scr, acc_scrs)
    outs = []
    for g, acc in enumerate(acc_scrs):
        h = g // per_head
        a = acc[...]
        den = a[(1 - h) * HEAD_DIM:(1 - h) * HEAD_DIM + 1]
        outs.append(a[h * HEAD_DIM:(h + 1) * HEAD_DIM] / den)
    o2 = jnp.concatenate([jnp.concatenate(outs[h * per_head:(h + 1) * per_head], axis=1) for h in range(2)],
                         axis=0)
    o_ref[0] = o2.T.astype(o_ref.dtype)


def _moba(proj, bias_b, dmax):
    B, S, _ = proj.shape
    NB = S // MOBA_BLOCK
    n_tiles = bias_b.shape[0]
    qb, kb, vb = BLK_B, BLK_B + 4, BLK_B + 8
    kern = functools.partial(_moba_kernel, dmax=dmax, n_blocks=NB)
    Q = 2 * MOBA_BLOCK
    return pl.pallas_call(
        kern,
        grid=(B, B_HEADS // 2, NB),
        in_specs=[pl.BlockSpec((1, MOBA_BLOCK, LANES), lambda b, p, c: (b, c, qb + p)),
                  pl.BlockSpec((1, S, LANES), lambda b, p, c: (b, 0, kb + p)),
                  pl.BlockSpec((1, S, LANES), lambda b, p, c: (b, 0, vb + p)),
                  pl.BlockSpec((n_tiles, 1, MOBA_BLOCK, Q), lambda b, p, c: (0, p, 0, 0))],
        out_specs=pl.BlockSpec((1, MOBA_BLOCK, LANES), lambda b, p, c: (b, c, p)),
        out_shape=jax.ShapeDtypeStruct((B, S, B_HEADS * HEAD_DIM), BF16),
        scratch_shapes=[pltpu.VMEM((NB, LANES), F32), pltpu.VMEM((NB, LANES, MOBA_BLOCK), BF16),
                        pltpu.VMEM((NB, Q), F32), pltpu.VMEM((1, Q), F32)]
        + [pltpu.VMEM((ACC_ROWS, GROUP_W), F32)] * (Q // GROUP_W),
        compiler_params=_cparams("arbitrary", "arbitrary", "arbitrary"),
        name="moba",
    )(proj, proj, proj, bias_b)


def _gelu_tanh(x):
    return 0.5 * x * (1.0 + jnp.tanh(math.sqrt(2.0 / math.pi) * (x + 0.044715 * (x * x * x))))


def _compress_kernel(sub_ref, pe_ref, w1_ref, w2_ref, o_ref, ot_ref, v_scr):
    n16 = sub_ref.shape[2]
    sub = sub_ref[0, 0].astype(F32)
    top = (sub + pe_ref[0:1, :]).astype(BF16)
    bot = (sub + pe_ref[1:2, :]).astype(BF16)
    u = _dot(top, w1_ref[0])
    v_scr[pl.ds(0, n16), :] = _dot(bot, w1_ref[1])
    v_scr[pl.ds(n16, 8), :] = jnp.zeros((8, v_scr.shape[1]), F32)
    hidden = u + v_scr[pl.ds(1, n16), :]
    out = _dot(_gelu_tanh(hidden).astype(BF16), w2_ref[...])
    o_ref[0, 0] = out.astype(o_ref.dtype)
    ot_ref[0, 0] = out.T.astype(ot_ref.dtype)


def _compress(sub, pe_rows, w1_blk, w2_blk):
    B, KV, n16, W = sub.shape
    return pl.pallas_call(
        _compress_kernel,
        grid=(B, KV),
        in_specs=[pl.BlockSpec((1, 1, n16, W), lambda b, n: (b, n, 0, 0)),
                  pl.BlockSpec((2, W), lambda b, n: (0, 0)),
                  pl.BlockSpec((2, W, 2 * CMP_HIDDEN), lambda b, n: (0, 0, 0)),
                  pl.BlockSpec((2 * CMP_HIDDEN, LANES), lambda b, n: (0, 0))],
        out_specs=[pl.BlockSpec((1, 1, n16, LANES), lambda b, n: (b, n, 0, 0)),
                   pl.BlockSpec((1, 1, LANES, n16), lambda b, n: (b, n, 0, 0))],
        out_shape=[jax.ShapeDtypeStruct((B, KV, n16, LANES), BF16),
                   jax.ShapeDtypeStruct((B, KV, LANES, n16), BF16)],
        scratch_shapes=[pltpu.VMEM((n16 + 8, 2 * CMP_HIDDEN), F32)],
        compiler_params=_cparams("arbitrary", "arbitrary"),
        name="nsa_compress",
    )(sub, pe_rows, w1_blk, w2_blk)


def _nsa_kernel(q_ref, kc_ref, kct_ref, slc_ref, win_ref, gate_ref, bias_ref, ovt_ref,
                o_ref, slct_scr, wint_scr, sel_scr, g_scr, m_slc, m_win, *acc_scrs, dmax, n_sel):
    n = pl.program_id(1)
    c = pl.program_id(2)
    CQ = C_CHUNK
    lane = _iota((CQ, LANES), 1)
    halves = NSA_KT // LANES

    @pl.when(c == 0)
    def _():
        def body(t, _):
            slct_scr[t] = _transpose_bf16(slc_ref[0, pl.ds(pl.multiple_of(t * NSA_KT, NSA_KT), NSA_KT), :])
            for h in range(halves):
                rows = pl.ds(pl.multiple_of(t * NSA_KT + h * LANES, LANES), LANES)
                wint_scr[halves * t + h] = _transpose_bf16(win_ref[0, rows, :])
            return 0
        lax.fori_loop(0, slct_scr.shape[0], body, 0)

    q = q_ref[0].astype(F32)
    heads = []
    for pr in range(2):
        qp = q[:, pr * LANES:(pr + 1) * LANES]
        heads.append(jnp.where(lane < HEAD_DIM, qp, 0.0))
        heads.append(jnp.where(lane < HEAD_DIM, pltpu.roll(qp, HEAD_DIM, 1), 0.0))
    q_heads = [h.astype(BF16) for h in heads]

    kc = kc_ref[0, 0]
    kct = kct_ref[0, 0]
    NC = kc.shape[0]
    visible = (_iota((NC, CQ), 0) * CMP_STRIDE + (CMP_LEN - 1)) <= c * CQ + _iota((NC, CQ), 1)
    o_cmp = []
    psum = jnp.zeros((NC, CQ), F32)
    s_next = _dot_nt(kc, q_heads[0])
    for gi in range(C_GROUP):
        s = jnp.where(visible, s_next, NEG)
        if gi + 1 < C_GROUP:
            s_next = _dot_nt(kc, q_heads[gi + 1])
        m = jnp.max(s, axis=0, keepdims=True)
        e = jnp.where(visible, jnp.exp2(s - m), 0.0)
        den = jnp.sum(e, axis=0, keepdims=True)
        p = e / jnp.maximum(den, 1e-30)
        o_cmp.append(_dot(kct, p.astype(BF16)))
        psum = psum + p

    ovt = ovt_ref[...]
    imp = jnp.zeros((ovt.shape[0], CQ), F32)
    rest = psum
    for _ in range(IMP_SPLITS):
        piece = rest.astype(BF16)
        imp = imp + _dot(ovt, piece)
        rest = rest - piece.astype(F32)
    NS = imp.shape[0]
    jj = _iota((NS, CQ), 0)
    own = (c * CQ + _iota((NS, CQ), 1)) // SLC_BLOCK
    forced = (jj == 0) | (jj == own) | (jj == own - 1)
    score = jnp.where(forced, -3e38, jnp.where(jj <= own, imp, -1.0))
    picked = _top_picks(score, max(n_sel - N_FORCED, 0))
    sel_scr[...] = jnp.where(forced, 1.0, picked)

    reps = GROUP_W // CQ
    n_wide = C_GROUP // reps
    q_groups = [jnp.concatenate(q_heads[w * reps:(w + 1) * reps], axis=0) for w in range(n_wide)]
    acc_slc = acc_scrs[:n_wide]
    acc_win = acc_scrs[n_wide:]

    _flash_init(m_slc, acc_slc)
    per_tile = NSA_KT // SLC_BLOCK
    n_kt = slct_scr.shape[0]
    key_row2 = _iota((NSA_KT, CQ), 0)
    q_col2 = _iota((NSA_KT, CQ), 1)

    def slc_tile(kt):
        ktc = jnp.minimum(kt, n_kt - 1)
        k = slc_ref[0, pl.ds(pl.multiple_of(ktc * NSA_KT, NSA_KT), NSA_KT), :]
        r = sel_scr[pl.ds(pl.multiple_of(per_tile * ktc, per_tile), per_tile), :]
        chosen = jnp.concatenate([jnp.broadcast_to(r[i:i + 1, :], (SLC_BLOCK, CQ)) for i in range(per_tile)], axis=0)
        ok = (chosen > 0.5) & (kt * NSA_KT + key_row2 <= c * CQ + q_col2)
        negw = jnp.concatenate([jnp.where(ok, 0.0, NEG)] * reps, axis=1)
        delta = c - halves * kt

        def terms(g):
            cols = pl.ds(g * GROUP_W, GROUP_W)
            bias = jnp.concatenate([bias_ref[jnp.clip(delta - h, 0, dmax), 0, :, cols] for h in range(halves)],
                                   axis=0)
            return bias, negw

        vt = _with_ones(slct_scr[ktc], 0)
        return k, lambda g: vt, terms

    def slc_body(it, _):
        _flash_tiles([slc_tile(NSA_UNROLL * it + u) for u in range(NSA_UNROLL)], q_groups, m_slc, acc_slc)
        return 0

    keys_per_iter = NSA_UNROLL * NSA_KT
    lax.fori_loop(0, ((c + 1) * CQ + keys_per_iter - 1) // keys_per_iter, slc_body, 0)

    _flash_init(m_win, acc_win)
    n_wt = wint_scr.shape[0]
    first = WIN // LANES
    key_row = _iota((LANES, CQ), 0)
    q_col = _iota((LANES, CQ), 1)

    def win_tile(w):
        ks, vts, negs, dds = [], [], [], []
        for h in range(halves):
            dd = first - halves * w - h
            kt = c - dd
            ktc = jnp.clip(kt, 0, n_wt - 1)
            ks.append(win_ref[0, pl.ds(pl.multiple_of(ktc * LANES, LANES), LANES), :])
            vts.append(wint_scr[ktc])
            dist = dd * LANES + q_col - key_row
            negs.append(jnp.where((dist >= 0) & (dist < WIN) & (kt >= 0), 0.0, NEG))
            dds.append(max(dd, 0))
        negw = jnp.concatenate([jnp.concatenate(negs, axis=0)] * reps, axis=1)

        def terms(g):
            cols = pl.ds(g * GROUP_W, GROUP_W)
            return jnp.concatenate([bias_ref[dd, 0, :, cols] for dd in dds], axis=0), negw

        vt = _with_ones(jnp.concatenate(vts, axis=1), 0)
        return jnp.concatenate(ks, axis=0), lambda g: vt, terms

    n_win_tiles = -(-(WIN + CQ) // NSA_KT)
    _flash_tiles([win_tile(w) for w in range(n_win_tiles)], q_groups, m_win, acc_win)

    g_scr[...] = jax.nn.sigmoid(gate_ref[0].astype(F32)).T
    res = []
    for gi in range(C_GROUP):
        col0 = (n * C_GROUP + gi) * 3
        wcols = pl.ds((gi % reps) * CQ, CQ)
        a_s = acc_slc[gi // reps][:, wcols]
        a_w = acc_win[gi // reps][:, wcols]
        tot = (g_scr[pl.ds(col0, 1), :] * o_cmp[gi][HEAD_DIM:, :]
               + g_scr[pl.ds(col0 + 1, 1), :] * (a_s[HEAD_DIM:] / a_s[0:1])
               + g_scr[pl.ds(col0 + 2, 1), :] * (a_w[HEAD_DIM:] / a_w[0:1]))
        res.append(tot)
    o_ref[0] = jnp.concatenate(res, axis=0).T.astype(o_ref.dtype)


def _nsa(proj, kcvc, kcvct, bias_c, overlap_t, dmax):
    B, S, _ = proj.shape
    n_chunks = S // C_CHUNK
    NC = kcvc.shape[2]
    NS = S // SLC_BLOCK
    n_tiles = bias_c.shape[0]
    Q = C_GROUP * C_CHUNK
    kern = functools.partial(_nsa_kernel, dmax=dmax, n_sel=min(SLC_TOPN, NS))
    return pl.pallas_call(
        kern,
        grid=(B, C_KV_HEADS, n_chunks),
        in_specs=[pl.BlockSpec((1, C_CHUNK, 2 * LANES), lambda b, n, c: (b, c, BLK_CQ // 2 + n)),
                  pl.BlockSpec((1, 1, NC, LANES), lambda b, n, c: (b, n, 0, 0)),
                  pl.BlockSpec((1, 1, LANES, NC), lambda b, n, c: (b, n, 0, 0)),
                  pl.BlockSpec((1, S, LANES), lambda b, n, c: (b, 0, BLK_CKV + 3 * n + 1)),
                  pl.BlockSpec((1, S, LANES), lambda b, n, c: (b, 0, BLK_CKV + 3 * n + 2)),
                  pl.BlockSpec((1, C_CHUNK, LANES), lambda b, n, c: (b, c, BLK_GATE)),
                  pl.BlockSpec((n_tiles, 1, C_CHUNK, Q), lambda b, n, c: (0, n, 0, 0)),
                  pl.BlockSpec((NS, NC), lambda b, n, c: (0, 0))],
        out_specs=pl.BlockSpec((1, C_CHUNK, 2 * LANES), lambda b, n, c: (b, c, n)),
        out_shape=jax.ShapeDtypeStruct((B, S, C_HEADS * HEAD_DIM), BF16),
        scratch_shapes=[pltpu.VMEM((S // NSA_KT, LANES, NSA_KT), BF16), pltpu.VMEM((n_chunks, LANES, LANES), BF16),
                        pltpu.VMEM((NS, C_CHUNK), F32), pltpu.VMEM((LANES, C_CHUNK), F32),
                        pltpu.VMEM((1, Q), F32), pltpu.VMEM((1, Q), F32)]
        + [pltpu.VMEM((ACC_ROWS, GROUP_W), F32)] * (2 * Q // GROUP_W),
        compiler_params=_cparams("arbitrary", "arbitrary", "arbitrary"),
        name="nsa",
    )(proj, kcvc, kcvct, proj, proj, proj, bias_c, overlap_t)


def _column_scale():
    scale = np.ones((P_WIDTH,), np.float32)
    for g in range(len(A_CONFIGS)):
        scale[g * A_GROUP_WIDTH:g * A_GROUP_WIDTH + A_HEADS_PER_GROUP * HEAD_DIM] = Q_SCALE
    scale[A_WIDTH:A_WIDTH + B_HEADS * HEAD_DIM] = Q_SCALE * LOG2E
    scale[A_WIDTH + B_WIDTH:A_WIDTH + B_WIDTH + C_Q_WIDTH] = Q_SCALE * LOG2E
    return scale


def _permute_w_in(w_in):
    depth, D, _ = w_in.shape
    kv0 = A_WIDTH + B_WIDTH + C_Q_WIDTH
    kv = w_in[:, :, kv0:kv0 + C_KV_WIDTH].reshape(depth, D, 3, 2, C_KV_HEADS, HEAD_DIM)
    kv = kv.transpose(0, 1, 4, 2, 3, 5).reshape(depth, D, C_KV_WIDTH)
    pad = jnp.zeros((depth, D, P_WIDTH - IN_WIDTH), w_in.dtype)
    w = jnp.concatenate([w_in[:, :, :kv0], kv, w_in[:, :, kv0 + C_KV_WIDTH:], pad], axis=2)
    return (w * _column_scale()).astype(BF16)


def _compress_weights(cmp_w1, cmp_w2, cmp_pe):
    half = CMP_LEN // 2
    w1 = cmp_w1.reshape(2, 2, half, HEAD_DIM, CMP_HIDDEN)
    w1_blk = jnp.zeros((2, half, 2, HEAD_DIM, 2, CMP_HIDDEN), F32)
    w1_blk = w1_blk.at[:, :, 0, :, 0, :].set(w1[0])
    w1_blk = w1_blk.at[:, :, 1, :, 1, :].set(w1[1])
    w1_blk = w1_blk.reshape(2, half * 2 * HEAD_DIM, 2 * CMP_HIDDEN).astype(BF16)
    w2_blk = jnp.zeros((2, CMP_HIDDEN, 2, HEAD_DIM), F32)
    w2_blk = w2_blk.at[0, :, 0, :].set(cmp_w2[0])
    w2_blk = w2_blk.at[1, :, 1, :].set(cmp_w2[1])
    w2_blk = w2_blk.reshape(2 * CMP_HIDDEN, 2 * HEAD_DIM).astype(BF16)
    pe = cmp_pe.reshape(2, 2, half, HEAD_DIM)
    pe_rows = pe.transpose(1, 2, 0, 3).reshape(2, half * 2 * HEAD_DIM)
    return w1_blk, w2_blk, pe_rows


def kernel(x, rel_table, w_in, w_out, cmp_w1, cmp_w2, cmp_pe, norm_attn, norm_mlp,
           w_up, conv_w, conv_b, w_down, norm_final):
    B, S, D = x.shape
    depth = w_in.shape[0]
    T = B * S

    a = jnp.arange(A_BLOCK)[:, None]
    bk = jnp.arange(2 * A_BLOCK)[None, :]
    rel = a + A_BLOCK - bk
    idx_a = jnp.stack([_t5_bucket(rel * d) for _, d in A_CONFIGS]).astype(jnp.int32)
    bias_a = _bias_tiles(rel_table[:A_HEADS], idx_a)
    dmax_b = _n_far_tiles(S // MOBA_BLOCK, MOBA_BLOCK)
    bias_b = _bias_tiles(rel_table[A_HEADS:A_HEADS + B_HEADS] * LOG2E,
                         _toeplitz_bucket_tiles_t(dmax_b + 1, MOBA_BLOCK), 2)
    dmax_c = _n_far_tiles(S // C_CHUNK, C_CHUNK)
    bias_c = _bias_tiles(rel_table[A_HEADS + B_HEADS:] * LOG2E,
                         _toeplitz_bucket_tiles_t(dmax_c + 1, C_CHUNK), C_GROUP)

    n16 = S // CMP_STRIDE
    ci = jnp.arange(n16)[None, :] * CMP_STRIDE
    sj = jnp.arange(S // SLC_BLOCK)[:, None] * SLC_BLOCK
    overlap_t = ((ci < sj + SLC_BLOCK) & (ci + CMP_LEN > sj)).astype(BF16)

    w_in_p = _permute_w_in(w_in)
    w_out_b = w_out.astype(BF16)
    w_up_b = w_up.astype(BF16)
    w_down_b = w_down.astype(BF16)
    wa_rows = A_HEADS * HEAD_DIM
    wb_rows = wa_rows + B_HEADS * HEAD_DIM

    x2 = x.reshape(T, D)
    for l in range(depth):
        proj = _inproj(x2, norm_attn[l], w_in_p[l]).reshape(B, S, P_WIDTH)
        os_, lses = [], []
        for g, (_, d) in enumerate(A_CONFIGS):
            o, lse = _dilated_group(proj, bias_a, g, d)
            os_.append(o)
            lses.append(lse)
        o_a = _combine_groups(os_, lses)
        o_b = _moba(proj, bias_b, dmax_b).reshape(T, B_HEADS * HEAD_DIM)
        w1_blk, w2_blk, pe_rows = _compress_weights(cmp_w1[l], cmp_w2[l], cmp_pe[l])
        cmp_cols = proj[:, :, BLK_CKV * LANES:BLK_GATE * LANES].reshape(B, S, C_KV_HEADS, 3, LANES)[:, :, :, 0]
        sub = cmp_cols.transpose(0, 2, 1, 3).reshape(B, C_KV_HEADS, n16, CMP_STRIDE * LANES)
        kcvc, kcvct = _compress(sub, pe_rows, w1_blk, w2_blk)
        o_c = _nsa(proj, kcvc, kcvct, bias_c, overlap_t, dmax_c).reshape(T, C_HEADS * HEAD_DIM)
        x2 = _outproj(x2, o_a, o_b, o_c, w_out_b[l, :wa_rows], w_out_b[l, wa_rows:wb_rows],
                      w_out_b[l, wb_rows:])
        x2 = _ffn(x2, norm_mlp[l], w_up_b[l], conv_w[l], conv_b[l], w_down_b[l], S)
    return _final_norm(x2, norm_final).reshape(B, S, D)
```

```python
import functools
import math

import numpy as np
import jax
import jax.numpy as jnp
from jax import lax
from jax.experimental import pallas as pl
from jax.experimental.pallas import tpu as pltpu

F32 = jnp.float32
BF16 = jnp.bfloat16

HEAD_DIM = 64
LANES = 128
A_CONFIGS = ((128, 1), (512, 4), (2048, 16))
A_HEADS_PER_GROUP = 4
A_HEADS = 12
A_BLOCK = 128
B_HEADS = 8
MOBA_BLOCK = 256
MOBA_TOPK = 3
MOBA_UNROLL = 4
C_KV_HEADS = 3
C_GROUP = 4
C_HEADS = 12
CMP_STRIDE = 16
CMP_LEN = 32
CMP_HIDDEN = 128
SLC_BLOCK = 64
SLC_TOPN = 16
N_FORCED = 3
IMP_SPLITS = 3
WIN = 512
FORCE_SCORE = 1e9
N_BUCKETS = 32
T5_MAX_DIST = 2048
EPS = 1e-6
NEG = -1e30
M_FLOOR = -1e20
Q_SCALE = HEAD_DIM ** -0.5
LOG2E = math.log2(math.e)
C_CHUNK = 128
ACC_ROWS = 128
GROUP_W = 128
NSA_KT = 256
NSA_UNROLL = 4

A_WIDTH = 3 * A_HEADS * HEAD_DIM
A_GROUP_WIDTH = 3 * A_HEADS_PER_GROUP * HEAD_DIM
B_WIDTH = 3 * B_HEADS * HEAD_DIM
C_Q_WIDTH = C_HEADS * HEAD_DIM
C_KV_WIDTH = 6 * C_KV_HEADS * HEAD_DIM
C_GATE_WIDTH = 3 * C_HEADS
IN_WIDTH = A_WIDTH + B_WIDTH + C_Q_WIDTH + C_KV_WIDTH + C_GATE_WIDTH
BLK_A = 0
BLK_B = A_WIDTH // LANES
BLK_CQ = BLK_B + B_WIDTH // LANES
HEAD_WIDTH = A_WIDTH + B_WIDTH + C_Q_WIDTH
BLK_CKV = 0
BLK_GATE = 3 * C_KV_HEADS
TAIL_WIDTH = (BLK_GATE + 1) * LANES

VMEM_LIMIT = 56 * 1024 * 1024


def _cparams(*sem):
    return pltpu.CompilerParams(dimension_semantics=sem, vmem_limit_bytes=VMEM_LIMIT)


def _dot_nt(a, b):
    return lax.dot_general(a, b, (((1,), (1,)), ((), ())), preferred_element_type=F32)


def _dot(a, b, **kw):
    return jnp.dot(a, b, preferred_element_type=F32, **kw)


def _transpose_bf16(x):
    return x.astype(F32).T.astype(BF16)


def _iota(shape, dim):
    return lax.broadcasted_iota(jnp.int32, shape, dim)


def _t5_bucket(dist):
    n = jnp.maximum(dist, 0)
    max_exact = N_BUCKETS // 2
    nf = jnp.maximum(n, 1).astype(F32)
    large = max_exact + (jnp.log(nf / max_exact) / math.log(T5_MAX_DIST / max_exact)
                         * (N_BUCKETS - max_exact)).astype(jnp.int32)
    large = jnp.minimum(large, N_BUCKETS - 1)
    return jnp.where(n < max_exact, n, large)


def _bias_lookup_kernel(tbl_ref, idx_ref, o_ref):
    h = pl.program_id(1)
    idx = idx_ref[0]
    acc = jnp.zeros(idx.shape, F32)
    for b in range(N_BUCKETS):
        acc = jnp.where(idx == b, tbl_ref[h, b], acc)
    o_ref[0, 0] = acc


def _bias_tiles(tbl, idx, heads_per_row=1):
    n, R, C = idx.shape
    H = tbl.shape[0]
    hpr = heads_per_row
    return pl.pallas_call(
        _bias_lookup_kernel,
        grid=(n, H),
        in_specs=[pl.BlockSpec(memory_space=pltpu.SMEM),
                  pl.BlockSpec((1, R, C), lambda t, h: (t, 0, 0))],
        out_specs=pl.BlockSpec((1, 1, R, C), lambda t, h: (t, h // hpr, 0, h % hpr)),
        out_shape=jax.ShapeDtypeStruct((n, H // hpr, R, hpr * C), F32),
        compiler_params=_cparams("arbitrary", "arbitrary"),
        name="bias_tiles",
    )(tbl, idx)


def _toeplitz_bucket_tiles_t(n_tiles, size):
    d = jnp.arange(n_tiles)[:, None, None] * size
    key = jnp.arange(size)[None, :, None]
    qry = jnp.arange(size)[None, None, :]
    return _t5_bucket(d + qry - key).astype(jnp.int32)


def _n_far_tiles(n_chunks, size):
    dmax = -(-(T5_MAX_DIST + size - 1) // size)
    dmax = min(dmax, n_chunks - 1)
    return dmax


def _norm_rows(x, g):
    ms = jnp.mean(x * x, axis=-1, keepdims=True)
    return x * lax.rsqrt(ms + EPS) * g


def _norm_kernel(x_ref, g_ref, o_ref):
    o_ref[...] = _norm_rows(x_ref[...], g_ref[...]).astype(o_ref.dtype)


def _norm(x2, gains, l, tm=1024):
    T, D = x2.shape
    return pl.pallas_call(
        _norm_kernel,
        grid=(T // tm,),
        in_specs=[pl.BlockSpec((tm, D), lambda i: (i, 0)),
                  pl.BlockSpec((None, 1, D), lambda i: (l, 0, 0))],
        out_specs=pl.BlockSpec((tm, D), lambda i: (i, 0)),
        out_shape=jax.ShapeDtypeStruct((T, D), BF16),
        compiler_params=_cparams("arbitrary"),
        name="first_norm",
    )(x2, gains)


def _inproj_kernel(h_ref, w_ref, o_ref):
    o_ref[...] = _dot(h_ref[...], w_ref[...]).astype(o_ref.dtype)


def _inproj(h, w_all, l, tn, tm=1024):
    T, D = h.shape
    N = w_all.shape[2]
    return pl.pallas_call(
        _inproj_kernel,
        grid=(T // tm, N // tn),
        in_specs=[pl.BlockSpec((tm, D), lambda i, j: (i, 0)),
                  pl.BlockSpec((None, D, tn), lambda i, j: (l, 0, j))],
        out_specs=pl.BlockSpec((tm, tn), lambda i, j: (i, j)),
        out_shape=jax.ShapeDtypeStruct((T, N), BF16),
        compiler_params=_cparams("arbitrary", "arbitrary"),
        name="inproj",
    )(h, w_all)


def _outproj_kernel(x_ref, a_ref, b_ref, c_ref, w_ref, g_ref, o_ref, h_ref):
    ra, rb = a_ref.shape[1], b_ref.shape[1]
    acc = _dot(a_ref[...], w_ref[pl.ds(0, ra), :])
    acc += _dot(b_ref[...], w_ref[pl.ds(ra, rb), :])
    acc += _dot(c_ref[...], w_ref[pl.ds(ra + rb, c_ref.shape[1]), :])
    x_new = x_ref[...] + acc
    o_ref[...] = x_new
    h_ref[...] = _norm_rows(x_new, g_ref[...]).astype(h_ref.dtype)


def _outproj(x2, oa, ob, oc, w_all, gains, l, tm=512):
    T, D = x2.shape
    row = lambda i: (i, 0)
    return pl.pallas_call(
        _outproj_kernel,
        grid=(T // tm,),
        in_specs=[pl.BlockSpec((tm, D), row),
                  pl.BlockSpec((tm, oa.shape[1]), row),
                  pl.BlockSpec((tm, ob.shape[1]), row),
                  pl.BlockSpec((tm, oc.shape[1]), row),
                  pl.BlockSpec((None, w_all.shape[1], D), lambda i: (l, 0, 0)),
                  pl.BlockSpec((None, 1, D), lambda i: (l, 0, 0))],
        out_specs=[pl.BlockSpec((tm, D), row), pl.BlockSpec((tm, D), row)],
        out_shape=[jax.ShapeDtypeStruct((T, D), F32), jax.ShapeDtypeStruct((T, D), BF16)],
        compiler_params=_cparams("arbitrary"),
        name="outproj",
    )(x2, oa, ob, oc, w_all, gains)


FFN_HALO = 16


def _ffn_kernel(x_ref, h_ref, halo_ref, wa_ref, wg_ref, cwa_ref, cwg_ref, cba_ref, cbg_ref, wd_ref, g_ref,
                o_ref, n_ref, h_scr, ua_scr, ug_scr, *, tm, blocks_per_seq):
    i = pl.program_id(0)
    f = pl.program_id(1)

    @pl.when(f == 0)
    def _():
        o_ref[...] = x_ref[...]
        keep = jnp.where(i % blocks_per_seq == 0, 0.0, 1.0).astype(BF16)
        h_scr[pl.ds(0, FFN_HALO), :] = halo_ref[0] * keep
        h_scr[pl.ds(FFN_HALO, tm), :] = h_ref[...]

    h = h_scr[...]
    ua_scr[...] = _dot(h, wa_ref[...])
    ug_scr[...] = _dot(h, wg_ref[...])

    def conv(u_scr, cw_ref, cb_ref):
        acc = cb_ref[...]
        for j in range(3):
            acc = acc + cw_ref[pl.ds(j, 1), :] * u_scr[pl.ds(FFN_HALO - j, tm), :]
        return acc

    ya = conv(ua_scr, cwa_ref, cba_ref)
    yg = conv(ug_scr, cwg_ref, cbg_ref)
    act = (yg * jax.nn.sigmoid(yg) * ya).astype(BF16)
    o_ref[...] += _dot(act, wd_ref[...])

    @pl.when(f == pl.num_programs(1) - 1)
    def _():
        n_ref[...] = _norm_rows(o_ref[...], g_ref[...]).astype(n_ref.dtype)


def _ffn(x2, h, w_up, conv_w, conv_b, w_down, l, next_gain, next_dtype, seq, tm=512, tf=512):
    T, D = x2.shape
    Fh = w_down.shape[1]
    nf = Fh // tf
    halo_view = h.reshape(T // FFN_HALO, FFN_HALO, D)
    rb = tm // FFN_HALO
    kern = functools.partial(_ffn_kernel, tm=tm, blocks_per_seq=seq // tm)
    row = lambda i, f: (i, 0)
    return pl.pallas_call(
        kern,
        grid=(T // tm, nf),
        in_specs=[pl.BlockSpec((tm, D), row),
                  pl.BlockSpec((tm, D), row),
                  pl.BlockSpec((1, FFN_HALO, D), lambda i, f: (jnp.maximum(i * rb - 1, 0), 0, 0)),
                  pl.BlockSpec((None, D, tf), lambda i, f: (l, 0, f)),
                  pl.BlockSpec((None, D, tf), lambda i, f: (l, 0, f + nf)),
                  pl.BlockSpec((None, 3, tf), lambda i, f: (l, 0, f)),
                  pl.BlockSpec((None, 3, tf), lambda i, f: (l, 0, f + nf)),
                  pl.BlockSpec((None, 1, tf), lambda i, f: (l, 0, f)),
                  pl.BlockSpec((None, 1, tf), lambda i, f: (l, 0, f + nf)),
                  pl.BlockSpec((None, tf, D), lambda i, f: (l, f, 0)),
                  pl.BlockSpec((1, D), lambda i, f: (0, 0))],
        out_specs=[pl.BlockSpec((tm, D), row), pl.BlockSpec((tm, D), row)],
        out_shape=[jax.ShapeDtypeStruct((T, D), F32), jax.ShapeDtypeStruct((T, D), next_dtype)],
        scratch_shapes=[pltpu.VMEM((tm + FFN_HALO, D), BF16),
                        pltpu.VMEM((tm + FFN_HALO, tf), F32),
                        pltpu.VMEM((tm + FFN_HALO, tf), F32)],
        compiler_params=_cparams("arbitrary", "arbitrary"),
        name="convffn",
    )(x2, h, halo_view, w_up, w_up, conv_w, conv_w, conv_b, conv_b, w_down, next_gain)


def _stack_pair(qp):
    lane = _iota(qp.shape, 1)
    zero = jnp.zeros_like(qp)
    return jnp.concatenate([jnp.where(lane < HEAD_DIM, qp, zero),
                            jnp.where(lane >= HEAD_DIM, qp, zero)], axis=0)


def _unstack_pair(o):
    R = o.shape[0] // 2
    lane = _iota((R, LANES), 1)
    return jnp.where(lane < HEAD_DIM, o[:R], o[R:])


def _dilated_kernel(q_ref, kc_ref, kp_ref, vc_ref, vp_ref, bias_ref, o_ref, lse_ref):
    i = pl.program_id(2)
    R = 2 * A_BLOCK
    a = _iota((R, 2 * A_BLOCK), 0) % A_BLOCK
    bk = _iota((R, 2 * A_BLOCK), 1)
    rel = a + A_BLOCK - bk
    mask = (rel >= 0) & (rel <= A_BLOCK) & ((bk >= A_BLOCK) | (i > 0))
    n_pairs = q_ref.shape[2] // LANES

    def scores(p):
        cols = slice(p * LANES, (p + 1) * LANES)
        qs = _stack_pair(q_ref[0, :, cols])
        k = jnp.concatenate([kp_ref[0, :, cols], kc_ref[0, :, cols]], axis=0)
        s = _dot_nt(qs, k) + bias_ref[0, 2 * p:2 * p + 2].reshape(R, 2 * A_BLOCK)
        return jnp.where(mask, s, NEG)

    s_all = [scores(p) for p in range(n_pairs)]
    for p, s in enumerate(s_all):
        cols = slice(p * LANES, (p + 1) * LANES)
        v = jnp.concatenate([vp_ref[0, :, cols], vc_ref[0, :, cols]], axis=0)
        m = jnp.max(s, axis=1, keepdims=True)
        e = jnp.where(mask, jnp.exp(s - m), 0.0)
        den = jnp.sum(e, axis=1, keepdims=True)
        o = _dot(e.astype(BF16), v) / den
        lse = m + jnp.log(den)
        o_ref[0, :, cols] = _unstack_pair(o)
        lse_ref[0, :, cols] = _unstack_pair(jnp.broadcast_to(lse, (R, LANES)))


def _dilated_group(proj, bias_a, g, dilation):
    B, S, _ = proj.shape
    L = S // dilation
    nq = L // A_BLOCK
    W = A_HEADS_PER_GROUP * HEAD_DIM
    cols = proj[:, :, g * A_GROUP_WIDTH:(g + 1) * A_GROUP_WIDTH]
    view = cols.reshape(B, L, dilation * A_GROUP_WIDTH)

    def cur(part):
        return pl.BlockSpec((1, A_BLOCK, W), lambda b, r, i: (b, i, r * 3 + part))

    def prev(part):
        return pl.BlockSpec((1, A_BLOCK, W), lambda b, r, i: (b, jnp.maximum(i - 1, 0), r * 3 + part))

    out_spec = pl.BlockSpec((1, A_BLOCK, W), lambda b, r, i: (b, i, r))
    o, lse = pl.pallas_call(
        _dilated_kernel,
        grid=(B, dilation, nq),
        in_specs=[cur(0), cur(1), prev(1), cur(2), prev(2),
                  pl.BlockSpec((1, A_HEADS_PER_GROUP, A_BLOCK, 2 * A_BLOCK), lambda b, r, i: (g, g, 0, 0))],
        out_specs=[out_spec, out_spec],
        out_shape=[jax.ShapeDtypeStruct((B, L, dilation * W), F32)] * 2,
        compiler_params=_cparams("arbitrary", "arbitrary", "arbitrary"),
        name=f"dilated_g{g}",
    )(view, view, view, view, view, bias_a)
    return o.reshape(B * S, W), lse.reshape(B * S, W)


def _combine_kernel(o0, o1, o2, l0, l1, l2, out_ref):
    ls = [l0[...], l1[...], l2[...]]
    m = jnp.maximum(jnp.maximum(ls[0], ls[1]), ls[2])
    es = [jnp.exp(l - m) for l in ls]
    den = es[0] + es[1] + es[2]
    W = o0.shape[1]
    for g, o in enumerate((o0, o1, o2)):
        out_ref[:, g * W:(g + 1) * W] = (o[...] * (es[g] / den)).astype(out_ref.dtype)


def _combine_groups(os_, lses, tm=1024):
    T, W = os_[0].shape
    spec = pl.BlockSpec((tm, W), lambda i: (i, 0))
    return pl.pallas_call(
        _combine_kernel,
        grid=(T // tm,),
        in_specs=[spec] * 6,
        out_specs=pl.BlockSpec((tm, 3 * W), lambda i: (i, 0)),
        out_shape=jax.ShapeDtypeStruct((T, 3 * W), BF16),
        compiler_params=_cparams("arbitrary"),
        name="dilated_combine",
    )(*os_, *lses)


def _flash_init(m_scr, acc_scrs):
    m_scr[...] = jnp.full(m_scr.shape, M_FLOOR, F32)
    for acc in acc_scrs:
        acc[...] = jnp.zeros(acc.shape, F32)


def _with_ones(vt, row):
    return jnp.where(_iota(vt.shape, 0) == row, jnp.ones_like(vt), vt)


def _flash_tiles(tiles, q_groups, m_scr, acc_scrs):
    G = len(q_groups)
    W = q_groups[0].shape[0]
    m = [m_scr[:, g * W:(g + 1) * W] for g in range(G)]
    units = [(t, g) for t in range(len(tiles)) for g in range(G)]

    def scores(t, g):
        k, _, terms_of = tiles[t]
        s = _dot_nt(k, q_groups[g])
        for term in terms_of(g):
            s = s + term
        return s

    def apply(g, alpha, pv):
        acc_scrs[g][...] = alpha * acc_scrs[g][...] + pv

    s_next = scores(*units[0])
    pending = None
    for n, (t, g) in enumerate(units):
        s = s_next
        if n + 1 < len(units):
            s_next = scores(*units[n + 1])
        m_new = jnp.maximum(m[g], jnp.max(s, axis=0, keepdims=True))
        alpha = jnp.exp2(m[g] - m_new)
        p = jnp.exp2(s - m_new)
        m[g] = m_new
        pv = _dot(tiles[t][1](g), p.astype(BF16))
        if pending is not None:
            apply(*pending)
        pending = (g, alpha, pv)
    apply(*pending)
    m_scr[...] = jnp.concatenate(m, axis=1)


def _top_picks(score, n_pick):
    idx = _iota(score.shape, 0).astype(F32)

    def pick(_, carry):
        g, sel = carry
        mx = jnp.max(g, axis=0, keepdims=True)
        first = jnp.min(jnp.where(g == mx, idx, 1e9), axis=0, keepdims=True)
        pk = idx == first
        return jnp.where(pk, -3e38, g), jnp.where(pk, 1.0, sel)

    _, sel = lax.fori_loop(0, n_pick, pick, (score, jnp.zeros(score.shape, F32)))
    return sel


def _moba_kernel(q_ref, k_ref, v_ref, bias_ref, o_ref, km_scr, vt_scr, sel_scr, m_scr, *acc_scrs,
                 dmax, n_blocks):
    c = pl.program_id(2)
    BLK = MOBA_BLOCK
    Q = 2 * BLK

    @pl.when(c == 0)
    def _():
        def body(j, _):
            rows = pl.ds(pl.multiple_of(j * BLK, BLK), BLK)
            km_scr[pl.ds(j, 1), :] = jnp.mean(k_ref[0, rows, :].astype(F32), axis=0, keepdims=True)
            vt_scr[j] = _transpose_bf16(v_ref[0, rows, :])
            return 0
        lax.fori_loop(0, n_blocks, body, 0)

    qs = _stack_pair(q_ref[0])
    gate = _dot_nt(km_scr[...].astype(BF16), qs)
    blk = _iota(gate.shape, 0)
    gate = jnp.where(blk < c, gate, NEG)
    sel = _top_picks(gate, min(MOBA_TOPK, n_blocks))
    sel_scr[...] = jnp.where((blk < c) & (sel > 0.5), 0.0, NEG)

    _flash_init(m_scr, acc_scrs)
    per_head = BLK // GROUP_W
    q_groups = [qs[g * GROUP_W:(g + 1) * GROUP_W] for g in range(Q // GROUP_W)]

    def head_values(vt):
        both = [_with_ones(vt, (1 - h) * HEAD_DIM) for h in range(2)]
        return lambda g: both[g // per_head]

    def past_tile(j):
        jc = jnp.minimum(j, n_blocks - 1)
        kj = k_ref[0, pl.ds(pl.multiple_of(jc * BLK, BLK), BLK), :]
        tile = jnp.clip(c - j, 0, dmax)
        selrow = sel_scr[pl.ds(jc, 1), :]

        def terms(g):
            return bias_ref[tile, 0, :, pl.ds(g * GROUP_W, GROUP_W)], selrow[:, g * GROUP_W:(g + 1) * GROUP_W]

        return kj, head_values(vt_scr[jc]), terms

    def body(it, _):
        _flash_tiles([past_tile(MOBA_UNROLL * it + u) for u in range(MOBA_UNROLL)], q_groups, m_scr, acc_scrs)
        return 0

    lax.fori_loop(0, (c + MOBA_UNROLL - 1) // MOBA_UNROLL, body, 0)

    ko = k_ref[0, pl.ds(pl.multiple_of(c * BLK, BLK), BLK), :]

    def own_terms(g):
        qpos = (g * GROUP_W + _iota((BLK, GROUP_W), 1)) % BLK
        return (bias_ref[0, 0, :, pl.ds(g * GROUP_W, GROUP_W)],
                jnp.where(_iota((BLK, GROUP_W), 0) <= qpos, 0.0, NEG))

    _flash_tiles([(ko, head_values(vt_scr[c]), own_terms)], q_groups, m_scr, acc_scrs)
    outs = []
    for g, acc in enumerate(acc_scrs):
        h = g // per_head
        a = acc[...]
        den = a[(1 - h) * HEAD_DIM:(1 - h) * HEAD_DIM + 1]
        outs.append(a[h * HEAD_DIM:(h + 1) * HEAD_DIM] / den)
    o2 = jnp.concatenate([jnp.concatenate(outs[h * per_head:(h + 1) * per_head], axis=1) for h in range(2)],
                         axis=0)
    o_ref[0] = o2.T.astype(o_ref.dtype)


def _moba(proj, bias_b, dmax):
    B, S, _ = proj.shape
    NB = S // MOBA_BLOCK
    n_tiles = bias_b.shape[0]
    qb, kb, vb = BLK_B, BLK_B + 4, BLK_B + 8
    kern = functools.partial(_moba_kernel, dmax=dmax, n_blocks=NB)
    Q = 2 * MOBA_BLOCK
    return pl.pallas_call(
        kern,
        grid=(B, B_HEADS // 2, NB),
        in_specs=[pl.BlockSpec((1, MOBA_BLOCK, LANES), lambda b, p, c: (b, c, qb + p)),
                  pl.BlockSpec((1, S, LANES), lambda b, p, c: (b, 0, kb + p)),
                  pl.BlockSpec((1, S, LANES), lambda b, p, c: (b, 0, vb + p)),
                  pl.BlockSpec((n_tiles, 1, MOBA_BLOCK, Q), lambda b, p, c: (0, p, 0, 0))],
        out_specs=pl.BlockSpec((1, MOBA_BLOCK, LANES), lambda b, p, c: (b, c, p)),
        out_shape=jax.ShapeDtypeStruct((B, S, B_HEADS * HEAD_DIM), BF16),
        scratch_shapes=[pltpu.VMEM((NB, LANES), F32), pltpu.VMEM((NB, LANES, MOBA_BLOCK), BF16),
                        pltpu.VMEM((NB, Q), F32), pltpu.VMEM((1, Q), F32)]
        + [pltpu.VMEM((ACC_ROWS, GROUP_W), F32)] * (Q // GROUP_W),
        compiler_params=_cparams("arbitrary", "arbitrary", "arbitrary"),
        name="moba",
    )(proj, proj, proj, bias_b)


def _gelu_tanh(x):
    return 0.5 * x * (1.0 + jnp.tanh(math.sqrt(2.0 / math.pi) * (x + 0.044715 * (x * x * x))))


def _compress_kernel(sub_ref, pe_ref, w1_ref, w2_ref, o_ref, ot_ref, v_scr):
    n16 = sub_ref.shape[2]
    sub = sub_ref[0, 0].astype(F32)
    top = (sub + pe_ref[0:1, :]).astype(BF16)
    bot = (sub + pe_ref[1:2, :]).astype(BF16)
    u = _dot(top, w1_ref[0])
    v_scr[pl.ds(0, n16), :] = _dot(bot, w1_ref[1])
    v_scr[pl.ds(n16, 8), :] = jnp.zeros((8, v_scr.shape[1]), F32)
    hidden = u + v_scr[pl.ds(1, n16), :]
    out = _dot(_gelu_tanh(hidden).astype(BF16), w2_ref[...])
    o_ref[0, 0] = out.astype(o_ref.dtype)
    ot_ref[0, 0] = out.T.astype(ot_ref.dtype)


def _compress(sub, pe_rows, w1_blk, w2_blk):
    B, KV, n16, W = sub.shape
    return pl.pallas_call(
        _compress_kernel,
        grid=(B, KV),
        in_specs=[pl.BlockSpec((1, 1, n16, W), lambda b, n: (b, n, 0, 0)),
                  pl.BlockSpec((2, W), lambda b, n: (0, 0)),
                  pl.BlockSpec((2, W, 2 * CMP_HIDDEN), lambda b, n: (0, 0, 0)),
                  pl.BlockSpec((2 * CMP_HIDDEN, LANES), lambda b, n: (0, 0))],
        out_specs=[pl.BlockSpec((1, 1, n16, LANES), lambda b, n: (b, n, 0, 0)),
                   pl.BlockSpec((1, 1, LANES, n16), lambda b, n: (b, n, 0, 0))],
        out_shape=[jax.ShapeDtypeStruct((B, KV, n16, LANES), BF16),
                   jax.ShapeDtypeStruct((B, KV, LANES, n16), BF16)],
        scratch_shapes=[pltpu.VMEM((n16 + 8, 2 * CMP_HIDDEN), F32)],
        compiler_params=_cparams("arbitrary", "arbitrary"),
        name="nsa_compress",
    )(sub, pe_rows, w1_blk, w2_blk)


def _nsa_kernel(q_ref, kc_ref, kct_ref, slc_ref, win_ref, gate_ref, bias_ref, ovt_ref,
                o_ref, slct_scr, wint_scr, sel_scr, g_scr, m_slc, m_win, *acc_scrs, dmax, n_sel):
    n = pl.program_id(1)
    c = pl.program_id(2)
    CQ = C_CHUNK
    lane = _iota((CQ, LANES), 1)
    halves = NSA_KT // LANES

    @pl.when(c == 0)
    def _():
        def body(t, _):
            slct_scr[t] = _transpose_bf16(slc_ref[0, pl.ds(pl.multiple_of(t * NSA_KT, NSA_KT), NSA_KT), :])
            for h in range(halves):
                rows = pl.ds(pl.multiple_of(t * NSA_KT + h * LANES, LANES), LANES)
                wint_scr[halves * t + h] = _transpose_bf16(win_ref[0, rows, :])
            return 0
        lax.fori_loop(0, slct_scr.shape[0], body, 0)

    q = q_ref[0].astype(F32)
    heads = []
    for pr in range(2):
        qp = q[:, pr * LANES:(pr + 1) * LANES]
        heads.append(jnp.where(lane < HEAD_DIM, qp, 0.0))
        heads.append(jnp.where(lane < HEAD_DIM, pltpu.roll(qp, HEAD_DIM, 1), 0.0))
    q_heads = [h.astype(BF16) for h in heads]

    kc = kc_ref[0, 0]
    kct = kct_ref[0, 0]
    NC = kc.shape[0]
    visible = (_iota((NC, CQ), 0) * CMP_STRIDE + (CMP_LEN - 1)) <= c * CQ + _iota((NC, CQ), 1)
    o_cmp = []
    psum = jnp.zeros((NC, CQ), F32)
    s_next = _dot_nt(kc, q_heads[0])
    for gi in range(C_GROUP):
        s = jnp.where(visible, s_next, NEG)
        if gi + 1 < C_GROUP:
            s_next = _dot_nt(kc, q_heads[gi + 1])
        m = jnp.max(s, axis=0, keepdims=True)
        e = jnp.where(visible, jnp.exp2(s - m), 0.0)
        den = jnp.sum(e, axis=0, keepdims=True)
        p = e / jnp.maximum(den, 1e-30)
        o_cmp.append(_dot(kct, p.astype(BF16)))
        psum = psum + p

    ovt = ovt_ref[...]
    imp = jnp.zeros((ovt.shape[0], CQ), F32)
    rest = psum
    for _ in range(IMP_SPLITS):
        piece = rest.astype(BF16)
        imp = imp + _dot(ovt, piece)
        rest = rest - piece.astype(F32)
    NS = imp.shape[0]
    jj = _iota((NS, CQ), 0)
    own = (c * CQ + _iota((NS, CQ), 1)) // SLC_BLOCK
    forced = (jj == 0) | (jj == own) | (jj == own - 1)
    score = jnp.where(forced, -3e38, jnp.where(jj <= own, imp, -1.0))
    picked = _top_picks(score, max(n_sel - N_FORCED, 0))
    sel_scr[...] = jnp.where(forced, 1.0, picked)

    reps = GROUP_W // CQ
    n_wide = C_GROUP // reps
    q_groups = [jnp.concatenate(q_heads[w * reps:(w + 1) * reps], axis=0) for w in range(n_wide)]
    acc_slc = acc_scrs[:n_wide]
    acc_win = acc_scrs[n_wide:]

    _flash_init(m_slc, acc_slc)
    per_tile = NSA_KT // SLC_BLOCK
    n_kt = slct_scr.shape[0]
    key_row2 = _iota((NSA_KT, CQ), 0)
    q_col2 = _iota((NSA_KT, CQ), 1)

    def slc_tile(kt):
        ktc = jnp.minimum(kt, n_kt - 1)
        k = slc_ref[0, pl.ds(pl.multiple_of(ktc * NSA_KT, NSA_KT), NSA_KT), :]
        r = sel_scr[pl.ds(pl.multiple_of(per_tile * ktc, per_tile), per_tile), :]
        chosen = jnp.concatenate([jnp.broadcast_to(r[i:i + 1, :], (SLC_BLOCK, CQ)) for i in range(per_tile)], axis=0)
        ok = (chosen > 0.5) & (kt * NSA_KT + key_row2 <= c * CQ + q_col2)
        negw = jnp.concatenate([jnp.where(ok, 0.0, NEG)] * reps, axis=1)
        delta = c - halves * kt

        def terms(g):
            cols = pl.ds(g * GROUP_W, GROUP_W)
            bias = jnp.concatenate([bias_ref[jnp.clip(delta - h, 0, dmax), 0, :, cols] for h in range(halves)],
                                   axis=0)
            return bias, negw

        vt = _with_ones(slct_scr[ktc], 0)
        return k, lambda g: vt, terms

    def slc_body(it, _):
        _flash_tiles([slc_tile(NSA_UNROLL * it + u) for u in range(NSA_UNROLL)], q_groups, m_slc, acc_slc)
        return 0

    keys_per_iter = NSA_UNROLL * NSA_KT
    lax.fori_loop(0, ((c + 1) * CQ + keys_per_iter - 1) // keys_per_iter, slc_body, 0)

    _flash_init(m_win, acc_win)
    n_wt = wint_scr.shape[0]
    first = WIN // LANES
    key_row = _iota((LANES, CQ), 0)
    q_col = _iota((LANES, CQ), 1)

    def win_tile(w):
        ks, vts, negs, dds = [], [], [], []
        for h in range(halves):
            dd = first - halves * w - h
            kt = c - dd
            ktc = jnp.clip(kt, 0, n_wt - 1)
            ks.append(win_ref[0, pl.ds(pl.multiple_of(ktc * LANES, LANES), LANES), :])
            vts.append(wint_scr[ktc])
            dist = dd * LANES + q_col - key_row
            negs.append(jnp.where((dist >= 0) & (dist < WIN) & (kt >= 0), 0.0, NEG))
            dds.append(max(dd, 0))
        negw = jnp.concatenate([jnp.concatenate(negs, axis=0)] * reps, axis=1)

        def terms(g):
            cols = pl.ds(g * GROUP_W, GROUP_W)
            return jnp.concatenate([bias_ref[dd, 0, :, cols] for dd in dds], axis=0), negw

        vt = _with_ones(jnp.concatenate(vts, axis=1), 0)
        return jnp.concatenate(ks, axis=0), lambda g: vt, terms

    n_win_tiles = -(-(WIN + CQ) // NSA_KT)
    _flash_tiles([win_tile(w) for w in range(n_win_tiles)], q_groups, m_win, acc_win)

    g_scr[...] = jax.nn.sigmoid(gate_ref[0].astype(F32)).T
    res = []
    for gi in range(C_GROUP):
        col0 = (n * C_GROUP + gi) * 3
        wcols = pl.ds((gi % reps) * CQ, CQ)
        a_s = acc_slc[gi // reps][:, wcols]
        a_w = acc_win[gi // reps][:, wcols]
        tot = (g_scr[pl.ds(col0, 1), :] * o_cmp[gi][HEAD_DIM:, :]
               + g_scr[pl.ds(col0 + 1, 1), :] * (a_s[HEAD_DIM:] / a_s[0:1])
               + g_scr[pl.ds(col0 + 2, 1), :] * (a_w[HEAD_DIM:] / a_w[0:1]))
        res.append(tot)
    o_ref[0] = jnp.concatenate(res, axis=0).T.astype(o_ref.dtype)


def _nsa(proj, tail, kcvc, kcvct, bias_c, overlap_t, dmax):
    B, S, _ = proj.shape
    n_chunks = S // C_CHUNK
    NC = kcvc.shape[2]
    NS = S // SLC_BLOCK
    n_tiles = bias_c.shape[0]
    Q = C_GROUP * C_CHUNK
    kern = functools.partial(_nsa_kernel, dmax=dmax, n_sel=min(SLC_TOPN, NS))
    return pl.pallas_call(
        kern,
        grid=(B, C_KV_HEADS, n_chunks),
        in_specs=[pl.BlockSpec((1, C_CHUNK, 2 * LANES), lambda b, n, c: (b, c, BLK_CQ // 2 + n)),
                  pl.BlockSpec((1, 1, NC, LANES), lambda b, n, c: (b, n, 0, 0)),
                  pl.BlockSpec((1, 1, LANES, NC), lambda b, n, c: (b, n, 0, 0)),
                  pl.BlockSpec((1, S, LANES), lambda b, n, c: (b, 0, BLK_CKV + 3 * n + 1)),
                  pl.BlockSpec((1, S, LANES), lambda b, n, c: (b, 0, BLK_CKV + 3 * n + 2)),
                  pl.BlockSpec((1, C_CHUNK, LANES), lambda b, n, c: (b, c, BLK_GATE)),
                  pl.BlockSpec((n_tiles, 1, C_CHUNK, Q), lambda b, n, c: (0, n, 0, 0)),
                  pl.BlockSpec((NS, NC), lambda b, n, c: (0, 0))],
        out_specs=pl.BlockSpec((1, C_CHUNK, 2 * LANES), lambda b, n, c: (b, c, n)),
        out_shape=jax.ShapeDtypeStruct((B, S, C_HEADS * HEAD_DIM), BF16),
        scratch_shapes=[pltpu.VMEM((S // NSA_KT, LANES, NSA_KT), BF16), pltpu.VMEM((n_chunks, LANES, LANES), BF16),
                        pltpu.VMEM((NS, C_CHUNK), F32), pltpu.VMEM((LANES, C_CHUNK), F32),
                        pltpu.VMEM((1, Q), F32), pltpu.VMEM((1, Q), F32)]
        + [pltpu.VMEM((ACC_ROWS, GROUP_W), F32)] * (2 * Q // GROUP_W),
        compiler_params=_cparams("arbitrary", "arbitrary", "arbitrary"),
        name="nsa",
    )(proj, kcvc, kcvct, tail, tail, tail, bias_c, overlap_t)


def _head_scale():
    scale = np.ones((HEAD_WIDTH,), np.float32)
    for g in range(len(A_CONFIGS)):
        scale[g * A_GROUP_WIDTH:g * A_GROUP_WIDTH + A_HEADS_PER_GROUP * HEAD_DIM] = Q_SCALE
    scale[A_WIDTH:A_WIDTH + B_HEADS * HEAD_DIM] = Q_SCALE * LOG2E
    scale[A_WIDTH + B_WIDTH:] = Q_SCALE * LOG2E
    return scale


def _split_w_in(w_in):
    depth, D, _ = w_in.shape
    head = (w_in[:, :, :HEAD_WIDTH] * _head_scale()).astype(BF16)
    kv = w_in[:, :, HEAD_WIDTH:HEAD_WIDTH + C_KV_WIDTH].reshape(depth, D, 3, 2, C_KV_HEADS, HEAD_DIM)
    kv = kv.transpose(0, 1, 4, 2, 3, 5).reshape(depth, D, C_KV_WIDTH)
    pad = jnp.zeros((depth, D, TAIL_WIDTH - C_KV_WIDTH - C_GATE_WIDTH), w_in.dtype)
    tail = jnp.concatenate([kv, w_in[:, :, HEAD_WIDTH + C_KV_WIDTH:], pad], axis=2).astype(BF16)
    return head, tail


def _compress_weights(cmp_w1, cmp_w2, cmp_pe):
    half = CMP_LEN // 2
    w1 = cmp_w1.reshape(2, 2, half, HEAD_DIM, CMP_HIDDEN)
    w1_blk = jnp.zeros((2, half, 2, HEAD_DIM, 2, CMP_HIDDEN), F32)
    w1_blk = w1_blk.at[:, :, 0, :, 0, :].set(w1[0])
    w1_blk = w1_blk.at[:, :, 1, :, 1, :].set(w1[1])
    w1_blk = w1_blk.reshape(2, half * 2 * HEAD_DIM, 2 * CMP_HIDDEN).astype(BF16)
    w2_blk = jnp.zeros((2, CMP_HIDDEN, 2, HEAD_DIM), F32)
    w2_blk = w2_blk.at[0, :, 0, :].set(cmp_w2[0])
    w2_blk = w2_blk.at[1, :, 1, :].set(cmp_w2[1])
    w2_blk = w2_blk.reshape(2 * CMP_HIDDEN, 2 * HEAD_DIM).astype(BF16)
    pe = cmp_pe.reshape(2, 2, half, HEAD_DIM)
    pe_rows = pe.transpose(1, 2, 0, 3).reshape(2, half * 2 * HEAD_DIM)
    return w1_blk, w2_blk, pe_rows


def kernel(x, rel_table, w_in, w_out, cmp_w1, cmp_w2, cmp_pe, norm_attn, norm_mlp,
           w_up, conv_w, conv_b, w_down, norm_final):
    B, S, D = x.shape
    depth = w_in.shape[0]
    T = B * S

    a = jnp.arange(A_BLOCK)[:, None]
    bk = jnp.arange(2 * A_BLOCK)[None, :]
    rel = a + A_BLOCK - bk
    idx_a = jnp.stack([_t5_bucket(rel * d) for _, d in A_CONFIGS]).astype(jnp.int32)
    bias_a = _bias_tiles(rel_table[:A_HEADS], idx_a)
    dmax_b = _n_far_tiles(S // MOBA_BLOCK, MOBA_BLOCK)
    bias_b = _bias_tiles(rel_table[A_HEADS:A_HEADS + B_HEADS] * LOG2E,
                         _toeplitz_bucket_tiles_t(dmax_b + 1, MOBA_BLOCK), 2)
    dmax_c = _n_far_tiles(S // C_CHUNK, C_CHUNK)
    bias_c = _bias_tiles(rel_table[A_HEADS + B_HEADS:] * LOG2E,
                         _toeplitz_bucket_tiles_t(dmax_c + 1, C_CHUNK), C_GROUP)

    n16 = S // CMP_STRIDE
    ci = jnp.arange(n16)[None, :] * CMP_STRIDE
    sj = jnp.arange(S // SLC_BLOCK)[:, None] * SLC_BLOCK
    overlap_t = ((ci < sj + SLC_BLOCK) & (ci + CMP_LEN > sj)).astype(BF16)

    w_head, w_tail = _split_w_in(w_in)
    w_out_b = w_out.astype(BF16)
    w_up_b = w_up.astype(BF16)
    w_down_b = w_down.astype(BF16)
    gains_attn = norm_attn.reshape(depth, 1, D)
    gains_mlp = norm_mlp.reshape(depth, 1, D)
    conv_b3 = conv_b.reshape(depth, 1, -1)

    x2 = x.reshape(T, D)
    h = _norm(x2, gains_attn, 0)
    for l in range(depth):
        proj = _inproj(h, w_head, l, tn=512).reshape(B, S, HEAD_WIDTH)
        tail = _inproj(h, w_tail, l, tn=TAIL_WIDTH // 2).reshape(B, S, TAIL_WIDTH)
        os_, lses = [], []
        for g, (_, d) in enumerate(A_CONFIGS):
            o, lse = _dilated_group(proj, bias_a, g, d)
            os_.append(o)
            lses.append(lse)
        o_a = _combine_groups(os_, lses)
        o_b = _moba(proj, bias_b, dmax_b).reshape(T, B_HEADS * HEAD_DIM)
        w1_blk, w2_blk, pe_rows = _compress_weights(cmp_w1[l], cmp_w2[l], cmp_pe[l])
        cmp_cols = tail[:, :, :BLK_GATE * LANES].reshape(B, S, C_KV_HEADS, 3, LANES)[:, :, :, 0]
        sub = cmp_cols.transpose(0, 2, 1, 3).reshape(B, C_KV_HEADS, n16, CMP_STRIDE * LANES)
        kcvc, kcvct = _compress(sub, pe_rows, w1_blk, w2_blk)
        o_c = _nsa(proj, tail, kcvc, kcvct, bias_c, overlap_t, dmax_c).reshape(T, C_HEADS * HEAD_DIM)
        x2, h = _outproj(x2, o_a, o_b, o_c, w_out_b, gains_mlp, l)
        last = l == depth - 1
        next_gain = norm_final.reshape(1, D) if last else norm_attn[l + 1].reshape(1, D)
        x2, h = _ffn(x2, h, w_up_b, conv_w, conv_b3, w_down_b, l, next_gain, F32 if last else BF16, S)
    return h.reshape(B, S, D)
```

```python
import functools
import math

import numpy as np
import jax
import jax.numpy as jnp
from jax import lax
from jax.experimental import pallas as pl
from jax.experimental.pallas import tpu as pltpu

F32 = jnp.float32
BF16 = jnp.bfloat16

HEAD_DIM = 64
LANES = 128
A_CONFIGS = ((128, 1), (512, 4), (2048, 16))
A_HEADS_PER_GROUP = 4
A_HEADS = 12
A_BLOCK = 128
B_HEADS = 8
MOBA_BLOCK = 256
MOBA_TOPK = 3
MOBA_UNROLL = 4
C_KV_HEADS = 3
C_GROUP = 4
C_HEADS = 12
CMP_STRIDE = 16
CMP_LEN = 32
CMP_HIDDEN = 128
SLC_BLOCK = 64
SLC_TOPN = 16
N_FORCED = 3
IMP_SPLITS = 3
WIN = 512
FORCE_SCORE = 1e9
N_BUCKETS = 32
T5_MAX_DIST = 2048
EPS = 1e-6
NEG = -1e30
M_FLOOR = -1e20
Q_SCALE = HEAD_DIM ** -0.5
LOG2E = math.log2(math.e)
C_CHUNK = 128
ACC_ROWS = 128
GROUP_W = 128
NSA_KT = 256
NSA_UNROLL = 4

A_WIDTH = 3 * A_HEADS * HEAD_DIM
A_GROUP_WIDTH = 3 * A_HEADS_PER_GROUP * HEAD_DIM
B_WIDTH = 3 * B_HEADS * HEAD_DIM
C_Q_WIDTH = C_HEADS * HEAD_DIM
C_KV_WIDTH = 6 * C_KV_HEADS * HEAD_DIM
C_GATE_WIDTH = 3 * C_HEADS
IN_WIDTH = A_WIDTH + B_WIDTH + C_Q_WIDTH + C_KV_WIDTH + C_GATE_WIDTH
BLK_A = 0
BLK_B = A_WIDTH // LANES
BLK_CQ = BLK_B + B_WIDTH // LANES
HEAD_WIDTH = A_WIDTH + B_WIDTH + C_Q_WIDTH
BLK_CKV = 0
BLK_GATE = 3 * C_KV_HEADS
TAIL_WIDTH = (BLK_GATE + 1) * LANES

VMEM_LIMIT = 56 * 1024 * 1024


def _cparams(*sem):
    return pltpu.CompilerParams(dimension_semantics=sem, vmem_limit_bytes=VMEM_LIMIT)


def _dot_nt(a, b):
    return lax.dot_general(a, b, (((1,), (1,)), ((), ())), preferred_element_type=F32)


def _dot(a, b, **kw):
    return jnp.dot(a, b, preferred_element_type=F32, **kw)


def _transpose_bf16(x):
    return x.astype(F32).T.astype(BF16)


def _iota(shape, dim):
    return lax.broadcasted_iota(jnp.int32, shape, dim)


def _t5_bucket(dist):
    n = jnp.maximum(dist, 0)
    max_exact = N_BUCKETS // 2
    nf = jnp.maximum(n, 1).astype(F32)
    large = max_exact + (jnp.log(nf / max_exact) / math.log(T5_MAX_DIST / max_exact)
                         * (N_BUCKETS - max_exact)).astype(jnp.int32)
    large = jnp.minimum(large, N_BUCKETS - 1)
    return jnp.where(n < max_exact, n, large)


def _bias_lookup_kernel(tbl_ref, idx_ref, o_ref, *, hpr):
    idx = idx_ref[0]
    C = idx.shape[1]
    for h in range(tbl_ref.shape[0]):
        acc = jnp.zeros(idx.shape, F32)
        for b in range(N_BUCKETS):
            acc = jnp.where(idx == b, tbl_ref[h, b], acc)
        o_ref[0, h // hpr, :, (h % hpr) * C:(h % hpr + 1) * C] = acc


def _bias_tiles(tbl, idx, heads_per_row=1):
    n, R, C = idx.shape
    H = tbl.shape[0]
    hpr = heads_per_row
    return pl.pallas_call(
        functools.partial(_bias_lookup_kernel, hpr=hpr),
        grid=(n,),
        in_specs=[pl.BlockSpec(memory_space=pltpu.SMEM),
                  pl.BlockSpec((1, R, C), lambda t: (t, 0, 0))],
        out_specs=pl.BlockSpec((1, H // hpr, R, hpr * C), lambda t: (t, 0, 0, 0)),
        out_shape=jax.ShapeDtypeStruct((n, H // hpr, R, hpr * C), F32),
        compiler_params=_cparams("arbitrary"),
        name="bias_tiles",
    )(tbl, idx)


def _toeplitz_bucket_tiles_t(n_tiles, size):
    d = jnp.arange(n_tiles)[:, None, None] * size
    key = jnp.arange(size)[None, :, None]
    qry = jnp.arange(size)[None, None, :]
    return _t5_bucket(d + qry - key).astype(jnp.int32)


def _far_delta(size):
    return -(-(T5_MAX_DIST + size - 1) // size)


def _n_far_tiles(n_chunks, size):
    return min(_far_delta(size), n_chunks - 1)


def _norm_rows(x, g):
    ms = jnp.mean(x * x, axis=-1, keepdims=True)
    return x * lax.rsqrt(ms + EPS) * g


def _norm_kernel(x_ref, g_ref, o_ref):
    o_ref[...] = _norm_rows(x_ref[...], g_ref[...]).astype(o_ref.dtype)


def _norm(x2, gains, l, tm=1024):
    T, D = x2.shape
    return pl.pallas_call(
        _norm_kernel,
        grid=(T // tm,),
        in_specs=[pl.BlockSpec((tm, D), lambda i: (i, 0)),
                  pl.BlockSpec((None, 1, D), lambda i: (l, 0, 0))],
        out_specs=pl.BlockSpec((tm, D), lambda i: (i, 0)),
        out_shape=jax.ShapeDtypeStruct((T, D), BF16),
        compiler_params=_cparams("arbitrary"),
        name="first_norm",
    )(x2, gains)


def _inproj_kernel(h_ref, w_ref, o_ref):
    o_ref[...] = _dot(h_ref[...], w_ref[...]).astype(o_ref.dtype)


def _inproj(h, w_all, l, N, tn, tm=1024):
    T, D = h.shape
    return pl.pallas_call(
        _inproj_kernel,
        grid=(T // tm, N // tn),
        in_specs=[pl.BlockSpec((tm, D), lambda i, j: (i, 0)),
                  pl.BlockSpec((None, D, tn), lambda i, j: (l, 0, j))],
        out_specs=pl.BlockSpec((tm, tn), lambda i, j: (i, j)),
        out_shape=jax.ShapeDtypeStruct((T, N), BF16),
        compiler_params=_cparams("arbitrary", "arbitrary"),
        name="inproj",
    )(h, w_all)


def _outproj_kernel(x_ref, a_ref, b_ref, c_ref, w_ref, g_ref, o_ref, h_ref):
    ra, rb = a_ref.shape[1], b_ref.shape[1]
    acc = _dot(a_ref[...], w_ref[pl.ds(0, ra), :])
    acc += _dot(b_ref[...], w_ref[pl.ds(ra, rb), :])
    acc += _dot(c_ref[...], w_ref[pl.ds(ra + rb, c_ref.shape[1]), :])
    x_new = x_ref[...] + acc
    o_ref[...] = x_new
    h_ref[...] = _norm_rows(x_new, g_ref[...]).astype(h_ref.dtype)


def _outproj(x2, oa, ob, oc, w_all, gains, l, tm=512):
    T, D = x2.shape
    row = lambda i: (i, 0)
    return pl.pallas_call(
        _outproj_kernel,
        grid=(T // tm,),
        in_specs=[pl.BlockSpec((tm, D), row),
                  pl.BlockSpec((tm, oa.shape[1]), row),
                  pl.BlockSpec((tm, ob.shape[1]), row),
                  pl.BlockSpec((tm, oc.shape[1]), row),
                  pl.BlockSpec((None, w_all.shape[1], D), lambda i: (l, 0, 0)),
                  pl.BlockSpec((None, 1, D), lambda i: (l, 0, 0))],
        out_specs=[pl.BlockSpec((tm, D), row), pl.BlockSpec((tm, D), row)],
        out_shape=[jax.ShapeDtypeStruct((T, D), F32), jax.ShapeDtypeStruct((T, D), BF16)],
        compiler_params=_cparams("arbitrary"),
        name="outproj",
    )(x2, oa, ob, oc, w_all, gains)


FFN_HALO = 16


def _ffn_kernel(x_ref, h_ref, halo_ref, wa_ref, wg_ref, cwa_ref, cwg_ref, cba_ref, cbg_ref, wd_ref, g_ref,
                o_ref, n_ref, h_scr, ua_scr, ug_scr, *, tm, blocks_per_seq):
    i = pl.program_id(0)
    f = pl.program_id(1)

    @pl.when(f == 0)
    def _():
        o_ref[...] = x_ref[...]
        keep = jnp.where(i % blocks_per_seq == 0, 0.0, 1.0).astype(BF16)
        h_scr[pl.ds(0, FFN_HALO), :] = halo_ref[0] * keep
        h_scr[pl.ds(FFN_HALO, tm), :] = h_ref[...]

    h = h_scr[...]
    ua_scr[...] = _dot(h, wa_ref[...])
    ug_scr[...] = _dot(h, wg_ref[...])

    def conv(u_scr, cw_ref, cb_ref):
        acc = cb_ref[...]
        for j in range(3):
            acc = acc + cw_ref[pl.ds(j, 1), :] * u_scr[pl.ds(FFN_HALO - j, tm), :]
        return acc

    ya = conv(ua_scr, cwa_ref, cba_ref)
    yg = conv(ug_scr, cwg_ref, cbg_ref)
    act = (yg * jax.nn.sigmoid(yg) * ya).astype(BF16)
    o_ref[...] += _dot(act, wd_ref[...])

    @pl.when(f == pl.num_programs(1) - 1)
    def _():
        n_ref[...] = _norm_rows(o_ref[...], g_ref[...]).astype(n_ref.dtype)


def _ffn(x2, h, w_up, conv_w, conv_b, w_down, l, next_gain, next_dtype, seq, tm=512, tf=512):
    T, D = x2.shape
    Fh = w_down.shape[1]
    nf = Fh // tf
    halo_view = h.reshape(T // FFN_HALO, FFN_HALO, D)
    rb = tm // FFN_HALO
    kern = functools.partial(_ffn_kernel, tm=tm, blocks_per_seq=seq // tm)
    row = lambda i, f: (i, 0)
    return pl.pallas_call(
        kern,
        grid=(T // tm, nf),
        in_specs=[pl.BlockSpec((tm, D), row),
                  pl.BlockSpec((tm, D), row),
                  pl.BlockSpec((1, FFN_HALO, D), lambda i, f: (jnp.maximum(i * rb - 1, 0), 0, 0)),
                  pl.BlockSpec((None, D, tf), lambda i, f: (l, 0, f)),
                  pl.BlockSpec((None, D, tf), lambda i, f: (l, 0, f + nf)),
                  pl.BlockSpec((None, 3, tf), lambda i, f: (l, 0, f)),
                  pl.BlockSpec((None, 3, tf), lambda i, f: (l, 0, f + nf)),
                  pl.BlockSpec((None, 1, tf), lambda i, f: (l, 0, f)),
                  pl.BlockSpec((None, 1, tf), lambda i, f: (l, 0, f + nf)),
                  pl.BlockSpec((None, tf, D), lambda i, f: (l, f, 0)),
                  pl.BlockSpec((1, D), lambda i, f: (0, 0))],
        out_specs=[pl.BlockSpec((tm, D), row), pl.BlockSpec((tm, D), row)],
        out_shape=[jax.ShapeDtypeStruct((T, D), F32), jax.ShapeDtypeStruct((T, D), next_dtype)],
        scratch_shapes=[pltpu.VMEM((tm + FFN_HALO, D), BF16),
                        pltpu.VMEM((tm + FFN_HALO, tf), F32),
                        pltpu.VMEM((tm + FFN_HALO, tf), F32)],
        compiler_params=_cparams("arbitrary", "arbitrary"),
        name="convffn",
    )(x2, h, halo_view, w_up, w_up, conv_w, conv_w, conv_b, conv_b, w_down, next_gain)


def _stack_pair(qp):
    lane = _iota(qp.shape, 1)
    zero = jnp.zeros_like(qp)
    return jnp.concatenate([jnp.where(lane < HEAD_DIM, qp, zero),
                            jnp.where(lane >= HEAD_DIM, qp, zero)], axis=0)


def _unstack_pair(o):
    R = o.shape[0] // 2
    lane = _iota((R, LANES), 1)
    return jnp.where(lane < HEAD_DIM, o[:R], o[R:])


def _dilated_kernel(q_ref, kc_ref, kp_ref, vc_ref, vp_ref, bias_ref, o_ref, lse_ref):
    i = pl.program_id(2)
    R = 2 * A_BLOCK
    a = _iota((R, 2 * A_BLOCK), 0) % A_BLOCK
    bk = _iota((R, 2 * A_BLOCK), 1)
    rel = a + A_BLOCK - bk
    mask = (rel >= 0) & (rel <= A_BLOCK) & ((bk >= A_BLOCK) | (i > 0))
    n_pairs = q_ref.shape[2] // LANES

    def scores(p):
        cols = slice(p * LANES, (p + 1) * LANES)
        qs = _stack_pair(q_ref[0, :, cols])
        k = jnp.concatenate([kp_ref[0, :, cols], kc_ref[0, :, cols]], axis=0)
        s = _dot_nt(qs, k) + bias_ref[0, 2 * p:2 * p + 2].reshape(R, 2 * A_BLOCK)
        return jnp.where(mask, s, NEG)

    s_all = [scores(p) for p in range(n_pairs)]
    for p, s in enumerate(s_all):
        cols = slice(p * LANES, (p + 1) * LANES)
        v = jnp.concatenate([vp_ref[0, :, cols], vc_ref[0, :, cols]], axis=0)
        m = jnp.max(s, axis=1, keepdims=True)
        e = jnp.where(mask, jnp.exp(s - m), 0.0)
        den = jnp.sum(e, axis=1, keepdims=True)
        o = _dot(e.astype(BF16), v) / den
        lse = m + jnp.log(den)
        o_ref[0, :, cols] = _unstack_pair(o)
        lse_ref[0, :, cols] = _unstack_pair(jnp.broadcast_to(lse, (R, LANES)))


def _dilated_group(proj, bias_a, g, dilation):
    B, S, _ = proj.shape
    L = S // dilation
    nq = L // A_BLOCK
    W = A_HEADS_PER_GROUP * HEAD_DIM
    cols = proj[:, :, g * A_GROUP_WIDTH:(g + 1) * A_GROUP_WIDTH]
    view = cols.reshape(B, L, dilation * A_GROUP_WIDTH)

    def cur(part):
        return pl.BlockSpec((1, A_BLOCK, W), lambda b, r, i: (b, i, r * 3 + part))

    def prev(part):
        return pl.BlockSpec((1, A_BLOCK, W), lambda b, r, i: (b, jnp.maximum(i - 1, 0), r * 3 + part))

    out_spec = pl.BlockSpec((1, A_BLOCK, W), lambda b, r, i: (b, i, r))
    o, lse = pl.pallas_call(
        _dilated_kernel,
        grid=(B, dilation, nq),
        in_specs=[cur(0), cur(1), prev(1), cur(2), prev(2),
                  pl.BlockSpec((1, A_HEADS_PER_GROUP, A_BLOCK, 2 * A_BLOCK), lambda b, r, i: (g, g, 0, 0))],
        out_specs=[out_spec, out_spec],
        out_shape=[jax.ShapeDtypeStruct((B, L, dilation * W), F32)] * 2,
        compiler_params=_cparams("arbitrary", "arbitrary", "arbitrary"),
        name=f"dilated_g{g}",
    )(view, view, view, view, view, bias_a)
    return o.reshape(B * S, W), lse.reshape(B * S, W)


def _combine_kernel(o0, o1, o2, l0, l1, l2, out_ref):
    ls = [l0[...], l1[...], l2[...]]
    m = jnp.maximum(jnp.maximum(ls[0], ls[1]), ls[2])
    es = [jnp.exp(l - m) for l in ls]
    den = es[0] + es[1] + es[2]
    W = o0.shape[1]
    for g, o in enumerate((o0, o1, o2)):
        out_ref[:, g * W:(g + 1) * W] = (o[...] * (es[g] / den)).astype(out_ref.dtype)


def _combine_groups(os_, lses, tm=1024):
    T, W = os_[0].shape
    spec = pl.BlockSpec((tm, W), lambda i: (i, 0))
    return pl.pallas_call(
        _combine_kernel,
        grid=(T // tm,),
        in_specs=[spec] * 6,
        out_specs=pl.BlockSpec((tm, 3 * W), lambda i: (i, 0)),
        out_shape=jax.ShapeDtypeStruct((T, 3 * W), BF16),
        compiler_params=_cparams("arbitrary"),
        name="dilated_combine",
    )(*os_, *lses)


def _flash_init(m_scr, acc_scrs):
    m_scr[...] = jnp.full(m_scr.shape, M_FLOOR, F32)
    for acc in acc_scrs:
        acc[...] = jnp.zeros(acc.shape, F32)


def _with_ones(vt, row):
    return jnp.where(_iota(vt.shape, 0) == row, jnp.ones_like(vt), vt)


def _flash_tiles(tiles, q_groups, m_scr, acc_scrs):
    G = len(q_groups)
    W = q_groups[0].shape[0]
    m = [m_scr[:, g * W:(g + 1) * W] for g in range(G)]
    units = [(t, g) for t in range(len(tiles)) for g in range(G)]

    def scores(t, g):
        k, _, elem_of, _ = tiles[t]
        s = _dot_nt(k, q_groups[g])
        for term in elem_of(g):
            s = s + term
        return s

    def apply(g, alpha, pv):
        acc_scrs[g][...] = alpha * acc_scrs[g][...] + pv

    s_next = scores(*units[0])
    pending = None
    for n, (t, g) in enumerate(units):
        s = s_next
        if n + 1 < len(units):
            s_next = scores(*units[n + 1])
        rows = tiles[t][3](g)
        seg = s.shape[0] // len(rows)
        parts = [s[i * seg:(i + 1) * seg] for i in range(len(rows))]
        m_new = m[g]
        for part, row in zip(parts, rows):
            m_new = jnp.maximum(m_new, jnp.max(part, axis=0, keepdims=True) + row)
        alpha = jnp.exp2(m[g] - m_new)
        p = jnp.concatenate([jnp.exp2(part - (m_new - row)) for part, row in zip(parts, rows)], axis=0)
        m[g] = m_new
        pv = _dot(tiles[t][1](g), p.astype(BF16))
        if pending is not None:
            apply(*pending)
        pending = (g, alpha, pv)
    apply(*pending)
    m_scr[...] = jnp.concatenate(m, axis=1)


PICKED = -3e38


def _top_picks(score, n_pick):
    idx = _iota(score.shape, 0).astype(F32)

    def pick(_, g):
        mx = jnp.max(g, axis=0, keepdims=True)
        first = jnp.min(jnp.where(g == mx, idx, 1e9), axis=0, keepdims=True)
        return jnp.where(idx == first, PICKED, g)

    return lax.fori_loop(0, n_pick, pick, score) == PICKED


def _moba_kernel(q_ref, k_ref, v_ref, bias_ref, o_ref, km_scr, vt_scr, sel_scr, m_scr, *acc_scrs,
                 dmax, n_blocks):
    c = pl.program_id(2)
    BLK = MOBA_BLOCK
    Q = 2 * BLK

    @pl.when(c == 0)
    def _():
        def body(j, _):
            rows = pl.ds(pl.multiple_of(j * BLK, BLK), BLK)
            km_scr[pl.ds(j, 1), :] = jnp.mean(k_ref[0, rows, :].astype(F32), axis=0, keepdims=True)
            vt_scr[j] = _transpose_bf16(v_ref[0, rows, :])
            return 0
        lax.fori_loop(0, n_blocks, body, 0)

    qs = _stack_pair(q_ref[0])
    gate = _dot_nt(km_scr[...].astype(BF16), qs)
    blk = _iota(gate.shape, 0)
    gate = jnp.where(blk < c, gate, NEG)
    sel = _top_picks(gate, min(MOBA_TOPK, n_blocks))
    sel_scr[...] = jnp.where((blk < c) & sel, 0.0, NEG)

    _flash_init(m_scr, acc_scrs)
    per_head = BLK // GROUP_W
    q_groups = [qs[g * GROUP_W:(g + 1) * GROUP_W] for g in range(Q // GROUP_W)]

    def head_values(vt):
        both = [_with_ones(vt, (1 - h) * HEAD_DIM) for h in range(2)]
        return lambda g: both[g // per_head]

    def far_rows(g):
        return bias_ref[dmax, 0, pl.ds(0, 1), pl.ds(g * GROUP_W, GROUP_W)]

    def past_tile(j, far):
        jc = jnp.minimum(j, n_blocks - 1)
        kj = k_ref[0, pl.ds(pl.multiple_of(jc * BLK, BLK), BLK), :]
        tile = jnp.clip(c - j, 0, dmax)
        selrow = sel_scr[pl.ds(jc, 1), :]

        def elem(g):
            return () if far else (bias_ref[tile, 0, :, pl.ds(g * GROUP_W, GROUP_W)],)

        def rows(g):
            row = selrow[:, g * GROUP_W:(g + 1) * GROUP_W]
            return (row + far_rows(g),) if far else (row,)

        return kj, head_values(vt_scr[jc]), elem, rows

    def body(far):
        def run(it, _):
            _flash_tiles([past_tile(MOBA_UNROLL * it + u, far) for u in range(MOBA_UNROLL)],
                         q_groups, m_scr, acc_scrs)
            return 0
        return run

    n_far_iters = jnp.maximum(c - _far_delta(BLK) + 1, 0) // MOBA_UNROLL
    lax.fori_loop(0, n_far_iters, body(True), 0)
    lax.fori_loop(n_far_iters, (c + MOBA_UNROLL - 1) // MOBA_UNROLL, body(False), 0)

    ko = k_ref[0, pl.ds(pl.multiple_of(c * BLK, BLK), BLK), :]

    def own_elem(g):
        qpos = (g * GROUP_W + _iota((BLK, GROUP_W), 1)) % BLK
        return (bias_ref[0, 0, :, pl.ds(g * GROUP_W, GROUP_W)],
                jnp.where(_iota((BLK, GROUP_W), 0) <= qpos, 0.0, NEG))

    zero_row = jnp.zeros((1, GROUP_W), F32)
    _flash_tiles([(ko, head_values(vt_scr[c]), own_elem, lambda g: (zero_row,))], q_groups, m_scr, acc_scrs)
    outs = []
    for g, acc in enumerate(acc_scrs):
        h = g // per_head
        a = acc[...]
        den = a[(1 - h) * HEAD_DIM:(1 - h) * HEAD_DIM + 1]
        outs.append(a[h * HEAD_DIM:(h + 1) * HEAD_DIM] / den)
    o2 = jnp.concatenate([jnp.concatenate(outs[h * per_head:(h + 1) * per_head], axis=1) for h in range(2)],
                         axis=0)
    o_ref[0] = o2.T.astype(o_ref.dtype)


def _moba(proj, bias_b, dmax):
    B, S, _ = proj.shape
    NB = S // MOBA_BLOCK
    n_tiles = bias_b.shape[0]
    qb, kb, vb = BLK_B, BLK_B + 4, BLK_B + 8
    kern = functools.partial(_moba_kernel, dmax=dmax, n_blocks=NB)
    Q = 2 * MOBA_BLOCK
    return pl.pallas_call(
        kern,
        grid=(B, B_HEADS // 2, NB),
        in_specs=[pl.BlockSpec((1, MOBA_BLOCK, LANES), lambda b, p, c: (b, c, qb + p)),
                  pl.BlockSpec((1, S, LANES), lambda b, p, c: (b, 0, kb + p)),
                  pl.BlockSpec((1, S, LANES), lambda b, p, c: (b, 0, vb + p)),
                  pl.BlockSpec((n_tiles, 1, MOBA_BLOCK, Q), lambda b, p, c: (0, p, 0, 0))],
        out_specs=pl.BlockSpec((1, MOBA_BLOCK, LANES), lambda b, p, c: (b, c, p)),
        out_shape=jax.ShapeDtypeStruct((B, S, B_HEADS * HEAD_DIM), BF16),
        scratch_shapes=[pltpu.VMEM((NB, LANES), F32), pltpu.VMEM((NB, LANES, MOBA_BLOCK), BF16),
                        pltpu.VMEM((NB, Q), F32), pltpu.VMEM((1, Q), F32)]
        + [pltpu.VMEM((ACC_ROWS, GROUP_W), F32)] * (Q // GROUP_W),
        compiler_params=_cparams("arbitrary", "arbitrary", "arbitrary"),
        name="moba",
    )(proj, proj, proj, bias_b)


def _gelu_tanh(x):
    return 0.5 * x * (1.0 + jnp.tanh(math.sqrt(2.0 / math.pi) * (x + 0.044715 * (x * x * x))))


def _compress_kernel(sub_ref, pe_ref, w1_ref, w2_ref, o_ref, ot_ref, v_scr):
    n16 = sub_ref.shape[2]
    sub = sub_ref[0, 0].astype(F32)
    top = (sub + pe_ref[0:1, :]).astype(BF16)
    bot = (sub + pe_ref[1:2, :]).astype(BF16)
    u = _dot(top, w1_ref[0])
    v_scr[pl.ds(0, n16), :] = _dot(bot, w1_ref[1])
    v_scr[pl.ds(n16, 8), :] = jnp.zeros((8, v_scr.shape[1]), F32)
    hidden = u + v_scr[pl.ds(1, n16), :]
    out = _dot(_gelu_tanh(hidden).astype(BF16), w2_ref[...])
    o_ref[0, 0] = out.astype(o_ref.dtype)
    ot_ref[0, 0] = out.T.astype(ot_ref.dtype)


def _compress(sub, pe_rows, w1_blk, w2_blk):
    B, KV, n16, W = sub.shape
    return pl.pallas_call(
        _compress_kernel,
        grid=(B, KV),
        in_specs=[pl.BlockSpec((1, 1, n16, W), lambda b, n: (b, n, 0, 0)),
                  pl.BlockSpec((2, W), lambda b, n: (0, 0)),
                  pl.BlockSpec((2, W, 2 * CMP_HIDDEN), lambda b, n: (0, 0, 0)),
                  pl.BlockSpec((2 * CMP_HIDDEN, LANES), lambda b, n: (0, 0))],
        out_specs=[pl.BlockSpec((1, 1, n16, LANES), lambda b, n: (b, n, 0, 0)),
                   pl.BlockSpec((1, 1, LANES, n16), lambda b, n: (b, n, 0, 0))],
        out_shape=[jax.ShapeDtypeStruct((B, KV, n16, LANES), BF16),
                   jax.ShapeDtypeStruct((B, KV, LANES, n16), BF16)],
        scratch_shapes=[pltpu.VMEM((n16 + 8, 2 * CMP_HIDDEN), F32)],
        compiler_params=_cparams("arbitrary", "arbitrary"),
        name="nsa_compress",
    )(sub, pe_rows, w1_blk, w2_blk)


def _nsa_kernel(q_ref, kc_ref, kct_ref, slc_ref, win_ref, gate_ref, bias_ref, ovt_ref,
                o_ref, slct_scr, wint_scr, ocmp_scr, imp_scr, sel_scr, g_scr, m_slc, m_win, *acc_scrs, dmax, n_sel):
    n = pl.program_id(1)
    c = pl.program_id(2)
    CQ = C_CHUNK
    lane = _iota((CQ, LANES), 1)
    halves = NSA_KT // LANES

    @pl.when(c == 0)
    def _():
        def body(t, _):
            slct_scr[t] = _transpose_bf16(slc_ref[0, pl.ds(pl.multiple_of(t * NSA_KT, NSA_KT), NSA_KT), :])
            for h in range(halves):
                rows = pl.ds(pl.multiple_of(t * NSA_KT + h * LANES, LANES), LANES)
                wint_scr[halves * t + h] = _transpose_bf16(win_ref[0, rows, :])
            return 0
        lax.fori_loop(0, slct_scr.shape[0], body, 0)

    q = q_ref[0].astype(F32)
    heads = []
    for pr in range(2):
        qp = q[:, pr * LANES:(pr + 1) * LANES]
        heads.append(jnp.where(lane < HEAD_DIM, qp, 0.0))
        heads.append(jnp.where(lane < HEAD_DIM, pltpu.roll(qp, HEAD_DIM, 1), 0.0))
    q_heads = [h.astype(BF16) for h in heads]

    NC = kc_ref.shape[2]
    NS = ovt_ref.shape[0]

    def cmp_branch(nk):
        kc = kc_ref[0, 0, pl.ds(0, nk), :]
        kct = kct_ref[0, 0, :, pl.ds(0, nk)]
        visible = (_iota((nk, CQ), 0) * CMP_STRIDE + (CMP_LEN - 1)) <= c * CQ + _iota((nk, CQ), 1)
        psum = jnp.zeros((nk, CQ), F32)
        s_next = _dot_nt(kc, q_heads[0])
        for gi in range(C_GROUP):
            s = jnp.where(visible, s_next, NEG)
            if gi + 1 < C_GROUP:
                s_next = _dot_nt(kc, q_heads[gi + 1])
            m = jnp.max(s, axis=0, keepdims=True)
            e = jnp.where(visible, jnp.exp2(s - m), 0.0)
            den = jnp.sum(e, axis=0, keepdims=True)
            p = e / jnp.maximum(den, 1e-30)
            ocmp_scr[gi] = _dot(kct, p.astype(BF16))[HEAD_DIM:]
            psum = psum + p
        ovt = ovt_ref[:, pl.ds(0, nk)]
        imp = jnp.zeros((NS, CQ), F32)
        rest = psum
        for _ in range(IMP_SPLITS):
            piece = rest.astype(BF16)
            imp = imp + _dot(ovt, piece)
            rest = rest - piece.astype(F32)
        imp_scr[...] = imp

    half_chunks = (NC // 2) // (CQ // CMP_STRIDE)

    @pl.when(c < half_chunks)
    def _():
        cmp_branch(NC // 2)

    @pl.when(c >= half_chunks)
    def _():
        cmp_branch(NC)

    imp = imp_scr[...]
    jj = _iota((NS, CQ), 0)
    own = (c * CQ + _iota((NS, CQ), 1)) // SLC_BLOCK
    forced = (jj == 0) | (jj == own) | (jj == own - 1)
    score = jnp.where(forced, -2.0, jnp.where(jj <= own, imp, -1.0))
    picked = _top_picks(score, max(n_sel - N_FORCED, 0))
    sel_scr[...] = jnp.where(forced | picked, 0.0, NEG)

    reps = GROUP_W // CQ
    n_wide = C_GROUP // reps
    q_groups = [jnp.concatenate(q_heads[w * reps:(w + 1) * reps], axis=0) for w in range(n_wide)]
    acc_slc = acc_scrs[:n_wide]
    acc_win = acc_scrs[n_wide:]

    _flash_init(m_slc, acc_slc)
    per_tile = NSA_KT // SLC_BLOCK
    n_kt = slct_scr.shape[0]
    key_row2 = _iota((NSA_KT, CQ), 0)
    q_col2 = _iota((NSA_KT, CQ), 1)

    def slc_tile(kt, kind):
        ktc = jnp.minimum(kt, n_kt - 1)
        k = slc_ref[0, pl.ds(pl.multiple_of(ktc * NSA_KT, NSA_KT), NSA_KT), :]
        r = sel_scr[pl.ds(pl.multiple_of(per_tile * ktc, per_tile), per_tile), :]
        delta = c - halves * kt
        causal = jnp.where(kt * NSA_KT + key_row2 <= c * CQ + q_col2, 0.0, NEG) if kind == "diag" else None

        def elem(g):
            if kind == "far":
                return ()
            cols = pl.ds(g * GROUP_W, GROUP_W)
            bias = jnp.concatenate([bias_ref[jnp.clip(delta - h, 0, dmax), 0, :, cols] for h in range(halves)],
                                   axis=0)
            return (bias,) if causal is None else (bias, jnp.concatenate([causal] * reps, axis=1))

        def rows(g):
            out = []
            for i in range(per_tile):
                row = jnp.concatenate([r[i:i + 1, :]] * reps, axis=1)
                if kind == "far":
                    row = row + bias_ref[dmax, 0, pl.ds(0, 1), pl.ds(g * GROUP_W, GROUP_W)]
                out.append(row)
            return out

        vt = _with_ones(slct_scr[ktc], 0)
        return k, lambda g: vt, elem, rows

    def slc_iter(it, kind):
        _flash_tiles([slc_tile(NSA_UNROLL * it + u, kind) for u in range(NSA_UNROLL)], q_groups, m_slc, acc_slc)

    def slc_body(kind):
        def run(it, _):
            slc_iter(it, kind)
            return 0
        return run

    keys_per_iter = NSA_UNROLL * NSA_KT
    last_iter = (c * CQ) // keys_per_iter
    n_far_tiles = jnp.maximum(c - (_far_delta(LANES) + halves - 1) + halves, 0) // halves
    n_far_iters = n_far_tiles // NSA_UNROLL
    lax.fori_loop(0, n_far_iters, slc_body("far"), 0)
    lax.fori_loop(n_far_iters, last_iter, slc_body("near"), 0)
    slc_iter(last_iter, "diag")

    _flash_init(m_win, acc_win)
    n_wt = wint_scr.shape[0]
    first = WIN // LANES
    key_row = _iota((LANES, CQ), 0)
    q_col = _iota((LANES, CQ), 1)

    def win_tile(w):
        ks, vts, negs, dds = [], [], [], []
        for h in range(halves):
            dd = first - halves * w - h
            kt = c - dd
            ktc = jnp.clip(kt, 0, n_wt - 1)
            ks.append(win_ref[0, pl.ds(pl.multiple_of(ktc * LANES, LANES), LANES), :])
            vts.append(wint_scr[ktc])
            dist = dd * LANES + q_col - key_row
            negs.append(jnp.where((dist >= 0) & (dist < WIN) & (kt >= 0), 0.0, NEG))
            dds.append(max(dd, 0))
        negw = jnp.concatenate([jnp.concatenate(negs, axis=0)] * reps, axis=1)

        def elem(g):
            cols = pl.ds(g * GROUP_W, GROUP_W)
            return jnp.concatenate([bias_ref[dd, 0, :, cols] for dd in dds], axis=0), negw

        vt = _with_ones(jnp.concatenate(vts, axis=1), 0)
        return jnp.concatenate(ks, axis=0), lambda g: vt, elem, lambda g: (zero_row,)

    zero_row = jnp.zeros((1, GROUP_W), F32)
    n_win_tiles = -(-(WIN + CQ) // NSA_KT)
    _flash_tiles([win_tile(w) for w in range(n_win_tiles)], q_groups, m_win, acc_win)

    g_scr[...] = jax.nn.sigmoid(gate_ref[0].astype(F32)).T
    res = []
    for gi in range(C_GROUP):
        col0 = (n * C_GROUP + gi) * 3
        wcols = pl.ds((gi % reps) * CQ, CQ)
        a_s = acc_slc[gi // reps][:, wcols]
        a_w = acc_win[gi // reps][:, wcols]
        tot = (g_scr[pl.ds(col0, 1), :] * ocmp_scr[gi]
               + g_scr[pl.ds(col0 + 1, 1), :] * (a_s[HEAD_DIM:] / a_s[0:1])
               + g_scr[pl.ds(col0 + 2, 1), :] * (a_w[HEAD_DIM:] / a_w[0:1]))
        res.append(tot)
    o_ref[0] = jnp.concatenate(res, axis=0).T.astype(o_ref.dtype)


def _nsa(proj, tail, kcvc, kcvct, bias_c, overlap_t, dmax):
    B, S, _ = proj.shape
    n_chunks = S // C_CHUNK
    NC = kcvc.shape[2]
    NS = S // SLC_BLOCK
    n_tiles = bias_c.shape[0]
    Q = C_GROUP * C_CHUNK
    kern = functools.partial(_nsa_kernel, dmax=dmax, n_sel=min(SLC_TOPN, NS))
    return pl.pallas_call(
        kern,
        grid=(B, C_KV_HEADS, n_chunks),
        in_specs=[pl.BlockSpec((1, C_CHUNK, 2 * LANES), lambda b, n, c: (b, c, BLK_CQ // 2 + n)),
                  pl.BlockSpec((1, 1, NC, LANES), lambda b, n, c: (b, n, 0, 0)),
                  pl.BlockSpec((1, 1, LANES, NC), lambda b, n, c: (b, n, 0, 0)),
                  pl.BlockSpec((1, S, LANES), lambda b, n, c: (b, 0, BLK_CKV + 3 * n + 1)),
                  pl.BlockSpec((1, S, LANES), lambda b, n, c: (b, 0, BLK_CKV + 3 * n + 2)),
                  pl.BlockSpec((1, C_CHUNK, LANES), lambda b, n, c: (b, c, BLK_GATE)),
                  pl.BlockSpec((n_tiles, 1, C_CHUNK, Q), lambda b, n, c: (0, n, 0, 0)),
                  pl.BlockSpec((NS, NC), lambda b, n, c: (0, 0))],
        out_specs=pl.BlockSpec((1, C_CHUNK, 2 * LANES), lambda b, n, c: (b, c, n)),
        out_shape=jax.ShapeDtypeStruct((B, S, C_HEADS * HEAD_DIM), BF16),
        scratch_shapes=[pltpu.VMEM((S // NSA_KT, LANES, NSA_KT), BF16), pltpu.VMEM((n_chunks, LANES, LANES), BF16),
                        pltpu.VMEM((C_GROUP, HEAD_DIM, C_CHUNK), F32), pltpu.VMEM((NS, C_CHUNK), F32),
                        pltpu.VMEM((NS, C_CHUNK), F32), pltpu.VMEM((LANES, C_CHUNK), F32),
                        pltpu.VMEM((1, Q), F32), pltpu.VMEM((1, Q), F32)]
        + [pltpu.VMEM((ACC_ROWS, GROUP_W), F32)] * (2 * Q // GROUP_W),
        compiler_params=_cparams("arbitrary", "arbitrary", "arbitrary"),
        name="nsa",
    )(proj, kcvc, kcvct, tail, tail, tail, bias_c, overlap_t)


def _head_scale():
    scale = np.ones((HEAD_WIDTH,), np.float32)
    for g in range(len(A_CONFIGS)):
        scale[g * A_GROUP_WIDTH:g * A_GROUP_WIDTH + A_HEADS_PER_GROUP * HEAD_DIM] = Q_SCALE
    scale[A_WIDTH:A_WIDTH + B_HEADS * HEAD_DIM] = Q_SCALE * LOG2E
    scale[A_WIDTH + B_WIDTH:] = Q_SCALE * LOG2E
    return scale


def _split_w_in(w_in):
    depth, D, _ = w_in.shape
    scale = np.concatenate([_head_scale(), np.ones((IN_WIDTH - HEAD_WIDTH,), np.float32)])
    w = (w_in * scale).astype(BF16)
    kv = w[:, :, HEAD_WIDTH:HEAD_WIDTH + C_KV_WIDTH].reshape(depth, D, 3, 2, C_KV_HEADS, HEAD_DIM)
    kv = kv.transpose(0, 1, 4, 2, 3, 5).reshape(depth, D, C_KV_WIDTH)
    pad = jnp.zeros((depth, D, TAIL_WIDTH - C_KV_WIDTH - C_GATE_WIDTH), BF16)
    tail = jnp.concatenate([kv, w[:, :, HEAD_WIDTH + C_KV_WIDTH:], pad], axis=2)
    return w, tail


def _compress_weights(cmp_w1, cmp_w2, cmp_pe):
    half = CMP_LEN // 2
    w1 = cmp_w1.reshape(2, 2, half, HEAD_DIM, CMP_HIDDEN)
    w1_blk = jnp.zeros((2, half, 2, HEAD_DIM, 2, CMP_HIDDEN), F32)
    w1_blk = w1_blk.at[:, :, 0, :, 0, :].set(w1[0])
    w1_blk = w1_blk.at[:, :, 1, :, 1, :].set(w1[1])
    w1_blk = w1_blk.reshape(2, half * 2 * HEAD_DIM, 2 * CMP_HIDDEN).astype(BF16)
    w2_blk = jnp.zeros((2, CMP_HIDDEN, 2, HEAD_DIM), F32)
    w2_blk = w2_blk.at[0, :, 0, :].set(cmp_w2[0])
    w2_blk = w2_blk.at[1, :, 1, :].set(cmp_w2[1])
    w2_blk = w2_blk.reshape(2 * CMP_HIDDEN, 2 * HEAD_DIM).astype(BF16)
    pe = cmp_pe.reshape(2, 2, half, HEAD_DIM)
    pe_rows = pe.transpose(1, 2, 0, 3).reshape(2, half * 2 * HEAD_DIM)
    return w1_blk, w2_blk, pe_rows


def kernel(x, rel_table, w_in, w_out, cmp_w1, cmp_w2, cmp_pe, norm_attn, norm_mlp,
           w_up, conv_w, conv_b, w_down, norm_final):
    B, S, D = x.shape
    depth = w_in.shape[0]
    T = B * S

    a = jnp.arange(A_BLOCK)[:, None]
    bk = jnp.arange(2 * A_BLOCK)[None, :]
    rel = a + A_BLOCK - bk
    idx_a = jnp.stack([_t5_bucket(rel * d) for _, d in A_CONFIGS]).astype(jnp.int32)
    bias_a = _bias_tiles(rel_table[:A_HEADS], idx_a)
    dmax_b = _n_far_tiles(S // MOBA_BLOCK, MOBA_BLOCK)
    bias_b = _bias_tiles(rel_table[A_HEADS:A_HEADS + B_HEADS] * LOG2E,
                         _toeplitz_bucket_tiles_t(dmax_b + 1, MOBA_BLOCK), 2)
    dmax_c = _n_far_tiles(S // C_CHUNK, C_CHUNK)
    bias_c = _bias_tiles(rel_table[A_HEADS + B_HEADS:] * LOG2E,
                         _toeplitz_bucket_tiles_t(dmax_c + 1, C_CHUNK), C_GROUP)

    n16 = S // CMP_STRIDE
    ci = jnp.arange(n16)[None, :] * CMP_STRIDE
    sj = jnp.arange(S // SLC_BLOCK)[:, None] * SLC_BLOCK
    overlap_t = ((ci < sj + SLC_BLOCK) & (ci + CMP_LEN > sj)).astype(BF16)

    w_head, w_tail = _split_w_in(w_in)
    w_out_b = w_out.astype(BF16)
    w_up_b = w_up.astype(BF16)
    w_down_b = w_down.astype(BF16)
    gains_attn = norm_attn.reshape(depth, 1, D)
    gains_mlp = norm_mlp.reshape(depth, 1, D)
    conv_b3 = conv_b.reshape(depth, 1, -1)

    x2 = x.reshape(T, D)
    h = _norm(x2, gains_attn, 0)
    for l in range(depth):
        proj = _inproj(h, w_head, l, HEAD_WIDTH, tn=512).reshape(B, S, HEAD_WIDTH)
        tail = _inproj(h, w_tail, l, TAIL_WIDTH, tn=TAIL_WIDTH // 2).reshape(B, S, TAIL_WIDTH)
        os_, lses = [], []
        for g, (_, d) in enumerate(A_CONFIGS):
            o, lse = _dilated_group(proj, bias_a, g, d)
            os_.append(o)
            lses.append(lse)
        o_a = _combine_groups(os_, lses)
        o_b = _moba(proj, bias_b, dmax_b).reshape(T, B_HEADS * HEAD_DIM)
        w1_blk, w2_blk, pe_rows = _compress_weights(cmp_w1[l], cmp_w2[l], cmp_pe[l])
        cmp_cols = tail[:, :, :BLK_GATE * LANES].reshape(B, S, C_KV_HEADS, 3, LANES)[:, :, :, 0]
        sub = cmp_cols.transpose(0, 2, 1, 3).reshape(B, C_KV_HEADS, n16, CMP_STRIDE * LANES)
        kcvc, kcvct = _compress(sub, pe_rows, w1_blk, w2_blk)
        o_c = _nsa(proj, tail, kcvc, kcvct, bias_c, overlap_t, dmax_c).reshape(T, C_HEADS * HEAD_DIM)
        x2, h = _outproj(x2, o_a, o_b, o_c, w_out_b, gains_mlp, l)
        last = l == depth - 1
        next_gain = norm_final.reshape(1, D) if last else norm_attn[l + 1].reshape(1, D)
        x2, h = _ffn(x2, h, w_up_b, conv_w, conv_b3, w_down_b, l, next_gain, F32 if last else BF16, S)
    return h.reshape(B, S, D)
```

```python
import functools
import math

import numpy as np
import jax
import jax.numpy as jnp
from jax import lax
from jax.experimental import pallas as pl
from jax.experimental.pallas import tpu as pltpu

F32 = jnp.float32
BF16 = jnp.bfloat16

HEAD_DIM = 64
LANES = 128
A_CONFIGS = ((128, 1), (512, 4), (2048, 16))
A_HEADS_PER_GROUP = 4
A_HEADS = 12
A_BLOCK = 128
A_BLOCKS_PER_STEP = 2
B_HEADS = 8
MOBA_BLOCK = 256
MOBA_TOPK = 3
MOBA_UNROLL = 4
C_KV_HEADS = 3
C_GROUP = 4
C_HEADS = 12
CMP_STRIDE = 16
CMP_LEN = 32
CMP_HIDDEN = 128
SLC_BLOCK = 64
SLC_TOPN = 16
N_FORCED = 3
IMP_SPLITS = 3
WIN = 512
INVALID_SCORE = -1.0
FORCED_SCORE = -2.0
N_BUCKETS = 32
T5_MAX_DIST = 2048
EPS = 1e-6
NEG = -1e30
M_FLOOR = -1e20
Q_SCALE = HEAD_DIM ** -0.5
LOG2E = math.log2(math.e)
C_CHUNK = 128
ACC_ROWS = 128
GROUP_W = 128
NSA_KT = 256
NSA_UNROLL = 4

A_WIDTH = 3 * A_HEADS * HEAD_DIM
A_GROUP_WIDTH = 3 * A_HEADS_PER_GROUP * HEAD_DIM
B_WIDTH = 3 * B_HEADS * HEAD_DIM
C_Q_WIDTH = C_HEADS * HEAD_DIM
C_KV_WIDTH = 6 * C_KV_HEADS * HEAD_DIM
C_GATE_WIDTH = 3 * C_HEADS
IN_WIDTH = A_WIDTH + B_WIDTH + C_Q_WIDTH + C_KV_WIDTH + C_GATE_WIDTH
BLK_A = 0
BLK_B = A_WIDTH // LANES
BLK_CQ = BLK_B + B_WIDTH // LANES
HEAD_WIDTH = A_WIDTH + B_WIDTH + C_Q_WIDTH
BLK_CKV = 0
BLK_GATE = 3 * C_KV_HEADS
TAIL_WIDTH = (BLK_GATE + 1) * LANES

VMEM_LIMIT = 56 * 1024 * 1024


def _cparams(*sem):
    return pltpu.CompilerParams(dimension_semantics=sem, vmem_limit_bytes=VMEM_LIMIT)


def _dot_nt(a, b):
    return lax.dot_general(a, b, (((1,), (1,)), ((), ())), preferred_element_type=F32)


def _dot(a, b, **kw):
    return jnp.dot(a, b, preferred_element_type=F32, **kw)


def _transpose_bf16(x):
    return x.astype(F32).T.astype(BF16)


def _iota(shape, dim):
    return lax.broadcasted_iota(jnp.int32, shape, dim)


def _t5_bucket(dist):
    n = jnp.maximum(dist, 0)
    max_exact = N_BUCKETS // 2
    nf = jnp.maximum(n, 1).astype(F32)
    large = max_exact + (jnp.log(nf / max_exact) / math.log(T5_MAX_DIST / max_exact)
                         * (N_BUCKETS - max_exact)).astype(jnp.int32)
    large = jnp.minimum(large, N_BUCKETS - 1)
    return jnp.where(n < max_exact, n, large)


def _bias_lookup_kernel(tbl_ref, idx_ref, o_ref, *, hpr):
    idx = idx_ref[0]
    C = idx.shape[1]
    for h in range(tbl_ref.shape[0]):
        acc = jnp.zeros(idx.shape, F32)
        for b in range(N_BUCKETS):
            acc = jnp.where(idx == b, tbl_ref[h, b], acc)
        o_ref[0, h // hpr, :, (h % hpr) * C:(h % hpr + 1) * C] = acc


def _bias_tiles(tbl, idx, heads_per_row=1):
    n, R, C = idx.shape
    H = tbl.shape[0]
    hpr = heads_per_row
    return pl.pallas_call(
        functools.partial(_bias_lookup_kernel, hpr=hpr),
        grid=(n,),
        in_specs=[pl.BlockSpec(memory_space=pltpu.SMEM),
                  pl.BlockSpec((1, R, C), lambda t: (t, 0, 0))],
        out_specs=pl.BlockSpec((1, H // hpr, R, hpr * C), lambda t: (t, 0, 0, 0)),
        out_shape=jax.ShapeDtypeStruct((n, H // hpr, R, hpr * C), F32),
        compiler_params=_cparams("arbitrary"),
        name="bias_tiles",
    )(tbl, idx)


def _toeplitz_bucket_tiles_t(n_tiles, size):
    d = jnp.arange(n_tiles)[:, None, None] * size
    key = jnp.arange(size)[None, :, None]
    qry = jnp.arange(size)[None, None, :]
    return _t5_bucket(d + qry - key).astype(jnp.int32)


def _far_delta(size):
    return -(-(T5_MAX_DIST + size - 1) // size)


def _n_far_tiles(n_chunks, size):
    return min(_far_delta(size), n_chunks - 1)


def _norm_rows(x, g):
    ms = jnp.mean(x * x, axis=-1, keepdims=True)
    return x * lax.rsqrt(ms + EPS) * g


def _norm_kernel(x_ref, g_ref, o_ref):
    o_ref[...] = _norm_rows(x_ref[...], g_ref[...]).astype(o_ref.dtype)


def _norm(x2, gains, l, tm=1024):
    T, D = x2.shape
    return pl.pallas_call(
        _norm_kernel,
        grid=(T // tm,),
        in_specs=[pl.BlockSpec((tm, D), lambda i: (i, 0)),
                  pl.BlockSpec((None, 1, D), lambda i: (l, 0, 0))],
        out_specs=pl.BlockSpec((tm, D), lambda i: (i, 0)),
        out_shape=jax.ShapeDtypeStruct((T, D), BF16),
        compiler_params=_cparams("arbitrary"),
        name="first_norm",
    )(x2, gains)


def _inproj_kernel(h_ref, w_ref, o_ref):
    o_ref[...] = _dot(h_ref[...], w_ref[...]).astype(o_ref.dtype)


def _inproj(h, w_all, l, N, tn, tm=1024):
    T, D = h.shape
    return pl.pallas_call(
        _inproj_kernel,
        grid=(T // tm, N // tn),
        in_specs=[pl.BlockSpec((tm, D), lambda i, j: (i, 0)),
                  pl.BlockSpec((None, D, tn), lambda i, j: (l, 0, j))],
        out_specs=pl.BlockSpec((tm, tn), lambda i, j: (i, j)),
        out_shape=jax.ShapeDtypeStruct((T, N), BF16),
        compiler_params=_cparams("arbitrary", "arbitrary"),
        name="inproj",
    )(h, w_all)


def _outproj_kernel(x_ref, o0_ref, o1_ref, o2_ref, l0_ref, l1_ref, l2_ref, b_ref, c_ref, w_ref, g_ref, o_ref, h_ref):
    ls = [l0_ref[...], l1_ref[...], l2_ref[...]]
    m = jnp.maximum(jnp.maximum(ls[0], ls[1]), ls[2])
    es = [jnp.exp(l - m) for l in ls]
    inv = 1.0 / (es[0] + es[1] + es[2])
    gw = o0_ref.shape[1]
    acc = jnp.zeros(o_ref.shape, F32)
    for g, og_ref in enumerate((o0_ref, o1_ref, o2_ref)):
        acc += _dot((og_ref[...] * (es[g] * inv)).astype(BF16), w_ref[pl.ds(g * gw, gw), :])
    ra, rb = 3 * gw, b_ref.shape[1]
    acc += _dot(b_ref[...], w_ref[pl.ds(ra, rb), :])
    acc += _dot(c_ref[...], w_ref[pl.ds(ra + rb, c_ref.shape[1]), :])
    x_new = x_ref[...] + acc
    o_ref[...] = x_new
    h_ref[...] = _norm_rows(x_new, g_ref[...]).astype(h_ref.dtype)


def _outproj(x2, os_, lses, ob, oc, w_all, gains, l, tm=512):
    T, D = x2.shape
    row = lambda i: (i, 0)
    gspec = pl.BlockSpec((tm, os_[0].shape[1]), row)
    return pl.pallas_call(
        _outproj_kernel,
        grid=(T // tm,),
        in_specs=[pl.BlockSpec((tm, D), row)] + [gspec] * 6 + [
                  pl.BlockSpec((tm, ob.shape[1]), row),
                  pl.BlockSpec((tm, oc.shape[1]), row),
                  pl.BlockSpec((None, w_all.shape[1], D), lambda i: (l, 0, 0)),
                  pl.BlockSpec((None, 1, D), lambda i: (l, 0, 0))],
        out_specs=[pl.BlockSpec((tm, D), row), pl.BlockSpec((tm, D), row)],
        out_shape=[jax.ShapeDtypeStruct((T, D), F32), jax.ShapeDtypeStruct((T, D), BF16)],
        compiler_params=_cparams("arbitrary"),
        name="outproj",
    )(x2, *os_, *lses, ob, oc, w_all, gains)


FFN_PARTS = 2
FFN_HALO = 16


def _ffn_kernel(x_ref, h_ref, halo_ref, wa_ref, wg_ref, cwa_ref, cwg_ref, cba_ref, cbg_ref, wd_ref, g_ref,
                o_ref, n_ref, h_scr, *u_scrs, tm, blocks_per_seq):
    i = pl.program_id(0)
    f = pl.program_id(1)

    @pl.when(f == 0)
    def _():
        o_ref[...] = x_ref[...]
        keep = jnp.where(i % blocks_per_seq == 0, 0.0, 1.0).astype(BF16)
        h_scr[pl.ds(0, FFN_HALO), :] = halo_ref[0] * keep
        h_scr[pl.ds(FFN_HALO, tm), :] = h_ref[...]

    h = h_scr[...]
    n_parts = len(u_scrs) // 2
    part = wa_ref.shape[1] // n_parts

    def up(pt):
        cols = pl.ds(pt * part, part)
        u_scrs[2 * pt][...] = _dot(h, wa_ref[:, cols])
        u_scrs[2 * pt + 1][...] = _dot(h, wg_ref[:, cols])

    def conv(u_scr, pt, cw_ref, cb_ref):
        cols = pl.ds(pt * part, part)
        acc = cb_ref[:, cols]
        for j in range(3):
            acc = acc + cw_ref[pl.ds(j, 1), cols] * u_scr[pl.ds(FFN_HALO - j, tm), :]
        return acc

    up(0)
    for pt in range(n_parts):
        if pt + 1 < n_parts:
            up(pt + 1)
        ya = conv(u_scrs[2 * pt], pt, cwa_ref, cba_ref)
        yg = conv(u_scrs[2 * pt + 1], pt, cwg_ref, cbg_ref)
        act = (yg * jax.nn.sigmoid(yg) * ya).astype(BF16)
        o_ref[...] += _dot(act, wd_ref[pl.ds(pt * part, part), :])

    @pl.when(f == pl.num_programs(1) - 1)
    def _():
        n_ref[...] = _norm_rows(o_ref[...], g_ref[...]).astype(n_ref.dtype)


def _ffn(x2, h, w_up, conv_w, conv_b, w_down, l, next_gain, next_dtype, seq, tm=512, tf=512):
    T, D = x2.shape
    Fh = w_down.shape[1]
    nf = Fh // tf
    halo_view = h.reshape(T // FFN_HALO, FFN_HALO, D)
    rb = tm // FFN_HALO
    kern = functools.partial(_ffn_kernel, tm=tm, blocks_per_seq=seq // tm)
    row = lambda i, f: (i, 0)
    return pl.pallas_call(
        kern,
        grid=(T // tm, nf),
        in_specs=[pl.BlockSpec((tm, D), row),
                  pl.BlockSpec((tm, D), row),
                  pl.BlockSpec((1, FFN_HALO, D), lambda i, f: (jnp.maximum(i * rb - 1, 0), 0, 0)),
                  pl.BlockSpec((None, D, tf), lambda i, f: (l, 0, f)),
                  pl.BlockSpec((None, D, tf), lambda i, f: (l, 0, f + nf)),
                  pl.BlockSpec((None, 3, tf), lambda i, f: (l, 0, f)),
                  pl.BlockSpec((None, 3, tf), lambda i, f: (l, 0, f + nf)),
                  pl.BlockSpec((None, 1, tf), lambda i, f: (l, 0, f)),
                  pl.BlockSpec((None, 1, tf), lambda i, f: (l, 0, f + nf)),
                  pl.BlockSpec((None, tf, D), lambda i, f: (l, f, 0)),
                  pl.BlockSpec((1, D), lambda i, f: (0, 0))],
        out_specs=[pl.BlockSpec((tm, D), row), pl.BlockSpec((tm, D), row)],
        out_shape=[jax.ShapeDtypeStruct((T, D), F32), jax.ShapeDtypeStruct((T, D), next_dtype)],
        scratch_shapes=[pltpu.VMEM((tm + FFN_HALO, D), BF16)]
        + [pltpu.VMEM((tm + FFN_HALO, tf // FFN_PARTS), F32)] * (2 * FFN_PARTS),
        compiler_params=_cparams("arbitrary", "arbitrary"),
        name="convffn",
    )(x2, h, halo_view, w_up, w_up, conv_w, conv_w, conv_b, conv_b, w_down, next_gain)


def _stack_pair(qp):
    lane = _iota(qp.shape, 1)
    zero = jnp.zeros_like(qp)
    return jnp.concatenate([jnp.where(lane < HEAD_DIM, qp, zero),
                            jnp.where(lane >= HEAD_DIM, qp, zero)], axis=0)


def _unstack_pair(o):
    R = o.shape[0] // 2
    lane = _iota((R, LANES), 1)
    return jnp.where(lane < HEAD_DIM, o[:R], o[R:])


def _dilated_kernel(q_ref, kc_ref, kp_ref, vc_ref, vp_ref, bias_ref, o_ref, lse_ref):
    i = pl.program_id(2)
    R = 2 * A_BLOCK
    n_blocks = q_ref.shape[1] // A_BLOCK
    a = _iota((R, 2 * A_BLOCK), 0) % A_BLOCK
    bk = _iota((R, 2 * A_BLOCK), 1)
    rel = a + A_BLOCK - bk
    band = (rel >= 0) & (rel <= A_BLOCK)
    first = band & ((bk >= A_BLOCK) | (i > 0))
    n_pairs = q_ref.shape[2] // LANES
    units = [(j, p) for j in range(n_blocks) for p in range(n_pairs)]

    def rows(j):
        return slice(j * A_BLOCK, (j + 1) * A_BLOCK)

    def with_prev(cur_ref, prev_ref, j, cols):
        prev = prev_ref[0, :, cols] if j == 0 else cur_ref[0, rows(j - 1), cols]
        return jnp.concatenate([prev, cur_ref[0, rows(j), cols]], axis=0)

    def scores(j, p):
        cols = slice(p * LANES, (p + 1) * LANES)
        qs = _stack_pair(q_ref[0, rows(j), cols])
        s = _dot_nt(qs, with_prev(kc_ref, kp_ref, j, cols)) + bias_ref[0, 2 * p:2 * p + 2].reshape(R, 2 * A_BLOCK)
        return jnp.where(first if j == 0 else band, s, NEG)

    s_all = [scores(j, p) for j, p in units]
    for (j, p), s in zip(units, s_all):
        cols = slice(p * LANES, (p + 1) * LANES)
        mask = first if j == 0 else band
        m = jnp.max(s, axis=1, keepdims=True)
        e = jnp.where(mask, jnp.exp(s - m), 0.0)
        den = jnp.sum(e, axis=1, keepdims=True)
        o = _dot(e.astype(BF16), with_prev(vc_ref, vp_ref, j, cols)) / den
        lse = m + jnp.log(den)
        o_ref[0, rows(j), cols] = _unstack_pair(o)
        lse_ref[0, rows(j), cols] = _unstack_pair(jnp.broadcast_to(lse, (R, LANES)))


def _dilated_group(proj, bias_a, g, dilation):
    B, S, _ = proj.shape
    L = S // dilation
    nq = L // A_BLOCK
    per_step = A_BLOCKS_PER_STEP if nq % A_BLOCKS_PER_STEP == 0 else 1
    W = A_HEADS_PER_GROUP * HEAD_DIM
    cols = proj[:, :, g * A_GROUP_WIDTH:(g + 1) * A_GROUP_WIDTH]
    view = cols.reshape(B, L, dilation * A_GROUP_WIDTH)

    def cur(part):
        return pl.BlockSpec((1, per_step * A_BLOCK, W), lambda b, r, i: (b, i, r * 3 + part))

    def prev(part):
        return pl.BlockSpec((1, A_BLOCK, W), lambda b, r, i: (b, jnp.maximum(i * per_step - 1, 0), r * 3 + part))

    out_spec = pl.BlockSpec((1, per_step * A_BLOCK, W), lambda b, r, i: (b, i, r))
    o, lse = pl.pallas_call(
        _dilated_kernel,
        grid=(B, dilation, nq // per_step),
        in_specs=[cur(0), cur(1), prev(1), cur(2), prev(2),
                  pl.BlockSpec((1, A_HEADS_PER_GROUP, A_BLOCK, 2 * A_BLOCK), lambda b, r, i: (g, g, 0, 0))],
        out_specs=[out_spec, out_spec],
        out_shape=[jax.ShapeDtypeStruct((B, L, dilation * W), F32)] * 2,
        compiler_params=_cparams("arbitrary", "arbitrary", "arbitrary"),
        name=f"dilated_g{g}",
    )(view, view, view, view, view, bias_a)
    return o.reshape(B * S, W), lse.reshape(B * S, W)


def _flash_init(m_scr, acc_scrs):
    m_scr[...] = jnp.full(m_scr.shape, M_FLOOR, F32)
    for acc in acc_scrs:
        acc[...] = jnp.zeros(acc.shape, F32)


def _with_ones(vt, row):
    return jnp.where(_iota(vt.shape, 0) == row, jnp.ones_like(vt), vt)


def _flash_tiles(tiles, q_groups, m_scr, acc_scrs):
    G = len(q_groups)
    W = q_groups[0].shape[0]
    m = [m_scr[:, g * W:(g + 1) * W] for g in range(G)]
    units = [(t, g) for t in range(len(tiles)) for g in range(G)]

    def scores(t, g):
        k, _, elem_of, _ = tiles[t]
        s = _dot_nt(k, q_groups[g])
        for term in elem_of(g):
            s = s + term
        return s

    def apply(g, alpha, pv):
        acc_scrs[g][...] = alpha * acc_scrs[g][...] + pv

    s_next = scores(*units[0])
    pending = None
    for n, (t, g) in enumerate(units):
        s = s_next
        if n + 1 < len(units):
            s_next = scores(*units[n + 1])
        rows = tiles[t][3](g)
        seg = s.shape[0] // len(rows)
        parts = [s[i * seg:(i + 1) * seg] for i in range(len(rows))]
        m_new = m[g]
        for part, row in zip(parts, rows):
            m_new = jnp.maximum(m_new, jnp.max(part, axis=0, keepdims=True) + row)
        alpha = jnp.exp2(m[g] - m_new)
        p = jnp.concatenate([jnp.exp2(part - (m_new - row)) for part, row in zip(parts, rows)], axis=0)
        m[g] = m_new
        pv = _dot(tiles[t][1](g), p.astype(BF16))
        if pending is not None:
            apply(*pending)
        pending = (g, alpha, pv)
    apply(*pending)
    m_scr[...] = jnp.concatenate(m, axis=1)


PICKED = -3e38
NO_INDEX = 1e9


def _top_picks(score, n_pick):
    idx = _iota(score.shape, 0).astype(F32)

    def pick(_, g):
        mx = jnp.max(g, axis=0, keepdims=True)
        first = jnp.min(jnp.where(g == mx, idx, NO_INDEX), axis=0, keepdims=True)
        return jnp.where(idx == first, PICKED, g)

    return lax.fori_loop(0, n_pick, pick, score) == PICKED


def _moba_kernel(q_ref, k_ref, v_ref, bias_ref, o_ref, km_scr, vt_scr, sel_scr, m_scr, *acc_scrs,
                 dmax, n_blocks):
    c = pl.program_id(2)
    BLK = MOBA_BLOCK
    Q = 2 * BLK

    @pl.when(c == 0)
    def _():
        def body(j, _):
            rows = pl.ds(pl.multiple_of(j * BLK, BLK), BLK)
            km_scr[pl.ds(j, 1), :] = jnp.mean(k_ref[0, rows, :].astype(F32), axis=0, keepdims=True)
            vt_scr[j] = _transpose_bf16(v_ref[0, rows, :])
            return 0
        lax.fori_loop(0, n_blocks, body, 0)

    qs = _stack_pair(q_ref[0])
    gate = _dot_nt(km_scr[...].astype(BF16), qs)
    blk = _iota(gate.shape, 0)
    gate = jnp.where(blk < c, gate, NEG)
    sel = _top_picks(gate, min(MOBA_TOPK, n_blocks))
    sel_scr[...] = jnp.where((blk < c) & sel, 0.0, NEG)

    _flash_init(m_scr, acc_scrs)
    per_head = BLK // GROUP_W
    q_groups = [qs[g * GROUP_W:(g + 1) * GROUP_W] for g in range(Q // GROUP_W)]

    def head_values(vt):
        both = [_with_ones(vt, (1 - h) * HEAD_DIM) for h in range(2)]
        return lambda g: both[g // per_head]

    def far_rows(g):
        return bias_ref[dmax, 0, pl.ds(0, 1), pl.ds(g * GROUP_W, GROUP_W)]

    def past_tile(j, far):
        jc = jnp.minimum(j, n_blocks - 1)
        kj = k_ref[0, pl.ds(pl.multiple_of(jc * BLK, BLK), BLK), :]
        tile = jnp.clip(c - j, 0, dmax)
        selrow = sel_scr[pl.ds(jc, 1), :]

        def elem(g):
            return () if far else (bias_ref[tile, 0, :, pl.ds(g * GROUP_W, GROUP_W)],)

        def rows(g):
            row = selrow[:, g * GROUP_W:(g + 1) * GROUP_W]
            return (row + far_rows(g),) if far else (row,)

        return kj, head_values(vt_scr[jc]), elem, rows

    def body(far):
        def run(it, _):
            _flash_tiles([past_tile(MOBA_UNROLL * it + u, far) for u in range(MOBA_UNROLL)],
                         q_groups, m_scr, acc_scrs)
            return 0
        return run

    n_far_iters = jnp.maximum(c - _far_delta(BLK) + 1, 0) // MOBA_UNROLL
    lax.fori_loop(0, n_far_iters, body(True), 0)
    lax.fori_loop(n_far_iters, (c + MOBA_UNROLL - 1) // MOBA_UNROLL, body(False), 0)

    ko = k_ref[0, pl.ds(pl.multiple_of(c * BLK, BLK), BLK), :]

    def own_elem(g):
        qpos = (g * GROUP_W + _iota((BLK, GROUP_W), 1)) % BLK
        return (bias_ref[0, 0, :, pl.ds(g * GROUP_W, GROUP_W)],
                jnp.where(_iota((BLK, GROUP_W), 0) <= qpos, 0.0, NEG))

    zero_row = jnp.zeros((1, GROUP_W), F32)
    _flash_tiles([(ko, head_values(vt_scr[c]), own_elem, lambda g: (zero_row,))], q_groups, m_scr, acc_scrs)
    outs = []
    for g, acc in enumerate(acc_scrs):
        h = g // per_head
        a = acc[...]
        den = a[(1 - h) * HEAD_DIM:(1 - h) * HEAD_DIM + 1]
        outs.append(a[h * HEAD_DIM:(h + 1) * HEAD_DIM] / den)
    o2 = jnp.concatenate([jnp.concatenate(outs[h * per_head:(h + 1) * per_head], axis=1) for h in range(2)],
                         axis=0)
    o_ref[0] = o2.T.astype(o_ref.dtype)


def _moba(proj, bias_b, dmax):
    B, S, _ = proj.shape
    NB = S // MOBA_BLOCK
    n_tiles = bias_b.shape[0]
    qb, kb, vb = BLK_B, BLK_B + 4, BLK_B + 8
    kern = functools.partial(_moba_kernel, dmax=dmax, n_blocks=NB)
    Q = 2 * MOBA_BLOCK
    return pl.pallas_call(
        kern,
        grid=(B, B_HEADS // 2, NB),
        in_specs=[pl.BlockSpec((1, MOBA_BLOCK, LANES), lambda b, p, c: (b, c, qb + p)),
                  pl.BlockSpec((1, S, LANES), lambda b, p, c: (b, 0, kb + p)),
                  pl.BlockSpec((1, S, LANES), lambda b, p, c: (b, 0, vb + p)),
                  pl.BlockSpec((n_tiles, 1, MOBA_BLOCK, Q), lambda b, p, c: (0, p, 0, 0))],
        out_specs=pl.BlockSpec((1, MOBA_BLOCK, LANES), lambda b, p, c: (b, c, p)),
        out_shape=jax.ShapeDtypeStruct((B, S, B_HEADS * HEAD_DIM), BF16),
        scratch_shapes=[pltpu.VMEM((NB, LANES), F32), pltpu.VMEM((NB, LANES, MOBA_BLOCK), BF16),
                        pltpu.VMEM((NB, Q), F32), pltpu.VMEM((1, Q), F32)]
        + [pltpu.VMEM((ACC_ROWS, GROUP_W), F32)] * (Q // GROUP_W),
        compiler_params=_cparams("arbitrary", "arbitrary", "arbitrary"),
        name="moba",
    )(proj, proj, proj, bias_b)


def _gelu_tanh(x):
    return 0.5 * x * (1.0 + jnp.tanh(math.sqrt(2.0 / math.pi) * (x + 0.044715 * (x * x * x))))


def _compress_kernel(sub_ref, pe_ref, w1_ref, w2_ref, o_ref, ot_ref, v_scr):
    n16 = sub_ref.shape[2]
    sub = sub_ref[0, 0].astype(F32)
    top = (sub + pe_ref[0:1, :]).astype(BF16)
    bot = (sub + pe_ref[1:2, :]).astype(BF16)
    u = _dot(top, w1_ref[0])
    v_scr[pl.ds(0, n16), :] = _dot(bot, w1_ref[1])
    v_scr[pl.ds(n16, 8), :] = jnp.zeros((8, v_scr.shape[1]), F32)
    hidden = u + v_scr[pl.ds(1, n16), :]
    out = _dot(_gelu_tanh(hidden).astype(BF16), w2_ref[...])
    o_ref[0, 0] = out.astype(o_ref.dtype)
    ot_ref[0, 0] = out.T.astype(ot_ref.dtype)


def _compress(sub, pe_rows, w1_blk, w2_blk):
    B, KV, n16, W = sub.shape
    return pl.pallas_call(
        _compress_kernel,
        grid=(B, KV),
        in_specs=[pl.BlockSpec((1, 1, n16, W), lambda b, n: (b, n, 0, 0)),
                  pl.BlockSpec((2, W), lambda b, n: (0, 0)),
                  pl.BlockSpec((2, W, 2 * CMP_HIDDEN), lambda b, n: (0, 0, 0)),
                  pl.BlockSpec((2 * CMP_HIDDEN, LANES), lambda b, n: (0, 0))],
        out_specs=[pl.BlockSpec((1, 1, n16, LANES), lambda b, n: (b, n, 0, 0)),
                   pl.BlockSpec((1, 1, LANES, n16), lambda b, n: (b, n, 0, 0))],
        out_shape=[jax.ShapeDtypeStruct((B, KV, n16, LANES), BF16),
                   jax.ShapeDtypeStruct((B, KV, LANES, n16), BF16)],
        scratch_shapes=[pltpu.VMEM((n16 + 8, 2 * CMP_HIDDEN), F32)],
        compiler_params=_cparams("arbitrary", "arbitrary"),
        name="nsa_compress",
    )(sub, pe_rows, w1_blk, w2_blk)


def _nsa_kernel(q_ref, kc_ref, kct_ref, slc_ref, win_ref, gate_ref, bias_ref, ovt_ref,
                o_ref, slct_scr, wint_scr, ocmp_scr, imp_scr, sel_scr, g_scr, m_slc, m_win, *acc_scrs, dmax, n_sel):
    n = pl.program_id(1)
    c = pl.program_id(2)
    CQ = C_CHUNK
    lane = _iota((CQ, LANES), 1)
    halves = NSA_KT // LANES

    @pl.when(c == 0)
    def _():
        def body(t, _):
            slct_scr[t] = _transpose_bf16(slc_ref[0, pl.ds(pl.multiple_of(t * NSA_KT, NSA_KT), NSA_KT), :])
            for h in range(halves):
                rows = pl.ds(pl.multiple_of(t * NSA_KT + h * LANES, LANES), LANES)
                wint_scr[halves * t + h] = _transpose_bf16(win_ref[0, rows, :])
            return 0
        lax.fori_loop(0, slct_scr.shape[0], body, 0)

    q = q_ref[0].astype(F32)
    heads = []
    for pr in range(2):
        qp = q[:, pr * LANES:(pr + 1) * LANES]
        heads.append(jnp.where(lane < HEAD_DIM, qp, 0.0))
        heads.append(jnp.where(lane < HEAD_DIM, pltpu.roll(qp, HEAD_DIM, 1), 0.0))
    q_heads = [h.astype(BF16) for h in heads]

    NC = kc_ref.shape[2]
    NS = ovt_ref.shape[0]

    def cmp_branch(nk):
        kc = kc_ref[0, 0, pl.ds(0, nk), :]
        kct = kct_ref[0, 0, :, pl.ds(0, nk)]
        visible = (_iota((nk, CQ), 0) * CMP_STRIDE + (CMP_LEN - 1)) <= c * CQ + _iota((nk, CQ), 1)
        psum = jnp.zeros((nk, CQ), F32)
        s_next = _dot_nt(kc, q_heads[0])
        for gi in range(C_GROUP):
            s = jnp.where(visible, s_next, NEG)
            if gi + 1 < C_GROUP:
                s_next = _dot_nt(kc, q_heads[gi + 1])
            m = jnp.max(s, axis=0, keepdims=True)
            e = jnp.where(visible, jnp.exp2(s - m), 0.0)
            den = jnp.sum(e, axis=0, keepdims=True)
            p = e / jnp.maximum(den, 1e-30)
            ocmp_scr[gi] = _dot(kct, p.astype(BF16))[HEAD_DIM:]
            psum = psum + p
        ovt = ovt_ref[:, pl.ds(0, nk)]
        imp = jnp.zeros((NS, CQ), F32)
        rest = psum
        for _ in range(IMP_SPLITS):
            piece = rest.astype(BF16)
            imp = imp + _dot(ovt, piece)
            rest = rest - piece.astype(F32)
        imp_scr[...] = imp

    half_chunks = (NC // 2) // (CQ // CMP_STRIDE)

    @pl.when(c < half_chunks)
    def _():
        cmp_branch(NC // 2)

    @pl.when(c >= half_chunks)
    def _():
        cmp_branch(NC)

    imp = imp_scr[...]
    jj = _iota((NS, CQ), 0)
    own = (c * CQ + _iota((NS, CQ), 1)) // SLC_BLOCK
    forced = (jj == 0) | (jj == own) | (jj == own - 1)
    score = jnp.where(forced, FORCED_SCORE, jnp.where(jj <= own, imp, INVALID_SCORE))
    picked = _top_picks(score, max(n_sel - N_FORCED, 0))
    sel_scr[...] = jnp.where(forced | picked, 0.0, NEG)

    reps = GROUP_W // CQ
    n_wide = C_GROUP // reps
    q_groups = [jnp.concatenate(q_heads[w * reps:(w + 1) * reps], axis=0) for w in range(n_wide)]
    acc_slc = acc_scrs[:n_wide]
    acc_win = acc_scrs[n_wide:]

    _flash_init(m_slc, acc_slc)
    per_tile = NSA_KT // SLC_BLOCK
    n_kt = slct_scr.shape[0]
    key_row2 = _iota((NSA_KT, CQ), 0)
    q_col2 = _iota((NSA_KT, CQ), 1)

    def slc_tile(kt, kind):
        ktc = jnp.minimum(kt, n_kt - 1)
        k = slc_ref[0, pl.ds(pl.multiple_of(ktc * NSA_KT, NSA_KT), NSA_KT), :]
        r = sel_scr[pl.ds(pl.multiple_of(per_tile * ktc, per_tile), per_tile), :]
        delta = c - halves * kt
        causal = jnp.where(kt * NSA_KT + key_row2 <= c * CQ + q_col2, 0.0, NEG) if kind == "diag" else None

        def elem(g):
            if kind == "far":
                return ()
            cols = pl.ds(g * GROUP_W, GROUP_W)
            bias = jnp.concatenate([bias_ref[jnp.clip(delta - h, 0, dmax), 0, :, cols] for h in range(halves)],
                                   axis=0)
            return (bias,) if causal is None else (bias, jnp.concatenate([causal] * reps, axis=1))

        def rows(g):
            out = []
            for i in range(per_tile):
                row = jnp.concatenate([r[i:i + 1, :]] * reps, axis=1)
                if kind == "far":
                    row = row + bias_ref[dmax, 0, pl.ds(0, 1), pl.ds(g * GROUP_W, GROUP_W)]
                out.append(row)
            return out

        vt = _with_ones(slct_scr[ktc], 0)
        return k, lambda g: vt, elem, rows

    def slc_iter(it, kind):
        _flash_tiles([slc_tile(NSA_UNROLL * it + u, kind) for u in range(NSA_UNROLL)], q_groups, m_slc, acc_slc)

    def slc_body(kind):
        def run(it, _):
            slc_iter(it, kind)
            return 0
        return run

    keys_per_iter = NSA_UNROLL * NSA_KT
    last_iter = (c * CQ) // keys_per_iter
    n_far_tiles = jnp.maximum(c - (_far_delta(LANES) + halves - 1) + halves, 0) // halves
    n_far_iters = n_far_tiles // NSA_UNROLL
    lax.fori_loop(0, n_far_iters, slc_body("far"), 0)
    lax.fori_loop(n_far_iters, last_iter, slc_body("near"), 0)
    slc_iter(last_iter, "diag")

    _flash_init(m_win, acc_win)
    n_wt = wint_scr.shape[0]
    first = WIN // LANES
    key_row = _iota((LANES, CQ), 0)
    q_col = _iota((LANES, CQ), 1)

    def win_tile(w):
        ks, vts, negs, dds = [], [], [], []
        for h in range(halves):
            dd = first - halves * w - h
            kt = c - dd
            ktc = jnp.clip(kt, 0, n_wt - 1)
            ks.append(win_ref[0, pl.ds(pl.multiple_of(ktc * LANES, LANES), LANES), :])
            vts.append(wint_scr[ktc])
            dist = dd * LANES + q_col - key_row
            negs.append(jnp.where((dist >= 0) & (dist < WIN) & (kt >= 0), 0.0, NEG))
            dds.append(max(dd, 0))
        negw = jnp.concatenate([jnp.concatenate(negs, axis=0)] * reps, axis=1)

        def elem(g):
            cols = pl.ds(g * GROUP_W, GROUP_W)
            return jnp.concatenate([bias_ref[dd, 0, :, cols] for dd in dds], axis=0), negw

        vt = _with_ones(jnp.concatenate(vts, axis=1), 0)
        return jnp.concatenate(ks, axis=0), lambda g: vt, elem, lambda g: (zero_row,)

    zero_row = jnp.zeros((1, GROUP_W), F32)
    n_win_tiles = -(-(WIN + CQ) // NSA_KT)
    _flash_tiles([win_tile(w) for w in range(n_win_tiles)], q_groups, m_win, acc_win)

    g_scr[...] = jax.nn.sigmoid(gate_ref[0].astype(F32)).T
    res = []
    for gi in range(C_GROUP):
        col0 = (n * C_GROUP + gi) * 3
        wcols = pl.ds((gi % reps) * CQ, CQ)
        a_s = acc_slc[gi // reps][:, wcols]
        a_w = acc_win[gi // reps][:, wcols]
        tot = (g_scr[pl.ds(col0, 1), :] * ocmp_scr[gi]
               + g_scr[pl.ds(col0 + 1, 1), :] * (a_s[HEAD_DIM:] / a_s[0:1])
               + g_scr[pl.ds(col0 + 2, 1), :] * (a_w[HEAD_DIM:] / a_w[0:1]))
        res.append(tot)
    o_ref[0] = jnp.concatenate(res, axis=0).T.astype(o_ref.dtype)


def _nsa(proj, tail, kcvc, kcvct, bias_c, overlap_t, dmax):
    B, S, _ = proj.shape
    n_chunks = S // C_CHUNK
    NC = kcvc.shape[2]
    NS = S // SLC_BLOCK
    n_tiles = bias_c.shape[0]
    Q = C_GROUP * C_CHUNK
    kern = functools.partial(_nsa_kernel, dmax=dmax, n_sel=min(SLC_TOPN, NS))
    return pl.pallas_call(
        kern,
        grid=(B, C_KV_HEADS, n_chunks),
        in_specs=[pl.BlockSpec((1, C_CHUNK, 2 * LANES), lambda b, n, c: (b, c, BLK_CQ // 2 + n)),
                  pl.BlockSpec((1, 1, NC, LANES), lambda b, n, c: (b, n, 0, 0)),
                  pl.BlockSpec((1, 1, LANES, NC), lambda b, n, c: (b, n, 0, 0)),
                  pl.BlockSpec((1, S, LANES), lambda b, n, c: (b, 0, BLK_CKV + 3 * n + 1)),
                  pl.BlockSpec((1, S, LANES), lambda b, n, c: (b, 0, BLK_CKV + 3 * n + 2)),
                  pl.BlockSpec((1, C_CHUNK, LANES), lambda b, n, c: (b, c, BLK_GATE)),
                  pl.BlockSpec((n_tiles, 1, C_CHUNK, Q), lambda b, n, c: (0, n, 0, 0)),
                  pl.BlockSpec((NS, NC), lambda b, n, c: (0, 0))],
        out_specs=pl.BlockSpec((1, C_CHUNK, 2 * LANES), lambda b, n, c: (b, c, n)),
        out_shape=jax.ShapeDtypeStruct((B, S, C_HEADS * HEAD_DIM), BF16),
        scratch_shapes=[pltpu.VMEM((S // NSA_KT, LANES, NSA_KT), BF16), pltpu.VMEM((n_chunks, LANES, LANES), BF16),
                        pltpu.VMEM((C_GROUP, HEAD_DIM, C_CHUNK), F32), pltpu.VMEM((NS, C_CHUNK), F32),
                        pltpu.VMEM((NS, C_CHUNK), F32), pltpu.VMEM((LANES, C_CHUNK), F32),
                        pltpu.VMEM((1, Q), F32), pltpu.VMEM((1, Q), F32)]
        + [pltpu.VMEM((ACC_ROWS, GROUP_W), F32)] * (2 * Q // GROUP_W),
        compiler_params=_cparams("arbitrary", "arbitrary", "arbitrary"),
        name="nsa",
    )(proj, kcvc, kcvct, tail, tail, tail, bias_c, overlap_t)


def _head_scale():
    scale = np.ones((HEAD_WIDTH,), np.float32)
    for g in range(len(A_CONFIGS)):
        scale[g * A_GROUP_WIDTH:g * A_GROUP_WIDTH + A_HEADS_PER_GROUP * HEAD_DIM] = Q_SCALE
    scale[A_WIDTH:A_WIDTH + B_HEADS * HEAD_DIM] = Q_SCALE * LOG2E
    scale[A_WIDTH + B_WIDTH:] = Q_SCALE * LOG2E
    return scale


def _split_w_in(w_in):
    depth, D, _ = w_in.shape
    scale = np.concatenate([_head_scale(), np.ones((IN_WIDTH - HEAD_WIDTH,), np.float32)])
    w = (w_in * scale).astype(BF16)
    kv = w[:, :, HEAD_WIDTH:HEAD_WIDTH + C_KV_WIDTH].reshape(depth, D, 3, 2, C_KV_HEADS, HEAD_DIM)
    kv = kv.transpose(0, 1, 4, 2, 3, 5).reshape(depth, D, C_KV_WIDTH)
    pad = jnp.zeros((depth, D, TAIL_WIDTH - C_KV_WIDTH - C_GATE_WIDTH), BF16)
    tail = jnp.concatenate([kv, w[:, :, HEAD_WIDTH + C_KV_WIDTH:], pad], axis=2)
    return w, tail


def _compress_weights(cmp_w1, cmp_w2, cmp_pe):
    half = CMP_LEN // 2
    w1 = cmp_w1.reshape(2, 2, half, HEAD_DIM, CMP_HIDDEN)
    w1_blk = jnp.zeros((2, half, 2, HEAD_DIM, 2, CMP_HIDDEN), F32)
    w1_blk = w1_blk.at[:, :, 0, :, 0, :].set(w1[0])
    w1_blk = w1_blk.at[:, :, 1, :, 1, :].set(w1[1])
    w1_blk = w1_blk.reshape(2, half * 2 * HEAD_DIM, 2 * CMP_HIDDEN).astype(BF16)
    w2_blk = jnp.zeros((2, CMP_HIDDEN, 2, HEAD_DIM), F32)
    w2_blk = w2_blk.at[0, :, 0, :].set(cmp_w2[0])
    w2_blk = w2_blk.at[1, :, 1, :].set(cmp_w2[1])
    w2_blk = w2_blk.reshape(2 * CMP_HIDDEN, 2 * HEAD_DIM).astype(BF16)
    pe = cmp_pe.reshape(2, 2, half, HEAD_DIM)
    pe_rows = pe.transpose(1, 2, 0, 3).reshape(2, half * 2 * HEAD_DIM)
    return w1_blk, w2_blk, pe_rows


def kernel(x, rel_table, w_in, w_out, cmp_w1, cmp_w2, cmp_pe, norm_attn, norm_mlp,
           w_up, conv_w, conv_b, w_down, norm_final):
    B, S, D = x.shape
    depth = w_in.shape[0]
    T = B * S

    a = jnp.arange(A_BLOCK)[:, None]
    bk = jnp.arange(2 * A_BLOCK)[None, :]
    rel = a + A_BLOCK - bk
    idx_a = jnp.stack([_t5_bucket(rel * d) for _, d in A_CONFIGS]).astype(jnp.int32)
    bias_a = _bias_tiles(rel_table[:A_HEADS], idx_a)
    dmax_b = _n_far_tiles(S // MOBA_BLOCK, MOBA_BLOCK)
    bias_b = _bias_tiles(rel_table[A_HEADS:A_HEADS + B_HEADS] * LOG2E,
                         _toeplitz_bucket_tiles_t(dmax_b + 1, MOBA_BLOCK), 2)
    dmax_c = _n_far_tiles(S // C_CHUNK, C_CHUNK)
    bias_c = _bias_tiles(rel_table[A_HEADS + B_HEADS:] * LOG2E,
                         _toeplitz_bucket_tiles_t(dmax_c + 1, C_CHUNK), C_GROUP)

    n16 = S // CMP_STRIDE
    ci = jnp.arange(n16)[None, :] * CMP_STRIDE
    sj = jnp.arange(S // SLC_BLOCK)[:, None] * SLC_BLOCK
    overlap_t = ((ci < sj + SLC_BLOCK) & (ci + CMP_LEN > sj)).astype(BF16)

    w_head, w_tail = _split_w_in(w_in)
    w_out_b = w_out.astype(BF16)
    w_up_b = w_up.astype(BF16)
    w_down_b = w_down.astype(BF16)
    gains_attn = norm_attn.reshape(depth, 1, D)
    gains_mlp = norm_mlp.reshape(depth, 1, D)
    conv_b3 = conv_b.reshape(depth, 1, -1)

    x2 = x.reshape(T, D)
    h = _norm(x2, gains_attn, 0)
    for l in range(depth):
        proj = _inproj(h, w_head, l, HEAD_WIDTH, tn=512).reshape(B, S, HEAD_WIDTH)
        tail = _inproj(h, w_tail, l, TAIL_WIDTH, tn=TAIL_WIDTH).reshape(B, S, TAIL_WIDTH)
        os_, lses = [], []
        for g, (_, d) in enumerate(A_CONFIGS):
            o, lse = _dilated_group(proj, bias_a, g, d)
            os_.append(o)
            lses.append(lse)
        o_b = _moba(proj, bias_b, dmax_b).reshape(T, B_HEADS * HEAD_DIM)
        w1_blk, w2_blk, pe_rows = _compress_weights(cmp_w1[l], cmp_w2[l], cmp_pe[l])
        cmp_cols = tail[:, :, :BLK_GATE * LANES].reshape(B, S, C_KV_HEADS, 3, LANES)[:, :, :, 0]
        sub = cmp_cols.transpose(0, 2, 1, 3).reshape(B, C_KV_HEADS, n16, CMP_STRIDE * LANES)
        kcvc, kcvct = _compress(sub, pe_rows, w1_blk, w2_blk)
        o_c = _nsa(proj, tail, kcvc, kcvct, bias_c, overlap_t, dmax_c).reshape(T, C_HEADS * HEAD_DIM)
        x2, h = _outproj(x2, os_, lses, o_b, o_c, w_out_b, gains_mlp, l)
        last = l == depth - 1
        next_gain = norm_final.reshape(1, D) if last else norm_attn[l + 1].reshape(1, D)
        x2, h = _ffn(x2, h, w_up_b, conv_w, conv_b3, w_down_b, l, next_gain, F32 if last else BF16, S)
    return h.reshape(B, S, D)
```

```python
import functools
import math

import numpy as np
import jax
import jax.numpy as jnp
from jax import lax
from jax.experimental import pallas as pl
from jax.experimental.pallas import tpu as pltpu

F32 = jnp.float32
BF16 = jnp.bfloat16

HEAD_DIM = 64
LANES = 128
A_CONFIGS = ((128, 1), (512, 4), (2048, 16))
A_HEADS_PER_GROUP = 4
A_HEADS = 12
A_BLOCK = 128
A_BLOCKS_PER_STEP = 2
B_HEADS = 8
MOBA_BLOCK = 256
MOBA_TOPK = 3
MOBA_UNROLL = 4
C_KV_HEADS = 3
C_GROUP = 4
C_HEADS = 12
CMP_STRIDE = 16
CMP_LEN = 32
CMP_HIDDEN = 128
SLC_BLOCK = 64
SLC_TOPN = 16
N_FORCED = 3
IMP_SPLITS = 3
WIN = 512
INVALID_SCORE = -1.0
FORCED_SCORE = -2.0
N_BUCKETS = 32
T5_MAX_DIST = 2048
EPS = 1e-6
NEG = -1e30
M_FLOOR = -1e20
Q_SCALE = HEAD_DIM ** -0.5
LOG2E = math.log2(math.e)
C_CHUNK = 128
ACC_ROWS = 128
GROUP_W = 128
NSA_KT = 256
NSA_UNROLL = 4

A_WIDTH = 3 * A_HEADS * HEAD_DIM
A_GROUP_WIDTH = 3 * A_HEADS_PER_GROUP * HEAD_DIM
B_WIDTH = 3 * B_HEADS * HEAD_DIM
C_Q_WIDTH = C_HEADS * HEAD_DIM
C_KV_WIDTH = 6 * C_KV_HEADS * HEAD_DIM
C_GATE_WIDTH = 3 * C_HEADS
IN_WIDTH = A_WIDTH + B_WIDTH + C_Q_WIDTH + C_KV_WIDTH + C_GATE_WIDTH
BLK_B = A_WIDTH // LANES
BLK_CQ = BLK_B + B_WIDTH // LANES
HEAD_WIDTH = A_WIDTH + B_WIDTH + C_Q_WIDTH
BLK_CKV = 0
BLK_GATE = 3 * C_KV_HEADS
TAIL_WIDTH = (BLK_GATE + 1) * LANES

VMEM_LIMIT = 56 * 1024 * 1024


def _cparams(*sem):
    return pltpu.CompilerParams(dimension_semantics=sem, vmem_limit_bytes=VMEM_LIMIT)


def _dot_nt(a, b):
    return lax.dot_general(a, b, (((1,), (1,)), ((), ())), preferred_element_type=F32)


def _dot(a, b, **kw):
    return jnp.dot(a, b, preferred_element_type=F32, **kw)


def _transpose_bf16(x):
    return x.astype(F32).T.astype(BF16)


def _iota(shape, dim):
    return lax.broadcasted_iota(jnp.int32, shape, dim)


def _t5_bucket(dist):
    n = jnp.maximum(dist, 0)
    max_exact = N_BUCKETS // 2
    nf = jnp.maximum(n, 1).astype(F32)
    large = max_exact + (jnp.log(nf / max_exact) / math.log(T5_MAX_DIST / max_exact)
                         * (N_BUCKETS - max_exact)).astype(jnp.int32)
    large = jnp.minimum(large, N_BUCKETS - 1)
    return jnp.where(n < max_exact, n, large)


def _bias_lookup_kernel(tbl_ref, idx_ref, o_ref, *, hpr):
    idx = idx_ref[0]
    C = idx.shape[1]
    for h in range(tbl_ref.shape[0]):
        acc = jnp.zeros(idx.shape, F32)
        for b in range(N_BUCKETS):
            acc = jnp.where(idx == b, tbl_ref[h, b], acc)
        o_ref[0, h // hpr, :, (h % hpr) * C:(h % hpr + 1) * C] = acc


def _bias_tiles(tbl, idx, heads_per_row=1):
    n, R, C = idx.shape
    H = tbl.shape[0]
    hpr = heads_per_row
    return pl.pallas_call(
        functools.partial(_bias_lookup_kernel, hpr=hpr),
        grid=(n,),
        in_specs=[pl.BlockSpec(memory_space=pltpu.SMEM),
                  pl.BlockSpec((1, R, C), lambda t: (t, 0, 0))],
        out_specs=pl.BlockSpec((1, H // hpr, R, hpr * C), lambda t: (t, 0, 0, 0)),
        out_shape=jax.ShapeDtypeStruct((n, H // hpr, R, hpr * C), F32),
        compiler_params=_cparams("arbitrary"),
        name="bias_tiles",
    )(tbl, idx)


def _toeplitz_bucket_tiles_t(n_tiles, size):
    d = jnp.arange(n_tiles)[:, None, None] * size
    key = jnp.arange(size)[None, :, None]
    qry = jnp.arange(size)[None, None, :]
    return _t5_bucket(d + qry - key).astype(jnp.int32)


def _far_delta(size):
    return -(-(T5_MAX_DIST + size - 1) // size)


def _n_far_tiles(n_chunks, size):
    return min(_far_delta(size), n_chunks - 1)


def _norm_rows(x, g):
    ms = jnp.mean(x * x, axis=-1, keepdims=True)
    return x * lax.rsqrt(ms + EPS) * g


def _norm_kernel(x_ref, g_ref, o_ref):
    o_ref[...] = _norm_rows(x_ref[...], g_ref[...]).astype(o_ref.dtype)


def _norm(x2, gains, l, tm=1024):
    T, D = x2.shape
    return pl.pallas_call(
        _norm_kernel,
        grid=(T // tm,),
        in_specs=[pl.BlockSpec((tm, D), lambda i: (i, 0)),
                  pl.BlockSpec((None, 1, D), lambda i: (l, 0, 0))],
        out_specs=pl.BlockSpec((tm, D), lambda i: (i, 0)),
        out_shape=jax.ShapeDtypeStruct((T, D), BF16),
        compiler_params=_cparams("arbitrary"),
        name="first_norm",
    )(x2, gains)


def _inproj_kernel(h_ref, w_ref, o_ref):
    o_ref[...] = _dot(h_ref[...], w_ref[...]).astype(o_ref.dtype)


def _inproj(h, w_all, l, N, tn, tm=1024):
    T, D = h.shape
    return pl.pallas_call(
        _inproj_kernel,
        grid=(T // tm, N // tn),
        in_specs=[pl.BlockSpec((tm, D), lambda i, j: (i, 0)),
                  pl.BlockSpec((None, D, tn), lambda i, j: (l, 0, j))],
        out_specs=pl.BlockSpec((tm, tn), lambda i, j: (i, j)),
        out_shape=jax.ShapeDtypeStruct((T, N), BF16),
        compiler_params=_cparams("arbitrary", "arbitrary"),
        name="inproj",
    )(h, w_all)


def _outproj_kernel(x_ref, o0_ref, o1_ref, o2_ref, l0_ref, l1_ref, l2_ref, b_ref, c_ref, w_ref, g_ref, o_ref, h_ref):
    ls = [l0_ref[...], l1_ref[...], l2_ref[...]]
    m = jnp.maximum(jnp.maximum(ls[0], ls[1]), ls[2])
    es = [jnp.exp(l - m) for l in ls]
    inv = 1.0 / (es[0] + es[1] + es[2])
    gw = o0_ref.shape[1]
    acc = jnp.zeros(o_ref.shape, F32)
    for g, og_ref in enumerate((o0_ref, o1_ref, o2_ref)):
        acc += _dot((og_ref[...] * (es[g] * inv)).astype(BF16), w_ref[pl.ds(g * gw, gw), :])
    ra, rb = 3 * gw, b_ref.shape[1]
    acc += _dot(b_ref[...], w_ref[pl.ds(ra, rb), :])
    acc += _dot(c_ref[...], w_ref[pl.ds(ra + rb, c_ref.shape[1]), :])
    x_new = x_ref[...] + acc
    o_ref[...] = x_new
    h_ref[...] = _norm_rows(x_new, g_ref[...]).astype(h_ref.dtype)


def _outproj(x2, os_, lses, ob, oc, w_all, gains, l, tm=512):
    T, D = x2.shape
    row = lambda i: (i, 0)
    gspec = pl.BlockSpec((tm, os_[0].shape[1]), row)
    return pl.pallas_call(
        _outproj_kernel,
        grid=(T // tm,),
        in_specs=[pl.BlockSpec((tm, D), row)] + [gspec] * 6 + [
                  pl.BlockSpec((tm, ob.shape[1]), row),
                  pl.BlockSpec((tm, oc.shape[1]), row),
                  pl.BlockSpec((None, w_all.shape[1], D), lambda i: (l, 0, 0)),
                  pl.BlockSpec((None, 1, D), lambda i: (l, 0, 0))],
        out_specs=[pl.BlockSpec((tm, D), row), pl.BlockSpec((tm, D), row)],
        out_shape=[jax.ShapeDtypeStruct((T, D), F32), jax.ShapeDtypeStruct((T, D), BF16)],
        compiler_params=_cparams("arbitrary"),
        name="outproj",
    )(x2, *os_, *lses, ob, oc, w_all, gains)


FFN_PARTS = 2
FFN_HALO = 16


def _ffn_kernel(x_ref, h_ref, halo_ref, wa_ref, wg_ref, cwa_ref, cwg_ref, cba_ref, cbg_ref, wd_ref, g_ref,
                o_ref, n_ref, h_scr, *u_scrs, tm, blocks_per_seq):
    i = pl.program_id(0)
    f = pl.program_id(1)

    @pl.when(f == 0)
    def _():
        o_ref[...] = x_ref[...]
        keep = jnp.where(i % blocks_per_seq == 0, 0.0, 1.0).astype(BF16)
        h_scr[pl.ds(0, FFN_HALO), :] = halo_ref[0] * keep
        h_scr[pl.ds(FFN_HALO, tm), :] = h_ref[...]

    h = h_scr[...]
    n_parts = len(u_scrs) // 2
    part = wa_ref.shape[1] // n_parts

    def up(pt):
        cols = pl.ds(pt * part, part)
        u_scrs[2 * pt][...] = _dot(h, wa_ref[:, cols])
        u_scrs[2 * pt + 1][...] = _dot(h, wg_ref[:, cols])

    def conv(u_scr, pt, cw_ref, cb_ref):
        cols = pl.ds(pt * part, part)
        acc = cb_ref[:, cols]
        for j in range(3):
            acc = acc + cw_ref[pl.ds(j, 1), cols] * u_scr[pl.ds(FFN_HALO - j, tm), :]
        return acc

    up(0)
    for pt in range(n_parts):
        if pt + 1 < n_parts:
            up(pt + 1)
        ya = conv(u_scrs[2 * pt], pt, cwa_ref, cba_ref)
        yg = conv(u_scrs[2 * pt + 1], pt, cwg_ref, cbg_ref)
        act = (yg * jax.nn.sigmoid(yg) * ya).astype(BF16)
        o_ref[...] += _dot(act, wd_ref[pl.ds(pt * part, part), :])

    @pl.when(f == pl.num_programs(1) - 1)
    def _():
        n_ref[...] = _norm_rows(o_ref[...], g_ref[...]).astype(n_ref.dtype)


def _ffn(x2, h, w_up, conv_w, conv_b, w_down, l, next_gain, next_dtype, seq, tm=512, tf=512):
    T, D = x2.shape
    Fh = w_down.shape[1]
    nf = Fh // tf
    halo_view = h.reshape(T // FFN_HALO, FFN_HALO, D)
    rb = tm // FFN_HALO
    kern = functools.partial(_ffn_kernel, tm=tm, blocks_per_seq=seq // tm)
    row = lambda i, f: (i, 0)
    return pl.pallas_call(
        kern,
        grid=(T // tm, nf),
        in_specs=[pl.BlockSpec((tm, D), row),
                  pl.BlockSpec((tm, D), row),
                  pl.BlockSpec((1, FFN_HALO, D), lambda i, f: (jnp.maximum(i * rb - 1, 0), 0, 0)),
                  pl.BlockSpec((None, D, tf), lambda i, f: (l, 0, f)),
                  pl.BlockSpec((None, D, tf), lambda i, f: (l, 0, f + nf)),
                  pl.BlockSpec((None, 3, tf), lambda i, f: (l, 0, f)),
                  pl.BlockSpec((None, 3, tf), lambda i, f: (l, 0, f + nf)),
                  pl.BlockSpec((None, 1, tf), lambda i, f: (l, 0, f)),
                  pl.BlockSpec((None, 1, tf), lambda i, f: (l, 0, f + nf)),
                  pl.BlockSpec((None, tf, D), lambda i, f: (l, f, 0)),
                  pl.BlockSpec((1, D), lambda i, f: (0, 0))],
        out_specs=[pl.BlockSpec((tm, D), row), pl.BlockSpec((tm, D), row)],
        out_shape=[jax.ShapeDtypeStruct((T, D), F32), jax.ShapeDtypeStruct((T, D), next_dtype)],
        scratch_shapes=[pltpu.VMEM((tm + FFN_HALO, D), BF16)]
        + [pltpu.VMEM((tm + FFN_HALO, tf // FFN_PARTS), F32)] * (2 * FFN_PARTS),
        compiler_params=_cparams("arbitrary", "arbitrary"),
        name="convffn",
    )(x2, h, halo_view, w_up, w_up, conv_w, conv_w, conv_b, conv_b, w_down, next_gain)


def _stack_pair(qp):
    lane = _iota(qp.shape, 1)
    zero = jnp.zeros_like(qp)
    return jnp.concatenate([jnp.where(lane < HEAD_DIM, qp, zero),
                            jnp.where(lane >= HEAD_DIM, qp, zero)], axis=0)


def _unstack_pair(o):
    R = o.shape[0] // 2
    lane = _iota((R, LANES), 1)
    return jnp.where(lane < HEAD_DIM, o[:R], o[R:])


def _dilated_kernel(q_ref, kc_ref, kp_ref, vc_ref, vp_ref, bias_ref, o_ref, lse_ref):
    i = pl.program_id(2)
    R = 2 * A_BLOCK
    n_blocks = q_ref.shape[1] // A_BLOCK
    a = _iota((R, 2 * A_BLOCK), 0) % A_BLOCK
    bk = _iota((R, 2 * A_BLOCK), 1)
    rel = a + A_BLOCK - bk
    band = (rel >= 0) & (rel <= A_BLOCK)
    first = band & ((bk >= A_BLOCK) | (i > 0))
    n_pairs = q_ref.shape[2] // LANES
    units = [(j, p) for j in range(n_blocks) for p in range(n_pairs)]

    def rows(j):
        return slice(j * A_BLOCK, (j + 1) * A_BLOCK)

    def with_prev(cur_ref, prev_ref, j, cols):
        prev = prev_ref[0, :, cols] if j == 0 else cur_ref[0, rows(j - 1), cols]
        return jnp.concatenate([prev, cur_ref[0, rows(j), cols]], axis=0)

    def scores(j, p):
        cols = slice(p * LANES, (p + 1) * LANES)
        qs = _stack_pair(q_ref[0, rows(j), cols])
        s = _dot_nt(qs, with_prev(kc_ref, kp_ref, j, cols)) + bias_ref[0, 2 * p:2 * p + 2].reshape(R, 2 * A_BLOCK)
        return jnp.where(first if j == 0 else band, s, NEG)

    s_all = [scores(j, p) for j, p in units]
    for (j, p), s in zip(units, s_all):
        cols = slice(p * LANES, (p + 1) * LANES)
        mask = first if j == 0 else band
        m = jnp.max(s, axis=1, keepdims=True)
        e = jnp.where(mask, jnp.exp(s - m), 0.0)
        den = jnp.sum(e, axis=1, keepdims=True)
        o = _dot(e.astype(BF16), with_prev(vc_ref, vp_ref, j, cols)) / den
        lse = m + jnp.log(den)
        o_ref[0, rows(j), cols] = _unstack_pair(o)
        lse_ref[0, rows(j), cols] = _unstack_pair(jnp.broadcast_to(lse, (R, LANES)))


def _dilated_group(proj, bias_a, g, dilation):
    B, S, _ = proj.shape
    L = S // dilation
    nq = L // A_BLOCK
    per_step = A_BLOCKS_PER_STEP if nq % A_BLOCKS_PER_STEP == 0 else 1
    W = A_HEADS_PER_GROUP * HEAD_DIM
    if dilation == 1:
        view, base = proj, g * A_GROUP_WIDTH // W
    else:
        cols = proj[:, :, g * A_GROUP_WIDTH:(g + 1) * A_GROUP_WIDTH]
        view, base = cols.reshape(B, L, dilation * A_GROUP_WIDTH), 0

    def cur(part):
        return pl.BlockSpec((1, per_step * A_BLOCK, W), lambda b, r, i: (b, i, base + r * 3 + part))

    def prev(part):
        return pl.BlockSpec((1, A_BLOCK, W),
                            lambda b, r, i: (b, jnp.maximum(i * per_step - 1, 0), base + r * 3 + part))

    out_spec = pl.BlockSpec((1, per_step * A_BLOCK, W), lambda b, r, i: (b, i, r))
    o, lse = pl.pallas_call(
        _dilated_kernel,
        grid=(B, dilation, nq // per_step),
        in_specs=[cur(0), cur(1), prev(1), cur(2), prev(2),
                  pl.BlockSpec((1, A_HEADS_PER_GROUP, A_BLOCK, 2 * A_BLOCK), lambda b, r, i: (g, g, 0, 0))],
        out_specs=[out_spec, out_spec],
        out_shape=[jax.ShapeDtypeStruct((B, L, dilation * W), F32)] * 2,
        compiler_params=_cparams("arbitrary", "arbitrary", "arbitrary"),
        name=f"dilated_g{g}",
    )(view, view, view, view, view, bias_a)
    return o.reshape(B * S, W), lse.reshape(B * S, W)


def _flash_init(m_scr, acc_scrs):
    m_scr[...] = jnp.full(m_scr.shape, M_FLOOR, F32)
    for acc in acc_scrs:
        acc[...] = jnp.zeros(acc.shape, F32)


def _with_ones(vt, row):
    return jnp.where(_iota(vt.shape, 0) == row, jnp.ones_like(vt), vt)


def _flash_tiles(tiles, q_groups, m_scr, acc_scrs):
    G = len(q_groups)
    W = q_groups[0].shape[0]
    m = [m_scr[:, g * W:(g + 1) * W] for g in range(G)]
    units = [(t, g) for t in range(len(tiles)) for g in range(G)]

    def scores(t, g):
        k, _, elem_of, _ = tiles[t]
        s = _dot_nt(k, q_groups[g])
        for term in elem_of(g):
            s = s + term
        return s

    def apply(g, alpha, pv):
        acc_scrs[g][...] = alpha * acc_scrs[g][...] + pv

    s_next = scores(*units[0])
    pending = None
    for n, (t, g) in enumerate(units):
        s = s_next
        if n + 1 < len(units):
            s_next = scores(*units[n + 1])
        rows = tiles[t][3](g)
        seg = s.shape[0] // len(rows)
        parts = [s[i * seg:(i + 1) * seg] for i in range(len(rows))]
        m_new = m[g]
        for part, row in zip(parts, rows):
            m_new = jnp.maximum(m_new, jnp.max(part, axis=0, keepdims=True) + row)
        alpha = jnp.exp2(m[g] - m_new)
        p = jnp.concatenate([jnp.exp2(part - (m_new - row)) for part, row in zip(parts, rows)], axis=0)
        m[g] = m_new
        pv = _dot(tiles[t][1](g), p.astype(BF16))
        if pending is not None:
            apply(*pending)
        pending = (g, alpha, pv)
    apply(*pending)
    m_scr[...] = jnp.concatenate(m, axis=1)


PICKED = -3e38
NO_INDEX = 1e9


def _top_picks(score, n_pick):
    idx = _iota(score.shape, 0).astype(F32)

    def pick(_, g):
        mx = jnp.max(g, axis=0, keepdims=True)
        first = jnp.min(jnp.where(g == mx, idx, NO_INDEX), axis=0, keepdims=True)
        return jnp.where(idx == first, PICKED, g)

    return lax.fori_loop(0, n_pick, pick, score) == PICKED


def _moba_kernel(q_ref, k_ref, v_ref, bias_ref, o_ref, km_scr, vt_scr, sel_scr, m_scr, *acc_scrs,
                 dmax, n_blocks):
    c = pl.program_id(2)
    BLK = MOBA_BLOCK
    Q = 2 * BLK

    @pl.when(c == 0)
    def _():
        def body(j, _):
            rows = pl.ds(pl.multiple_of(j * BLK, BLK), BLK)
            km_scr[pl.ds(j, 1), :] = jnp.mean(k_ref[0, rows, :].astype(F32), axis=0, keepdims=True)
            vt_scr[j] = _transpose_bf16(v_ref[0, rows, :])
            return 0
        lax.fori_loop(0, n_blocks, body, 0)

    qs = _stack_pair(q_ref[0])
    gate = _dot_nt(km_scr[...].astype(BF16), qs)
    blk = _iota(gate.shape, 0)
    gate = jnp.where(blk < c, gate, NEG)
    sel = _top_picks(gate, min(MOBA_TOPK, n_blocks))
    sel_scr[...] = jnp.where((blk < c) & sel, 0.0, NEG)

    _flash_init(m_scr, acc_scrs)
    per_head = BLK // GROUP_W
    q_groups = [qs[g * GROUP_W:(g + 1) * GROUP_W] for g in range(Q // GROUP_W)]

    def head_values(vt):
        both = [_with_ones(vt, (1 - h) * HEAD_DIM) for h in range(2)]
        return lambda g: both[g // per_head]

    def far_rows(g):
        return bias_ref[dmax, 0, pl.ds(0, 1), pl.ds(g * GROUP_W, GROUP_W)]

    def past_tile(j, far):
        jc = jnp.minimum(j, n_blocks - 1)
        kj = k_ref[0, pl.ds(pl.multiple_of(jc * BLK, BLK), BLK), :]
        tile = jnp.clip(c - j, 0, dmax)
        selrow = sel_scr[pl.ds(jc, 1), :]

        def elem(g):
            return () if far else (bias_ref[tile, 0, :, pl.ds(g * GROUP_W, GROUP_W)],)

        def rows(g):
            row = selrow[:, g * GROUP_W:(g + 1) * GROUP_W]
            return (row + far_rows(g),) if far else (row,)

        return kj, head_values(vt_scr[jc]), elem, rows

    def body(far):
        def run(it, _):
            _flash_tiles([past_tile(MOBA_UNROLL * it + u, far) for u in range(MOBA_UNROLL)],
                         q_groups, m_scr, acc_scrs)
            return 0
        return run

    n_far_iters = jnp.maximum(c - _far_delta(BLK) + 1, 0) // MOBA_UNROLL
    lax.fori_loop(0, n_far_iters, body(True), 0)
    lax.fori_loop(n_far_iters, (c + MOBA_UNROLL - 1) // MOBA_UNROLL, body(False), 0)

    ko = k_ref[0, pl.ds(pl.multiple_of(c * BLK, BLK), BLK), :]

    def own_elem(g):
        qpos = (g * GROUP_W + _iota((BLK, GROUP_W), 1)) % BLK
        return (bias_ref[0, 0, :, pl.ds(g * GROUP_W, GROUP_W)],
                jnp.where(_iota((BLK, GROUP_W), 0) <= qpos, 0.0, NEG))

    zero_row = jnp.zeros((1, GROUP_W), F32)
    _flash_tiles([(ko, head_values(vt_scr[c]), own_elem, lambda g: (zero_row,))], q_groups, m_scr, acc_scrs)
    outs = []
    for g, acc in enumerate(acc_scrs):
        h = g // per_head
        a = acc[...]
        den = a[(1 - h) * HEAD_DIM:(1 - h) * HEAD_DIM + 1]
        outs.append(a[h * HEAD_DIM:(h + 1) * HEAD_DIM] / den)
    o2 = jnp.concatenate([jnp.concatenate(outs[h * per_head:(h + 1) * per_head], axis=1) for h in range(2)],
                         axis=0)
    o_ref[0] = o2.T.astype(o_ref.dtype)


def _moba(proj, bias_b, dmax):
    B, S, _ = proj.shape
    NB = S // MOBA_BLOCK
    n_tiles = bias_b.shape[0]
    qb, kb, vb = BLK_B, BLK_B + 4, BLK_B + 8
    kern = functools.partial(_moba_kernel, dmax=dmax, n_blocks=NB)
    Q = 2 * MOBA_BLOCK
    return pl.pallas_call(
        kern,
        grid=(B, B_HEADS // 2, NB),
        in_specs=[pl.BlockSpec((1, MOBA_BLOCK, LANES), lambda b, p, c: (b, c, qb + p)),
                  pl.BlockSpec((1, S, LANES), lambda b, p, c: (b, 0, kb + p)),
                  pl.BlockSpec((1, S, LANES), lambda b, p, c: (b, 0, vb + p)),
                  pl.BlockSpec((n_tiles, 1, MOBA_BLOCK, Q), lambda b, p, c: (0, p, 0, 0))],
        out_specs=pl.BlockSpec((1, MOBA_BLOCK, LANES), lambda b, p, c: (b, c, p)),
        out_shape=jax.ShapeDtypeStruct((B, S, B_HEADS * HEAD_DIM), BF16),
        scratch_shapes=[pltpu.VMEM((NB, LANES), F32), pltpu.VMEM((NB, LANES, MOBA_BLOCK), BF16),
                        pltpu.VMEM((NB, Q), F32), pltpu.VMEM((1, Q), F32)]
        + [pltpu.VMEM((ACC_ROWS, GROUP_W), F32)] * (Q // GROUP_W),
        compiler_params=_cparams("arbitrary", "arbitrary", "arbitrary"),
        name="moba",
    )(proj, proj, proj, bias_b)


def _gelu_tanh(x):
    return 0.5 * x * (1.0 + jnp.tanh(math.sqrt(2.0 / math.pi) * (x + 0.044715 * (x * x * x))))


def _compress_kernel(sub_ref, pe_ref, w1_ref, w2_ref, o_ref, ot_ref, v_scr):
    n16 = sub_ref.shape[2]
    sub = sub_ref[0, 0].astype(F32)
    top = (sub + pe_ref[0:1, :]).astype(BF16)
    bot = (sub + pe_ref[1:2, :]).astype(BF16)
    u = _dot(top, w1_ref[0])
    v_scr[pl.ds(0, n16), :] = _dot(bot, w1_ref[1])
    v_scr[pl.ds(n16, 8), :] = jnp.zeros((8, v_scr.shape[1]), F32)
    hidden = u + v_scr[pl.ds(1, n16), :]
    out = _dot(_gelu_tanh(hidden).astype(BF16), w2_ref[...])
    o_ref[0, 0] = out.astype(o_ref.dtype)
    ot_ref[0, 0] = out.T.astype(ot_ref.dtype)


def _compress(sub, pe_rows, w1_blk, w2_blk, l):
    B, KV, n16, W = sub.shape
    return pl.pallas_call(
        _compress_kernel,
        grid=(B, KV),
        in_specs=[pl.BlockSpec((1, 1, n16, W), lambda b, n: (b, n, 0, 0)),
                  pl.BlockSpec((None, 2, W), lambda b, n: (l, 0, 0)),
                  pl.BlockSpec((None, 2, W, 2 * CMP_HIDDEN), lambda b, n: (l, 0, 0, 0)),
                  pl.BlockSpec((None, 2 * CMP_HIDDEN, LANES), lambda b, n: (l, 0, 0))],
        out_specs=[pl.BlockSpec((1, 1, n16, LANES), lambda b, n: (b, n, 0, 0)),
                   pl.BlockSpec((1, 1, LANES, n16), lambda b, n: (b, n, 0, 0))],
        out_shape=[jax.ShapeDtypeStruct((B, KV, n16, LANES), BF16),
                   jax.ShapeDtypeStruct((B, KV, LANES, n16), BF16)],
        scratch_shapes=[pltpu.VMEM((n16 + 8, 2 * CMP_HIDDEN), F32)],
        compiler_params=_cparams("arbitrary", "arbitrary"),
        name="nsa_compress",
    )(sub, pe_rows, w1_blk, w2_blk)


def _nsa_kernel(q_ref, kc_ref, kct_ref, slc_ref, win_ref, gate_ref, bias_ref, ovt_ref,
                o_ref, slct_scr, wint_scr, ocmp_scr, imp_scr, sel_scr, g_scr, m_slc, m_win, *acc_scrs, dmax, n_sel):
    n = pl.program_id(1)
    c = pl.program_id(2)
    CQ = C_CHUNK
    lane = _iota((CQ, LANES), 1)
    halves = NSA_KT // LANES

    @pl.when(c == 0)
    def _():
        def body(t, _):
            slct_scr[t] = _transpose_bf16(slc_ref[0, pl.ds(pl.multiple_of(t * NSA_KT, NSA_KT), NSA_KT), :])
            for h in range(halves):
                rows = pl.ds(pl.multiple_of(t * NSA_KT + h * LANES, LANES), LANES)
                wint_scr[halves * t + h] = _transpose_bf16(win_ref[0, rows, :])
            return 0
        lax.fori_loop(0, slct_scr.shape[0], body, 0)

    q = q_ref[0].astype(F32)
    heads = []
    for pr in range(2):
        qp = q[:, pr * LANES:(pr + 1) * LANES]
        heads.append(jnp.where(lane < HEAD_DIM, qp, 0.0))
        heads.append(jnp.where(lane < HEAD_DIM, pltpu.roll(qp, HEAD_DIM, 1), 0.0))
    q_heads = [h.astype(BF16) for h in heads]

    NC = kc_ref.shape[2]
    NS = ovt_ref.shape[0]

    def cmp_branch(nk):
        kc = kc_ref[0, 0, pl.ds(0, nk), :]
        kct = kct_ref[0, 0, :, pl.ds(0, nk)]
        visible = (_iota((nk, CQ), 0) * CMP_STRIDE + (CMP_LEN - 1)) <= c * CQ + _iota((nk, CQ), 1)
        psum = jnp.zeros((nk, CQ), F32)
        s_next = _dot_nt(kc, q_heads[0])
        for gi in range(C_GROUP):
            s = jnp.where(visible, s_next, NEG)
            if gi + 1 < C_GROUP:
                s_next = _dot_nt(kc, q_heads[gi + 1])
            m = jnp.max(s, axis=0, keepdims=True)
            e = jnp.where(visible, jnp.exp2(s - m), 0.0)
            den = jnp.sum(e, axis=0, keepdims=True)
            p = e / jnp.maximum(den, 1e-30)
            ocmp_scr[gi] = _dot(kct, p.astype(BF16))[HEAD_DIM:]
            psum = psum + p
        ovt = ovt_ref[:, pl.ds(0, nk)]
        imp = jnp.zeros((NS, CQ), F32)
        rest = psum
        for _ in range(IMP_SPLITS):
            piece = rest.astype(BF16)
            imp = imp + _dot(ovt, piece)
            rest = rest - piece.astype(F32)
        imp_scr[...] = imp

    half_chunks = (NC // 2) // (CQ // CMP_STRIDE)

    @pl.when(c < half_chunks)
    def _():
        cmp_branch(NC // 2)

    @pl.when(c >= half_chunks)
    def _():
        cmp_branch(NC)

    imp = imp_scr[...]
    jj = _iota((NS, CQ), 0)
    own = (c * CQ + _iota((NS, CQ), 1)) // SLC_BLOCK
    forced = (jj == 0) | (jj == own) | (jj == own - 1)
    score = jnp.where(forced, FORCED_SCORE, jnp.where(jj <= own, imp, INVALID_SCORE))
    picked = _top_picks(score, max(n_sel - N_FORCED, 0))
    sel_scr[...] = jnp.where(forced | picked, 0.0, NEG)

    reps = GROUP_W // CQ
    n_wide = C_GROUP // reps
    q_groups = [jnp.concatenate(q_heads[w * reps:(w + 1) * reps], axis=0) for w in range(n_wide)]
    acc_slc = acc_scrs[:n_wide]
    acc_win = acc_scrs[n_wide:]

    _flash_init(m_slc, acc_slc)
    per_tile = NSA_KT // SLC_BLOCK
    n_kt = slct_scr.shape[0]
    key_row2 = _iota((NSA_KT, CQ), 0)
    q_col2 = _iota((NSA_KT, CQ), 1)

    def slc_tile(kt, kind):
        ktc = jnp.minimum(kt, n_kt - 1)
        k = slc_ref[0, pl.ds(pl.multiple_of(ktc * NSA_KT, NSA_KT), NSA_KT), :]
        r = sel_scr[pl.ds(pl.multiple_of(per_tile * ktc, per_tile), per_tile), :]
        delta = c - halves * kt
        causal = jnp.where(kt * NSA_KT + key_row2 <= c * CQ + q_col2, 0.0, NEG) if kind == "diag" else None

        def elem(g):
            if kind == "far":
                return ()
            cols = pl.ds(g * GROUP_W, GROUP_W)
            bias = jnp.concatenate([bias_ref[jnp.clip(delta - h, 0, dmax), 0, :, cols] for h in range(halves)],
                                   axis=0)
            return (bias,) if causal is None else (bias, jnp.concatenate([causal] * reps, axis=1))

        def rows(g):
            out = []
            for i in range(per_tile):
                row = jnp.concatenate([r[i:i + 1, :]] * reps, axis=1)
                if kind == "far":
                    row = row + bias_ref[dmax, 0, pl.ds(0, 1), pl.ds(g * GROUP_W, GROUP_W)]
                out.append(row)
            return out

        vt = _with_ones(slct_scr[ktc], 0)
        return k, lambda g: vt, elem, rows

    def slc_iter(it, kind):
        _flash_tiles([slc_tile(NSA_UNROLL * it + u, kind) for u in range(NSA_UNROLL)], q_groups, m_slc, acc_slc)

    def slc_body(kind):
        def run(it, _):
            slc_iter(it, kind)
            return 0
        return run

    keys_per_iter = NSA_UNROLL * NSA_KT
    last_iter = (c * CQ) // keys_per_iter
    n_far_tiles = jnp.maximum(c - (_far_delta(LANES) + halves - 1) + halves, 0) // halves
    n_far_iters = n_far_tiles // NSA_UNROLL
    lax.fori_loop(0, n_far_iters, slc_body("far"), 0)
    lax.fori_loop(n_far_iters, last_iter, slc_body("near"), 0)
    slc_iter(last_iter, "diag")

    _flash_init(m_win, acc_win)
    n_wt = wint_scr.shape[0]
    first = WIN // LANES
    key_row = _iota((LANES, CQ), 0)
    q_col = _iota((LANES, CQ), 1)

    def win_tile(w):
        ks, vts, negs, dds = [], [], [], []
        for h in range(halves):
            dd = first - halves * w - h
            kt = c - dd
            ktc = jnp.clip(kt, 0, n_wt - 1)
            ks.append(win_ref[0, pl.ds(pl.multiple_of(ktc * LANES, LANES), LANES), :])
            vts.append(wint_scr[ktc])
            dist = dd * LANES + q_col - key_row
            negs.append(jnp.where((dist >= 0) & (dist < WIN) & (kt >= 0), 0.0, NEG))
            dds.append(max(dd, 0))
        negw = jnp.concatenate([jnp.concatenate(negs, axis=0)] * reps, axis=1)

        def elem(g):
            cols = pl.ds(g * GROUP_W, GROUP_W)
            return jnp.concatenate([bias_ref[dd, 0, :, cols] for dd in dds], axis=0), negw

        vt = _with_ones(jnp.concatenate(vts, axis=1), 0)
        return jnp.concatenate(ks, axis=0), lambda g: vt, elem, lambda g: (zero_row,)

    zero_row = jnp.zeros((1, GROUP_W), F32)
    n_win_tiles = -(-(WIN + CQ) // NSA_KT)
    _flash_tiles([win_tile(w) for w in range(n_win_tiles)], q_groups, m_win, acc_win)

    g_scr[...] = jax.nn.sigmoid(gate_ref[0].astype(F32)).T
    res = []
    for gi in range(C_GROUP):
        col0 = (n * C_GROUP + gi) * 3
        wcols = pl.ds((gi % reps) * CQ, CQ)
        a_s = acc_slc[gi // reps][:, wcols]
        a_w = acc_win[gi // reps][:, wcols]
        tot = (g_scr[pl.ds(col0, 1), :] * ocmp_scr[gi]
               + g_scr[pl.ds(col0 + 1, 1), :] * (a_s[HEAD_DIM:] / a_s[0:1])
               + g_scr[pl.ds(col0 + 2, 1), :] * (a_w[HEAD_DIM:] / a_w[0:1]))
        res.append(tot)
    o_ref[0] = jnp.concatenate(res, axis=0).T.astype(o_ref.dtype)


def _nsa(proj, tail, kcvc, kcvct, bias_c, overlap_t, dmax):
    B, S, _ = proj.shape
    n_chunks = S // C_CHUNK
    NC = kcvc.shape[2]
    NS = S // SLC_BLOCK
    n_tiles = bias_c.shape[0]
    Q = C_GROUP * C_CHUNK
    kern = functools.partial(_nsa_kernel, dmax=dmax, n_sel=min(SLC_TOPN, NS))
    return pl.pallas_call(
        kern,
        grid=(B, C_KV_HEADS, n_chunks),
        in_specs=[pl.BlockSpec((1, C_CHUNK, 2 * LANES), lambda b, n, c: (b, c, BLK_CQ // 2 + n)),
                  pl.BlockSpec((1, 1, NC, LANES), lambda b, n, c: (b, n, 0, 0)),
                  pl.BlockSpec((1, 1, LANES, NC), lambda b, n, c: (b, n, 0, 0)),
                  pl.BlockSpec((1, S, LANES), lambda b, n, c: (b, 0, BLK_CKV + 3 * n + 1)),
                  pl.BlockSpec((1, S, LANES), lambda b, n, c: (b, 0, BLK_CKV + 3 * n + 2)),
                  pl.BlockSpec((1, C_CHUNK, LANES), lambda b, n, c: (b, c, BLK_GATE)),
                  pl.BlockSpec((n_tiles, 1, C_CHUNK, Q), lambda b, n, c: (0, n, 0, 0)),
                  pl.BlockSpec((NS, NC), lambda b, n, c: (0, 0))],
        out_specs=pl.BlockSpec((1, C_CHUNK, 2 * LANES), lambda b, n, c: (b, c, n)),
        out_shape=jax.ShapeDtypeStruct((B, S, C_HEADS * HEAD_DIM), BF16),
        scratch_shapes=[pltpu.VMEM((S // NSA_KT, LANES, NSA_KT), BF16), pltpu.VMEM((n_chunks, LANES, LANES), BF16),
                        pltpu.VMEM((C_GROUP, HEAD_DIM, C_CHUNK), F32), pltpu.VMEM((NS, C_CHUNK), F32),
                        pltpu.VMEM((NS, C_CHUNK), F32), pltpu.VMEM((LANES, C_CHUNK), F32),
                        pltpu.VMEM((1, Q), F32), pltpu.VMEM((1, Q), F32)]
        + [pltpu.VMEM((ACC_ROWS, GROUP_W), F32)] * (2 * Q // GROUP_W),
        compiler_params=_cparams("arbitrary", "arbitrary", "arbitrary"),
        name="nsa",
    )(proj, kcvc, kcvct, tail, tail, tail, bias_c, overlap_t)


def _head_scale():
    scale = np.ones((HEAD_WIDTH,), np.float32)
    for g in range(len(A_CONFIGS)):
        scale[g * A_GROUP_WIDTH:g * A_GROUP_WIDTH + A_HEADS_PER_GROUP * HEAD_DIM] = Q_SCALE
    scale[A_WIDTH:A_WIDTH + B_HEADS * HEAD_DIM] = Q_SCALE * LOG2E
    scale[A_WIDTH + B_WIDTH:] = Q_SCALE * LOG2E
    return scale


def _split_w_in(w_in):
    depth, D, _ = w_in.shape
    scale = np.concatenate([_head_scale(), np.ones((IN_WIDTH - HEAD_WIDTH,), np.float32)])
    w = (w_in * scale).astype(BF16)
    kv = w[:, :, HEAD_WIDTH:HEAD_WIDTH + C_KV_WIDTH].reshape(depth, D, 3, 2, C_KV_HEADS, HEAD_DIM)
    kv = kv.transpose(0, 1, 4, 2, 3, 5).reshape(depth, D, C_KV_WIDTH)
    pad = jnp.zeros((depth, D, TAIL_WIDTH - C_KV_WIDTH - C_GATE_WIDTH), BF16)
    tail = jnp.concatenate([kv, w[:, :, HEAD_WIDTH + C_KV_WIDTH:], pad], axis=2)
    return w, tail


def _compress_weights(cmp_w1, cmp_w2, cmp_pe):
    depth = cmp_w1.shape[0]
    half = CMP_LEN // 2
    w1 = cmp_w1.reshape(depth, 2, 2, half, HEAD_DIM, CMP_HIDDEN)
    w1_blk = jnp.zeros((depth, 2, half, 2, HEAD_DIM, 2, CMP_HIDDEN), F32)
    w1_blk = w1_blk.at[:, :, :, 0, :, 0, :].set(w1[:, 0])
    w1_blk = w1_blk.at[:, :, :, 1, :, 1, :].set(w1[:, 1])
    w1_blk = w1_blk.reshape(depth, 2, half * 2 * HEAD_DIM, 2 * CMP_HIDDEN).astype(BF16)
    w2_blk = jnp.zeros((depth, 2, CMP_HIDDEN, 2, HEAD_DIM), F32)
    w2_blk = w2_blk.at[:, 0, :, 0, :].set(cmp_w2[:, 0])
    w2_blk = w2_blk.at[:, 1, :, 1, :].set(cmp_w2[:, 1])
    w2_blk = w2_blk.reshape(depth, 2 * CMP_HIDDEN, 2 * HEAD_DIM).astype(BF16)
    pe = cmp_pe.reshape(depth, 2, 2, half, HEAD_DIM)
    pe_rows = pe.transpose(0, 2, 3, 1, 4).reshape(depth, 2, half * 2 * HEAD_DIM)
    return w1_blk, w2_blk, pe_rows


def kernel(x, rel_table, w_in, w_out, cmp_w1, cmp_w2, cmp_pe, norm_attn, norm_mlp,
           w_up, conv_w, conv_b, w_down, norm_final):
    B, S, D = x.shape
    depth = w_in.shape[0]
    T = B * S

    a = jnp.arange(A_BLOCK)[:, None]
    bk = jnp.arange(2 * A_BLOCK)[None, :]
    rel = a + A_BLOCK - bk
    idx_a = jnp.stack([_t5_bucket(rel * d) for _, d in A_CONFIGS]).astype(jnp.int32)
    bias_a = _bias_tiles(rel_table[:A_HEADS], idx_a)
    dmax_b = _n_far_tiles(S // MOBA_BLOCK, MOBA_BLOCK)
    bias_b = _bias_tiles(rel_table[A_HEADS:A_HEADS + B_HEADS] * LOG2E,
                         _toeplitz_bucket_tiles_t(dmax_b + 1, MOBA_BLOCK), 2)
    dmax_c = _n_far_tiles(S // C_CHUNK, C_CHUNK)
    bias_c = _bias_tiles(rel_table[A_HEADS + B_HEADS:] * LOG2E,
                         _toeplitz_bucket_tiles_t(dmax_c + 1, C_CHUNK), C_GROUP)

    n16 = S // CMP_STRIDE
    ci = jnp.arange(n16)[None, :] * CMP_STRIDE
    sj = jnp.arange(S // SLC_BLOCK)[:, None] * SLC_BLOCK
    overlap_t = ((ci < sj + SLC_BLOCK) & (ci + CMP_LEN > sj)).astype(BF16)

    w_head, w_tail = _split_w_in(w_in)
    w_out_b = w_out.astype(BF16)
    w_up_b = w_up.astype(BF16)
    w_down_b = w_down.astype(BF16)
    gains_attn = norm_attn.reshape(depth, 1, D)
    gains_mlp = norm_mlp.reshape(depth, 1, D)
    conv_b3 = conv_b.reshape(depth, 1, -1)
    w1_blk, w2_blk, pe_rows = _compress_weights(cmp_w1, cmp_w2, cmp_pe)

    x2 = x.reshape(T, D)
    h = _norm(x2, gains_attn, 0)
    for l in range(depth):
        proj = _inproj(h, w_head, l, HEAD_WIDTH, tn=512).reshape(B, S, HEAD_WIDTH)
        tail = _inproj(h, w_tail, l, TAIL_WIDTH, tn=TAIL_WIDTH).reshape(B, S, TAIL_WIDTH)
        os_, lses = [], []
        for g, (_, d) in enumerate(A_CONFIGS):
            o, lse = _dilated_group(proj, bias_a, g, d)
            os_.append(o)
            lses.append(lse)
        o_b = _moba(proj, bias_b, dmax_b).reshape(T, B_HEADS * HEAD_DIM)
        cmp_cols = tail[:, :, :BLK_GATE * LANES].reshape(B, S, C_KV_HEADS, 3, LANES)[:, :, :, 0]
        sub = cmp_cols.transpose(0, 2, 1, 3).reshape(B, C_KV_HEADS, n16, CMP_STRIDE * LANES)
        kcvc, kcvct = _compress(sub, pe_rows, w1_blk, w2_blk, l)
        o_c = _nsa(proj, tail, kcvc, kcvct, bias_c, overlap_t, dmax_c).reshape(T, C_HEADS * HEAD_DIM)
        x2, h = _outproj(x2, os_, lses, o_b, o_c, w_out_b, gains_mlp, l)
        last = l == depth - 1
        next_gain = norm_final.reshape(1, D) if last else norm_attn[l + 1].reshape(1, D)
        x2, h = _ffn(x2, h, w_up_b, conv_w, conv_b3, w_down_b, l, next_gain, F32 if last else BF16, S)
    return h.reshape(B, S, D)
```

```python
import functools
import math

import numpy as np
import jax
import jax.numpy as jnp
from jax import lax
from jax.experimental import pallas as pl
from jax.experimental.pallas import tpu as pltpu

F32 = jnp.float32
BF16 = jnp.bfloat16

HEAD_DIM = 64
LANES = 128
A_CONFIGS = ((128, 1), (512, 4), (2048, 16))
A_HEADS_PER_GROUP = 4
A_HEADS = 12
A_BLOCK = 128
A_BLOCKS_PER_STEP = 2
B_HEADS = 8
MOBA_BLOCK = 256
MOBA_TOPK = 3
MOBA_UNROLL = 4
C_KV_HEADS = 3
C_GROUP = 4
C_HEADS = 12
CMP_STRIDE = 16
CMP_LEN = 32
CMP_HIDDEN = 128
SLC_BLOCK = 64
SLC_TOPN = 16
N_FORCED = 3
IMP_SPLITS = 3
WIN = 512
INVALID_SCORE = -1.0
FORCED_SCORE = -2.0
N_BUCKETS = 32
T5_MAX_DIST = 2048
EPS = 1e-6
NEG = -1e30
M_FLOOR = -1e20
Q_SCALE = HEAD_DIM ** -0.5
LOG2E = math.log2(math.e)
C_CHUNK = 256
ACC_ROWS = 128
GROUP_W = 128
NSA_KT = 256
NSA_UNROLL = 4

A_WIDTH = 3 * A_HEADS * HEAD_DIM
A_GROUP_WIDTH = 3 * A_HEADS_PER_GROUP * HEAD_DIM
B_WIDTH = 3 * B_HEADS * HEAD_DIM
C_Q_WIDTH = C_HEADS * HEAD_DIM
C_KV_WIDTH = 6 * C_KV_HEADS * HEAD_DIM
C_GATE_WIDTH = 3 * C_HEADS
IN_WIDTH = A_WIDTH + B_WIDTH + C_Q_WIDTH + C_KV_WIDTH + C_GATE_WIDTH
BLK_B = A_WIDTH // LANES
BLK_CQ = BLK_B + B_WIDTH // LANES
HEAD_WIDTH = A_WIDTH + B_WIDTH + C_Q_WIDTH
BLK_CKV = 0
BLK_GATE = 3 * C_KV_HEADS
TAIL_WIDTH = (BLK_GATE + 1) * LANES

VMEM_LIMIT = 56 * 1024 * 1024


def _cparams(*sem):
    return pltpu.CompilerParams(dimension_semantics=sem, vmem_limit_bytes=VMEM_LIMIT)


def _dot_nt(a, b):
    return lax.dot_general(a, b, (((1,), (1,)), ((), ())), preferred_element_type=F32)


def _dot(a, b, **kw):
    return jnp.dot(a, b, preferred_element_type=F32, **kw)


def _transpose_bf16(x):
    return x.astype(F32).T.astype(BF16)


def _iota(shape, dim):
    return lax.broadcasted_iota(jnp.int32, shape, dim)


def _t5_bucket(dist):
    n = jnp.maximum(dist, 0)
    max_exact = N_BUCKETS // 2
    nf = jnp.maximum(n, 1).astype(F32)
    large = max_exact + (jnp.log(nf / max_exact) / math.log(T5_MAX_DIST / max_exact)
                         * (N_BUCKETS - max_exact)).astype(jnp.int32)
    large = jnp.minimum(large, N_BUCKETS - 1)
    return jnp.where(n < max_exact, n, large)


def _bias_lookup_kernel(tbl_ref, idx_ref, o_ref, *, hpr):
    idx = idx_ref[0]
    C = idx.shape[1]
    for h in range(tbl_ref.shape[0]):
        acc = jnp.zeros(idx.shape, F32)
        for b in range(N_BUCKETS):
            acc = jnp.where(idx == b, tbl_ref[h, b], acc)
        o_ref[0, h // hpr, :, (h % hpr) * C:(h % hpr + 1) * C] = acc


def _bias_tiles(tbl, idx, heads_per_row=1):
    n, R, C = idx.shape
    H = tbl.shape[0]
    hpr = heads_per_row
    return pl.pallas_call(
        functools.partial(_bias_lookup_kernel, hpr=hpr),
        grid=(n,),
        in_specs=[pl.BlockSpec(memory_space=pltpu.SMEM),
                  pl.BlockSpec((1, R, C), lambda t: (t, 0, 0))],
        out_specs=pl.BlockSpec((1, H // hpr, R, hpr * C), lambda t: (t, 0, 0, 0)),
        out_shape=jax.ShapeDtypeStruct((n, H // hpr, R, hpr * C), F32),
        compiler_params=_cparams("arbitrary"),
        name="bias_tiles",
    )(tbl, idx)


def _toeplitz_bucket_tiles_t(n_tiles, size):
    d = jnp.arange(n_tiles)[:, None, None] * size
    key = jnp.arange(size)[None, :, None]
    qry = jnp.arange(size)[None, None, :]
    return _t5_bucket(d + qry - key).astype(jnp.int32)


def _far_delta(size):
    return -(-(T5_MAX_DIST + size - 1) // size)


def _n_far_tiles(n_chunks, size):
    return min(_far_delta(size), n_chunks - 1)


def _norm_rows(x, g):
    ms = jnp.mean(x * x, axis=-1, keepdims=True)
    return x * lax.rsqrt(ms + EPS) * g


def _norm_kernel(x_ref, g_ref, o_ref):
    o_ref[...] = _norm_rows(x_ref[...], g_ref[...]).astype(o_ref.dtype)


def _norm(x2, gains, l, tm=1024):
    T, D = x2.shape
    return pl.pallas_call(
        _norm_kernel,
        grid=(T // tm,),
        in_specs=[pl.BlockSpec((tm, D), lambda i: (i, 0)),
                  pl.BlockSpec((None, 1, D), lambda i: (l, 0, 0))],
        out_specs=pl.BlockSpec((tm, D), lambda i: (i, 0)),
        out_shape=jax.ShapeDtypeStruct((T, D), BF16),
        compiler_params=_cparams("arbitrary"),
        name="first_norm",
    )(x2, gains)


def _inproj_kernel(h_ref, w_ref, o_ref):
    o_ref[...] = _dot(h_ref[...], w_ref[...]).astype(o_ref.dtype)


def _inproj(h, w_all, l, N, tn, tm=1024):
    T, D = h.shape
    return pl.pallas_call(
        _inproj_kernel,
        grid=(T // tm, N // tn),
        in_specs=[pl.BlockSpec((tm, D), lambda i, j: (i, 0)),
                  pl.BlockSpec((None, D, tn), lambda i, j: (l, 0, j))],
        out_specs=pl.BlockSpec((tm, tn), lambda i, j: (i, j)),
        out_shape=jax.ShapeDtypeStruct((T, N), BF16),
        compiler_params=_cparams("arbitrary", "arbitrary"),
        name="inproj",
    )(h, w_all)


def _outproj_kernel(x_ref, o0_ref, o1_ref, o2_ref, l0_ref, l1_ref, l2_ref, b_ref, c_ref, w_ref, g_ref, o_ref, h_ref):
    ls = [l0_ref[...], l1_ref[...], l2_ref[...]]
    m = jnp.maximum(jnp.maximum(ls[0], ls[1]), ls[2])
    es = [jnp.exp(l - m) for l in ls]
    inv = 1.0 / (es[0] + es[1] + es[2])
    gw = o0_ref.shape[1]
    acc = jnp.zeros(o_ref.shape, F32)
    for g, og_ref in enumerate((o0_ref, o1_ref, o2_ref)):
        acc += _dot((og_ref[...] * (es[g] * inv)).astype(BF16), w_ref[pl.ds(g * gw, gw), :])
    ra, rb = 3 * gw, b_ref.shape[1]
    acc += _dot(b_ref[...], w_ref[pl.ds(ra, rb), :])
    acc += _dot(c_ref[...], w_ref[pl.ds(ra + rb, c_ref.shape[1]), :])
    x_new = x_ref[...] + acc
    o_ref[...] = x_new
    h_ref[...] = _norm_rows(x_new, g_ref[...]).astype(h_ref.dtype)


def _outproj(x2, os_, lses, ob, oc, w_all, gains, l, tm=512):
    T, D = x2.shape
    row = lambda i: (i, 0)
    gspec = pl.BlockSpec((tm, os_[0].shape[1]), row)
    return pl.pallas_call(
        _outproj_kernel,
        grid=(T // tm,),
        in_specs=[pl.BlockSpec((tm, D), row)] + [gspec] * 6 + [
                  pl.BlockSpec((tm, ob.shape[1]), row),
                  pl.BlockSpec((tm, oc.shape[1]), row),
                  pl.BlockSpec((None, w_all.shape[1], D), lambda i: (l, 0, 0)),
                  pl.BlockSpec((None, 1, D), lambda i: (l, 0, 0))],
        out_specs=[pl.BlockSpec((tm, D), row), pl.BlockSpec((tm, D), row)],
        out_shape=[jax.ShapeDtypeStruct((T, D), F32), jax.ShapeDtypeStruct((T, D), BF16)],
        compiler_params=_cparams("arbitrary"),
        name="outproj",
    )(x2, *os_, *lses, ob, oc, w_all, gains)


FFN_PARTS = 2
FFN_HALO = 16


def _ffn_kernel(x_ref, h_ref, halo_ref, wa_ref, wg_ref, cwa_ref, cwg_ref, cba_ref, cbg_ref, wd_ref, g_ref,
                o_ref, n_ref, h_scr, *u_scrs, tm, blocks_per_seq):
    i = pl.program_id(0)
    f = pl.program_id(1)

    @pl.when(f == 0)
    def _():
        o_ref[...] = x_ref[...]
        keep = jnp.where(i % blocks_per_seq == 0, 0.0, 1.0).astype(BF16)
        h_scr[pl.ds(0, FFN_HALO), :] = halo_ref[0] * keep
        h_scr[pl.ds(FFN_HALO, tm), :] = h_ref[...]

    h = h_scr[...]
    n_parts = len(u_scrs) // 2
    part = wa_ref.shape[1] // n_parts

    def up(pt):
        cols = pl.ds(pt * part, part)
        u_scrs[2 * pt][...] = _dot(h, wa_ref[:, cols])
        u_scrs[2 * pt + 1][...] = _dot(h, wg_ref[:, cols])

    def conv(u_scr, pt, cw_ref, cb_ref):
        cols = pl.ds(pt * part, part)
        acc = cb_ref[:, cols]
        for j in range(3):
            acc = acc + cw_ref[pl.ds(j, 1), cols] * u_scr[pl.ds(FFN_HALO - j, tm), :]
        return acc

    up(0)
    for pt in range(n_parts):
        if pt + 1 < n_parts:
            up(pt + 1)
        ya = conv(u_scrs[2 * pt], pt, cwa_ref, cba_ref)
        yg = conv(u_scrs[2 * pt + 1], pt, cwg_ref, cbg_ref)
        act = (yg * jax.nn.sigmoid(yg) * ya).astype(BF16)
        o_ref[...] += _dot(act, wd_ref[pl.ds(pt * part, part), :])

    @pl.when(f == pl.num_programs(1) - 1)
    def _():
        n_ref[...] = _norm_rows(o_ref[...], g_ref[...]).astype(n_ref.dtype)


def _ffn(x2, h, w_up, conv_w, conv_b, w_down, l, next_gain, next_dtype, seq, tm=512, tf=512):
    T, D = x2.shape
    Fh = w_down.shape[1]
    nf = Fh // tf
    halo_view = h.reshape(T // FFN_HALO, FFN_HALO, D)
    rb = tm // FFN_HALO
    kern = functools.partial(_ffn_kernel, tm=tm, blocks_per_seq=seq // tm)
    row = lambda i, f: (i, 0)
    return pl.pallas_call(
        kern,
        grid=(T // tm, nf),
        in_specs=[pl.BlockSpec((tm, D), row),
                  pl.BlockSpec((tm, D), row),
                  pl.BlockSpec((1, FFN_HALO, D), lambda i, f: (jnp.maximum(i * rb - 1, 0), 0, 0)),
                  pl.BlockSpec((None, D, tf), lambda i, f: (l, 0, f)),
                  pl.BlockSpec((None, D, tf), lambda i, f: (l, 0, f + nf)),
                  pl.BlockSpec((None, 3, tf), lambda i, f: (l, 0, f)),
                  pl.BlockSpec((None, 3, tf), lambda i, f: (l, 0, f + nf)),
                  pl.BlockSpec((None, 1, tf), lambda i, f: (l, 0, f)),
                  pl.BlockSpec((None, 1, tf), lambda i, f: (l, 0, f + nf)),
                  pl.BlockSpec((None, tf, D), lambda i, f: (l, f, 0)),
                  pl.BlockSpec((1, D), lambda i, f: (0, 0))],
        out_specs=[pl.BlockSpec((tm, D), row), pl.BlockSpec((tm, D), row)],
        out_shape=[jax.ShapeDtypeStruct((T, D), F32), jax.ShapeDtypeStruct((T, D), next_dtype)],
        scratch_shapes=[pltpu.VMEM((tm + FFN_HALO, D), BF16)]
        + [pltpu.VMEM((tm + FFN_HALO, tf // FFN_PARTS), F32)] * (2 * FFN_PARTS),
        compiler_params=_cparams("arbitrary", "arbitrary"),
        name="convffn",
    )(x2, h, halo_view, w_up, w_up, conv_w, conv_w, conv_b, conv_b, w_down, next_gain)


def _stack_pair(qp):
    lane = _iota(qp.shape, 1)
    zero = jnp.zeros_like(qp)
    return jnp.concatenate([jnp.where(lane < HEAD_DIM, qp, zero),
                            jnp.where(lane >= HEAD_DIM, qp, zero)], axis=0)


def _unstack_pair(o):
    R = o.shape[0] // 2
    lane = _iota((R, LANES), 1)
    return jnp.where(lane < HEAD_DIM, o[:R], o[R:])


def _dilated_kernel(q_ref, kc_ref, kp_ref, vc_ref, vp_ref, bias_ref, o_ref, lse_ref):
    i = pl.program_id(2)
    R = 2 * A_BLOCK
    n_blocks = q_ref.shape[1] // A_BLOCK
    a = _iota((R, 2 * A_BLOCK), 0) % A_BLOCK
    bk = _iota((R, 2 * A_BLOCK), 1)
    rel = a + A_BLOCK - bk
    band = (rel >= 0) & (rel <= A_BLOCK)
    first = band & ((bk >= A_BLOCK) | (i > 0))
    n_pairs = q_ref.shape[2] // LANES
    units = [(j, p) for j in range(n_blocks) for p in range(n_pairs)]

    def rows(j):
        return slice(j * A_BLOCK, (j + 1) * A_BLOCK)

    def with_prev(cur_ref, prev_ref, j, cols):
        prev = prev_ref[0, :, cols] if j == 0 else cur_ref[0, rows(j - 1), cols]
        return jnp.concatenate([prev, cur_ref[0, rows(j), cols]], axis=0)

    def scores(j, p):
        cols = slice(p * LANES, (p + 1) * LANES)
        qs = _stack_pair(q_ref[0, rows(j), cols])
        s = _dot_nt(qs, with_prev(kc_ref, kp_ref, j, cols)) + bias_ref[0, 2 * p:2 * p + 2].reshape(R, 2 * A_BLOCK)
        return jnp.where(first if j == 0 else band, s, NEG)

    s_all = [scores(j, p) for j, p in units]
    for (j, p), s in zip(units, s_all):
        cols = slice(p * LANES, (p + 1) * LANES)
        mask = first if j == 0 else band
        m = jnp.max(s, axis=1, keepdims=True)
        e = jnp.where(mask, jnp.exp(s - m), 0.0)
        den = jnp.sum(e, axis=1, keepdims=True)
        o = _dot(e.astype(BF16), with_prev(vc_ref, vp_ref, j, cols)) / den
        lse = m + jnp.log(den)
        o_ref[0, rows(j), cols] = _unstack_pair(o)
        lse_ref[0, rows(j), cols] = _unstack_pair(jnp.broadcast_to(lse, (R, LANES)))


def _dilated_group(proj, bias_a, g, dilation):
    B, S, _ = proj.shape
    L = S // dilation
    nq = L // A_BLOCK
    per_step = A_BLOCKS_PER_STEP if nq % A_BLOCKS_PER_STEP == 0 else 1
    W = A_HEADS_PER_GROUP * HEAD_DIM
    if dilation == 1:
        view, base = proj, g * A_GROUP_WIDTH // W
    else:
        cols = proj[:, :, g * A_GROUP_WIDTH:(g + 1) * A_GROUP_WIDTH]
        view, base = cols.reshape(B, L, dilation * A_GROUP_WIDTH), 0

    def cur(part):
        return pl.BlockSpec((1, per_step * A_BLOCK, W), lambda b, r, i: (b, i, base + r * 3 + part))

    def prev(part):
        return pl.BlockSpec((1, A_BLOCK, W),
                            lambda b, r, i: (b, jnp.maximum(i * per_step - 1, 0), base + r * 3 + part))

    out_spec = pl.BlockSpec((1, per_step * A_BLOCK, W), lambda b, r, i: (b, i, r))
    o, lse = pl.pallas_call(
        _dilated_kernel,
        grid=(B, dilation, nq // per_step),
        in_specs=[cur(0), cur(1), prev(1), cur(2), prev(2),
                  pl.BlockSpec((1, A_HEADS_PER_GROUP, A_BLOCK, 2 * A_BLOCK), lambda b, r, i: (g, g, 0, 0))],
        out_specs=[out_spec, out_spec],
        out_shape=[jax.ShapeDtypeStruct((B, L, dilation * W), F32)] * 2,
        compiler_params=_cparams("arbitrary", "arbitrary", "arbitrary"),
        name=f"dilated_g{g}",
    )(view, view, view, view, view, bias_a)
    return o.reshape(B * S, W), lse.reshape(B * S, W)


def _flash_init(m_scr, acc_scrs):
    m_scr[...] = jnp.full(m_scr.shape, M_FLOOR, F32)
    for acc in acc_scrs:
        acc[...] = jnp.zeros(acc.shape, F32)


def _with_ones(vt, row):
    return jnp.where(_iota(vt.shape, 0) == row, jnp.ones_like(vt), vt)


def _flash_tiles(tiles, q_groups, m_scr, acc_scrs):
    G = len(q_groups)
    W = q_groups[0].shape[0]
    m = [m_scr[:, g * W:(g + 1) * W] for g in range(G)]
    units = [(t, g) for t in range(len(tiles)) for g in range(G)]

    def scores(t, g):
        k, _, elem_of, _ = tiles[t]
        s = _dot_nt(k, q_groups[g])
        for term in elem_of(g):
            s = s + term
        return s

    def apply(g, alpha, pv):
        acc_scrs[g][...] = alpha * acc_scrs[g][...] + pv

    s_next = scores(*units[0])
    pending = None
    for n, (t, g) in enumerate(units):
        s = s_next
        if n + 1 < len(units):
            s_next = scores(*units[n + 1])
        rows = tiles[t][3](g)
        seg = s.shape[0] // len(rows)
        parts = [s[i * seg:(i + 1) * seg] for i in range(len(rows))]
        m_new = m[g]
        for part, row in zip(parts, rows):
            m_new = jnp.maximum(m_new, jnp.max(part, axis=0, keepdims=True) + row)
        alpha = jnp.exp2(m[g] - m_new)
        p = jnp.concatenate([jnp.exp2(part - (m_new - row)) for part, row in zip(parts, rows)], axis=0)
        m[g] = m_new
        pv = _dot(tiles[t][1](g), p.astype(BF16))
        if pending is not None:
            apply(*pending)
        pending = (g, alpha, pv)
    apply(*pending)
    m_scr[...] = jnp.concatenate(m, axis=1)


PICKED = -3e38
NO_INDEX = 1e9


def _top_picks(score, n_pick):
    idx = _iota(score.shape, 0).astype(F32)

    def pick(_, g):
        mx = jnp.max(g, axis=0, keepdims=True)
        first = jnp.min(jnp.where(g == mx, idx, NO_INDEX), axis=0, keepdims=True)
        return jnp.where(idx == first, PICKED, g)

    return lax.fori_loop(0, n_pick, pick, score) == PICKED


def _moba_kernel(q_ref, k_ref, v_ref, bias_ref, o_ref, km_scr, vt_scr, sel_scr, m_scr, *acc_scrs,
                 dmax, n_blocks):
    c = pl.program_id(2)
    BLK = MOBA_BLOCK
    Q = 2 * BLK

    @pl.when(c == 0)
    def _():
        def body(j, _):
            rows = pl.ds(pl.multiple_of(j * BLK, BLK), BLK)
            km_scr[pl.ds(j, 1), :] = jnp.mean(k_ref[0, rows, :].astype(F32), axis=0, keepdims=True)
            vt_scr[j] = _transpose_bf16(v_ref[0, rows, :])
            return 0
        lax.fori_loop(0, n_blocks, body, 0)

    qs = _stack_pair(q_ref[0])
    gate = _dot_nt(km_scr[...].astype(BF16), qs)
    blk = _iota(gate.shape, 0)
    gate = jnp.where(blk < c, gate, NEG)
    sel = _top_picks(gate, min(MOBA_TOPK, n_blocks))
    sel_scr[...] = jnp.where((blk < c) & sel, 0.0, NEG)

    _flash_init(m_scr, acc_scrs)
    per_head = BLK // GROUP_W
    q_groups = [qs[g * GROUP_W:(g + 1) * GROUP_W] for g in range(Q // GROUP_W)]

    def head_values(vt):
        both = [_with_ones(vt, (1 - h) * HEAD_DIM) for h in range(2)]
        return lambda g: both[g // per_head]

    def far_rows(g):
        return bias_ref[dmax, 0, pl.ds(0, 1), pl.ds(g * GROUP_W, GROUP_W)]

    def past_tile(j, far):
        jc = jnp.minimum(j, n_blocks - 1)
        kj = k_ref[0, pl.ds(pl.multiple_of(jc * BLK, BLK), BLK), :]
        tile = jnp.clip(c - j, 0, dmax)
        selrow = sel_scr[pl.ds(jc, 1), :]

        def elem(g):
            return () if far else (bias_ref[tile, 0, :, pl.ds(g * GROUP_W, GROUP_W)],)

        def rows(g):
            row = selrow[:, g * GROUP_W:(g + 1) * GROUP_W]
            return (row + far_rows(g),) if far else (row,)

        return kj, head_values(vt_scr[jc]), elem, rows

    def body(far):
        def run(it, _):
            _flash_tiles([past_tile(MOBA_UNROLL * it + u, far) for u in range(MOBA_UNROLL)],
                         q_groups, m_scr, acc_scrs)
            return 0
        return run

    n_far_iters = jnp.maximum(c - _far_delta(BLK) + 1, 0) // MOBA_UNROLL
    lax.fori_loop(0, n_far_iters, body(True), 0)
    lax.fori_loop(n_far_iters, (c + MOBA_UNROLL - 1) // MOBA_UNROLL, body(False), 0)

    ko = k_ref[0, pl.ds(pl.multiple_of(c * BLK, BLK), BLK), :]

    def own_elem(g):
        qpos = (g * GROUP_W + _iota((BLK, GROUP_W), 1)) % BLK
        return (bias_ref[0, 0, :, pl.ds(g * GROUP_W, GROUP_W)],
                jnp.where(_iota((BLK, GROUP_W), 0) <= qpos, 0.0, NEG))

    zero_row = jnp.zeros((1, GROUP_W), F32)
    _flash_tiles([(ko, head_values(vt_scr[c]), own_elem, lambda g: (zero_row,))], q_groups, m_scr, acc_scrs)
    outs = []
    for g, acc in enumerate(acc_scrs):
        h = g // per_head
        a = acc[...]
        den = a[(1 - h) * HEAD_DIM:(1 - h) * HEAD_DIM + 1]
        outs.append(a[h * HEAD_DIM:(h + 1) * HEAD_DIM] / den)
    o2 = jnp.concatenate([jnp.concatenate(outs[h * per_head:(h + 1) * per_head], axis=1) for h in range(2)],
                         axis=0)
    o_ref[0] = o2.T.astype(o_ref.dtype)


def _moba(proj, bias_b, dmax):
    B, S, _ = proj.shape
    NB = S // MOBA_BLOCK
    n_tiles = bias_b.shape[0]
    qb, kb, vb = BLK_B, BLK_B + 4, BLK_B + 8
    kern = functools.partial(_moba_kernel, dmax=dmax, n_blocks=NB)
    Q = 2 * MOBA_BLOCK
    return pl.pallas_call(
        kern,
        grid=(B, B_HEADS // 2, NB),
        in_specs=[pl.BlockSpec((1, MOBA_BLOCK, LANES), lambda b, p, c: (b, c, qb + p)),
                  pl.BlockSpec((1, S, LANES), lambda b, p, c: (b, 0, kb + p)),
                  pl.BlockSpec((1, S, LANES), lambda b, p, c: (b, 0, vb + p)),
                  pl.BlockSpec((n_tiles, 1, MOBA_BLOCK, Q), lambda b, p, c: (0, p, 0, 0))],
        out_specs=pl.BlockSpec((1, MOBA_BLOCK, LANES), lambda b, p, c: (b, c, p)),
        out_shape=jax.ShapeDtypeStruct((B, S, B_HEADS * HEAD_DIM), BF16),
        scratch_shapes=[pltpu.VMEM((NB, LANES), F32), pltpu.VMEM((NB, LANES, MOBA_BLOCK), BF16),
                        pltpu.VMEM((NB, Q), F32), pltpu.VMEM((1, Q), F32)]
        + [pltpu.VMEM((ACC_ROWS, GROUP_W), F32)] * (Q // GROUP_W),
        compiler_params=_cparams("arbitrary", "arbitrary", "arbitrary"),
        name="moba",
    )(proj, proj, proj, bias_b)


def _gelu_tanh(x):
    return 0.5 * x * (1.0 + jnp.tanh(math.sqrt(2.0 / math.pi) * (x + 0.044715 * (x * x * x))))


def _compress_kernel(sub_ref, pe_ref, w1_ref, w2_ref, o_ref, ot_ref, v_scr):
    n16 = sub_ref.shape[2]
    sub = sub_ref[0, 0].astype(F32)
    top = (sub + pe_ref[0:1, :]).astype(BF16)
    bot = (sub + pe_ref[1:2, :]).astype(BF16)
    u = _dot(top, w1_ref[0])
    v_scr[pl.ds(0, n16), :] = _dot(bot, w1_ref[1])
    v_scr[pl.ds(n16, 8), :] = jnp.zeros((8, v_scr.shape[1]), F32)
    hidden = u + v_scr[pl.ds(1, n16), :]
    out = _dot(_gelu_tanh(hidden).astype(BF16), w2_ref[...])
    o_ref[0, 0] = out.astype(o_ref.dtype)
    ot_ref[0, 0] = out.T.astype(ot_ref.dtype)


def _compress(sub, pe_rows, w1_blk, w2_blk, l):
    B, KV, n16, W = sub.shape
    return pl.pallas_call(
        _compress_kernel,
        grid=(B, KV),
        in_specs=[pl.BlockSpec((1, 1, n16, W), lambda b, n: (b, n, 0, 0)),
                  pl.BlockSpec((None, 2, W), lambda b, n: (l, 0, 0)),
                  pl.BlockSpec((None, 2, W, 2 * CMP_HIDDEN), lambda b, n: (l, 0, 0, 0)),
                  pl.BlockSpec((None, 2 * CMP_HIDDEN, LANES), lambda b, n: (l, 0, 0))],
        out_specs=[pl.BlockSpec((1, 1, n16, LANES), lambda b, n: (b, n, 0, 0)),
                   pl.BlockSpec((1, 1, LANES, n16), lambda b, n: (b, n, 0, 0))],
        out_shape=[jax.ShapeDtypeStruct((B, KV, n16, LANES), BF16),
                   jax.ShapeDtypeStruct((B, KV, LANES, n16), BF16)],
        scratch_shapes=[pltpu.VMEM((n16 + 8, 2 * CMP_HIDDEN), F32)],
        compiler_params=_cparams("arbitrary", "arbitrary"),
        name="nsa_compress",
    )(sub, pe_rows, w1_blk, w2_blk)


def _nsa_kernel(q_ref, kc_ref, kct_ref, slc_ref, win_ref, gate_ref, bias_ref, ovt_ref,
                o_ref, slct_scr, wint_scr, ocmp_scr, imp_scr, sel_scr, g_scr, m_slc, m_win, *acc_scrs, dmax, n_sel):
    n = pl.program_id(1)
    c = pl.program_id(2)
    CQ = C_CHUNK
    W = GROUP_W
    nq = CQ // W
    G = C_GROUP * nq
    halves = NSA_KT // LANES
    per128 = CQ // LANES

    @pl.when(c == 0)
    def _():
        def body(t, _):
            rows = pl.ds(pl.multiple_of(t * NSA_KT, NSA_KT), NSA_KT)
            slct_scr[t] = _transpose_bf16(slc_ref[0, rows, :])
            wint_scr[t] = _transpose_bf16(win_ref[0, rows, :])
            return 0
        lax.fori_loop(0, slct_scr.shape[0], body, 0)

    lane = _iota((CQ, LANES), 1)
    q = q_ref[0].astype(F32)
    q_groups = []
    for pr in range(2):
        qp = q[:, pr * LANES:(pr + 1) * LANES]
        for head in (jnp.where(lane < HEAD_DIM, qp, 0.0), jnp.where(lane < HEAD_DIM, pltpu.roll(qp, HEAD_DIM, 1), 0.0)):
            hb = head.astype(BF16)
            q_groups.extend(hb[j * W:(j + 1) * W] for j in range(nq))

    def tok(j, shape):
        return c * CQ + j * W + _iota(shape, 1)

    NC = kc_ref.shape[2]
    NS = ovt_ref.shape[0]

    def cmp_branch(nk):
        kc = kc_ref[0, 0, pl.ds(0, nk), :]
        kct = kct_ref[0, 0, :, pl.ds(0, nk)]
        ovt = ovt_ref[:, pl.ds(0, nk)]
        key_end = _iota((nk, W), 0) * CMP_STRIDE + (CMP_LEN - 1)
        visible = [key_end <= tok(j, (nk, W)) for j in range(nq)]
        psum = [jnp.zeros((nk, W), F32) for _ in range(nq)]
        s_next = _dot_nt(kc, q_groups[0])
        for g in range(G):
            j = g % nq
            s = jnp.where(visible[j], s_next, NEG)
            if g + 1 < G:
                s_next = _dot_nt(kc, q_groups[g + 1])
            m = jnp.max(s, axis=0, keepdims=True)
            e = jnp.where(visible[j], jnp.exp2(s - m), 0.0)
            den = jnp.sum(e, axis=0, keepdims=True)
            p = e / jnp.maximum(den, 1e-30)
            ocmp_scr[g] = _dot(kct, p.astype(BF16))[HEAD_DIM:]
            psum[j] = psum[j] + p
        for j in range(nq):
            imp = jnp.zeros((NS, W), F32)
            rest = psum[j]
            for _ in range(IMP_SPLITS):
                piece = rest.astype(BF16)
                imp = imp + _dot(ovt, piece)
                rest = rest - piece.astype(F32)
            imp_scr[:, j * W:(j + 1) * W] = imp

    half_chunks = (NC // 2) // (CQ // CMP_STRIDE)

    @pl.when(c < half_chunks)
    def _():
        cmp_branch(NC // 2)

    @pl.when(c >= half_chunks)
    def _():
        cmp_branch(NC)

    imp = imp_scr[...]
    jj = _iota((NS, CQ), 0)
    own = (c * CQ + _iota((NS, CQ), 1)) // SLC_BLOCK
    forced = (jj == 0) | (jj == own) | (jj == own - 1)
    score = jnp.where(forced, FORCED_SCORE, jnp.where(jj <= own, imp, INVALID_SCORE))
    picked = _top_picks(score, max(n_sel - N_FORCED, 0))
    sel_scr[...] = jnp.where(forced | picked, 0.0, NEG)

    acc_slc = acc_scrs[:G]
    acc_win = acc_scrs[G:]
    key_row = _iota((NSA_KT, W), 0)
    zero_row = jnp.zeros((1, W), F32)

    def bias_tile(g, delta0, dynamic):
        gi, j = divmod(g, nq)
        cols = pl.ds(gi * LANES, LANES)
        idx = [delta0 + j - h for h in range(halves)]
        idx = [jnp.clip(d, 0, dmax) for d in idx] if dynamic else [max(d, 0) for d in idx]
        return jnp.concatenate([bias_ref[d, 0, :, cols] for d in idx], axis=0)

    _flash_init(m_slc, acc_slc)
    per_tile = NSA_KT // SLC_BLOCK
    n_kt = slct_scr.shape[0]

    def slc_tile(kt, kind):
        ktc = jnp.minimum(kt, n_kt - 1)
        k = slc_ref[0, pl.ds(pl.multiple_of(ktc * NSA_KT, NSA_KT), NSA_KT), :]
        pair = sel_scr[pl.ds(pl.multiple_of((ktc // 2) * 2 * per_tile, 2 * per_tile), 2 * per_tile), :]
        r = jnp.where(ktc % 2 == 1, pair[per_tile:], pair[:per_tile])
        delta0 = c * per128 - halves * kt
        causal = [jnp.where(kt * NSA_KT + key_row <= tok(j, (NSA_KT, W)), 0.0, NEG) for j in range(nq)]

        def elem(g):
            if kind == "far":
                return ()
            bias = bias_tile(g, delta0, True)
            return (bias, causal[g % nq]) if kind == "diag" else (bias,)

        def rows(g):
            gi, j = divmod(g, nq)
            out = []
            for i in range(per_tile):
                row = r[i:i + 1, j * W:(j + 1) * W]
                if kind == "far":
                    row = row + bias_ref[dmax, 0, pl.ds(0, 1), pl.ds(gi * LANES, LANES)]
                out.append(row)
            return out

        vt = _with_ones(slct_scr[ktc], 0)
        return k, lambda g: vt, elem, rows

    def slc_iter(it, kind):
        _flash_tiles([slc_tile(NSA_UNROLL * it + u, kind) for u in range(NSA_UNROLL)], q_groups, m_slc, acc_slc)

    def slc_body(kind):
        def run(it, _):
            slc_iter(it, kind)
            return 0
        return run

    keys_per_iter = NSA_UNROLL * NSA_KT
    last_iter = (c * CQ) // keys_per_iter
    n_far_tiles = jnp.maximum(c * per128 - (halves - 1) - _far_delta(LANES) + halves, 0) // halves
    n_far_iters = n_far_tiles // NSA_UNROLL
    lax.fori_loop(0, n_far_iters, slc_body("far"), 0)
    lax.fori_loop(n_far_iters, last_iter, slc_body("near"), 0)
    slc_iter(last_iter, "diag")

    _flash_init(m_win, acc_win)
    back = WIN // NSA_KT

    def win_tile(w):
        kt = c - back + w
        ktc = jnp.clip(kt, 0, n_kt - 1)
        k = win_ref[0, pl.ds(pl.multiple_of(ktc * NSA_KT, NSA_KT), NSA_KT), :]
        dist0 = (back - w) * NSA_KT + _iota((NSA_KT, W), 1) - key_row
        negs = []
        for j in range(nq):
            dist = dist0 + j * W
            negs.append(jnp.where((dist >= 0) & (dist < WIN) & (kt >= 0), 0.0, NEG))

        def elem(g):
            return bias_tile(g, (back - w) * halves, False), negs[g % nq]

        vt = _with_ones(wint_scr[ktc], 0)
        return k, lambda g: vt, elem, lambda g: (zero_row,)

    _flash_tiles([win_tile(w) for w in range((WIN + CQ) // NSA_KT)], q_groups, m_win, acc_win)

    g_scr[...] = jax.nn.sigmoid(gate_ref[0].astype(F32)).T
    res = []
    for gi in range(C_GROUP):
        col0 = (n * C_GROUP + gi) * 3
        gates = [g_scr[pl.ds(col0 + br, 1), :] for br in range(3)]
        parts = []
        for j in range(nq):
            g = gi * nq + j
            toks = slice(j * W, (j + 1) * W)
            a_s = acc_slc[g][...]
            a_w = acc_win[g][...]
            parts.append(gates[0][:, toks] * ocmp_scr[g]
                         + gates[1][:, toks] * (a_s[HEAD_DIM:] / a_s[0:1])
                         + gates[2][:, toks] * (a_w[HEAD_DIM:] / a_w[0:1]))
        res.append(jnp.concatenate(parts, axis=1))
    o_ref[0] = jnp.concatenate(res, axis=0).T.astype(o_ref.dtype)


def _nsa(proj, tail, kcvc, kcvct, bias_c, overlap_t, dmax):
    B, S, _ = proj.shape
    CQ = C_CHUNK
    assert CQ % NSA_KT == 0 and WIN % NSA_KT == 0 and CQ % GROUP_W == 0
    n_chunks = S // CQ
    NC = kcvc.shape[2]
    NS = S // SLC_BLOCK
    n_tiles = bias_c.shape[0]
    G = C_GROUP * (CQ // GROUP_W)
    kern = functools.partial(_nsa_kernel, dmax=dmax, n_sel=min(SLC_TOPN, NS))
    return pl.pallas_call(
        kern,
        grid=(B, C_KV_HEADS, n_chunks),
        in_specs=[pl.BlockSpec((1, CQ, 2 * LANES), lambda b, n, c: (b, c, BLK_CQ // 2 + n)),
                  pl.BlockSpec((1, 1, NC, LANES), lambda b, n, c: (b, n, 0, 0)),
                  pl.BlockSpec((1, 1, LANES, NC), lambda b, n, c: (b, n, 0, 0)),
                  pl.BlockSpec((1, S, LANES), lambda b, n, c: (b, 0, BLK_CKV + 3 * n + 1)),
                  pl.BlockSpec((1, S, LANES), lambda b, n, c: (b, 0, BLK_CKV + 3 * n + 2)),
                  pl.BlockSpec((1, CQ, LANES), lambda b, n, c: (b, c, BLK_GATE)),
                  pl.BlockSpec((n_tiles, 1, LANES, C_GROUP * LANES), lambda b, n, c: (0, n, 0, 0)),
                  pl.BlockSpec((NS, NC), lambda b, n, c: (0, 0))],
        out_specs=pl.BlockSpec((1, CQ, 2 * LANES), lambda b, n, c: (b, c, n)),
        out_shape=jax.ShapeDtypeStruct((B, S, C_HEADS * HEAD_DIM), BF16),
        scratch_shapes=[pltpu.VMEM((S // NSA_KT, LANES, NSA_KT), BF16), pltpu.VMEM((S // NSA_KT, LANES, NSA_KT), BF16),
                        pltpu.VMEM((G, HEAD_DIM, GROUP_W), F32), pltpu.VMEM((NS, CQ), F32),
                        pltpu.VMEM((NS, CQ), F32), pltpu.VMEM((LANES, CQ), F32),
                        pltpu.VMEM((1, G * GROUP_W), F32), pltpu.VMEM((1, G * GROUP_W), F32)]
        + [pltpu.VMEM((ACC_ROWS, GROUP_W), F32)] * (2 * G),
        compiler_params=_cparams("arbitrary", "arbitrary", "arbitrary"),
        name="nsa",
    )(proj, kcvc, kcvct, tail, tail, tail, bias_c, overlap_t)


def _head_scale():
    scale = np.ones((HEAD_WIDTH,), np.float32)
    for g in range(len(A_CONFIGS)):
        scale[g * A_GROUP_WIDTH:g * A_GROUP_WIDTH + A_HEADS_PER_GROUP * HEAD_DIM] = Q_SCALE
    scale[A_WIDTH:A_WIDTH + B_HEADS * HEAD_DIM] = Q_SCALE * LOG2E
    scale[A_WIDTH + B_WIDTH:] = Q_SCALE * LOG2E
    return scale


def _split_w_in(w_in):
    depth, D, _ = w_in.shape
    scale = np.concatenate([_head_scale(), np.ones((IN_WIDTH - HEAD_WIDTH,), np.float32)])
    w = (w_in * scale).astype(BF16)
    kv = w[:, :, HEAD_WIDTH:HEAD_WIDTH + C_KV_WIDTH].reshape(depth, D, 3, 2, C_KV_HEADS, HEAD_DIM)
    kv = kv.transpose(0, 1, 4, 2, 3, 5).reshape(depth, D, C_KV_WIDTH)
    pad = jnp.zeros((depth, D, TAIL_WIDTH - C_KV_WIDTH - C_GATE_WIDTH), BF16)
    tail = jnp.concatenate([kv, w[:, :, HEAD_WIDTH + C_KV_WIDTH:], pad], axis=2)
    return w, tail


def _compress_weights(cmp_w1, cmp_w2, cmp_pe):
    depth = cmp_w1.shape[0]
    half = CMP_LEN // 2
    w1 = cmp_w1.reshape(depth, 2, 2, half, HEAD_DIM, CMP_HIDDEN)
    w1_blk = jnp.zeros((depth, 2, half, 2, HEAD_DIM, 2, CMP_HIDDEN), F32)
    w1_blk = w1_blk.at[:, :, :, 0, :, 0, :].set(w1[:, 0])
    w1_blk = w1_blk.at[:, :, :, 1, :, 1, :].set(w1[:, 1])
    w1_blk = w1_blk.reshape(depth, 2, half * 2 * HEAD_DIM, 2 * CMP_HIDDEN).astype(BF16)
    w2_blk = jnp.zeros((depth, 2, CMP_HIDDEN, 2, HEAD_DIM), F32)
    w2_blk = w2_blk.at[:, 0, :, 0, :].set(cmp_w2[:, 0])
    w2_blk = w2_blk.at[:, 1, :, 1, :].set(cmp_w2[:, 1])
    w2_blk = w2_blk.reshape(depth, 2 * CMP_HIDDEN, 2 * HEAD_DIM).astype(BF16)
    pe = cmp_pe.reshape(depth, 2, 2, half, HEAD_DIM)
    pe_rows = pe.transpose(0, 2, 3, 1, 4).reshape(depth, 2, half * 2 * HEAD_DIM)
    return w1_blk, w2_blk, pe_rows


def kernel(x, rel_table, w_in, w_out, cmp_w1, cmp_w2, cmp_pe, norm_attn, norm_mlp,
           w_up, conv_w, conv_b, w_down, norm_final):
    B, S, D = x.shape
    depth = w_in.shape[0]
    T = B * S

    a = jnp.arange(A_BLOCK)[:, None]
    bk = jnp.arange(2 * A_BLOCK)[None, :]
    rel = a + A_BLOCK - bk
    idx_a = jnp.stack([_t5_bucket(rel * d) for _, d in A_CONFIGS]).astype(jnp.int32)
    bias_a = _bias_tiles(rel_table[:A_HEADS], idx_a)
    dmax_b = _n_far_tiles(S // MOBA_BLOCK, MOBA_BLOCK)
    bias_b = _bias_tiles(rel_table[A_HEADS:A_HEADS + B_HEADS] * LOG2E,
                         _toeplitz_bucket_tiles_t(dmax_b + 1, MOBA_BLOCK), 2)
    dmax_c = _n_far_tiles(S // LANES, LANES)
    bias_c = _bias_tiles(rel_table[A_HEADS + B_HEADS:] * LOG2E,
                         _toeplitz_bucket_tiles_t(dmax_c + 1, LANES), C_GROUP)

    n16 = S // CMP_STRIDE
    ci = jnp.arange(n16)[None, :] * CMP_STRIDE
    sj = jnp.arange(S // SLC_BLOCK)[:, None] * SLC_BLOCK
    overlap_t = ((ci < sj + SLC_BLOCK) & (ci + CMP_LEN > sj)).astype(BF16)

    w_head, w_tail = _split_w_in(w_in)
    w_out_b = w_out.astype(BF16)
    w_up_b = w_up.astype(BF16)
    w_down_b = w_down.astype(BF16)
    gains_attn = norm_attn.reshape(depth, 1, D)
    gains_mlp = norm_mlp.reshape(depth, 1, D)
    conv_b3 = conv_b.reshape(depth, 1, -1)
    w1_blk, w2_blk, pe_rows = _compress_weights(cmp_w1, cmp_w2, cmp_pe)

    x2 = x.reshape(T, D)
    h = _norm(x2, gains_attn, 0)
    for l in range(depth):
        proj = _inproj(h, w_head, l, HEAD_WIDTH, tn=512).reshape(B, S, HEAD_WIDTH)
        tail = _inproj(h, w_tail, l, TAIL_WIDTH, tn=TAIL_WIDTH).reshape(B, S, TAIL_WIDTH)
        os_, lses = [], []
        for g, (_, d) in enumerate(A_CONFIGS):
            o, lse = _dilated_group(proj, bias_a, g, d)
            os_.append(o)
            lses.append(lse)
        o_b = _moba(proj, bias_b, dmax_b).reshape(T, B_HEADS * HEAD_DIM)
        cmp_cols = tail[:, :, :BLK_GATE * LANES].reshape(B, S, C_KV_HEADS, 3, LANES)[:, :, :, 0]
        sub = cmp_cols.transpose(0, 2, 1, 3).reshape(B, C_KV_HEADS, n16, CMP_STRIDE * LANES)
        kcvc, kcvct = _compress(sub, pe_rows, w1_blk, w2_blk, l)
        o_c = _nsa(proj, tail, kcvc, kcvct, bias_c, overlap_t, dmax_c).reshape(T, C_HEADS * HEAD_DIM)
        x2, h = _outproj(x2, os_, lses, o_b, o_c, w_out_b, gains_mlp, l)
        last = l == depth - 1
        next_gain = norm_final.reshape(1, D) if last else norm_attn[l + 1].reshape(1, D)
        x2, h = _ffn(x2, h, w_up_b, conv_w, conv_b3, w_down_b, l, next_gain, F32 if last else BF16, S)
    return h.reshape(B, S, D)
```

```python
import functools
import math

import numpy as np
import jax
import jax.numpy as jnp
from jax import lax
from jax.experimental import pallas as pl
from jax.experimental.pallas import tpu as pltpu

F32 = jnp.float32
BF16 = jnp.bfloat16

HEAD_DIM = 64
LANES = 128
A_CONFIGS = ((128, 1), (512, 4), (2048, 16))
A_HEADS_PER_GROUP = 4
A_HEADS = 12
A_BLOCK = 128
A_BLOCKS_PER_STEP = 2
B_HEADS = 8
MOBA_BLOCK = 256
MOBA_TOPK = 3
MOBA_PAIRS = 2
MOBA_UNROLL = 4
C_KV_HEADS = 3
C_GROUP = 4
C_HEADS = 12
CMP_STRIDE = 16
CMP_LEN = 32
CMP_HIDDEN = 128
SLC_BLOCK = 64
SLC_TOPN = 16
N_FORCED = 3
IMP_SPLITS = 3
WIN = 512
INVALID_SCORE = -1.0
FORCED_SCORE = -2.0
N_BUCKETS = 32
T5_MAX_DIST = 2048
EPS = 1e-6
NEG = -1e30
M_FLOOR = -1e20
Q_SCALE = HEAD_DIM ** -0.5
LOG2E = math.log2(math.e)
C_CHUNK = 256
ACC_ROWS = 128
GROUP_W = 128
NSA_KT = 256
NSA_UNROLL = 4

A_WIDTH = 3 * A_HEADS * HEAD_DIM
A_GROUP_WIDTH = 3 * A_HEADS_PER_GROUP * HEAD_DIM
B_WIDTH = 3 * B_HEADS * HEAD_DIM
C_Q_WIDTH = C_HEADS * HEAD_DIM
C_KV_WIDTH = 6 * C_KV_HEADS * HEAD_DIM
C_GATE_WIDTH = 3 * C_HEADS
IN_WIDTH = A_WIDTH + B_WIDTH + C_Q_WIDTH + C_KV_WIDTH + C_GATE_WIDTH
BLK_B = A_WIDTH // LANES
BLK_CQ = BLK_B + B_WIDTH // LANES
HEAD_WIDTH = A_WIDTH + B_WIDTH + C_Q_WIDTH
BLK_CKV = 0
BLK_GATE = 3 * C_KV_HEADS
TAIL_WIDTH = (BLK_GATE + 1) * LANES

VMEM_LIMIT = 56 * 1024 * 1024


def _cparams(*sem):
    return pltpu.CompilerParams(dimension_semantics=sem, vmem_limit_bytes=VMEM_LIMIT)


def _dot_nt(a, b):
    return lax.dot_general(a, b, (((1,), (1,)), ((), ())), preferred_element_type=F32)


def _dot(a, b, **kw):
    return jnp.dot(a, b, preferred_element_type=F32, **kw)


def _transpose_bf16(x):
    return x.astype(F32).T.astype(BF16)


def _iota(shape, dim):
    return lax.broadcasted_iota(jnp.int32, shape, dim)


def _t5_bucket(dist):
    n = jnp.maximum(dist, 0)
    max_exact = N_BUCKETS // 2
    nf = jnp.maximum(n, 1).astype(F32)
    large = max_exact + (jnp.log(nf / max_exact) / math.log(T5_MAX_DIST / max_exact)
                         * (N_BUCKETS - max_exact)).astype(jnp.int32)
    large = jnp.minimum(large, N_BUCKETS - 1)
    return jnp.where(n < max_exact, n, large)


def _bias_lookup_kernel(tbl_ref, idx_ref, o_ref, *, hpr):
    idx = idx_ref[0]
    C = idx.shape[1]
    for h in range(tbl_ref.shape[0]):
        acc = jnp.zeros(idx.shape, F32)
        for b in range(N_BUCKETS):
            acc = jnp.where(idx == b, tbl_ref[h, b], acc)
        o_ref[0, h // hpr, :, (h % hpr) * C:(h % hpr + 1) * C] = acc


def _bias_tiles(tbl, idx, heads_per_row=1):
    n, R, C = idx.shape
    H = tbl.shape[0]
    hpr = heads_per_row
    return pl.pallas_call(
        functools.partial(_bias_lookup_kernel, hpr=hpr),
        grid=(n,),
        in_specs=[pl.BlockSpec(memory_space=pltpu.SMEM),
                  pl.BlockSpec((1, R, C), lambda t: (t, 0, 0))],
        out_specs=pl.BlockSpec((1, H // hpr, R, hpr * C), lambda t: (t, 0, 0, 0)),
        out_shape=jax.ShapeDtypeStruct((n, H // hpr, R, hpr * C), F32),
        compiler_params=_cparams("arbitrary"),
        name="bias_tiles",
    )(tbl, idx)


def _toeplitz_bucket_tiles_t(n_tiles, size):
    d = jnp.arange(n_tiles)[:, None, None] * size
    key = jnp.arange(size)[None, :, None]
    qry = jnp.arange(size)[None, None, :]
    return _t5_bucket(d + qry - key).astype(jnp.int32)


def _far_delta(size):
    return -(-(T5_MAX_DIST + size - 1) // size)


def _n_far_tiles(n_chunks, size):
    return min(_far_delta(size), n_chunks - 1)


def _norm_rows(x, g):
    ms = jnp.mean(x * x, axis=-1, keepdims=True)
    return x * lax.rsqrt(ms + EPS) * g


def _norm_kernel(x_ref, g_ref, o_ref):
    o_ref[...] = _norm_rows(x_ref[...], g_ref[...]).astype(o_ref.dtype)


def _norm(x2, gains, l, tm=1024):
    T, D = x2.shape
    return pl.pallas_call(
        _norm_kernel,
        grid=(T // tm,),
        in_specs=[pl.BlockSpec((tm, D), lambda i: (i, 0)),
                  pl.BlockSpec((None, 1, D), lambda i: (l, 0, 0))],
        out_specs=pl.BlockSpec((tm, D), lambda i: (i, 0)),
        out_shape=jax.ShapeDtypeStruct((T, D), BF16),
        compiler_params=_cparams("arbitrary"),
        name="first_norm",
    )(x2, gains)


def _inproj_kernel(h_ref, w_ref, o_ref):
    o_ref[...] = _dot(h_ref[...], w_ref[...]).astype(o_ref.dtype)


def _inproj(h, w_all, l, N, tn, tm=1024):
    T, D = h.shape
    return pl.pallas_call(
        _inproj_kernel,
        grid=(T // tm, N // tn),
        in_specs=[pl.BlockSpec((tm, D), lambda i, j: (i, 0)),
                  pl.BlockSpec((None, D, tn), lambda i, j: (l, 0, j))],
        out_specs=pl.BlockSpec((tm, tn), lambda i, j: (i, j)),
        out_shape=jax.ShapeDtypeStruct((T, N), BF16),
        compiler_params=_cparams("arbitrary", "arbitrary"),
        name="inproj",
    )(h, w_all)


def _outproj_kernel(x_ref, o0_ref, o1_ref, o2_ref, l0_ref, l1_ref, l2_ref, b_ref, c_ref, w_ref, g_ref, o_ref, h_ref):
    ls = [l0_ref[...], l1_ref[...], l2_ref[...]]
    m = jnp.maximum(jnp.maximum(ls[0], ls[1]), ls[2])
    es = [jnp.exp(l - m) for l in ls]
    inv = 1.0 / (es[0] + es[1] + es[2])
    gw = o0_ref.shape[1]
    acc = jnp.zeros(o_ref.shape, F32)
    for g, og_ref in enumerate((o0_ref, o1_ref, o2_ref)):
        acc += _dot((og_ref[...] * (es[g] * inv)).astype(BF16), w_ref[pl.ds(g * gw, gw), :])
    ra, rb = 3 * gw, b_ref.shape[1]
    acc += _dot(b_ref[...], w_ref[pl.ds(ra, rb), :])
    acc += _dot(c_ref[...], w_ref[pl.ds(ra + rb, c_ref.shape[1]), :])
    x_new = x_ref[...] + acc
    o_ref[...] = x_new
    h_ref[...] = _norm_rows(x_new, g_ref[...]).astype(h_ref.dtype)


def _outproj(x2, os_, lses, ob, oc, w_all, gains, l, tm=512):
    T, D = x2.shape
    row = lambda i: (i, 0)
    gspec = pl.BlockSpec((tm, os_[0].shape[1]), row)
    return pl.pallas_call(
        _outproj_kernel,
        grid=(T // tm,),
        in_specs=[pl.BlockSpec((tm, D), row)] + [gspec] * 6 + [
                  pl.BlockSpec((tm, ob.shape[1]), row),
                  pl.BlockSpec((tm, oc.shape[1]), row),
                  pl.BlockSpec((None, w_all.shape[1], D), lambda i: (l, 0, 0)),
                  pl.BlockSpec((None, 1, D), lambda i: (l, 0, 0))],
        out_specs=[pl.BlockSpec((tm, D), row), pl.BlockSpec((tm, D), row)],
        out_shape=[jax.ShapeDtypeStruct((T, D), F32), jax.ShapeDtypeStruct((T, D), BF16)],
        compiler_params=_cparams("arbitrary"),
        name="outproj",
    )(x2, *os_, *lses, ob, oc, w_all, gains)


FFN_PARTS = 2
FFN_HALO = 16


def _ffn_kernel(x_ref, h_ref, halo_ref, wa_ref, wg_ref, cwa_ref, cwg_ref, cba_ref, cbg_ref, wd_ref, g_ref,
                o_ref, n_ref, h_scr, *u_scrs, tm, blocks_per_seq):
    i = pl.program_id(0)
    f = pl.program_id(1)

    @pl.when(f == 0)
    def _():
        o_ref[...] = x_ref[...]
        keep = jnp.where(i % blocks_per_seq == 0, 0.0, 1.0).astype(BF16)
        h_scr[pl.ds(0, FFN_HALO), :] = halo_ref[0] * keep
        h_scr[pl.ds(FFN_HALO, tm), :] = h_ref[...]

    h = h_scr[...]
    n_parts = len(u_scrs) // 2
    part = wa_ref.shape[1] // n_parts

    def up(pt):
        cols = pl.ds(pt * part, part)
        u_scrs[2 * pt][...] = _dot(h, wa_ref[:, cols])
        u_scrs[2 * pt + 1][...] = _dot(h, wg_ref[:, cols])

    def conv(u_scr, pt, cw_ref, cb_ref):
        cols = pl.ds(pt * part, part)
        acc = cb_ref[:, cols]
        for j in range(3):
            acc = acc + cw_ref[pl.ds(j, 1), cols] * u_scr[pl.ds(FFN_HALO - j, tm), :]
        return acc

    up(0)
    for pt in range(n_parts):
        if pt + 1 < n_parts:
            up(pt + 1)
        ya = conv(u_scrs[2 * pt], pt, cwa_ref, cba_ref)
        yg = conv(u_scrs[2 * pt + 1], pt, cwg_ref, cbg_ref)
        act = (yg * jax.nn.sigmoid(yg) * ya).astype(BF16)
        o_ref[...] += _dot(act, wd_ref[pl.ds(pt * part, part), :])

    @pl.when(f == pl.num_programs(1) - 1)
    def _():
        n_ref[...] = _norm_rows(o_ref[...], g_ref[...]).astype(n_ref.dtype)


def _ffn(x2, h, w_up, conv_w, conv_b, w_down, l, next_gain, next_dtype, seq, tm=512, tf=512):
    T, D = x2.shape
    Fh = w_down.shape[1]
    nf = Fh // tf
    halo_view = h.reshape(T // FFN_HALO, FFN_HALO, D)
    rb = tm // FFN_HALO
    kern = functools.partial(_ffn_kernel, tm=tm, blocks_per_seq=seq // tm)
    row = lambda i, f: (i, 0)
    return pl.pallas_call(
        kern,
        grid=(T // tm, nf),
        in_specs=[pl.BlockSpec((tm, D), row),
                  pl.BlockSpec((tm, D), row),
                  pl.BlockSpec((1, FFN_HALO, D), lambda i, f: (jnp.maximum(i * rb - 1, 0), 0, 0)),
                  pl.BlockSpec((None, D, tf), lambda i, f: (l, 0, f)),
                  pl.BlockSpec((None, D, tf), lambda i, f: (l, 0, f + nf)),
                  pl.BlockSpec((None, 3, tf), lambda i, f: (l, 0, f)),
                  pl.BlockSpec((None, 3, tf), lambda i, f: (l, 0, f + nf)),
                  pl.BlockSpec((None, 1, tf), lambda i, f: (l, 0, f)),
                  pl.BlockSpec((None, 1, tf), lambda i, f: (l, 0, f + nf)),
                  pl.BlockSpec((None, tf, D), lambda i, f: (l, f, 0)),
                  pl.BlockSpec((1, D), lambda i, f: (0, 0))],
        out_specs=[pl.BlockSpec((tm, D), row), pl.BlockSpec((tm, D), row)],
        out_shape=[jax.ShapeDtypeStruct((T, D), F32), jax.ShapeDtypeStruct((T, D), next_dtype)],
        scratch_shapes=[pltpu.VMEM((tm + FFN_HALO, D), BF16)]
        + [pltpu.VMEM((tm + FFN_HALO, tf // FFN_PARTS), F32)] * (2 * FFN_PARTS),
        compiler_params=_cparams("arbitrary", "arbitrary"),
        name="convffn",
    )(x2, h, halo_view, w_up, w_up, conv_w, conv_w, conv_b, conv_b, w_down, next_gain)


def _stack_pair(qp):
    lane = _iota(qp.shape, 1)
    zero = jnp.zeros_like(qp)
    return jnp.concatenate([jnp.where(lane < HEAD_DIM, qp, zero),
                            jnp.where(lane >= HEAD_DIM, qp, zero)], axis=0)


def _unstack_pair(o):
    R = o.shape[0] // 2
    lane = _iota((R, LANES), 1)
    return jnp.where(lane < HEAD_DIM, o[:R], o[R:])


def _dilated_kernel(q_ref, kc_ref, kp_ref, vc_ref, vp_ref, bias_ref, o_ref, lse_ref):
    i = pl.program_id(2)
    R = 2 * A_BLOCK
    n_blocks = q_ref.shape[1] // A_BLOCK
    a = _iota((R, 2 * A_BLOCK), 0) % A_BLOCK
    bk = _iota((R, 2 * A_BLOCK), 1)
    rel = a + A_BLOCK - bk
    band = (rel >= 0) & (rel <= A_BLOCK)
    first = band & ((bk >= A_BLOCK) | (i > 0))
    n_pairs = q_ref.shape[2] // LANES
    units = [(j, p) for j in range(n_blocks) for p in range(n_pairs)]

    def rows(j):
        return slice(j * A_BLOCK, (j + 1) * A_BLOCK)

    def with_prev(cur_ref, prev_ref, j, cols):
        prev = prev_ref[0, :, cols] if j == 0 else cur_ref[0, rows(j - 1), cols]
        return jnp.concatenate([prev, cur_ref[0, rows(j), cols]], axis=0)

    def scores(j, p):
        cols = slice(p * LANES, (p + 1) * LANES)
        qs = _stack_pair(q_ref[0, rows(j), cols])
        s = _dot_nt(qs, with_prev(kc_ref, kp_ref, j, cols)) + bias_ref[0, 2 * p:2 * p + 2].reshape(R, 2 * A_BLOCK)
        return jnp.where(first if j == 0 else band, s, NEG)

    s_all = [scores(j, p) for j, p in units]
    for (j, p), s in zip(units, s_all):
        cols = slice(p * LANES, (p + 1) * LANES)
        mask = first if j == 0 else band
        m = jnp.max(s, axis=1, keepdims=True)
        e = jnp.where(mask, jnp.exp(s - m), 0.0)
        den = jnp.sum(e, axis=1, keepdims=True)
        o = _dot(e.astype(BF16), with_prev(vc_ref, vp_ref, j, cols)) / den
        lse = m + jnp.log(den)
        o_ref[0, rows(j), cols] = _unstack_pair(o)
        lse_ref[0, rows(j), cols] = _unstack_pair(jnp.broadcast_to(lse, (R, LANES)))


def _dilated_group(proj, bias_a, g, dilation):
    B, S, _ = proj.shape
    L = S // dilation
    nq = L // A_BLOCK
    per_step = A_BLOCKS_PER_STEP if nq % A_BLOCKS_PER_STEP == 0 else 1
    W = A_HEADS_PER_GROUP * HEAD_DIM
    if dilation == 1:
        view, base = proj, g * A_GROUP_WIDTH // W
    else:
        cols = proj[:, :, g * A_GROUP_WIDTH:(g + 1) * A_GROUP_WIDTH]
        view, base = cols.reshape(B, L, dilation * A_GROUP_WIDTH), 0

    def cur(part):
        return pl.BlockSpec((1, per_step * A_BLOCK, W), lambda b, r, i: (b, i, base + r * 3 + part))

    def prev(part):
        return pl.BlockSpec((1, A_BLOCK, W),
                            lambda b, r, i: (b, jnp.maximum(i * per_step - 1, 0), base + r * 3 + part))

    out_spec = pl.BlockSpec((1, per_step * A_BLOCK, W), lambda b, r, i: (b, i, r))
    o, lse = pl.pallas_call(
        _dilated_kernel,
        grid=(B, dilation, nq // per_step),
        in_specs=[cur(0), cur(1), prev(1), cur(2), prev(2),
                  pl.BlockSpec((1, A_HEADS_PER_GROUP, A_BLOCK, 2 * A_BLOCK), lambda b, r, i: (g, g, 0, 0))],
        out_specs=[out_spec, out_spec],
        out_shape=[jax.ShapeDtypeStruct((B, L, dilation * W), F32)] * 2,
        compiler_params=_cparams("arbitrary", "arbitrary", "arbitrary"),
        name=f"dilated_g{g}",
    )(view, view, view, view, view, bias_a)
    return o.reshape(B * S, W), lse.reshape(B * S, W)


def _flash_init(m_scr, acc_scrs):
    m_scr[...] = jnp.full(m_scr.shape, M_FLOOR, F32)
    for acc in acc_scrs:
        acc[...] = jnp.zeros(acc.shape, F32)


def _with_ones(vt, row):
    return jnp.where(_iota(vt.shape, 0) == row, jnp.ones_like(vt), vt)


def _flash_tiles(tiles, q_groups, m_scr, acc_scrs):
    G = len(q_groups)
    W = q_groups[0].shape[0]
    m = [m_scr[:, g * W:(g + 1) * W] for g in range(G)]
    units = [(t, g) for t in range(len(tiles)) for g in range(G)]

    def scores(t, g):
        k_of, _, elem_of, _ = tiles[t]
        s = _dot_nt(k_of(g), q_groups[g])
        for term in elem_of(g):
            s = s + term
        return s

    def apply(g, alpha, pv):
        acc_scrs[g][...] = alpha * acc_scrs[g][...] + pv

    s_next = scores(*units[0])
    pending = None
    for n, (t, g) in enumerate(units):
        s = s_next
        if n + 1 < len(units):
            s_next = scores(*units[n + 1])
        rows = tiles[t][3](g)
        seg = s.shape[0] // len(rows)
        parts = [s[i * seg:(i + 1) * seg] for i in range(len(rows))]
        m_new = m[g]
        for part, row in zip(parts, rows):
            m_new = jnp.maximum(m_new, jnp.max(part, axis=0, keepdims=True) + row)
        alpha = jnp.exp2(m[g] - m_new)
        p = jnp.concatenate([jnp.exp2(part - (m_new - row)) for part, row in zip(parts, rows)], axis=0)
        m[g] = m_new
        pv = _dot(tiles[t][1](g), p.astype(BF16))
        if pending is not None:
            apply(*pending)
        pending = (g, alpha, pv)
    apply(*pending)
    m_scr[...] = jnp.concatenate(m, axis=1)


PICKED = -3e38
NO_INDEX = 1e9


def _top_picks(score, n_pick):
    idx = _iota(score.shape, 0).astype(F32)

    def pick(_, g):
        mx = jnp.max(g, axis=0, keepdims=True)
        first = jnp.min(jnp.where(g == mx, idx, NO_INDEX), axis=0, keepdims=True)
        return jnp.where(idx == first, PICKED, g)

    return lax.fori_loop(0, n_pick, pick, score) == PICKED


def _moba_kernel(q_ref, k_ref, v_ref, bias_ref, o_ref, km_scr, vt_scr, sel_scr, m_scr, *acc_scrs,
                 dmax, n_blocks):
    c = pl.program_id(2)
    BLK = MOBA_BLOCK
    W = GROUP_W
    per_head = BLK // W
    per_pair = 2 * per_head
    G = MOBA_PAIRS * per_pair

    @pl.when(c == 0)
    def _():
        def body(j, _):
            rows = pl.ds(pl.multiple_of(j * BLK, BLK), BLK)
            km_scr[pl.ds(j, 1), :] = jnp.mean(k_ref[0, rows, :].astype(F32), axis=0, keepdims=True)
            vt_scr[j] = _transpose_bf16(v_ref[0, rows, :])
            return 0
        lax.fori_loop(0, n_blocks, body, 0)

    def lanes(pi):
        return slice(pi * LANES, (pi + 1) * LANES)

    q = q_ref[0]
    km = km_scr[...].astype(BF16)
    q_groups, gates = [], []
    for pi in range(MOBA_PAIRS):
        qs = _stack_pair(q[:, lanes(pi)])
        gates.append(_dot_nt(km[:, lanes(pi)], qs))
        q_groups.extend(qs[g * W:(g + 1) * W] for g in range(per_pair))
    gate = jnp.concatenate(gates, axis=1)
    blk = _iota(gate.shape, 0)
    gate = jnp.where(blk < c, gate, NEG)
    sel = _top_picks(gate, min(MOBA_TOPK, n_blocks))
    sel_scr[...] = jnp.where((blk < c) & sel, 0.0, NEG)

    _flash_init(m_scr, acc_scrs)

    def group(g):
        pi, gp = divmod(g, per_pair)
        return pi, gp // per_head, gp * W

    def keys_of(kj):
        return lambda g: kj[:, lanes(group(g)[0])]

    def values_of(vt):
        opts = [[_with_ones(vt[lanes(pi)], (1 - h) * HEAD_DIM) for h in range(2)] for pi in range(MOBA_PAIRS)]
        return lambda g: opts[group(g)[0]][group(g)[1]]

    def bias_of(tile, g, rows=slice(None)):
        pi, _, col = group(g)
        return bias_ref[tile, pi, rows, pl.ds(col, W)]

    def past_tile(j, far):
        jc = jnp.minimum(j, n_blocks - 1)
        kj = k_ref[0, pl.ds(pl.multiple_of(jc * BLK, BLK), BLK), :]
        tile = jnp.clip(c - j, 0, dmax)
        selrow = sel_scr[pl.ds(jc, 1), :]

        def elem(g):
            return () if far else (bias_of(tile, g),)

        def rows(g):
            row = selrow[:, g * W:(g + 1) * W]
            return (row + bias_of(dmax, g, pl.ds(0, 1)),) if far else (row,)

        return keys_of(kj), values_of(vt_scr[jc]), elem, rows

    def body(far):
        def run(it, _):
            _flash_tiles([past_tile(MOBA_UNROLL * it + u, far) for u in range(MOBA_UNROLL)],
                         q_groups, m_scr, acc_scrs)
            return 0
        return run

    n_far_iters = jnp.maximum(c - _far_delta(BLK) + 1, 0) // MOBA_UNROLL
    lax.fori_loop(0, n_far_iters, body(True), 0)
    lax.fori_loop(n_far_iters, (c + MOBA_UNROLL - 1) // MOBA_UNROLL, body(False), 0)

    ko = k_ref[0, pl.ds(pl.multiple_of(c * BLK, BLK), BLK), :]

    def own_elem(g):
        qpos = (g * W + _iota((BLK, W), 1)) % BLK
        return bias_of(0, g), jnp.where(_iota((BLK, W), 0) <= qpos, 0.0, NEG)

    zero_row = jnp.zeros((1, W), F32)
    _flash_tiles([(keys_of(ko), values_of(vt_scr[c]), own_elem, lambda g: (zero_row,))], q_groups, m_scr, acc_scrs)
    pairs_out = []
    for pi in range(MOBA_PAIRS):
        heads_out = []
        for h in range(2):
            parts = []
            for jh in range(per_head):
                a = acc_scrs[pi * per_pair + h * per_head + jh][...]
                den = a[(1 - h) * HEAD_DIM:(1 - h) * HEAD_DIM + 1]
                parts.append(a[h * HEAD_DIM:(h + 1) * HEAD_DIM] / den)
            heads_out.append(jnp.concatenate(parts, axis=1))
        pairs_out.append(jnp.concatenate(heads_out, axis=0).T)
    o_ref[0] = jnp.concatenate(pairs_out, axis=1).astype(o_ref.dtype)


def _moba(proj, bias_b, dmax):
    B, S, _ = proj.shape
    NB = S // MOBA_BLOCK
    n_tiles = bias_b.shape[0]
    PW = MOBA_PAIRS * LANES
    qb, kb, vb = (BLK_B * LANES // PW, (BLK_B + 4) * LANES // PW, (BLK_B + 8) * LANES // PW)
    kern = functools.partial(_moba_kernel, dmax=dmax, n_blocks=NB)
    Q = MOBA_PAIRS * 2 * MOBA_BLOCK
    return pl.pallas_call(
        kern,
        grid=(B, B_HEADS // (2 * MOBA_PAIRS), NB),
        in_specs=[pl.BlockSpec((1, MOBA_BLOCK, PW), lambda b, p, c: (b, c, qb + p)),
                  pl.BlockSpec((1, S, PW), lambda b, p, c: (b, 0, kb + p)),
                  pl.BlockSpec((1, S, PW), lambda b, p, c: (b, 0, vb + p)),
                  pl.BlockSpec((n_tiles, MOBA_PAIRS, MOBA_BLOCK, 2 * MOBA_BLOCK), lambda b, p, c: (0, p, 0, 0))],
        out_specs=pl.BlockSpec((1, MOBA_BLOCK, PW), lambda b, p, c: (b, c, p)),
        out_shape=jax.ShapeDtypeStruct((B, S, B_HEADS * HEAD_DIM), BF16),
        scratch_shapes=[pltpu.VMEM((NB, PW), F32), pltpu.VMEM((NB, PW, MOBA_BLOCK), BF16),
                        pltpu.VMEM((NB, Q), F32), pltpu.VMEM((1, Q), F32)]
        + [pltpu.VMEM((ACC_ROWS, GROUP_W), F32)] * (Q // GROUP_W),
        compiler_params=_cparams("arbitrary", "arbitrary", "arbitrary"),
        name="moba",
    )(proj, proj, proj, bias_b)


def _gelu_tanh(x):
    return 0.5 * x * (1.0 + jnp.tanh(math.sqrt(2.0 / math.pi) * (x + 0.044715 * (x * x * x))))


def _compress_kernel(sub_ref, pe_ref, w1_ref, w2_ref, o_ref, ot_ref, v_scr):
    n16 = sub_ref.shape[2]
    sub = sub_ref[0, 0].astype(F32)
    top = (sub + pe_ref[0:1, :]).astype(BF16)
    bot = (sub + pe_ref[1:2, :]).astype(BF16)
    u = _dot(top, w1_ref[0])
    v_scr[pl.ds(0, n16), :] = _dot(bot, w1_ref[1])
    v_scr[pl.ds(n16, 8), :] = jnp.zeros((8, v_scr.shape[1]), F32)
    hidden = u + v_scr[pl.ds(1, n16), :]
    out = _dot(_gelu_tanh(hidden).astype(BF16), w2_ref[...])
    o_ref[0, 0] = out.astype(o_ref.dtype)
    ot_ref[0, 0] = out.T.astype(ot_ref.dtype)


def _compress(sub, pe_rows, w1_blk, w2_blk, l):
    B, KV, n16, W = sub.shape
    return pl.pallas_call(
        _compress_kernel,
        grid=(B, KV),
        in_specs=[pl.BlockSpec((1, 1, n16, W), lambda b, n: (b, n, 0, 0)),
                  pl.BlockSpec((None, 2, W), lambda b, n: (l, 0, 0)),
                  pl.BlockSpec((None, 2, W, 2 * CMP_HIDDEN), lambda b, n: (l, 0, 0, 0)),
                  pl.BlockSpec((None, 2 * CMP_HIDDEN, LANES), lambda b, n: (l, 0, 0))],
        out_specs=[pl.BlockSpec((1, 1, n16, LANES), lambda b, n: (b, n, 0, 0)),
                   pl.BlockSpec((1, 1, LANES, n16), lambda b, n: (b, n, 0, 0))],
        out_shape=[jax.ShapeDtypeStruct((B, KV, n16, LANES), BF16),
                   jax.ShapeDtypeStruct((B, KV, LANES, n16), BF16)],
        scratch_shapes=[pltpu.VMEM((n16 + 8, 2 * CMP_HIDDEN), F32)],
        compiler_params=_cparams("arbitrary", "arbitrary"),
        name="nsa_compress",
    )(sub, pe_rows, w1_blk, w2_blk)


def _nsa_kernel(q_ref, kc_ref, kct_ref, slc_ref, win_ref, gate_ref, bias_ref, ovt_ref,
                o_ref, slct_scr, wint_scr, ocmp_scr, imp_scr, sel_scr, g_scr, m_slc, m_win, *acc_scrs, dmax, n_sel):
    n = pl.program_id(1)
    c = pl.program_id(2)
    CQ = C_CHUNK
    W = GROUP_W
    nq = CQ // W
    G = C_GROUP * nq
    halves = NSA_KT // LANES
    per128 = CQ // LANES

    @pl.when(c == 0)
    def _():
        def body(t, _):
            rows = pl.ds(pl.multiple_of(t * NSA_KT, NSA_KT), NSA_KT)
            slct_scr[t] = _transpose_bf16(slc_ref[0, rows, :])
            wint_scr[t] = _transpose_bf16(win_ref[0, rows, :])
            return 0
        lax.fori_loop(0, slct_scr.shape[0], body, 0)

    lane = _iota((CQ, LANES), 1)
    q = q_ref[0].astype(F32)
    q_groups = []
    for pr in range(2):
        qp = q[:, pr * LANES:(pr + 1) * LANES]
        for head in (jnp.where(lane < HEAD_DIM, qp, 0.0), jnp.where(lane < HEAD_DIM, pltpu.roll(qp, HEAD_DIM, 1), 0.0)):
            hb = head.astype(BF16)
            q_groups.extend(hb[j * W:(j + 1) * W] for j in range(nq))

    def tok(j, shape):
        return c * CQ + j * W + _iota(shape, 1)

    NC = kc_ref.shape[2]
    NS = ovt_ref.shape[0]

    def cmp_branch(nk):
        kc = kc_ref[0, 0, pl.ds(0, nk), :]
        kct = kct_ref[0, 0, :, pl.ds(0, nk)]
        ovt = ovt_ref[:, pl.ds(0, nk)]
        key_end = _iota((nk, W), 0) * CMP_STRIDE + (CMP_LEN - 1)
        visible = [key_end <= tok(j, (nk, W)) for j in range(nq)]
        psum = [jnp.zeros((nk, W), F32) for _ in range(nq)]
        s_next = _dot_nt(kc, q_groups[0])
        for g in range(G):
            j = g % nq
            s = jnp.where(visible[j], s_next, NEG)
            if g + 1 < G:
                s_next = _dot_nt(kc, q_groups[g + 1])
            m = jnp.max(s, axis=0, keepdims=True)
            e = jnp.where(visible[j], jnp.exp2(s - m), 0.0)
            den = jnp.sum(e, axis=0, keepdims=True)
            p = e / jnp.maximum(den, 1e-30)
            ocmp_scr[g] = _dot(kct, p.astype(BF16))[HEAD_DIM:]
            psum[j] = psum[j] + p
        for j in range(nq):
            imp = jnp.zeros((NS, W), F32)
            rest = psum[j]
            for _ in range(IMP_SPLITS):
                piece = rest.astype(BF16)
                imp = imp + _dot(ovt, piece)
                rest = rest - piece.astype(F32)
            imp_scr[:, j * W:(j + 1) * W] = imp

    half_chunks = (NC // 2) // (CQ // CMP_STRIDE)

    @pl.when(c < half_chunks)
    def _():
        cmp_branch(NC // 2)

    @pl.when(c >= half_chunks)
    def _():
        cmp_branch(NC)

    imp = imp_scr[...]
    jj = _iota((NS, CQ), 0)
    own = (c * CQ + _iota((NS, CQ), 1)) // SLC_BLOCK
    forced = (jj == 0) | (jj == own) | (jj == own - 1)
    score = jnp.where(forced, FORCED_SCORE, jnp.where(jj <= own, imp, INVALID_SCORE))
    picked = _top_picks(score, max(n_sel - N_FORCED, 0))
    sel_scr[...] = jnp.where(forced | picked, 0.0, NEG)

    acc_slc = acc_scrs[:G]
    acc_win = acc_scrs[G:]
    key_row = _iota((NSA_KT, W), 0)
    zero_row = jnp.zeros((1, W), F32)

    def bias_tile(g, delta0, dynamic):
        gi, j = divmod(g, nq)
        cols = pl.ds(gi * LANES, LANES)
        idx = [delta0 + j - h for h in range(halves)]
        idx = [jnp.clip(d, 0, dmax) for d in idx] if dynamic else [max(d, 0) for d in idx]
        return jnp.concatenate([bias_ref[d, 0, :, cols] for d in idx], axis=0)

    _flash_init(m_slc, acc_slc)
    per_tile = NSA_KT // SLC_BLOCK
    n_kt = slct_scr.shape[0]

    def slc_tile(kt, kind):
        ktc = jnp.minimum(kt, n_kt - 1)
        k = slc_ref[0, pl.ds(pl.multiple_of(ktc * NSA_KT, NSA_KT), NSA_KT), :]
        pair = sel_scr[pl.ds(pl.multiple_of((ktc // 2) * 2 * per_tile, 2 * per_tile), 2 * per_tile), :]
        r = jnp.where(ktc % 2 == 1, pair[per_tile:], pair[:per_tile])
        delta0 = c * per128 - halves * kt
        causal = [jnp.where(kt * NSA_KT + key_row <= tok(j, (NSA_KT, W)), 0.0, NEG) for j in range(nq)]

        def elem(g):
            if kind == "far":
                return ()
            bias = bias_tile(g, delta0, True)
            return (bias, causal[g % nq]) if kind == "diag" else (bias,)

        def rows(g):
            gi, j = divmod(g, nq)
            out = []
            for i in range(per_tile):
                row = r[i:i + 1, j * W:(j + 1) * W]
                if kind == "far":
                    row = row + bias_ref[dmax, 0, pl.ds(0, 1), pl.ds(gi * LANES, LANES)]
                out.append(row)
            return out

        vt = _with_ones(slct_scr[ktc], 0)
        return lambda g: k, lambda g: vt, elem, rows

    def slc_iter(it, kind):
        _flash_tiles([slc_tile(NSA_UNROLL * it + u, kind) for u in range(NSA_UNROLL)], q_groups, m_slc, acc_slc)

    def slc_body(kind):
        def run(it, _):
            slc_iter(it, kind)
            return 0
        return run

    keys_per_iter = NSA_UNROLL * NSA_KT
    last_iter = (c * CQ) // keys_per_iter
    n_far_tiles = jnp.maximum(c * per128 - (halves - 1) - _far_delta(LANES) + halves, 0) // halves
    n_far_iters = n_far_tiles // NSA_UNROLL
    lax.fori_loop(0, n_far_iters, slc_body("far"), 0)
    lax.fori_loop(n_far_iters, last_iter, slc_body("near"), 0)
    slc_iter(last_iter, "diag")

    _flash_init(m_win, acc_win)
    back = WIN // NSA_KT

    def win_tile(w):
        kt = c - back + w
        ktc = jnp.clip(kt, 0, n_kt - 1)
        k = win_ref[0, pl.ds(pl.multiple_of(ktc * NSA_KT, NSA_KT), NSA_KT), :]
        dist0 = (back - w) * NSA_KT + _iota((NSA_KT, W), 1) - key_row
        negs = []
        for j in range(nq):
            dist = dist0 + j * W
            negs.append(jnp.where((dist >= 0) & (dist < WIN) & (kt >= 0), 0.0, NEG))

        def elem(g):
            return bias_tile(g, (back - w) * halves, False), negs[g % nq]

        vt = _with_ones(wint_scr[ktc], 0)
        return lambda g: k, lambda g: vt, elem, lambda g: (zero_row,)

    _flash_tiles([win_tile(w) for w in range((WIN + CQ) // NSA_KT)], q_groups, m_win, acc_win)

    g_scr[...] = jax.nn.sigmoid(gate_ref[0].astype(F32)).T
    res = []
    for gi in range(C_GROUP):
        col0 = (n * C_GROUP + gi) * 3
        gates = [g_scr[pl.ds(col0 + br, 1), :] for br in range(3)]
        parts = []
        for j in range(nq):
            g = gi * nq + j
            toks = slice(j * W, (j + 1) * W)
            a_s = acc_slc[g][...]
            a_w = acc_win[g][...]
            parts.append(gates[0][:, toks] * ocmp_scr[g]
                         + gates[1][:, toks] * (a_s[HEAD_DIM:] / a_s[0:1])
                         + gates[2][:, toks] * (a_w[HEAD_DIM:] / a_w[0:1]))
        res.append(jnp.concatenate(parts, axis=1))
    o_ref[0] = jnp.concatenate(res, axis=0).T.astype(o_ref.dtype)


def _nsa(proj, tail, kcvc, kcvct, bias_c, overlap_t, dmax):
    B, S, _ = proj.shape
    CQ = C_CHUNK
    assert CQ % NSA_KT == 0 and WIN % NSA_KT == 0 and CQ % GROUP_W == 0
    n_chunks = S // CQ
    NC = kcvc.shape[2]
    NS = S // SLC_BLOCK
    n_tiles = bias_c.shape[0]
    G = C_GROUP * (CQ // GROUP_W)
    kern = functools.partial(_nsa_kernel, dmax=dmax, n_sel=min(SLC_TOPN, NS))
    return pl.pallas_call(
        kern,
        grid=(B, C_KV_HEADS, n_chunks),
        in_specs=[pl.BlockSpec((1, CQ, 2 * LANES), lambda b, n, c: (b, c, BLK_CQ // 2 + n)),
                  pl.BlockSpec((1, 1, NC, LANES), lambda b, n, c: (b, n, 0, 0)),
                  pl.BlockSpec((1, 1, LANES, NC), lambda b, n, c: (b, n, 0, 0)),
                  pl.BlockSpec((1, S, LANES), lambda b, n, c: (b, 0, BLK_CKV + 3 * n + 1)),
                  pl.BlockSpec((1, S, LANES), lambda b, n, c: (b, 0, BLK_CKV + 3 * n + 2)),
                  pl.BlockSpec((1, CQ, LANES), lambda b, n, c: (b, c, BLK_GATE)),
                  pl.BlockSpec((n_tiles, 1, LANES, C_GROUP * LANES), lambda b, n, c: (0, n, 0, 0)),
                  pl.BlockSpec((NS, NC), lambda b, n, c: (0, 0))],
        out_specs=pl.BlockSpec((1, CQ, 2 * LANES), lambda b, n, c: (b, c, n)),
        out_shape=jax.ShapeDtypeStruct((B, S, C_HEADS * HEAD_DIM), BF16),
        scratch_shapes=[pltpu.VMEM((S // NSA_KT, LANES, NSA_KT), BF16), pltpu.VMEM((S // NSA_KT, LANES, NSA_KT), BF16),
                        pltpu.VMEM((G, HEAD_DIM, GROUP_W), F32), pltpu.VMEM((NS, CQ), F32),
                        pltpu.VMEM((NS, CQ), F32), pltpu.VMEM((LANES, CQ), F32),
                        pltpu.VMEM((1, G * GROUP_W), F32), pltpu.VMEM((1, G * GROUP_W), F32)]
        + [pltpu.VMEM((ACC_ROWS, GROUP_W), F32)] * (2 * G),
        compiler_params=_cparams("arbitrary", "arbitrary", "arbitrary"),
        name="nsa",
    )(proj, kcvc, kcvct, tail, tail, tail, bias_c, overlap_t)


def _head_scale():
    scale = np.ones((HEAD_WIDTH,), np.float32)
    for g in range(len(A_CONFIGS)):
        scale[g * A_GROUP_WIDTH:g * A_GROUP_WIDTH + A_HEADS_PER_GROUP * HEAD_DIM] = Q_SCALE
    scale[A_WIDTH:A_WIDTH + B_HEADS * HEAD_DIM] = Q_SCALE * LOG2E
    scale[A_WIDTH + B_WIDTH:] = Q_SCALE * LOG2E
    return scale


def _split_w_in(w_in):
    depth, D, _ = w_in.shape
    scale = np.concatenate([_head_scale(), np.ones((IN_WIDTH - HEAD_WIDTH,), np.float32)])
    w = (w_in * scale).astype(BF16)
    kv = w[:, :, HEAD_WIDTH:HEAD_WIDTH + C_KV_WIDTH].reshape(depth, D, 3, 2, C_KV_HEADS, HEAD_DIM)
    kv = kv.transpose(0, 1, 4, 2, 3, 5).reshape(depth, D, C_KV_WIDTH)
    pad = jnp.zeros((depth, D, TAIL_WIDTH - C_KV_WIDTH - C_GATE_WIDTH), BF16)
    tail = jnp.concatenate([kv, w[:, :, HEAD_WIDTH + C_KV_WIDTH:], pad], axis=2)
    return w, tail


def _compress_weights(cmp_w1, cmp_w2, cmp_pe):
    depth = cmp_w1.shape[0]
    half = CMP_LEN // 2
    w1 = cmp_w1.reshape(depth, 2, 2, half, HEAD_DIM, CMP_HIDDEN)
    w1_blk = jnp.zeros((depth, 2, half, 2, HEAD_DIM, 2, CMP_HIDDEN), F32)
    w1_blk = w1_blk.at[:, :, :, 0, :, 0, :].set(w1[:, 0])
    w1_blk = w1_blk.at[:, :, :, 1, :, 1, :].set(w1[:, 1])
    w1_blk = w1_blk.reshape(depth, 2, half * 2 * HEAD_DIM, 2 * CMP_HIDDEN).astype(BF16)
    w2_blk = jnp.zeros((depth, 2, CMP_HIDDEN, 2, HEAD_DIM), F32)
    w2_blk = w2_blk.at[:, 0, :, 0, :].set(cmp_w2[:, 0])
    w2_blk = w2_blk.at[:, 1, :, 1, :].set(cmp_w2[:, 1])
    w2_blk = w2_blk.reshape(depth, 2 * CMP_HIDDEN, 2 * HEAD_DIM).astype(BF16)
    pe = cmp_pe.reshape(depth, 2, 2, half, HEAD_DIM)
    pe_rows = pe.transpose(0, 2, 3, 1, 4).reshape(depth, 2, half * 2 * HEAD_DIM)
    return w1_blk, w2_blk, pe_rows


def kernel(x, rel_table, w_in, w_out, cmp_w1, cmp_w2, cmp_pe, norm_attn, norm_mlp,
           w_up, conv_w, conv_b, w_down, norm_final):
    B, S, D = x.shape
    depth = w_in.shape[0]
    T = B * S

    a = jnp.arange(A_BLOCK)[:, None]
    bk = jnp.arange(2 * A_BLOCK)[None, :]
    rel = a + A_BLOCK - bk
    idx_a = jnp.stack([_t5_bucket(rel * d) for _, d in A_CONFIGS]).astype(jnp.int32)
    bias_a = _bias_tiles(rel_table[:A_HEADS], idx_a)
    dmax_b = _n_far_tiles(S // MOBA_BLOCK, MOBA_BLOCK)
    bias_b = _bias_tiles(rel_table[A_HEADS:A_HEADS + B_HEADS] * LOG2E,
                         _toeplitz_bucket_tiles_t(dmax_b + 1, MOBA_BLOCK), 2)
    dmax_c = _n_far_tiles(S // LANES, LANES)
    bias_c = _bias_tiles(rel_table[A_HEADS + B_HEADS:] * LOG2E,
                         _toeplitz_bucket_tiles_t(dmax_c + 1, LANES), C_GROUP)

    n16 = S // CMP_STRIDE
    ci = jnp.arange(n16)[None, :] * CMP_STRIDE
    sj = jnp.arange(S // SLC_BLOCK)[:, None] * SLC_BLOCK
    overlap_t = ((ci < sj + SLC_BLOCK) & (ci + CMP_LEN > sj)).astype(BF16)

    w_head, w_tail = _split_w_in(w_in)
    w_out_b = w_out.astype(BF16)
    w_up_b = w_up.astype(BF16)
    w_down_b = w_down.astype(BF16)
    gains_attn = norm_attn.reshape(depth, 1, D)
    gains_mlp = norm_mlp.reshape(depth, 1, D)
    conv_b3 = conv_b.reshape(depth, 1, -1)
    w1_blk, w2_blk, pe_rows = _compress_weights(cmp_w1, cmp_w2, cmp_pe)

    x2 = x.reshape(T, D)
    h = _norm(x2, gains_attn, 0)
    for l in range(depth):
        proj = _inproj(h, w_head, l, HEAD_WIDTH, tn=512).reshape(B, S, HEAD_WIDTH)
        tail = _inproj(h, w_tail, l, TAIL_WIDTH, tn=TAIL_WIDTH).reshape(B, S, TAIL_WIDTH)
        os_, lses = [], []
        for g, (_, d) in enumerate(A_CONFIGS):
            o, lse = _dilated_group(proj, bias_a, g, d)
            os_.append(o)
            lses.append(lse)
        o_b = _moba(proj, bias_b, dmax_b).reshape(T, B_HEADS * HEAD_DIM)
        cmp_cols = tail[:, :, :BLK_GATE * LANES].reshape(B, S, C_KV_HEADS, 3, LANES)[:, :, :, 0]
        sub = cmp_cols.transpose(0, 2, 1, 3).reshape(B, C_KV_HEADS, n16, CMP_STRIDE * LANES)
        kcvc, kcvct = _compress(sub, pe_rows, w1_blk, w2_blk, l)
        o_c = _nsa(proj, tail, kcvc, kcvct, bias_c, overlap_t, dmax_c).reshape(T, C_HEADS * HEAD_DIM)
        x2, h = _outproj(x2, os_, lses, o_b, o_c, w_out_b, gains_mlp, l)
        last = l == depth - 1
        next_gain = norm_final.reshape(1, D) if last else norm_attn[l + 1].reshape(1, D)
        x2, h = _ffn(x2, h, w_up_b, conv_w, conv_b3, w_down_b, l, next_gain, F32 if last else BF16, S)
    return h.reshape(B, S, D)
```

```python
import functools
import math

import numpy as np
import jax
import jax.numpy as jnp
from jax import lax
from jax.experimental import pallas as pl
from jax.experimental.pallas import tpu as pltpu

F32 = jnp.float32
BF16 = jnp.bfloat16

HEAD_DIM = 64
LANES = 128
A_CONFIGS = ((128, 1), (512, 4), (2048, 16))
A_HEADS_PER_GROUP = 4
A_HEADS = 12
A_BLOCK = 128
A_BLOCKS_PER_STEP = 2
B_HEADS = 8
MOBA_BLOCK = 256
MOBA_TOPK = 3
MOBA_PAIRS = 2
MOBA_UNROLL = 4
C_KV_HEADS = 3
C_GROUP = 4
C_HEADS = 12
CMP_STRIDE = 16
CMP_LEN = 32
CMP_HIDDEN = 128
SLC_BLOCK = 64
SLC_TOPN = 16
N_FORCED = 3
IMP_SPLITS = 3
WIN = 512
INVALID_SCORE = -1.0
FORCED_SCORE = -2.0
N_BUCKETS = 32
T5_MAX_DIST = 2048
EPS = 1e-6
NEG = -1e30
M_FLOOR = -1e20
Q_SCALE = HEAD_DIM ** -0.5
LOG2E = math.log2(math.e)
C_CHUNK = 512
ACC_ROWS = 128
GROUP_W = 128
NSA_KT = 256
NSA_UNROLL = 4

A_WIDTH = 3 * A_HEADS * HEAD_DIM
A_GROUP_WIDTH = 3 * A_HEADS_PER_GROUP * HEAD_DIM
B_WIDTH = 3 * B_HEADS * HEAD_DIM
C_Q_WIDTH = C_HEADS * HEAD_DIM
C_KV_WIDTH = 6 * C_KV_HEADS * HEAD_DIM
C_GATE_WIDTH = 3 * C_HEADS
IN_WIDTH = A_WIDTH + B_WIDTH + C_Q_WIDTH + C_KV_WIDTH + C_GATE_WIDTH
BLK_B = A_WIDTH // LANES
BLK_CQ = BLK_B + B_WIDTH // LANES
HEAD_WIDTH = A_WIDTH + B_WIDTH + C_Q_WIDTH
BLK_CKV = 0
BLK_GATE = 3 * C_KV_HEADS
TAIL_WIDTH = (BLK_GATE + 1) * LANES

VMEM_LIMIT = 56 * 1024 * 1024


def _cparams(*sem):
    return pltpu.CompilerParams(dimension_semantics=sem, vmem_limit_bytes=VMEM_LIMIT)


def _dot_nt(a, b):
    return lax.dot_general(a, b, (((1,), (1,)), ((), ())), preferred_element_type=F32)


def _dot(a, b, **kw):
    return jnp.dot(a, b, preferred_element_type=F32, **kw)


def _transpose_bf16(x):
    return x.astype(F32).T.astype(BF16)


def _iota(shape, dim):
    return lax.broadcasted_iota(jnp.int32, shape, dim)


def _t5_bucket(dist):
    n = jnp.maximum(dist, 0)
    max_exact = N_BUCKETS // 2
    nf = jnp.maximum(n, 1).astype(F32)
    large = max_exact + (jnp.log(nf / max_exact) / math.log(T5_MAX_DIST / max_exact)
                         * (N_BUCKETS - max_exact)).astype(jnp.int32)
    large = jnp.minimum(large, N_BUCKETS - 1)
    return jnp.where(n < max_exact, n, large)


def _bias_lookup_kernel(tbl_ref, idx_ref, o_ref, *, hpr):
    idx = idx_ref[0]
    C = idx.shape[1]
    for h in range(tbl_ref.shape[0]):
        acc = jnp.zeros(idx.shape, F32)
        for b in range(N_BUCKETS):
            acc = jnp.where(idx == b, tbl_ref[h, b], acc)
        o_ref[0, h // hpr, :, (h % hpr) * C:(h % hpr + 1) * C] = acc


def _bias_tiles(tbl, idx, heads_per_row=1):
    n, R, C = idx.shape
    H = tbl.shape[0]
    hpr = heads_per_row
    return pl.pallas_call(
        functools.partial(_bias_lookup_kernel, hpr=hpr),
        grid=(n,),
        in_specs=[pl.BlockSpec(memory_space=pltpu.SMEM),
                  pl.BlockSpec((1, R, C), lambda t: (t, 0, 0))],
        out_specs=pl.BlockSpec((1, H // hpr, R, hpr * C), lambda t: (t, 0, 0, 0)),
        out_shape=jax.ShapeDtypeStruct((n, H // hpr, R, hpr * C), F32),
        compiler_params=_cparams("arbitrary"),
        name="bias_tiles",
    )(tbl, idx)


def _toeplitz_bucket_tiles_t(n_tiles, size):
    d = jnp.arange(n_tiles)[:, None, None] * size
    key = jnp.arange(size)[None, :, None]
    qry = jnp.arange(size)[None, None, :]
    return _t5_bucket(d + qry - key).astype(jnp.int32)


def _far_delta(size):
    return -(-(T5_MAX_DIST + size - 1) // size)


def _n_far_tiles(n_chunks, size):
    return min(_far_delta(size), n_chunks - 1)


def _norm_rows(x, g):
    ms = jnp.mean(x * x, axis=-1, keepdims=True)
    return x * lax.rsqrt(ms + EPS) * g


def _norm_kernel(x_ref, g_ref, o_ref):
    o_ref[...] = _norm_rows(x_ref[...], g_ref[...]).astype(o_ref.dtype)


def _norm(x2, gains, l, tm=1024):
    T, D = x2.shape
    return pl.pallas_call(
        _norm_kernel,
        grid=(T // tm,),
        in_specs=[pl.BlockSpec((tm, D), lambda i: (i, 0)),
                  pl.BlockSpec((None, 1, D), lambda i: (l, 0, 0))],
        out_specs=pl.BlockSpec((tm, D), lambda i: (i, 0)),
        out_shape=jax.ShapeDtypeStruct((T, D), BF16),
        compiler_params=_cparams("arbitrary"),
        name="first_norm",
    )(x2, gains)


def _inproj_kernel(h_ref, w_ref, o_ref):
    o_ref[...] = _dot(h_ref[...], w_ref[...]).astype(o_ref.dtype)


def _inproj(h, w_all, l, N, tn, tm=1024):
    T, D = h.shape
    return pl.pallas_call(
        _inproj_kernel,
        grid=(T // tm, N // tn),
        in_specs=[pl.BlockSpec((tm, D), lambda i, j: (i, 0)),
                  pl.BlockSpec((None, D, tn), lambda i, j: (l, 0, j))],
        out_specs=pl.BlockSpec((tm, tn), lambda i, j: (i, j)),
        out_shape=jax.ShapeDtypeStruct((T, N), BF16),
        compiler_params=_cparams("arbitrary", "arbitrary"),
        name="inproj",
    )(h, w_all)


def _outproj_kernel(x_ref, o0_ref, o1_ref, o2_ref, l0_ref, l1_ref, l2_ref, b_ref, c_ref, w_ref, g_ref, o_ref, h_ref):
    ls = [l0_ref[...], l1_ref[...], l2_ref[...]]
    m = jnp.maximum(jnp.maximum(ls[0], ls[1]), ls[2])
    es = [jnp.exp(l - m) for l in ls]
    inv = 1.0 / (es[0] + es[1] + es[2])
    gw = o0_ref.shape[1]
    acc = jnp.zeros(o_ref.shape, F32)
    for g, og_ref in enumerate((o0_ref, o1_ref, o2_ref)):
        acc += _dot((og_ref[...] * (es[g] * inv)).astype(BF16), w_ref[pl.ds(g * gw, gw), :])
    ra, rb = 3 * gw, b_ref.shape[1]
    acc += _dot(b_ref[...], w_ref[pl.ds(ra, rb), :])
    acc += _dot(c_ref[...], w_ref[pl.ds(ra + rb, c_ref.shape[1]), :])
    x_new = x_ref[...] + acc
    o_ref[...] = x_new
    h_ref[...] = _norm_rows(x_new, g_ref[...]).astype(h_ref.dtype)


def _outproj(x2, os_, lses, ob, oc, w_all, gains, l, tm=512):
    T, D = x2.shape
    row = lambda i: (i, 0)
    gspec = pl.BlockSpec((tm, os_[0].shape[1]), row)
    return pl.pallas_call(
        _outproj_kernel,
        grid=(T // tm,),
        in_specs=[pl.BlockSpec((tm, D), row)] + [gspec] * 6 + [
                  pl.BlockSpec((tm, ob.shape[1]), row),
                  pl.BlockSpec((tm, oc.shape[1]), row),
                  pl.BlockSpec((None, w_all.shape[1], D), lambda i: (l, 0, 0)),
                  pl.BlockSpec((None, 1, D), lambda i: (l, 0, 0))],
        out_specs=[pl.BlockSpec((tm, D), row), pl.BlockSpec((tm, D), row)],
        out_shape=[jax.ShapeDtypeStruct((T, D), F32), jax.ShapeDtypeStruct((T, D), BF16)],
        compiler_params=_cparams("arbitrary"),
        name="outproj",
    )(x2, *os_, *lses, ob, oc, w_all, gains)


FFN_PARTS = 2
FFN_HALO = 16


def _ffn_kernel(x_ref, h_ref, halo_ref, wa_ref, wg_ref, cwa_ref, cwg_ref, cba_ref, cbg_ref, wd_ref, g_ref,
                o_ref, n_ref, h_scr, *u_scrs, tm, blocks_per_seq):
    i = pl.program_id(0)
    f = pl.program_id(1)

    @pl.when(f == 0)
    def _():
        o_ref[...] = x_ref[...]
        keep = jnp.where(i % blocks_per_seq == 0, 0.0, 1.0).astype(BF16)
        h_scr[pl.ds(0, FFN_HALO), :] = halo_ref[0] * keep
        h_scr[pl.ds(FFN_HALO, tm), :] = h_ref[...]

    h = h_scr[...]
    n_parts = len(u_scrs) // 2
    part = wa_ref.shape[1] // n_parts

    def up(pt):
        cols = pl.ds(pt * part, part)
        u_scrs[2 * pt][...] = _dot(h, wa_ref[:, cols])
        u_scrs[2 * pt + 1][...] = _dot(h, wg_ref[:, cols])

    def conv(u_scr, pt, cw_ref, cb_ref):
        cols = pl.ds(pt * part, part)
        acc = cb_ref[:, cols]
        for j in range(3):
            acc = acc + cw_ref[pl.ds(j, 1), cols] * u_scr[pl.ds(FFN_HALO - j, tm), :]
        return acc

    up(0)
    for pt in range(n_parts):
        if pt + 1 < n_parts:
            up(pt + 1)
        ya = conv(u_scrs[2 * pt], pt, cwa_ref, cba_ref)
        yg = conv(u_scrs[2 * pt + 1], pt, cwg_ref, cbg_ref)
        act = (yg * jax.nn.sigmoid(yg) * ya).astype(BF16)
        o_ref[...] += _dot(act, wd_ref[pl.ds(pt * part, part), :])

    @pl.when(f == pl.num_programs(1) - 1)
    def _():
        n_ref[...] = _norm_rows(o_ref[...], g_ref[...]).astype(n_ref.dtype)


def _ffn(x2, h, w_up, conv_w, conv_b, w_down, l, next_gain, next_dtype, seq, tm=512, tf=512):
    T, D = x2.shape
    Fh = w_down.shape[1]
    nf = Fh // tf
    halo_view = h.reshape(T // FFN_HALO, FFN_HALO, D)
    rb = tm // FFN_HALO
    kern = functools.partial(_ffn_kernel, tm=tm, blocks_per_seq=seq // tm)
    row = lambda i, f: (i, 0)
    return pl.pallas_call(
        kern,
        grid=(T // tm, nf),
        in_specs=[pl.BlockSpec((tm, D), row),
                  pl.BlockSpec((tm, D), row),
                  pl.BlockSpec((1, FFN_HALO, D), lambda i, f: (jnp.maximum(i * rb - 1, 0), 0, 0)),
                  pl.BlockSpec((None, D, tf), lambda i, f: (l, 0, f)),
                  pl.BlockSpec((None, D, tf), lambda i, f: (l, 0, f + nf)),
                  pl.BlockSpec((None, 3, tf), lambda i, f: (l, 0, f)),
                  pl.BlockSpec((None, 3, tf), lambda i, f: (l, 0, f + nf)),
                  pl.BlockSpec((None, 1, tf), lambda i, f: (l, 0, f)),
                  pl.BlockSpec((None, 1, tf), lambda i, f: (l, 0, f + nf)),
                  pl.BlockSpec((None, tf, D), lambda i, f: (l, f, 0)),
                  pl.BlockSpec((1, D), lambda i, f: (0, 0))],
        out_specs=[pl.BlockSpec((tm, D), row), pl.BlockSpec((tm, D), row)],
        out_shape=[jax.ShapeDtypeStruct((T, D), F32), jax.ShapeDtypeStruct((T, D), next_dtype)],
        scratch_shapes=[pltpu.VMEM((tm + FFN_HALO, D), BF16)]
        + [pltpu.VMEM((tm + FFN_HALO, tf // FFN_PARTS), F32)] * (2 * FFN_PARTS),
        compiler_params=_cparams("arbitrary", "arbitrary"),
        name="convffn",
    )(x2, h, halo_view, w_up, w_up, conv_w, conv_w, conv_b, conv_b, w_down, next_gain)


def _stack_pair(qp):
    lane = _iota(qp.shape, 1)
    zero = jnp.zeros_like(qp)
    return jnp.concatenate([jnp.where(lane < HEAD_DIM, qp, zero),
                            jnp.where(lane >= HEAD_DIM, qp, zero)], axis=0)


def _unstack_pair(o):
    R = o.shape[0] // 2
    lane = _iota((R, LANES), 1)
    return jnp.where(lane < HEAD_DIM, o[:R], o[R:])


def _dilated_kernel(q_ref, kc_ref, kp_ref, vc_ref, vp_ref, bias_ref, o_ref, lse_ref):
    i = pl.program_id(2)
    R = 2 * A_BLOCK
    n_blocks = q_ref.shape[1] // A_BLOCK
    a = _iota((R, 2 * A_BLOCK), 0) % A_BLOCK
    bk = _iota((R, 2 * A_BLOCK), 1)
    rel = a + A_BLOCK - bk
    band = (rel >= 0) & (rel <= A_BLOCK)
    first = band & ((bk >= A_BLOCK) | (i > 0))
    n_pairs = q_ref.shape[2] // LANES
    units = [(j, p) for j in range(n_blocks) for p in range(n_pairs)]

    def rows(j):
        return slice(j * A_BLOCK, (j + 1) * A_BLOCK)

    def with_prev(cur_ref, prev_ref, j, cols):
        prev = prev_ref[0, :, cols] if j == 0 else cur_ref[0, rows(j - 1), cols]
        return jnp.concatenate([prev, cur_ref[0, rows(j), cols]], axis=0)

    def scores(j, p):
        cols = slice(p * LANES, (p + 1) * LANES)
        qs = _stack_pair(q_ref[0, rows(j), cols])
        s = _dot_nt(qs, with_prev(kc_ref, kp_ref, j, cols)) + bias_ref[0, 2 * p:2 * p + 2].reshape(R, 2 * A_BLOCK)
        return jnp.where(first if j == 0 else band, s, NEG)

    s_all = [scores(j, p) for j, p in units]
    for (j, p), s in zip(units, s_all):
        cols = slice(p * LANES, (p + 1) * LANES)
        mask = first if j == 0 else band
        m = jnp.max(s, axis=1, keepdims=True)
        e = jnp.where(mask, jnp.exp(s - m), 0.0)
        den = jnp.sum(e, axis=1, keepdims=True)
        o = _dot(e.astype(BF16), with_prev(vc_ref, vp_ref, j, cols)) / den
        lse = m + jnp.log(den)
        o_ref[0, rows(j), cols] = _unstack_pair(o)
        lse_ref[0, rows(j), cols] = _unstack_pair(jnp.broadcast_to(lse, (R, LANES)))


def _dilated_group(proj, bias_a, g, dilation):
    B, S, _ = proj.shape
    L = S // dilation
    nq = L // A_BLOCK
    per_step = A_BLOCKS_PER_STEP if nq % A_BLOCKS_PER_STEP == 0 else 1
    W = A_HEADS_PER_GROUP * HEAD_DIM
    if dilation == 1:
        view, base = proj, g * A_GROUP_WIDTH // W
    else:
        cols = proj[:, :, g * A_GROUP_WIDTH:(g + 1) * A_GROUP_WIDTH]
        view, base = cols.reshape(B, L, dilation * A_GROUP_WIDTH), 0

    def cur(part):
        return pl.BlockSpec((1, per_step * A_BLOCK, W), lambda b, r, i: (b, i, base + r * 3 + part))

    def prev(part):
        return pl.BlockSpec((1, A_BLOCK, W),
                            lambda b, r, i: (b, jnp.maximum(i * per_step - 1, 0), base + r * 3 + part))

    out_spec = pl.BlockSpec((1, per_step * A_BLOCK, W), lambda b, r, i: (b, i, r))
    o, lse = pl.pallas_call(
        _dilated_kernel,
        grid=(B, dilation, nq // per_step),
        in_specs=[cur(0), cur(1), prev(1), cur(2), prev(2),
                  pl.BlockSpec((1, A_HEADS_PER_GROUP, A_BLOCK, 2 * A_BLOCK), lambda b, r, i: (g, g, 0, 0))],
        out_specs=[out_spec, out_spec],
        out_shape=[jax.ShapeDtypeStruct((B, L, dilation * W), F32)] * 2,
        compiler_params=_cparams("arbitrary", "arbitrary", "arbitrary"),
        name=f"dilated_g{g}",
    )(view, view, view, view, view, bias_a)
    return o.reshape(B * S, W), lse.reshape(B * S, W)


def _flash_init(m_scr, acc_scrs):
    m_scr[...] = jnp.full(m_scr.shape, M_FLOOR, F32)
    for acc in acc_scrs:
        acc[...] = jnp.zeros(acc.shape, F32)


def _with_ones(vt, row):
    return jnp.where(_iota(vt.shape, 0) == row, jnp.ones_like(vt), vt)


def _flash_tiles(tiles, q_groups, m_scr, acc_scrs):
    G = len(q_groups)
    W = q_groups[0].shape[0]
    m = [m_scr[:, g * W:(g + 1) * W] for g in range(G)]
    units = [(t, g) for t in range(len(tiles)) for g in range(G)]

    def scores(t, g):
        k_of, _, elem_of, _ = tiles[t]
        s = _dot_nt(k_of(g), q_groups[g])
        for term in elem_of(g):
            s = s + term
        return s

    def apply(g, alpha, pv):
        acc_scrs[g][...] = alpha * acc_scrs[g][...] + pv

    s_next = scores(*units[0])
    pending = None
    for n, (t, g) in enumerate(units):
        s = s_next
        if n + 1 < len(units):
            s_next = scores(*units[n + 1])
        rows = tiles[t][3](g)
        seg = s.shape[0] // len(rows)
        parts = [s[i * seg:(i + 1) * seg] for i in range(len(rows))]
        m_new = m[g]
        for part, row in zip(parts, rows):
            m_new = jnp.maximum(m_new, jnp.max(part, axis=0, keepdims=True) + row)
        alpha = jnp.exp2(m[g] - m_new)
        p = jnp.concatenate([jnp.exp2(part - (m_new - row)) for part, row in zip(parts, rows)], axis=0)
        m[g] = m_new
        pv = _dot(tiles[t][1](g), p.astype(BF16))
        if pending is not None:
            apply(*pending)
        pending = (g, alpha, pv)
    apply(*pending)
    m_scr[...] = jnp.concatenate(m, axis=1)


PICKED = -3e38
NO_INDEX = 1e9


def _top_picks(score, n_pick):
    idx = _iota(score.shape, 0).astype(F32)

    def pick(_, g):
        mx = jnp.max(g, axis=0, keepdims=True)
        first = jnp.min(jnp.where(g == mx, idx, NO_INDEX), axis=0, keepdims=True)
        return jnp.where(idx == first, PICKED, g)

    return lax.fori_loop(0, n_pick, pick, score) == PICKED


def _moba_kernel(q_ref, k_ref, v_ref, bias_ref, o_ref, km_scr, vt_scr, sel_scr, m_scr, *acc_scrs,
                 dmax, n_blocks):
    c = pl.program_id(2)
    BLK = MOBA_BLOCK
    W = GROUP_W
    per_head = BLK // W
    per_pair = 2 * per_head
    G = MOBA_PAIRS * per_pair

    @pl.when(c == 0)
    def _():
        def body(j, _):
            rows = pl.ds(pl.multiple_of(j * BLK, BLK), BLK)
            km_scr[pl.ds(j, 1), :] = jnp.mean(k_ref[0, rows, :].astype(F32), axis=0, keepdims=True)
            vt_scr[j] = _transpose_bf16(v_ref[0, rows, :])
            return 0
        lax.fori_loop(0, n_blocks, body, 0)

    def lanes(pi):
        return slice(pi * LANES, (pi + 1) * LANES)

    q = q_ref[0]
    km = km_scr[...].astype(BF16)
    q_groups, gates = [], []
    for pi in range(MOBA_PAIRS):
        qs = _stack_pair(q[:, lanes(pi)])
        gates.append(_dot_nt(km[:, lanes(pi)], qs))
        q_groups.extend(qs[g * W:(g + 1) * W] for g in range(per_pair))
    gate = jnp.concatenate(gates, axis=1)
    blk = _iota(gate.shape, 0)
    gate = jnp.where(blk < c, gate, NEG)
    sel = _top_picks(gate, min(MOBA_TOPK, n_blocks))
    sel_scr[...] = jnp.where((blk < c) & sel, 0.0, NEG)

    _flash_init(m_scr, acc_scrs)

    def group(g):
        pi, gp = divmod(g, per_pair)
        return pi, gp // per_head, gp * W

    def keys_of(kj):
        return lambda g: kj[:, lanes(group(g)[0])]

    def values_of(vt):
        opts = [[_with_ones(vt[lanes(pi)], (1 - h) * HEAD_DIM) for h in range(2)] for pi in range(MOBA_PAIRS)]
        return lambda g: opts[group(g)[0]][group(g)[1]]

    def bias_of(tile, g, rows=slice(None)):
        pi, _, col = group(g)
        return bias_ref[tile, pi, rows, pl.ds(col, W)]

    def past_tile(j, far):
        jc = jnp.minimum(j, n_blocks - 1)
        kj = k_ref[0, pl.ds(pl.multiple_of(jc * BLK, BLK), BLK), :]
        tile = jnp.clip(c - j, 0, dmax)
        selrow = sel_scr[pl.ds(jc, 1), :]

        def elem(g):
            return () if far else (bias_of(tile, g),)

        def rows(g):
            row = selrow[:, g * W:(g + 1) * W]
            return (row + bias_of(dmax, g, pl.ds(0, 1)),) if far else (row,)

        return keys_of(kj), values_of(vt_scr[jc]), elem, rows

    def body(far):
        def run(it, _):
            _flash_tiles([past_tile(MOBA_UNROLL * it + u, far) for u in range(MOBA_UNROLL)],
                         q_groups, m_scr, acc_scrs)
            return 0
        return run

    n_far_iters = jnp.maximum(c - _far_delta(BLK) + 1, 0) // MOBA_UNROLL
    lax.fori_loop(0, n_far_iters, body(True), 0)
    lax.fori_loop(n_far_iters, (c + MOBA_UNROLL - 1) // MOBA_UNROLL, body(False), 0)

    ko = k_ref[0, pl.ds(pl.multiple_of(c * BLK, BLK), BLK), :]

    def own_elem(g):
        qpos = (g * W + _iota((BLK, W), 1)) % BLK
        return bias_of(0, g), jnp.where(_iota((BLK, W), 0) <= qpos, 0.0, NEG)

    zero_row = jnp.zeros((1, W), F32)
    _flash_tiles([(keys_of(ko), values_of(vt_scr[c]), own_elem, lambda g: (zero_row,))], q_groups, m_scr, acc_scrs)
    pairs_out = []
    for pi in range(MOBA_PAIRS):
        heads_out = []
        for h in range(2):
            parts = []
            for jh in range(per_head):
                a = acc_scrs[pi * per_pair + h * per_head + jh][...]
                den = a[(1 - h) * HEAD_DIM:(1 - h) * HEAD_DIM + 1]
                parts.append(a[h * HEAD_DIM:(h + 1) * HEAD_DIM] / den)
            heads_out.append(jnp.concatenate(parts, axis=1))
        pairs_out.append(jnp.concatenate(heads_out, axis=0).T)
    o_ref[0] = jnp.concatenate(pairs_out, axis=1).astype(o_ref.dtype)


def _moba(proj, bias_b, dmax):
    B, S, _ = proj.shape
    NB = S // MOBA_BLOCK
    n_tiles = bias_b.shape[0]
    PW = MOBA_PAIRS * LANES
    qb, kb, vb = (BLK_B * LANES // PW, (BLK_B + 4) * LANES // PW, (BLK_B + 8) * LANES // PW)
    kern = functools.partial(_moba_kernel, dmax=dmax, n_blocks=NB)
    Q = MOBA_PAIRS * 2 * MOBA_BLOCK
    return pl.pallas_call(
        kern,
        grid=(B, B_HEADS // (2 * MOBA_PAIRS), NB),
        in_specs=[pl.BlockSpec((1, MOBA_BLOCK, PW), lambda b, p, c: (b, c, qb + p)),
                  pl.BlockSpec((1, S, PW), lambda b, p, c: (b, 0, kb + p)),
                  pl.BlockSpec((1, S, PW), lambda b, p, c: (b, 0, vb + p)),
                  pl.BlockSpec((n_tiles, MOBA_PAIRS, MOBA_BLOCK, 2 * MOBA_BLOCK), lambda b, p, c: (0, p, 0, 0))],
        out_specs=pl.BlockSpec((1, MOBA_BLOCK, PW), lambda b, p, c: (b, c, p)),
        out_shape=jax.ShapeDtypeStruct((B, S, B_HEADS * HEAD_DIM), BF16),
        scratch_shapes=[pltpu.VMEM((NB, PW), F32), pltpu.VMEM((NB, PW, MOBA_BLOCK), BF16),
                        pltpu.VMEM((NB, Q), F32), pltpu.VMEM((1, Q), F32)]
        + [pltpu.VMEM((ACC_ROWS, GROUP_W), F32)] * (Q // GROUP_W),
        compiler_params=_cparams("arbitrary", "arbitrary", "arbitrary"),
        name="moba",
    )(proj, proj, proj, bias_b)


def _gelu_tanh(x):
    return 0.5 * x * (1.0 + jnp.tanh(math.sqrt(2.0 / math.pi) * (x + 0.044715 * (x * x * x))))


def _compress_kernel(sub_ref, pe_ref, w1_ref, w2_ref, o_ref, ot_ref, v_scr):
    n16 = sub_ref.shape[2]
    sub = sub_ref[0, 0].astype(F32)
    top = (sub + pe_ref[0:1, :]).astype(BF16)
    bot = (sub + pe_ref[1:2, :]).astype(BF16)
    u = _dot(top, w1_ref[0])
    v_scr[pl.ds(0, n16), :] = _dot(bot, w1_ref[1])
    v_scr[pl.ds(n16, 8), :] = jnp.zeros((8, v_scr.shape[1]), F32)
    hidden = u + v_scr[pl.ds(1, n16), :]
    out = _dot(_gelu_tanh(hidden).astype(BF16), w2_ref[...])
    o_ref[0, 0] = out.astype(o_ref.dtype)
    ot_ref[0, 0] = out.T.astype(ot_ref.dtype)


def _compress(sub, pe_rows, w1_blk, w2_blk, l):
    B, KV, n16, W = sub.shape
    return pl.pallas_call(
        _compress_kernel,
        grid=(B, KV),
        in_specs=[pl.BlockSpec((1, 1, n16, W), lambda b, n: (b, n, 0, 0)),
                  pl.BlockSpec((None, 2, W), lambda b, n: (l, 0, 0)),
                  pl.BlockSpec((None, 2, W, 2 * CMP_HIDDEN), lambda b, n: (l, 0, 0, 0)),
                  pl.BlockSpec((None, 2 * CMP_HIDDEN, LANES), lambda b, n: (l, 0, 0))],
        out_specs=[pl.BlockSpec((1, 1, n16, LANES), lambda b, n: (b, n, 0, 0)),
                   pl.BlockSpec((1, 1, LANES, n16), lambda b, n: (b, n, 0, 0))],
        out_shape=[jax.ShapeDtypeStruct((B, KV, n16, LANES), BF16),
                   jax.ShapeDtypeStruct((B, KV, LANES, n16), BF16)],
        scratch_shapes=[pltpu.VMEM((n16 + 8, 2 * CMP_HIDDEN), F32)],
        compiler_params=_cparams("arbitrary", "arbitrary"),
        name="nsa_compress",
    )(sub, pe_rows, w1_blk, w2_blk)


def _nsa_kernel(q_ref, kc_ref, kct_ref, slc_ref, win_ref, gate_ref, bias_ref, ovt_ref,
                o_ref, slct_scr, wint_scr, ocmp_scr, imp_scr, sel_scr, g_scr, m_slc, m_win, *acc_scrs, dmax, n_sel):
    n = pl.program_id(1)
    c = pl.program_id(2)
    CQ = C_CHUNK
    W = GROUP_W
    nq = CQ // W
    G = C_GROUP * nq
    halves = NSA_KT // LANES
    per128 = CQ // LANES

    @pl.when(c == 0)
    def _():
        def body(t, _):
            rows = pl.ds(pl.multiple_of(t * NSA_KT, NSA_KT), NSA_KT)
            slct_scr[t] = _transpose_bf16(slc_ref[0, rows, :])
            wint_scr[t] = _transpose_bf16(win_ref[0, rows, :])
            return 0
        lax.fori_loop(0, slct_scr.shape[0], body, 0)

    lane = _iota((CQ, LANES), 1)
    q = q_ref[0].astype(F32)
    q_groups = []
    for pr in range(2):
        qp = q[:, pr * LANES:(pr + 1) * LANES]
        for head in (jnp.where(lane < HEAD_DIM, qp, 0.0), jnp.where(lane < HEAD_DIM, pltpu.roll(qp, HEAD_DIM, 1), 0.0)):
            hb = head.astype(BF16)
            q_groups.extend(hb[j * W:(j + 1) * W] for j in range(nq))

    def tok(j, shape):
        return c * CQ + j * W + _iota(shape, 1)

    NC = kc_ref.shape[2]
    NS = ovt_ref.shape[0]

    def cmp_branch(nk):
        kc = kc_ref[0, 0, pl.ds(0, nk), :]
        kct = kct_ref[0, 0, :, pl.ds(0, nk)]
        ovt = ovt_ref[:, pl.ds(0, nk)]
        key_end = _iota((nk, W), 0) * CMP_STRIDE + (CMP_LEN - 1)
        visible = [key_end <= tok(j, (nk, W)) for j in range(nq)]
        psum = [jnp.zeros((nk, W), F32) for _ in range(nq)]
        s_next = _dot_nt(kc, q_groups[0])
        for g in range(G):
            j = g % nq
            s = jnp.where(visible[j], s_next, NEG)
            if g + 1 < G:
                s_next = _dot_nt(kc, q_groups[g + 1])
            m = jnp.max(s, axis=0, keepdims=True)
            e = jnp.where(visible[j], jnp.exp2(s - m), 0.0)
            den = jnp.sum(e, axis=0, keepdims=True)
            p = e / jnp.maximum(den, 1e-30)
            ocmp_scr[g] = _dot(kct, p.astype(BF16))[HEAD_DIM:]
            psum[j] = psum[j] + p
        for j in range(nq):
            imp = jnp.zeros((NS, W), F32)
            rest = psum[j]
            for _ in range(IMP_SPLITS):
                piece = rest.astype(BF16)
                imp = imp + _dot(ovt, piece)
                rest = rest - piece.astype(F32)
            imp_scr[:, j * W:(j + 1) * W] = imp

    half_chunks = (NC // 2) // (CQ // CMP_STRIDE)

    @pl.when(c < half_chunks)
    def _():
        cmp_branch(NC // 2)

    @pl.when(c >= half_chunks)
    def _():
        cmp_branch(NC)

    imp = imp_scr[...]
    jj = _iota((NS, CQ), 0)
    own = (c * CQ + _iota((NS, CQ), 1)) // SLC_BLOCK
    forced = (jj == 0) | (jj == own) | (jj == own - 1)
    score = jnp.where(forced, FORCED_SCORE, jnp.where(jj <= own, imp, INVALID_SCORE))
    picked = _top_picks(score, max(n_sel - N_FORCED, 0))
    sel_scr[...] = jnp.where(forced | picked, 0.0, NEG)

    acc_slc = acc_scrs[:G]
    acc_win = acc_scrs[G:]
    key_row = _iota((NSA_KT, W), 0)
    zero_row = jnp.zeros((1, W), F32)

    def bias_tile(g, delta0, dynamic):
        gi, j = divmod(g, nq)
        cols = pl.ds(gi * LANES, LANES)
        idx = [delta0 + j - h for h in range(halves)]
        idx = [jnp.clip(d, 0, dmax) for d in idx] if dynamic else [max(d, 0) for d in idx]
        return jnp.concatenate([bias_ref[d, 0, :, cols] for d in idx], axis=0)

    _flash_init(m_slc, acc_slc)
    per_tile = NSA_KT // SLC_BLOCK
    n_kt = slct_scr.shape[0]

    def slc_tile(kt, kind):
        ktc = jnp.minimum(kt, n_kt - 1)
        k = slc_ref[0, pl.ds(pl.multiple_of(ktc * NSA_KT, NSA_KT), NSA_KT), :]
        pair = sel_scr[pl.ds(pl.multiple_of((ktc // 2) * 2 * per_tile, 2 * per_tile), 2 * per_tile), :]
        r = jnp.where(ktc % 2 == 1, pair[per_tile:], pair[:per_tile])
        delta0 = c * per128 - halves * kt
        causal = [jnp.where(kt * NSA_KT + key_row <= tok(j, (NSA_KT, W)), 0.0, NEG) for j in range(nq)]

        def elem(g):
            if kind == "far":
                return ()
            bias = bias_tile(g, delta0, True)
            return (bias, causal[g % nq]) if kind == "diag" else (bias,)

        def rows(g):
            gi, j = divmod(g, nq)
            out = []
            for i in range(per_tile):
                row = r[i:i + 1, j * W:(j + 1) * W]
                if kind == "far":
                    row = row + bias_ref[dmax, 0, pl.ds(0, 1), pl.ds(gi * LANES, LANES)]
                out.append(row)
            return out

        vt = _with_ones(slct_scr[ktc], 0)
        return lambda g: k, lambda g: vt, elem, rows

    def slc_iter(it, kind):
        _flash_tiles([slc_tile(NSA_UNROLL * it + u, kind) for u in range(NSA_UNROLL)], q_groups, m_slc, acc_slc)

    def slc_body(kind):
        def run(it, _):
            slc_iter(it, kind)
            return 0
        return run

    keys_per_iter = NSA_UNROLL * NSA_KT
    last_iter = (c * CQ) // keys_per_iter
    n_far_tiles = jnp.maximum(c * per128 - (halves - 1) - _far_delta(LANES) + halves, 0) // halves
    n_far_iters = n_far_tiles // NSA_UNROLL
    lax.fori_loop(0, n_far_iters, slc_body("far"), 0)
    lax.fori_loop(n_far_iters, last_iter, slc_body("near"), 0)
    slc_iter(last_iter, "diag")

    _flash_init(m_win, acc_win)
    back = WIN // NSA_KT

    def win_tile(w):
        kt = c * (CQ // NSA_KT) - back + w
        ktc = jnp.clip(kt, 0, n_kt - 1)
        k = win_ref[0, pl.ds(pl.multiple_of(ktc * NSA_KT, NSA_KT), NSA_KT), :]
        dist0 = (back - w) * NSA_KT + _iota((NSA_KT, W), 1) - key_row
        negs = []
        for j in range(nq):
            dist = dist0 + j * W
            negs.append(jnp.where((dist >= 0) & (dist < WIN) & (kt >= 0), 0.0, NEG))

        def elem(g):
            return bias_tile(g, (back - w) * halves, False), negs[g % nq]

        vt = _with_ones(wint_scr[ktc], 0)
        return lambda g: k, lambda g: vt, elem, lambda g: (zero_row,)

    _flash_tiles([win_tile(w) for w in range((WIN + CQ) // NSA_KT)], q_groups, m_win, acc_win)

    g_scr[...] = jax.nn.sigmoid(gate_ref[0].astype(F32)).T
    res = []
    for gi in range(C_GROUP):
        col0 = (n * C_GROUP + gi) * 3
        gates = [g_scr[pl.ds(col0 + br, 1), :] for br in range(3)]
        parts = []
        for j in range(nq):
            g = gi * nq + j
            toks = slice(j * W, (j + 1) * W)
            a_s = acc_slc[g][...]
            a_w = acc_win[g][...]
            parts.append(gates[0][:, toks] * ocmp_scr[g]
                         + gates[1][:, toks] * (a_s[HEAD_DIM:] / a_s[0:1])
                         + gates[2][:, toks] * (a_w[HEAD_DIM:] / a_w[0:1]))
        res.append(jnp.concatenate(parts, axis=1))
    o_ref[0] = jnp.concatenate(res, axis=0).T.astype(o_ref.dtype)


def _nsa(proj, tail, kcvc, kcvct, bias_c, overlap_t, dmax):
    B, S, _ = proj.shape
    CQ = C_CHUNK
    assert CQ % NSA_KT == 0 and WIN % NSA_KT == 0 and CQ % GROUP_W == 0
    n_chunks = S // CQ
    NC = kcvc.shape[2]
    NS = S // SLC_BLOCK
    n_tiles = bias_c.shape[0]
    G = C_GROUP * (CQ // GROUP_W)
    kern = functools.partial(_nsa_kernel, dmax=dmax, n_sel=min(SLC_TOPN, NS))
    return pl.pallas_call(
        kern,
        grid=(B, C_KV_HEADS, n_chunks),
        in_specs=[pl.BlockSpec((1, CQ, 2 * LANES), lambda b, n, c: (b, c, BLK_CQ // 2 + n)),
                  pl.BlockSpec((1, 1, NC, LANES), lambda b, n, c: (b, n, 0, 0)),
                  pl.BlockSpec((1, 1, LANES, NC), lambda b, n, c: (b, n, 0, 0)),
                  pl.BlockSpec((1, S, LANES), lambda b, n, c: (b, 0, BLK_CKV + 3 * n + 1)),
                  pl.BlockSpec((1, S, LANES), lambda b, n, c: (b, 0, BLK_CKV + 3 * n + 2)),
                  pl.BlockSpec((1, CQ, LANES), lambda b, n, c: (b, c, BLK_GATE)),
                  pl.BlockSpec((n_tiles, 1, LANES, C_GROUP * LANES), lambda b, n, c: (0, n, 0, 0)),
                  pl.BlockSpec((NS, NC), lambda b, n, c: (0, 0))],
        out_specs=pl.BlockSpec((1, CQ, 2 * LANES), lambda b, n, c: (b, c, n)),
        out_shape=jax.ShapeDtypeStruct((B, S, C_HEADS * HEAD_DIM), BF16),
        scratch_shapes=[pltpu.VMEM((S // NSA_KT, LANES, NSA_KT), BF16), pltpu.VMEM((S // NSA_KT, LANES, NSA_KT), BF16),
                        pltpu.VMEM((G, HEAD_DIM, GROUP_W), F32), pltpu.VMEM((NS, CQ), F32),
                        pltpu.VMEM((NS, CQ), F32), pltpu.VMEM((LANES, CQ), F32),
                        pltpu.VMEM((1, G * GROUP_W), F32), pltpu.VMEM((1, G * GROUP_W), F32)]
        + [pltpu.VMEM((ACC_ROWS, GROUP_W), F32)] * (2 * G),
        compiler_params=_cparams("arbitrary", "arbitrary", "arbitrary"),
        name="nsa",
    )(proj, kcvc, kcvct, tail, tail, tail, bias_c, overlap_t)


def _head_scale():
    scale = np.ones((HEAD_WIDTH,), np.float32)
    for g in range(len(A_CONFIGS)):
        scale[g * A_GROUP_WIDTH:g * A_GROUP_WIDTH + A_HEADS_PER_GROUP * HEAD_DIM] = Q_SCALE
    scale[A_WIDTH:A_WIDTH + B_HEADS * HEAD_DIM] = Q_SCALE * LOG2E
    scale[A_WIDTH + B_WIDTH:] = Q_SCALE * LOG2E
    return scale


def _split_w_in(w_in):
    depth, D, _ = w_in.shape
    scale = np.concatenate([_head_scale(), np.ones((IN_WIDTH - HEAD_WIDTH,), np.float32)])
    w = (w_in * scale).astype(BF16)
    kv = w[:, :, HEAD_WIDTH:HEAD_WIDTH + C_KV_WIDTH].reshape(depth, D, 3, 2, C_KV_HEADS, HEAD_DIM)
    kv = kv.transpose(0, 1, 4, 2, 3, 5).reshape(depth, D, C_KV_WIDTH)
    pad = jnp.zeros((depth, D, TAIL_WIDTH - C_KV_WIDTH - C_GATE_WIDTH), BF16)
    tail = jnp.concatenate([kv, w[:, :, HEAD_WIDTH + C_KV_WIDTH:], pad], axis=2)
    return w, tail


def _compress_weights(cmp_w1, cmp_w2, cmp_pe):
    depth = cmp_w1.shape[0]
    half = CMP_LEN // 2
    w1 = cmp_w1.reshape(depth, 2, 2, half, HEAD_DIM, CMP_HIDDEN)
    w1_blk = jnp.zeros((depth, 2, half, 2, HEAD_DIM, 2, CMP_HIDDEN), F32)
    w1_blk = w1_blk.at[:, :, :, 0, :, 0, :].set(w1[:, 0])
    w1_blk = w1_blk.at[:, :, :, 1, :, 1, :].set(w1[:, 1])
    w1_blk = w1_blk.reshape(depth, 2, half * 2 * HEAD_DIM, 2 * CMP_HIDDEN).astype(BF16)
    w2_blk = jnp.zeros((depth, 2, CMP_HIDDEN, 2, HEAD_DIM), F32)
    w2_blk = w2_blk.at[:, 0, :, 0, :].set(cmp_w2[:, 0])
    w2_blk = w2_blk.at[:, 1, :, 1, :].set(cmp_w2[:, 1])
    w2_blk = w2_blk.reshape(depth, 2 * CMP_HIDDEN, 2 * HEAD_DIM).astype(BF16)
    pe = cmp_pe.reshape(depth, 2, 2, half, HEAD_DIM)
    pe_rows = pe.transpose(0, 2, 3, 1, 4).reshape(depth, 2, half * 2 * HEAD_DIM)
    return w1_blk, w2_blk, pe_rows


def kernel(x, rel_table, w_in, w_out, cmp_w1, cmp_w2, cmp_pe, norm_attn, norm_mlp,
           w_up, conv_w, conv_b, w_down, norm_final):
    B, S, D = x.shape
    depth = w_in.shape[0]
    T = B * S

    a = jnp.arange(A_BLOCK)[:, None]
    bk = jnp.arange(2 * A_BLOCK)[None, :]
    rel = a + A_BLOCK - bk
    idx_a = jnp.stack([_t5_bucket(rel * d) for _, d in A_CONFIGS]).astype(jnp.int32)
    bias_a = _bias_tiles(rel_table[:A_HEADS], idx_a)
    dmax_b = _n_far_tiles(S // MOBA_BLOCK, MOBA_BLOCK)
    bias_b = _bias_tiles(rel_table[A_HEADS:A_HEADS + B_HEADS] * LOG2E,
                         _toeplitz_bucket_tiles_t(dmax_b + 1, MOBA_BLOCK), 2)
    dmax_c = _n_far_tiles(S // LANES, LANES)
    bias_c = _bias_tiles(rel_table[A_HEADS + B_HEADS:] * LOG2E,
                         _toeplitz_bucket_tiles_t(dmax_c + 1, LANES), C_GROUP)

    n16 = S // CMP_STRIDE
    ci = jnp.arange(n16)[None, :] * CMP_STRIDE
    sj = jnp.arange(S // SLC_BLOCK)[:, None] * SLC_BLOCK
    overlap_t = ((ci < sj + SLC_BLOCK) & (ci + CMP_LEN > sj)).astype(BF16)

    w_head, w_tail = _split_w_in(w_in)
    w_out_b = w_out.astype(BF16)
    w_up_b = w_up.astype(BF16)
    w_down_b = w_down.astype(BF16)
    gains_attn = norm_attn.reshape(depth, 1, D)
    gains_mlp = norm_mlp.reshape(depth, 1, D)
    conv_b3 = conv_b.reshape(depth, 1, -1)
    w1_blk, w2_blk, pe_rows = _compress_weights(cmp_w1, cmp_w2, cmp_pe)

    x2 = x.reshape(T, D)
    h = _norm(x2, gains_attn, 0)
    for l in range(depth):
        proj = _inproj(h, w_head, l, HEAD_WIDTH, tn=512).reshape(B, S, HEAD_WIDTH)
        tail = _inproj(h, w_tail, l, TAIL_WIDTH, tn=TAIL_WIDTH).reshape(B, S, TAIL_WIDTH)
        os_, lses = [], []
        for g, (_, d) in enumerate(A_CONFIGS):
            o, lse = _dilated_group(proj, bias_a, g, d)
            os_.append(o)
            lses.append(lse)
        o_b = _moba(proj, bias_b, dmax_b).reshape(T, B_HEADS * HEAD_DIM)
        cmp_cols = tail[:, :, :BLK_GATE * LANES].reshape(B, S, C_KV_HEADS, 3, LANES)[:, :, :, 0]
        sub = cmp_cols.transpose(0, 2, 1, 3).reshape(B, C_KV_HEADS, n16, CMP_STRIDE * LANES)
        kcvc, kcvct = _compress(sub, pe_rows, w1_blk, w2_blk, l)
        o_c = _nsa(proj, tail, kcvc, kcvct, bias_c, overlap_t, dmax_c).reshape(T, C_HEADS * HEAD_DIM)
        x2, h = _outproj(x2, os_, lses, o_b, o_c, w_out_b, gains_mlp, l)
        last = l == depth - 1
        next_gain = norm_final.reshape(1, D) if last else norm_attn[l + 1].reshape(1, D)
        x2, h = _ffn(x2, h, w_up_b, conv_w, conv_b3, w_down_b, l, next_gain, F32 if last else BF16, S)
    return h.reshape(B, S, D)
```

```python
import functools
import math

import numpy as np
import jax
import jax.numpy as jnp
from jax import lax
from jax.experimental import pallas as pl
from jax.experimental.pallas import tpu as pltpu

F32 = jnp.float32
BF16 = jnp.bfloat16

HEAD_DIM = 64
LANES = 128
A_CONFIGS = ((128, 1), (512, 4), (2048, 16))
A_HEADS_PER_GROUP = 4
A_HEADS = 12
A_BLOCK = 128
A_BLOCKS_PER_STEP = 4
B_HEADS = 8
MOBA_BLOCK = 256
MOBA_TOPK = 3
MOBA_PAIRS = 2
MOBA_UNROLL = 4
C_KV_HEADS = 3
C_GROUP = 4
C_HEADS = 12
CMP_STRIDE = 16
CMP_LEN = 32
CMP_HIDDEN = 128
SLC_BLOCK = 64
SLC_TOPN = 16
N_FORCED = 3
IMP_SPLITS = 3
WIN = 512
INVALID_SCORE = -1.0
FORCED_SCORE = -2.0
N_BUCKETS = 32
T5_MAX_DIST = 2048
EPS = 1e-6
NEG = -1e30
M_FLOOR = -1e20
Q_SCALE = HEAD_DIM ** -0.5
LOG2E = math.log2(math.e)
C_CHUNK = 512
ACC_ROWS = 128
GROUP_W = 128
NSA_KT = 256
NSA_UNROLL = 4

A_WIDTH = 3 * A_HEADS * HEAD_DIM
A_GROUP_WIDTH = 3 * A_HEADS_PER_GROUP * HEAD_DIM
B_WIDTH = 3 * B_HEADS * HEAD_DIM
C_Q_WIDTH = C_HEADS * HEAD_DIM
C_KV_WIDTH = 6 * C_KV_HEADS * HEAD_DIM
C_GATE_WIDTH = 3 * C_HEADS
IN_WIDTH = A_WIDTH + B_WIDTH + C_Q_WIDTH + C_KV_WIDTH + C_GATE_WIDTH
BLK_B = A_WIDTH // LANES
BLK_CQ = BLK_B + B_WIDTH // LANES
HEAD_WIDTH = A_WIDTH + B_WIDTH + C_Q_WIDTH
BLK_CKV = 0
BLK_GATE = 3 * C_KV_HEADS
TAIL_WIDTH = (BLK_GATE + 1) * LANES

VMEM_LIMIT = 56 * 1024 * 1024


def _cparams(*sem):
    return pltpu.CompilerParams(dimension_semantics=sem, vmem_limit_bytes=VMEM_LIMIT)


def _dot_nt(a, b):
    return lax.dot_general(a, b, (((1,), (1,)), ((), ())), preferred_element_type=F32)


def _dot(a, b, **kw):
    return jnp.dot(a, b, preferred_element_type=F32, **kw)


def _transpose_bf16(x):
    return x.astype(F32).T.astype(BF16)


def _iota(shape, dim):
    return lax.broadcasted_iota(jnp.int32, shape, dim)


def _t5_bucket(dist):
    n = jnp.maximum(dist, 0)
    max_exact = N_BUCKETS // 2
    nf = jnp.maximum(n, 1).astype(F32)
    large = max_exact + (jnp.log(nf / max_exact) / math.log(T5_MAX_DIST / max_exact)
                         * (N_BUCKETS - max_exact)).astype(jnp.int32)
    large = jnp.minimum(large, N_BUCKETS - 1)
    return jnp.where(n < max_exact, n, large)


def _bias_lookup_kernel(tbl_ref, idx_ref, o_ref, *, hpr):
    idx = idx_ref[0]
    C = idx.shape[1]
    for h in range(tbl_ref.shape[0]):
        acc = jnp.zeros(idx.shape, F32)
        for b in range(N_BUCKETS):
            acc = jnp.where(idx == b, tbl_ref[h, b], acc)
        o_ref[0, h // hpr, :, (h % hpr) * C:(h % hpr + 1) * C] = acc


def _bias_tiles(tbl, idx, heads_per_row=1):
    n, R, C = idx.shape
    H = tbl.shape[0]
    hpr = heads_per_row
    return pl.pallas_call(
        functools.partial(_bias_lookup_kernel, hpr=hpr),
        grid=(n,),
        in_specs=[pl.BlockSpec(memory_space=pltpu.SMEM),
                  pl.BlockSpec((1, R, C), lambda t: (t, 0, 0))],
        out_specs=pl.BlockSpec((1, H // hpr, R, hpr * C), lambda t: (t, 0, 0, 0)),
        out_shape=jax.ShapeDtypeStruct((n, H // hpr, R, hpr * C), F32),
        compiler_params=_cparams("arbitrary"),
        name="bias_tiles",
    )(tbl, idx)


def _toeplitz_bucket_tiles_t(n_tiles, size):
    d = jnp.arange(n_tiles)[:, None, None] * size
    key = jnp.arange(size)[None, :, None]
    qry = jnp.arange(size)[None, None, :]
    return _t5_bucket(d + qry - key).astype(jnp.int32)


def _far_delta(size):
    return -(-(T5_MAX_DIST + size - 1) // size)


def _n_far_tiles(n_chunks, size):
    return min(_far_delta(size), n_chunks - 1)


def _norm_rows(x, g):
    ms = jnp.mean(x * x, axis=-1, keepdims=True)
    return x * lax.rsqrt(ms + EPS) * g


def _norm_kernel(x_ref, g_ref, o_ref):
    o_ref[...] = _norm_rows(x_ref[...], g_ref[...]).astype(o_ref.dtype)


def _norm(x2, gains, l, tm=1024):
    T, D = x2.shape
    return pl.pallas_call(
        _norm_kernel,
        grid=(T // tm,),
        in_specs=[pl.BlockSpec((tm, D), lambda i: (i, 0)),
                  pl.BlockSpec((None, 1, D), lambda i: (l, 0, 0))],
        out_specs=pl.BlockSpec((tm, D), lambda i: (i, 0)),
        out_shape=jax.ShapeDtypeStruct((T, D), BF16),
        compiler_params=_cparams("arbitrary"),
        name="first_norm",
    )(x2, gains)


def _inproj_kernel(h_ref, w_ref, o_ref):
    o_ref[...] = _dot(h_ref[...], w_ref[...]).astype(o_ref.dtype)


def _inproj(h, w_all, l, N, tn, tm=1024):
    T, D = h.shape
    return pl.pallas_call(
        _inproj_kernel,
        grid=(T // tm, N // tn),
        in_specs=[pl.BlockSpec((tm, D), lambda i, j: (i, 0)),
                  pl.BlockSpec((None, D, tn), lambda i, j: (l, 0, j))],
        out_specs=pl.BlockSpec((tm, tn), lambda i, j: (i, j)),
        out_shape=jax.ShapeDtypeStruct((T, N), BF16),
        compiler_params=_cparams("arbitrary", "arbitrary"),
        name="inproj",
    )(h, w_all)


def _outproj_kernel(x_ref, o0_ref, o1_ref, o2_ref, l0_ref, l1_ref, l2_ref, b_ref, c_ref, w_ref, g_ref, o_ref, h_ref):
    ls = [l0_ref[...], l1_ref[...], l2_ref[...]]
    m = jnp.maximum(jnp.maximum(ls[0], ls[1]), ls[2])
    es = [jnp.exp(l - m) for l in ls]
    inv = 1.0 / (es[0] + es[1] + es[2])
    gw = o0_ref.shape[1]
    acc = jnp.zeros(o_ref.shape, F32)
    for g, og_ref in enumerate((o0_ref, o1_ref, o2_ref)):
        acc += _dot((og_ref[...] * (es[g] * inv)).astype(BF16), w_ref[pl.ds(g * gw, gw), :])
    ra, rb = 3 * gw, b_ref.shape[1]
    acc += _dot(b_ref[...], w_ref[pl.ds(ra, rb), :])
    acc += _dot(c_ref[...], w_ref[pl.ds(ra + rb, c_ref.shape[1]), :])
    x_new = x_ref[...] + acc
    o_ref[...] = x_new
    h_ref[...] = _norm_rows(x_new, g_ref[...]).astype(h_ref.dtype)


def _outproj(x2, os_, lses, ob, oc, w_all, gains, l, tm=512):
    T, D = x2.shape
    row = lambda i: (i, 0)
    gspec = pl.BlockSpec((tm, os_[0].shape[1]), row)
    return pl.pallas_call(
        _outproj_kernel,
        grid=(T // tm,),
        in_specs=[pl.BlockSpec((tm, D), row)] + [gspec] * 6 + [
                  pl.BlockSpec((tm, ob.shape[1]), row),
                  pl.BlockSpec((tm, oc.shape[1]), row),
                  pl.BlockSpec((None, w_all.shape[1], D), lambda i: (l, 0, 0)),
                  pl.BlockSpec((None, 1, D), lambda i: (l, 0, 0))],
        out_specs=[pl.BlockSpec((tm, D), row), pl.BlockSpec((tm, D), row)],
        out_shape=[jax.ShapeDtypeStruct((T, D), F32), jax.ShapeDtypeStruct((T, D), BF16)],
        compiler_params=_cparams("arbitrary"),
        name="outproj",
    )(x2, *os_, *lses, ob, oc, w_all, gains)


FFN_PARTS = 2
FFN_HALO = 16


def _ffn_kernel(x_ref, h_ref, halo_ref, wa_ref, wg_ref, cwa_ref, cwg_ref, cba_ref, cbg_ref, wd_ref, g_ref,
                o_ref, n_ref, h_scr, *u_scrs, tm, blocks_per_seq):
    i = pl.program_id(0)
    f = pl.program_id(1)

    @pl.when(f == 0)
    def _():
        o_ref[...] = x_ref[...]
        keep = jnp.where(i % blocks_per_seq == 0, 0.0, 1.0).astype(BF16)
        h_scr[pl.ds(0, FFN_HALO), :] = halo_ref[0] * keep
        h_scr[pl.ds(FFN_HALO, tm), :] = h_ref[...]

    h = h_scr[...]
    n_parts = len(u_scrs) // 2
    part = wa_ref.shape[1] // n_parts

    def up(pt):
        cols = pl.ds(pt * part, part)
        u_scrs[2 * pt][...] = _dot(h, wa_ref[:, cols])
        u_scrs[2 * pt + 1][...] = _dot(h, wg_ref[:, cols])

    def conv(u_scr, pt, cw_ref, cb_ref):
        cols = pl.ds(pt * part, part)
        acc = cb_ref[:, cols]
        for j in range(3):
            acc = acc + cw_ref[pl.ds(j, 1), cols] * u_scr[pl.ds(FFN_HALO - j, tm), :]
        return acc

    up(0)
    for pt in range(n_parts):
        if pt + 1 < n_parts:
            up(pt + 1)
        ya = conv(u_scrs[2 * pt], pt, cwa_ref, cba_ref)
        yg = conv(u_scrs[2 * pt + 1], pt, cwg_ref, cbg_ref)
        act = (yg * jax.nn.sigmoid(yg) * ya).astype(BF16)
        o_ref[...] += _dot(act, wd_ref[pl.ds(pt * part, part), :])

    @pl.when(f == pl.num_programs(1) - 1)
    def _():
        n_ref[...] = _norm_rows(o_ref[...], g_ref[...]).astype(n_ref.dtype)


def _ffn(x2, h, w_up, conv_w, conv_b, w_down, l, next_gain, next_dtype, seq, tm=512, tf=512):
    T, D = x2.shape
    Fh = w_down.shape[1]
    nf = Fh // tf
    halo_view = h.reshape(T // FFN_HALO, FFN_HALO, D)
    rb = tm // FFN_HALO
    kern = functools.partial(_ffn_kernel, tm=tm, blocks_per_seq=seq // tm)
    row = lambda i, f: (i, 0)
    return pl.pallas_call(
        kern,
        grid=(T // tm, nf),
        in_specs=[pl.BlockSpec((tm, D), row),
                  pl.BlockSpec((tm, D), row),
                  pl.BlockSpec((1, FFN_HALO, D), lambda i, f: (jnp.maximum(i * rb - 1, 0), 0, 0)),
                  pl.BlockSpec((None, D, tf), lambda i, f: (l, 0, f)),
                  pl.BlockSpec((None, D, tf), lambda i, f: (l, 0, f + nf)),
                  pl.BlockSpec((None, 3, tf), lambda i, f: (l, 0, f)),
                  pl.BlockSpec((None, 3, tf), lambda i, f: (l, 0, f + nf)),
                  pl.BlockSpec((None, 1, tf), lambda i, f: (l, 0, f)),
                  pl.BlockSpec((None, 1, tf), lambda i, f: (l, 0, f + nf)),
                  pl.BlockSpec((None, tf, D), lambda i, f: (l, f, 0)),
                  pl.BlockSpec((1, D), lambda i, f: (0, 0))],
        out_specs=[pl.BlockSpec((tm, D), row), pl.BlockSpec((tm, D), row)],
        out_shape=[jax.ShapeDtypeStruct((T, D), F32), jax.ShapeDtypeStruct((T, D), next_dtype)],
        scratch_shapes=[pltpu.VMEM((tm + FFN_HALO, D), BF16)]
        + [pltpu.VMEM((tm + FFN_HALO, tf // FFN_PARTS), F32)] * (2 * FFN_PARTS),
        compiler_params=_cparams("arbitrary", "arbitrary"),
        name="convffn",
    )(x2, h, halo_view, w_up, w_up, conv_w, conv_w, conv_b, conv_b, w_down, next_gain)


def _stack_pair(qp):
    lane = _iota(qp.shape, 1)
    zero = jnp.zeros_like(qp)
    return jnp.concatenate([jnp.where(lane < HEAD_DIM, qp, zero),
                            jnp.where(lane >= HEAD_DIM, qp, zero)], axis=0)


def _unstack_pair(o):
    R = o.shape[0] // 2
    lane = _iota((R, LANES), 1)
    return jnp.where(lane < HEAD_DIM, o[:R], o[R:])


def _dilated_kernel(q_ref, kc_ref, kp_ref, vc_ref, vp_ref, bias_ref, o_ref, lse_ref):
    i = pl.program_id(2)
    R = 2 * A_BLOCK
    n_blocks = q_ref.shape[1] // A_BLOCK
    a = _iota((R, 2 * A_BLOCK), 0) % A_BLOCK
    bk = _iota((R, 2 * A_BLOCK), 1)
    rel = a + A_BLOCK - bk
    band = (rel >= 0) & (rel <= A_BLOCK)
    first = band & ((bk >= A_BLOCK) | (i > 0))
    n_pairs = q_ref.shape[2] // LANES
    units = [(j, p) for j in range(n_blocks) for p in range(n_pairs)]

    def rows(j):
        return slice(j * A_BLOCK, (j + 1) * A_BLOCK)

    def with_prev(cur_ref, prev_ref, j, cols):
        prev = prev_ref[0, :, cols] if j == 0 else cur_ref[0, rows(j - 1), cols]
        return jnp.concatenate([prev, cur_ref[0, rows(j), cols]], axis=0)

    def scores(j, p):
        cols = slice(p * LANES, (p + 1) * LANES)
        qs = _stack_pair(q_ref[0, rows(j), cols])
        s = _dot_nt(qs, with_prev(kc_ref, kp_ref, j, cols)) + bias_ref[0, 2 * p:2 * p + 2].reshape(R, 2 * A_BLOCK)
        return jnp.where(first if j == 0 else band, s, NEG)

    s_all = [scores(j, p) for j, p in units]
    for (j, p), s in zip(units, s_all):
        cols = slice(p * LANES, (p + 1) * LANES)
        mask = first if j == 0 else band
        m = jnp.max(s, axis=1, keepdims=True)
        e = jnp.where(mask, jnp.exp(s - m), 0.0)
        den = jnp.sum(e, axis=1, keepdims=True)
        o = _dot(e.astype(BF16), with_prev(vc_ref, vp_ref, j, cols)) / den
        lse = m + jnp.log(den)
        o_ref[0, rows(j), cols] = _unstack_pair(o)
        lse_ref[0, rows(j), cols] = _unstack_pair(jnp.broadcast_to(lse, (R, LANES)))


def _dilated_group(proj, bias_a, g, dilation):
    B, S, _ = proj.shape
    L = S // dilation
    nq = L // A_BLOCK
    per_step = A_BLOCKS_PER_STEP if nq % A_BLOCKS_PER_STEP == 0 else 1
    W = A_HEADS_PER_GROUP * HEAD_DIM
    if dilation == 1:
        view, base = proj, g * A_GROUP_WIDTH // W
    else:
        cols = proj[:, :, g * A_GROUP_WIDTH:(g + 1) * A_GROUP_WIDTH]
        view, base = cols.reshape(B, L, dilation * A_GROUP_WIDTH), 0

    def cur(part):
        return pl.BlockSpec((1, per_step * A_BLOCK, W), lambda b, r, i: (b, i, base + r * 3 + part))

    def prev(part):
        return pl.BlockSpec((1, A_BLOCK, W),
                            lambda b, r, i: (b, jnp.maximum(i * per_step - 1, 0), base + r * 3 + part))

    out_spec = pl.BlockSpec((1, per_step * A_BLOCK, W), lambda b, r, i: (b, i, r))
    o, lse = pl.pallas_call(
        _dilated_kernel,
        grid=(B, dilation, nq // per_step),
        in_specs=[cur(0), cur(1), prev(1), cur(2), prev(2),
                  pl.BlockSpec((1, A_HEADS_PER_GROUP, A_BLOCK, 2 * A_BLOCK), lambda b, r, i: (g, g, 0, 0))],
        out_specs=[out_spec, out_spec],
        out_shape=[jax.ShapeDtypeStruct((B, L, dilation * W), F32)] * 2,
        compiler_params=_cparams("arbitrary", "arbitrary", "arbitrary"),
        name=f"dilated_g{g}",
    )(view, view, view, view, view, bias_a)
    return o.reshape(B * S, W), lse.reshape(B * S, W)


def _flash_init(m_scr, acc_scrs):
    m_scr[...] = jnp.full(m_scr.shape, M_FLOOR, F32)
    for acc in acc_scrs:
        acc[...] = jnp.zeros(acc.shape, F32)


def _with_ones(vt, row):
    return jnp.where(_iota(vt.shape, 0) == row, jnp.ones_like(vt), vt)


def _flash_tiles(tiles, q_groups, m_scr, acc_scrs):
    G = len(q_groups)
    W = q_groups[0].shape[0]
    m = [m_scr[:, g * W:(g + 1) * W] for g in range(G)]
    units = [(t, g) for t in range(len(tiles)) for g in range(G)]

    def scores(t, g):
        k_of, _, elem_of, _ = tiles[t]
        s = _dot_nt(k_of(g), q_groups[g])
        for term in elem_of(g):
            s = s + term
        return s

    def apply(g, alpha, pv):
        acc_scrs[g][...] = alpha * acc_scrs[g][...] + pv

    s_next = scores(*units[0])
    pending = None
    for n, (t, g) in enumerate(units):
        s = s_next
        if n + 1 < len(units):
            s_next = scores(*units[n + 1])
        rows = tiles[t][3](g)
        seg = s.shape[0] // len(rows)
        parts = [s[i * seg:(i + 1) * seg] for i in range(len(rows))]
        m_new = m[g]
        for part, row in zip(parts, rows):
            m_new = jnp.maximum(m_new, jnp.max(part, axis=0, keepdims=True) + row)
        alpha = jnp.exp2(m[g] - m_new)
        p = jnp.concatenate([jnp.exp2(part - (m_new - row)) for part, row in zip(parts, rows)], axis=0)
        m[g] = m_new
        pv = _dot(tiles[t][1](g), p.astype(BF16))
        if pending is not None:
            apply(*pending)
        pending = (g, alpha, pv)
    apply(*pending)
    m_scr[...] = jnp.concatenate(m, axis=1)


PICKED = -3e38
NO_INDEX = 1e9


def _top_picks(score, n_pick):
    idx = _iota(score.shape, 0).astype(F32)

    def pick(_, g):
        mx = jnp.max(g, axis=0, keepdims=True)
        first = jnp.min(jnp.where(g == mx, idx, NO_INDEX), axis=0, keepdims=True)
        return jnp.where(idx == first, PICKED, g)

    return lax.fori_loop(0, n_pick, pick, score) == PICKED


def _moba_kernel(q_ref, k_ref, v_ref, bias_ref, o_ref, km_scr, vt_scr, sel_scr, m_scr, *acc_scrs,
                 dmax, n_blocks):
    c = pl.program_id(2)
    BLK = MOBA_BLOCK
    W = GROUP_W
    per_head = BLK // W
    per_pair = 2 * per_head
    G = MOBA_PAIRS * per_pair

    @pl.when(c == 0)
    def _():
        def body(j, _):
            rows = pl.ds(pl.multiple_of(j * BLK, BLK), BLK)
            km_scr[pl.ds(j, 1), :] = jnp.mean(k_ref[0, rows, :].astype(F32), axis=0, keepdims=True)
            vt_scr[j] = _transpose_bf16(v_ref[0, rows, :])
            return 0
        lax.fori_loop(0, n_blocks, body, 0)

    def lanes(pi):
        return slice(pi * LANES, (pi + 1) * LANES)

    q = q_ref[0]
    km = km_scr[...].astype(BF16)
    q_groups, gates = [], []
    for pi in range(MOBA_PAIRS):
        qs = _stack_pair(q[:, lanes(pi)])
        gates.append(_dot_nt(km[:, lanes(pi)], qs))
        q_groups.extend(qs[g * W:(g + 1) * W] for g in range(per_pair))
    gate = jnp.concatenate(gates, axis=1)
    blk = _iota(gate.shape, 0)
    gate = jnp.where(blk < c, gate, NEG)
    sel = _top_picks(gate, min(MOBA_TOPK, n_blocks))
    sel_scr[...] = jnp.where((blk < c) & sel, 0.0, NEG)

    _flash_init(m_scr, acc_scrs)

    def group(g):
        pi, gp = divmod(g, per_pair)
        return pi, gp // per_head, gp * W

    def keys_of(kj):
        return lambda g: kj[:, lanes(group(g)[0])]

    def values_of(vt):
        opts = [[_with_ones(vt[lanes(pi)], (1 - h) * HEAD_DIM) for h in range(2)] for pi in range(MOBA_PAIRS)]
        return lambda g: opts[group(g)[0]][group(g)[1]]

    def bias_of(tile, g, rows=slice(None)):
        pi, _, col = group(g)
        return bias_ref[tile, pi, rows, pl.ds(col, W)]

    def past_tile(j, far):
        jc = jnp.minimum(j, n_blocks - 1)
        kj = k_ref[0, pl.ds(pl.multiple_of(jc * BLK, BLK), BLK), :]
        tile = jnp.clip(c - j, 0, dmax)
        selrow = sel_scr[pl.ds(jc, 1), :]

        def elem(g):
            return () if far else (bias_of(tile, g),)

        def rows(g):
            row = selrow[:, g * W:(g + 1) * W]
            return (row + bias_of(dmax, g, pl.ds(0, 1)),) if far else (row,)

        return keys_of(kj), values_of(vt_scr[jc]), elem, rows

    def body(far):
        def run(it, _):
            _flash_tiles([past_tile(MOBA_UNROLL * it + u, far) for u in range(MOBA_UNROLL)],
                         q_groups, m_scr, acc_scrs)
            return 0
        return run

    n_far_iters = jnp.maximum(c - _far_delta(BLK) + 1, 0) // MOBA_UNROLL
    lax.fori_loop(0, n_far_iters, body(True), 0)
    lax.fori_loop(n_far_iters, (c + MOBA_UNROLL - 1) // MOBA_UNROLL, body(False), 0)

    ko = k_ref[0, pl.ds(pl.multiple_of(c * BLK, BLK), BLK), :]

    def own_elem(g):
        qpos = (g * W + _iota((BLK, W), 1)) % BLK
        return bias_of(0, g), jnp.where(_iota((BLK, W), 0) <= qpos, 0.0, NEG)

    zero_row = jnp.zeros((1, W), F32)
    _flash_tiles([(keys_of(ko), values_of(vt_scr[c]), own_elem, lambda g: (zero_row,))], q_groups, m_scr, acc_scrs)
    pairs_out = []
    for pi in range(MOBA_PAIRS):
        heads_out = []
        for h in range(2):
            parts = []
            for jh in range(per_head):
                a = acc_scrs[pi * per_pair + h * per_head + jh][...]
                den = a[(1 - h) * HEAD_DIM:(1 - h) * HEAD_DIM + 1]
                parts.append(a[h * HEAD_DIM:(h + 1) * HEAD_DIM] / den)
            heads_out.append(jnp.concatenate(parts, axis=1))
        pairs_out.append(jnp.concatenate(heads_out, axis=0).T)
    o_ref[0] = jnp.concatenate(pairs_out, axis=1).astype(o_ref.dtype)


def _moba(proj, bias_b, dmax):
    B, S, _ = proj.shape
    NB = S // MOBA_BLOCK
    n_tiles = bias_b.shape[0]
    PW = MOBA_PAIRS * LANES
    qb, kb, vb = (BLK_B * LANES // PW, (BLK_B + 4) * LANES // PW, (BLK_B + 8) * LANES // PW)
    kern = functools.partial(_moba_kernel, dmax=dmax, n_blocks=NB)
    Q = MOBA_PAIRS * 2 * MOBA_BLOCK
    return pl.pallas_call(
        kern,
        grid=(B, B_HEADS // (2 * MOBA_PAIRS), NB),
        in_specs=[pl.BlockSpec((1, MOBA_BLOCK, PW), lambda b, p, c: (b, c, qb + p)),
                  pl.BlockSpec((1, S, PW), lambda b, p, c: (b, 0, kb + p)),
                  pl.BlockSpec((1, S, PW), lambda b, p, c: (b, 0, vb + p)),
                  pl.BlockSpec((n_tiles, MOBA_PAIRS, MOBA_BLOCK, 2 * MOBA_BLOCK), lambda b, p, c: (0, p, 0, 0))],
        out_specs=pl.BlockSpec((1, MOBA_BLOCK, PW), lambda b, p, c: (b, c, p)),
        out_shape=jax.ShapeDtypeStruct((B, S, B_HEADS * HEAD_DIM), BF16),
        scratch_shapes=[pltpu.VMEM((NB, PW), F32), pltpu.VMEM((NB, PW, MOBA_BLOCK), BF16),
                        pltpu.VMEM((NB, Q), F32), pltpu.VMEM((1, Q), F32)]
        + [pltpu.VMEM((ACC_ROWS, GROUP_W), F32)] * (Q // GROUP_W),
        compiler_params=_cparams("arbitrary", "arbitrary", "arbitrary"),
        name="moba",
    )(proj, proj, proj, bias_b)


def _gelu_tanh(x):
    return 0.5 * x * (1.0 + jnp.tanh(math.sqrt(2.0 / math.pi) * (x + 0.044715 * (x * x * x))))


def _compress_kernel(sub_ref, pe_ref, w1_ref, w2_ref, o_ref, ot_ref, v_scr):
    n16 = sub_ref.shape[2]
    sub = sub_ref[0, 0].astype(F32)
    top = (sub + pe_ref[0:1, :]).astype(BF16)
    bot = (sub + pe_ref[1:2, :]).astype(BF16)
    u = _dot(top, w1_ref[0])
    v_scr[pl.ds(0, n16), :] = _dot(bot, w1_ref[1])
    v_scr[pl.ds(n16, 8), :] = jnp.zeros((8, v_scr.shape[1]), F32)
    hidden = u + v_scr[pl.ds(1, n16), :]
    out = _dot(_gelu_tanh(hidden).astype(BF16), w2_ref[...])
    o_ref[0, 0] = out.astype(o_ref.dtype)
    ot_ref[0, 0] = out.T.astype(ot_ref.dtype)


def _compress(sub, pe_rows, w1_blk, w2_blk, l):
    B, KV, n16, W = sub.shape
    return pl.pallas_call(
        _compress_kernel,
        grid=(B, KV),
        in_specs=[pl.BlockSpec((1, 1, n16, W), lambda b, n: (b, n, 0, 0)),
                  pl.BlockSpec((None, 2, W), lambda b, n: (l, 0, 0)),
                  pl.BlockSpec((None, 2, W, 2 * CMP_HIDDEN), lambda b, n: (l, 0, 0, 0)),
                  pl.BlockSpec((None, 2 * CMP_HIDDEN, LANES), lambda b, n: (l, 0, 0))],
        out_specs=[pl.BlockSpec((1, 1, n16, LANES), lambda b, n: (b, n, 0, 0)),
                   pl.BlockSpec((1, 1, LANES, n16), lambda b, n: (b, n, 0, 0))],
        out_shape=[jax.ShapeDtypeStruct((B, KV, n16, LANES), BF16),
                   jax.ShapeDtypeStruct((B, KV, LANES, n16), BF16)],
        scratch_shapes=[pltpu.VMEM((n16 + 8, 2 * CMP_HIDDEN), F32)],
        compiler_params=_cparams("arbitrary", "arbitrary"),
        name="nsa_compress",
    )(sub, pe_rows, w1_blk, w2_blk)


def _nsa_kernel(q_ref, kc_ref, kct_ref, slc_ref, win_ref, gate_ref, bias_ref, ovt_ref,
                o_ref, slct_scr, wint_scr, ocmp_scr, imp_scr, sel_scr, g_scr, m_slc, m_win, *acc_scrs, dmax, n_sel):
    n = pl.program_id(1)
    c = pl.program_id(2)
    CQ = C_CHUNK
    W = GROUP_W
    nq = CQ // W
    G = C_GROUP * nq
    halves = NSA_KT // LANES
    per128 = CQ // LANES

    @pl.when(c == 0)
    def _():
        def body(t, _):
            rows = pl.ds(pl.multiple_of(t * NSA_KT, NSA_KT), NSA_KT)
            slct_scr[t] = _transpose_bf16(slc_ref[0, rows, :])
            wint_scr[t] = _transpose_bf16(win_ref[0, rows, :])
            return 0
        lax.fori_loop(0, slct_scr.shape[0], body, 0)

    lane = _iota((CQ, LANES), 1)
    q = q_ref[0].astype(F32)
    q_groups = []
    for pr in range(2):
        qp = q[:, pr * LANES:(pr + 1) * LANES]
        for head in (jnp.where(lane < HEAD_DIM, qp, 0.0), jnp.where(lane < HEAD_DIM, pltpu.roll(qp, HEAD_DIM, 1), 0.0)):
            hb = head.astype(BF16)
            q_groups.extend(hb[j * W:(j + 1) * W] for j in range(nq))

    def tok(j, shape):
        return c * CQ + j * W + _iota(shape, 1)

    NC = kc_ref.shape[2]
    NS = ovt_ref.shape[0]

    def cmp_branch(nk):
        kc = kc_ref[0, 0, pl.ds(0, nk), :]
        kct = kct_ref[0, 0, :, pl.ds(0, nk)]
        ovt = ovt_ref[:, pl.ds(0, nk)]
        key_end = _iota((nk, W), 0) * CMP_STRIDE + (CMP_LEN - 1)
        visible = [key_end <= tok(j, (nk, W)) for j in range(nq)]
        psum = [jnp.zeros((nk, W), F32) for _ in range(nq)]
        s_next = _dot_nt(kc, q_groups[0])
        for g in range(G):
            j = g % nq
            s = jnp.where(visible[j], s_next, NEG)
            if g + 1 < G:
                s_next = _dot_nt(kc, q_groups[g + 1])
            m = jnp.max(s, axis=0, keepdims=True)
            e = jnp.where(visible[j], jnp.exp2(s - m), 0.0)
            den = jnp.sum(e, axis=0, keepdims=True)
            p = e / jnp.maximum(den, 1e-30)
            ocmp_scr[g] = _dot(kct, p.astype(BF16))[HEAD_DIM:]
            psum[j] = psum[j] + p
        for j in range(nq):
            imp = jnp.zeros((NS, W), F32)
            rest = psum[j]
            for _ in range(IMP_SPLITS):
                piece = rest.astype(BF16)
                imp = imp + _dot(ovt, piece)
                rest = rest - piece.astype(F32)
            imp_scr[:, j * W:(j + 1) * W] = imp

    half_chunks = (NC // 2) // (CQ // CMP_STRIDE)

    @pl.when(c < half_chunks)
    def _():
        cmp_branch(NC // 2)

    @pl.when(c >= half_chunks)
    def _():
        cmp_branch(NC)

    imp = imp_scr[...]
    jj = _iota((NS, CQ), 0)
    own = (c * CQ + _iota((NS, CQ), 1)) // SLC_BLOCK
    forced = (jj == 0) | (jj == own) | (jj == own - 1)
    score = jnp.where(forced, FORCED_SCORE, jnp.where(jj <= own, imp, INVALID_SCORE))
    picked = _top_picks(score, max(n_sel - N_FORCED, 0))
    sel_scr[...] = jnp.where(forced | picked, 0.0, NEG)

    acc_slc = acc_scrs[:G]
    acc_win = acc_scrs[G:]
    key_row = _iota((NSA_KT, W), 0)
    zero_row = jnp.zeros((1, W), F32)

    def bias_tile(g, delta0, dynamic):
        gi, j = divmod(g, nq)
        cols = pl.ds(gi * LANES, LANES)
        idx = [delta0 + j - h for h in range(halves)]
        idx = [jnp.clip(d, 0, dmax) for d in idx] if dynamic else [max(d, 0) for d in idx]
        return jnp.concatenate([bias_ref[d, 0, :, cols] for d in idx], axis=0)

    _flash_init(m_slc, acc_slc)
    per_tile = NSA_KT // SLC_BLOCK
    n_kt = slct_scr.shape[0]

    def slc_tile(kt, kind):
        ktc = jnp.minimum(kt, n_kt - 1)
        k = slc_ref[0, pl.ds(pl.multiple_of(ktc * NSA_KT, NSA_KT), NSA_KT), :]
        pair = sel_scr[pl.ds(pl.multiple_of((ktc // 2) * 2 * per_tile, 2 * per_tile), 2 * per_tile), :]
        r = jnp.where(ktc % 2 == 1, pair[per_tile:], pair[:per_tile])
        delta0 = c * per128 - halves * kt
        causal = [jnp.where(kt * NSA_KT + key_row <= tok(j, (NSA_KT, W)), 0.0, NEG) for j in range(nq)]

        def elem(g):
            if kind == "far":
                return ()
            bias = bias_tile(g, delta0, True)
            return (bias, causal[g % nq]) if kind == "diag" else (bias,)

        def rows(g):
            gi, j = divmod(g, nq)
            out = []
            for i in range(per_tile):
                row = r[i:i + 1, j * W:(j + 1) * W]
                if kind == "far":
                    row = row + bias_ref[dmax, 0, pl.ds(0, 1), pl.ds(gi * LANES, LANES)]
                out.append(row)
            return out

        vt = _with_ones(slct_scr[ktc], 0)
        return lambda g: k, lambda g: vt, elem, rows

    def slc_iter(it, kind):
        _flash_tiles([slc_tile(NSA_UNROLL * it + u, kind) for u in range(NSA_UNROLL)], q_groups, m_slc, acc_slc)

    def slc_body(kind):
        def run(it, _):
            slc_iter(it, kind)
            return 0
        return run

    keys_per_iter = NSA_UNROLL * NSA_KT
    last_iter = (c * CQ) // keys_per_iter
    n_far_tiles = jnp.maximum(c * per128 - (halves - 1) - _far_delta(LANES) + halves, 0) // halves
    n_far_iters = n_far_tiles // NSA_UNROLL
    lax.fori_loop(0, n_far_iters, slc_body("far"), 0)
    lax.fori_loop(n_far_iters, last_iter, slc_body("near"), 0)
    slc_iter(last_iter, "diag")

    _flash_init(m_win, acc_win)
    back = WIN // NSA_KT

    def win_tile(w):
        kt = c * (CQ // NSA_KT) - back + w
        ktc = jnp.clip(kt, 0, n_kt - 1)
        k = win_ref[0, pl.ds(pl.multiple_of(ktc * NSA_KT, NSA_KT), NSA_KT), :]
        dist0 = (back - w) * NSA_KT + _iota((NSA_KT, W), 1) - key_row
        negs = []
        for j in range(nq):
            dist = dist0 + j * W
            negs.append(jnp.where((dist >= 0) & (dist < WIN) & (kt >= 0), 0.0, NEG))

        def elem(g):
            return bias_tile(g, (back - w) * halves, False), negs[g % nq]

        vt = _with_ones(wint_scr[ktc], 0)
        return lambda g: k, lambda g: vt, elem, lambda g: (zero_row,)

    _flash_tiles([win_tile(w) for w in range((WIN + CQ) // NSA_KT)], q_groups, m_win, acc_win)

    g_scr[...] = jax.nn.sigmoid(gate_ref[0].astype(F32)).T
    res = []
    for gi in range(C_GROUP):
        col0 = (n * C_GROUP + gi) * 3
        gates = [g_scr[pl.ds(col0 + br, 1), :] for br in range(3)]
        parts = []
        for j in range(nq):
            g = gi * nq + j
            toks = slice(j * W, (j + 1) * W)
            a_s = acc_slc[g][...]
            a_w = acc_win[g][...]
            parts.append(gates[0][:, toks] * ocmp_scr[g]
                         + gates[1][:, toks] * (a_s[HEAD_DIM:] / a_s[0:1])
                         + gates[2][:, toks] * (a_w[HEAD_DIM:] / a_w[0:1]))
        res.append(jnp.concatenate(parts, axis=1))
    o_ref[0] = jnp.concatenate(res, axis=0).T.astype(o_ref.dtype)


def _nsa(proj, tail, kcvc, kcvct, bias_c, overlap_t, dmax):
    B, S, _ = proj.shape
    CQ = C_CHUNK
    assert CQ % NSA_KT == 0 and WIN % NSA_KT == 0 and CQ % GROUP_W == 0
    n_chunks = S // CQ
    NC = kcvc.shape[2]
    NS = S // SLC_BLOCK
    n_tiles = bias_c.shape[0]
    G = C_GROUP * (CQ // GROUP_W)
    kern = functools.partial(_nsa_kernel, dmax=dmax, n_sel=min(SLC_TOPN, NS))
    return pl.pallas_call(
        kern,
        grid=(B, C_KV_HEADS, n_chunks),
        in_specs=[pl.BlockSpec((1, CQ, 2 * LANES), lambda b, n, c: (b, c, BLK_CQ // 2 + n)),
                  pl.BlockSpec((1, 1, NC, LANES), lambda b, n, c: (b, n, 0, 0)),
                  pl.BlockSpec((1, 1, LANES, NC), lambda b, n, c: (b, n, 0, 0)),
                  pl.BlockSpec((1, S, LANES), lambda b, n, c: (b, 0, BLK_CKV + 3 * n + 1)),
                  pl.BlockSpec((1, S, LANES), lambda b, n, c: (b, 0, BLK_CKV + 3 * n + 2)),
                  pl.BlockSpec((1, CQ, LANES), lambda b, n, c: (b, c, BLK_GATE)),
                  pl.BlockSpec((n_tiles, 1, LANES, C_GROUP * LANES), lambda b, n, c: (0, n, 0, 0)),
                  pl.BlockSpec((NS, NC), lambda b, n, c: (0, 0))],
        out_specs=pl.BlockSpec((1, CQ, 2 * LANES), lambda b, n, c: (b, c, n)),
        out_shape=jax.ShapeDtypeStruct((B, S, C_HEADS * HEAD_DIM), BF16),
        scratch_shapes=[pltpu.VMEM((S // NSA_KT, LANES, NSA_KT), BF16), pltpu.VMEM((S // NSA_KT, LANES, NSA_KT), BF16),
                        pltpu.VMEM((G, HEAD_DIM, GROUP_W), F32), pltpu.VMEM((NS, CQ), F32),
                        pltpu.VMEM((NS, CQ), F32), pltpu.VMEM((LANES, CQ), F32),
                        pltpu.VMEM((1, G * GROUP_W), F32), pltpu.VMEM((1, G * GROUP_W), F32)]
        + [pltpu.VMEM((ACC_ROWS, GROUP_W), F32)] * (2 * G),
        compiler_params=_cparams("arbitrary", "arbitrary", "arbitrary"),
        name="nsa",
    )(proj, kcvc, kcvct, tail, tail, tail, bias_c, overlap_t)


def _head_scale():
    scale = np.ones((HEAD_WIDTH,), np.float32)
    for g in range(len(A_CONFIGS)):
        scale[g * A_GROUP_WIDTH:g * A_GROUP_WIDTH + A_HEADS_PER_GROUP * HEAD_DIM] = Q_SCALE
    scale[A_WIDTH:A_WIDTH + B_HEADS * HEAD_DIM] = Q_SCALE * LOG2E
    scale[A_WIDTH + B_WIDTH:] = Q_SCALE * LOG2E
    return scale


def _split_w_in(w_in):
    depth, D, _ = w_in.shape
    scale = np.concatenate([_head_scale(), np.ones((IN_WIDTH - HEAD_WIDTH,), np.float32)])
    w = (w_in * scale).astype(BF16)
    kv = w[:, :, HEAD_WIDTH:HEAD_WIDTH + C_KV_WIDTH].reshape(depth, D, 3, 2, C_KV_HEADS, HEAD_DIM)
    kv = kv.transpose(0, 1, 4, 2, 3, 5).reshape(depth, D, C_KV_WIDTH)
    pad = jnp.zeros((depth, D, TAIL_WIDTH - C_KV_WIDTH - C_GATE_WIDTH), BF16)
    tail = jnp.concatenate([kv, w[:, :, HEAD_WIDTH + C_KV_WIDTH:], pad], axis=2)
    return w, tail


def _compress_weights(cmp_w1, cmp_w2, cmp_pe):
    depth = cmp_w1.shape[0]
    half = CMP_LEN // 2
    w1 = cmp_w1.reshape(depth, 2, 2, half, HEAD_DIM, CMP_HIDDEN)
    w1_blk = jnp.zeros((depth, 2, half, 2, HEAD_DIM, 2, CMP_HIDDEN), F32)
    w1_blk = w1_blk.at[:, :, :, 0, :, 0, :].set(w1[:, 0])
    w1_blk = w1_blk.at[:, :, :, 1, :, 1, :].set(w1[:, 1])
    w1_blk = w1_blk.reshape(depth, 2, half * 2 * HEAD_DIM, 2 * CMP_HIDDEN).astype(BF16)
    w2_blk = jnp.zeros((depth, 2, CMP_HIDDEN, 2, HEAD_DIM), F32)
    w2_blk = w2_blk.at[:, 0, :, 0, :].set(cmp_w2[:, 0])
    w2_blk = w2_blk.at[:, 1, :, 1, :].set(cmp_w2[:, 1])
    w2_blk = w2_blk.reshape(depth, 2 * CMP_HIDDEN, 2 * HEAD_DIM).astype(BF16)
    pe = cmp_pe.reshape(depth, 2, 2, half, HEAD_DIM)
    pe_rows = pe.transpose(0, 2, 3, 1, 4).reshape(depth, 2, half * 2 * HEAD_DIM)
    return w1_blk, w2_blk, pe_rows


def kernel(x, rel_table, w_in, w_out, cmp_w1, cmp_w2, cmp_pe, norm_attn, norm_mlp,
           w_up, conv_w, conv_b, w_down, norm_final):
    B, S, D = x.shape
    depth = w_in.shape[0]
    T = B * S

    a = jnp.arange(A_BLOCK)[:, None]
    bk = jnp.arange(2 * A_BLOCK)[None, :]
    rel = a + A_BLOCK - bk
    idx_a = jnp.stack([_t5_bucket(rel * d) for _, d in A_CONFIGS]).astype(jnp.int32)
    bias_a = _bias_tiles(rel_table[:A_HEADS], idx_a)
    dmax_b = _n_far_tiles(S // MOBA_BLOCK, MOBA_BLOCK)
    bias_b = _bias_tiles(rel_table[A_HEADS:A_HEADS + B_HEADS] * LOG2E,
                         _toeplitz_bucket_tiles_t(dmax_b + 1, MOBA_BLOCK), 2)
    dmax_c = _n_far_tiles(S // LANES, LANES)
    bias_c = _bias_tiles(rel_table[A_HEADS + B_HEADS:] * LOG2E,
                         _toeplitz_bucket_tiles_t(dmax_c + 1, LANES), C_GROUP)

    n16 = S // CMP_STRIDE
    ci = jnp.arange(n16)[None, :] * CMP_STRIDE
    sj = jnp.arange(S // SLC_BLOCK)[:, None] * SLC_BLOCK
    overlap_t = ((ci < sj + SLC_BLOCK) & (ci + CMP_LEN > sj)).astype(BF16)

    w_head, w_tail = _split_w_in(w_in)
    w_out_b = w_out.astype(BF16)
    w_up_b = w_up.astype(BF16)
    w_down_b = w_down.astype(BF16)
    gains_attn = norm_attn.reshape(depth, 1, D)
    gains_mlp = norm_mlp.reshape(depth, 1, D)
    conv_b3 = conv_b.reshape(depth, 1, -1)
    w1_blk, w2_blk, pe_rows = _compress_weights(cmp_w1, cmp_w2, cmp_pe)

    x2 = x.reshape(T, D)
    h = _norm(x2, gains_attn, 0)
    for l in range(depth):
        proj = _inproj(h, w_head, l, HEAD_WIDTH, tn=512).reshape(B, S, HEAD_WIDTH)
        tail = _inproj(h, w_tail, l, TAIL_WIDTH, tn=TAIL_WIDTH).reshape(B, S, TAIL_WIDTH)
        os_, lses = [], []
        for g, (_, d) in enumerate(A_CONFIGS):
            o, lse = _dilated_group(proj, bias_a, g, d)
            os_.append(o)
            lses.append(lse)
        o_b = _moba(proj, bias_b, dmax_b).reshape(T, B_HEADS * HEAD_DIM)
        cmp_cols = tail[:, :, :BLK_GATE * LANES].reshape(B, S, C_KV_HEADS, 3, LANES)[:, :, :, 0]
        sub = cmp_cols.transpose(0, 2, 1, 3).reshape(B, C_KV_HEADS, n16, CMP_STRIDE * LANES)
        kcvc, kcvct = _compress(sub, pe_rows, w1_blk, w2_blk, l)
        o_c = _nsa(proj, tail, kcvc, kcvct, bias_c, overlap_t, dmax_c).reshape(T, C_HEADS * HEAD_DIM)
        x2, h = _outproj(x2, os_, lses, o_b, o_c, w_out_b, gains_mlp, l)
        last = l == depth - 1
        next_gain = norm_final.reshape(1, D) if last else norm_attn[l + 1].reshape(1, D)
        x2, h = _ffn(x2, h, w_up_b, conv_w, conv_b3, w_down_b, l, next_gain, F32 if last else BF16, S)
    return h.reshape(B, S, D)
```

```python
import functools
import math

import numpy as np
import jax
import jax.numpy as jnp
from jax import lax
from jax.experimental import pallas as pl
from jax.experimental.pallas import tpu as pltpu

F32 = jnp.float32
BF16 = jnp.bfloat16

HEAD_DIM = 64
LANES = 128
A_CONFIGS = ((128, 1), (512, 4), (2048, 16))
A_HEADS_PER_GROUP = 4
A_HEADS = 12
A_BLOCK = 128
A_BLOCKS_PER_STEP = 4
B_HEADS = 8
MOBA_BLOCK = 256
MOBA_TOPK = 3
MOBA_CHUNK = 2
MOBA_PAIRS = 2
MOBA_UNROLL = 4
C_KV_HEADS = 3
C_GROUP = 4
C_HEADS = 12
CMP_STRIDE = 16
CMP_LEN = 32
CMP_HIDDEN = 128
SLC_BLOCK = 64
SLC_TOPN = 16
N_FORCED = 3
IMP_SPLITS = 3
WIN = 512
INVALID_SCORE = -1.0
FORCED_SCORE = -2.0
N_BUCKETS = 32
T5_MAX_DIST = 2048
EPS = 1e-6
NEG = -1e30
M_FLOOR = -1e20
Q_SCALE = HEAD_DIM ** -0.5
LOG2E = math.log2(math.e)
C_CHUNK = 512
ACC_ROWS = 128
GROUP_W = 128
NSA_KT = 256
NSA_UNROLL = 4

A_WIDTH = 3 * A_HEADS * HEAD_DIM
A_GROUP_WIDTH = 3 * A_HEADS_PER_GROUP * HEAD_DIM
B_WIDTH = 3 * B_HEADS * HEAD_DIM
C_Q_WIDTH = C_HEADS * HEAD_DIM
C_KV_WIDTH = 6 * C_KV_HEADS * HEAD_DIM
C_GATE_WIDTH = 3 * C_HEADS
IN_WIDTH = A_WIDTH + B_WIDTH + C_Q_WIDTH + C_KV_WIDTH + C_GATE_WIDTH
BLK_B = A_WIDTH // LANES
BLK_CQ = BLK_B + B_WIDTH // LANES
HEAD_WIDTH = A_WIDTH + B_WIDTH + C_Q_WIDTH
BLK_CKV = 0
BLK_GATE = 3 * C_KV_HEADS
TAIL_WIDTH = (BLK_GATE + 1) * LANES

VMEM_LIMIT = 56 * 1024 * 1024


def _cparams(*sem):
    return pltpu.CompilerParams(dimension_semantics=sem, vmem_limit_bytes=VMEM_LIMIT)


def _dot_nt(a, b):
    return lax.dot_general(a, b, (((1,), (1,)), ((), ())), preferred_element_type=F32)


def _dot(a, b, **kw):
    return jnp.dot(a, b, preferred_element_type=F32, **kw)


def _transpose_bf16(x):
    return x.astype(F32).T.astype(BF16)


def _iota(shape, dim):
    return lax.broadcasted_iota(jnp.int32, shape, dim)


def _t5_bucket(dist):
    n = jnp.maximum(dist, 0)
    max_exact = N_BUCKETS // 2
    nf = jnp.maximum(n, 1).astype(F32)
    large = max_exact + (jnp.log(nf / max_exact) / math.log(T5_MAX_DIST / max_exact)
                         * (N_BUCKETS - max_exact)).astype(jnp.int32)
    large = jnp.minimum(large, N_BUCKETS - 1)
    return jnp.where(n < max_exact, n, large)


def _bias_lookup_kernel(tbl_ref, idx_ref, o_ref, *, hpr):
    idx = idx_ref[0]
    C = idx.shape[1]
    for h in range(tbl_ref.shape[0]):
        acc = jnp.zeros(idx.shape, F32)
        for b in range(N_BUCKETS):
            acc = jnp.where(idx == b, tbl_ref[h, b], acc)
        o_ref[0, h // hpr, :, (h % hpr) * C:(h % hpr + 1) * C] = acc


def _bias_tiles(tbl, idx, heads_per_row=1):
    n, R, C = idx.shape
    H = tbl.shape[0]
    hpr = heads_per_row
    return pl.pallas_call(
        functools.partial(_bias_lookup_kernel, hpr=hpr),
        grid=(n,),
        in_specs=[pl.BlockSpec(memory_space=pltpu.SMEM),
                  pl.BlockSpec((1, R, C), lambda t: (t, 0, 0))],
        out_specs=pl.BlockSpec((1, H // hpr, R, hpr * C), lambda t: (t, 0, 0, 0)),
        out_shape=jax.ShapeDtypeStruct((n, H // hpr, R, hpr * C), F32),
        compiler_params=_cparams("arbitrary"),
        name="bias_tiles",
    )(tbl, idx)


def _toeplitz_bucket_tiles_t(n_tiles, size):
    d = jnp.arange(n_tiles)[:, None, None] * size
    key = jnp.arange(size)[None, :, None]
    qry = jnp.arange(size)[None, None, :]
    return _t5_bucket(d + qry - key).astype(jnp.int32)


def _far_delta(size):
    return -(-(T5_MAX_DIST + size - 1) // size)


def _n_far_tiles(n_chunks, size):
    return min(_far_delta(size), n_chunks - 1)


def _norm_rows(x, g):
    ms = jnp.mean(x * x, axis=-1, keepdims=True)
    return x * lax.rsqrt(ms + EPS) * g


def _norm_kernel(x_ref, g_ref, o_ref):
    o_ref[...] = _norm_rows(x_ref[...], g_ref[...]).astype(o_ref.dtype)


def _norm(x2, gains, l, tm=1024):
    T, D = x2.shape
    return pl.pallas_call(
        _norm_kernel,
        grid=(T // tm,),
        in_specs=[pl.BlockSpec((tm, D), lambda i: (i, 0)),
                  pl.BlockSpec((None, 1, D), lambda i: (l, 0, 0))],
        out_specs=pl.BlockSpec((tm, D), lambda i: (i, 0)),
        out_shape=jax.ShapeDtypeStruct((T, D), BF16),
        compiler_params=_cparams("arbitrary"),
        name="first_norm",
    )(x2, gains)


def _inproj_kernel(h_ref, w_ref, o_ref):
    o_ref[...] = _dot(h_ref[...], w_ref[...]).astype(o_ref.dtype)


def _inproj(h, w_all, l, N, tn, tm=1024):
    T, D = h.shape
    return pl.pallas_call(
        _inproj_kernel,
        grid=(T // tm, N // tn),
        in_specs=[pl.BlockSpec((tm, D), lambda i, j: (i, 0)),
                  pl.BlockSpec((None, D, tn), lambda i, j: (l, 0, j))],
        out_specs=pl.BlockSpec((tm, tn), lambda i, j: (i, j)),
        out_shape=jax.ShapeDtypeStruct((T, N), BF16),
        compiler_params=_cparams("arbitrary", "arbitrary"),
        name="inproj",
    )(h, w_all)


def _outproj_kernel(x_ref, o0_ref, o1_ref, o2_ref, l0_ref, l1_ref, l2_ref, b_ref, c_ref, w_ref, g_ref, o_ref, h_ref):
    ls = [l0_ref[...], l1_ref[...], l2_ref[...]]
    m = jnp.maximum(jnp.maximum(ls[0], ls[1]), ls[2])
    es = [jnp.exp(l - m) for l in ls]
    inv = 1.0 / (es[0] + es[1] + es[2])
    gw = o0_ref.shape[1]
    acc = jnp.zeros(o_ref.shape, F32)
    for g, og_ref in enumerate((o0_ref, o1_ref, o2_ref)):
        acc += _dot((og_ref[...] * (es[g] * inv)).astype(BF16), w_ref[pl.ds(g * gw, gw), :])
    ra, rb = 3 * gw, b_ref.shape[1]
    acc += _dot(b_ref[...], w_ref[pl.ds(ra, rb), :])
    acc += _dot(c_ref[...], w_ref[pl.ds(ra + rb, c_ref.shape[1]), :])
    x_new = x_ref[...] + acc
    o_ref[...] = x_new
    h_ref[...] = _norm_rows(x_new, g_ref[...]).astype(h_ref.dtype)


def _outproj(x2, os_, lses, ob, oc, w_all, gains, l, tm=512):
    T, D = x2.shape
    row = lambda i: (i, 0)
    gspec = pl.BlockSpec((tm, os_[0].shape[1]), row)
    return pl.pallas_call(
        _outproj_kernel,
        grid=(T // tm,),
        in_specs=[pl.BlockSpec((tm, D), row)] + [gspec] * 6 + [
                  pl.BlockSpec((tm, ob.shape[1]), row),
                  pl.BlockSpec((tm, oc.shape[1]), row),
                  pl.BlockSpec((None, w_all.shape[1], D), lambda i: (l, 0, 0)),
                  pl.BlockSpec((None, 1, D), lambda i: (l, 0, 0))],
        out_specs=[pl.BlockSpec((tm, D), row), pl.BlockSpec((tm, D), row)],
        out_shape=[jax.ShapeDtypeStruct((T, D), F32), jax.ShapeDtypeStruct((T, D), BF16)],
        compiler_params=_cparams("arbitrary"),
        name="outproj",
    )(x2, *os_, *lses, ob, oc, w_all, gains)


FFN_PARTS = 2
FFN_HALO = 16


def _ffn_kernel(x_ref, h_ref, halo_ref, wa_ref, wg_ref, cwa_ref, cwg_ref, cba_ref, cbg_ref, wd_ref, g_ref,
                o_ref, n_ref, h_scr, *u_scrs, tm, blocks_per_seq):
    i = pl.program_id(0)
    f = pl.program_id(1)

    @pl.when(f == 0)
    def _():
        o_ref[...] = x_ref[...]
        keep = jnp.where(i % blocks_per_seq == 0, 0.0, 1.0).astype(BF16)
        h_scr[pl.ds(0, FFN_HALO), :] = halo_ref[0] * keep
        h_scr[pl.ds(FFN_HALO, tm), :] = h_ref[...]

    h = h_scr[...]
    n_parts = len(u_scrs) // 2
    part = wa_ref.shape[1] // n_parts

    def up(pt):
        cols = pl.ds(pt * part, part)
        u_scrs[2 * pt][...] = _dot(h, wa_ref[:, cols])
        u_scrs[2 * pt + 1][...] = _dot(h, wg_ref[:, cols])

    def conv(u_scr, pt, cw_ref, cb_ref):
        cols = pl.ds(pt * part, part)
        acc = cb_ref[:, cols]
        for j in range(3):
            acc = acc + cw_ref[pl.ds(j, 1), cols] * u_scr[pl.ds(FFN_HALO - j, tm), :]
        return acc

    up(0)
    for pt in range(n_parts):
        if pt + 1 < n_parts:
            up(pt + 1)
        ya = conv(u_scrs[2 * pt], pt, cwa_ref, cba_ref)
        yg = conv(u_scrs[2 * pt + 1], pt, cwg_ref, cbg_ref)
        act = (yg * jax.nn.sigmoid(yg) * ya).astype(BF16)
        o_ref[...] += _dot(act, wd_ref[pl.ds(pt * part, part), :])

    @pl.when(f == pl.num_programs(1) - 1)
    def _():
        n_ref[...] = _norm_rows(o_ref[...], g_ref[...]).astype(n_ref.dtype)


def _ffn(x2, h, w_up, conv_w, conv_b, w_down, l, next_gain, next_dtype, seq, tm=512, tf=512):
    T, D = x2.shape
    Fh = w_down.shape[1]
    nf = Fh // tf
    halo_view = h.reshape(T // FFN_HALO, FFN_HALO, D)
    rb = tm // FFN_HALO
    kern = functools.partial(_ffn_kernel, tm=tm, blocks_per_seq=seq // tm)
    row = lambda i, f: (i, 0)
    return pl.pallas_call(
        kern,
        grid=(T // tm, nf),
        in_specs=[pl.BlockSpec((tm, D), row),
                  pl.BlockSpec((tm, D), row),
                  pl.BlockSpec((1, FFN_HALO, D), lambda i, f: (jnp.maximum(i * rb - 1, 0), 0, 0)),
                  pl.BlockSpec((None, D, tf), lambda i, f: (l, 0, f)),
                  pl.BlockSpec((None, D, tf), lambda i, f: (l, 0, f + nf)),
                  pl.BlockSpec((None, 3, tf), lambda i, f: (l, 0, f)),
                  pl.BlockSpec((None, 3, tf), lambda i, f: (l, 0, f + nf)),
                  pl.BlockSpec((None, 1, tf), lambda i, f: (l, 0, f)),
                  pl.BlockSpec((None, 1, tf), lambda i, f: (l, 0, f + nf)),
                  pl.BlockSpec((None, tf, D), lambda i, f: (l, f, 0)),
                  pl.BlockSpec((1, D), lambda i, f: (0, 0))],
        out_specs=[pl.BlockSpec((tm, D), row), pl.BlockSpec((tm, D), row)],
        out_shape=[jax.ShapeDtypeStruct((T, D), F32), jax.ShapeDtypeStruct((T, D), next_dtype)],
        scratch_shapes=[pltpu.VMEM((tm + FFN_HALO, D), BF16)]
        + [pltpu.VMEM((tm + FFN_HALO, tf // FFN_PARTS), F32)] * (2 * FFN_PARTS),
        compiler_params=_cparams("arbitrary", "arbitrary"),
        name="convffn",
    )(x2, h, halo_view, w_up, w_up, conv_w, conv_w, conv_b, conv_b, w_down, next_gain)


def _stack_pair(qp):
    lane = _iota(qp.shape, 1)
    zero = jnp.zeros_like(qp)
    return jnp.concatenate([jnp.where(lane < HEAD_DIM, qp, zero),
                            jnp.where(lane >= HEAD_DIM, qp, zero)], axis=0)


def _unstack_pair(o):
    R = o.shape[0] // 2
    lane = _iota((R, LANES), 1)
    return jnp.where(lane < HEAD_DIM, o[:R], o[R:])


def _dilated_kernel(q_ref, kc_ref, kp_ref, vc_ref, vp_ref, bias_ref, o_ref, lse_ref):
    i = pl.program_id(2)
    R = 2 * A_BLOCK
    n_blocks = q_ref.shape[1] // A_BLOCK
    a = _iota((R, 2 * A_BLOCK), 0) % A_BLOCK
    bk = _iota((R, 2 * A_BLOCK), 1)
    rel = a + A_BLOCK - bk
    band = (rel >= 0) & (rel <= A_BLOCK)
    first = band & ((bk >= A_BLOCK) | (i > 0))
    n_pairs = q_ref.shape[2] // LANES
    units = [(j, p) for j in range(n_blocks) for p in range(n_pairs)]

    def rows(j):
        return slice(j * A_BLOCK, (j + 1) * A_BLOCK)

    def with_prev(cur_ref, prev_ref, j, cols):
        prev = prev_ref[0, :, cols] if j == 0 else cur_ref[0, rows(j - 1), cols]
        return jnp.concatenate([prev, cur_ref[0, rows(j), cols]], axis=0)

    def scores(j, p):
        cols = slice(p * LANES, (p + 1) * LANES)
        qs = _stack_pair(q_ref[0, rows(j), cols])
        s = _dot_nt(qs, with_prev(kc_ref, kp_ref, j, cols)) + bias_ref[0, 2 * p:2 * p + 2].reshape(R, 2 * A_BLOCK)
        return jnp.where(first if j == 0 else band, s, NEG)

    s_all = [scores(j, p) for j, p in units]
    for (j, p), s in zip(units, s_all):
        cols = slice(p * LANES, (p + 1) * LANES)
        mask = first if j == 0 else band
        m = jnp.max(s, axis=1, keepdims=True)
        e = jnp.where(mask, jnp.exp(s - m), 0.0)
        den = jnp.sum(e, axis=1, keepdims=True)
        o = _dot(e.astype(BF16), with_prev(vc_ref, vp_ref, j, cols)) / den
        lse = m + jnp.log(den)
        o_ref[0, rows(j), cols] = _unstack_pair(o)
        lse_ref[0, rows(j), cols] = _unstack_pair(jnp.broadcast_to(lse, (R, LANES)))


def _dilated_group(proj, bias_a, g, dilation):
    B, S, _ = proj.shape
    L = S // dilation
    nq = L // A_BLOCK
    per_step = A_BLOCKS_PER_STEP if nq % A_BLOCKS_PER_STEP == 0 else 1
    W = A_HEADS_PER_GROUP * HEAD_DIM
    if dilation == 1:
        view, base = proj, g * A_GROUP_WIDTH // W
    else:
        cols = proj[:, :, g * A_GROUP_WIDTH:(g + 1) * A_GROUP_WIDTH]
        view, base = cols.reshape(B, L, dilation * A_GROUP_WIDTH), 0

    def cur(part):
        return pl.BlockSpec((1, per_step * A_BLOCK, W), lambda b, r, i: (b, i, base + r * 3 + part))

    def prev(part):
        return pl.BlockSpec((1, A_BLOCK, W),
                            lambda b, r, i: (b, jnp.maximum(i * per_step - 1, 0), base + r * 3 + part))

    out_spec = pl.BlockSpec((1, per_step * A_BLOCK, W), lambda b, r, i: (b, i, r))
    o, lse = pl.pallas_call(
        _dilated_kernel,
        grid=(B, dilation, nq // per_step),
        in_specs=[cur(0), cur(1), prev(1), cur(2), prev(2),
                  pl.BlockSpec((1, A_HEADS_PER_GROUP, A_BLOCK, 2 * A_BLOCK), lambda b, r, i: (g, g, 0, 0))],
        out_specs=[out_spec, out_spec],
        out_shape=[jax.ShapeDtypeStruct((B, L, dilation * W), F32)] * 2,
        compiler_params=_cparams("arbitrary", "arbitrary", "arbitrary"),
        name=f"dilated_g{g}",
    )(view, view, view, view, view, bias_a)
    return o.reshape(B * S, W), lse.reshape(B * S, W)


def _flash_init(m_scr, acc_scrs):
    m_scr[...] = jnp.full(m_scr.shape, M_FLOOR, F32)
    for acc in acc_scrs:
        acc[...] = jnp.zeros(acc.shape, F32)


def _with_ones(vt, row):
    return jnp.where(_iota(vt.shape, 0) == row, jnp.ones_like(vt), vt)


def _flash_tiles(tiles, q_groups, m_scr, acc_scrs):
    G = len(q_groups)
    W = q_groups[0].shape[0]
    m = [m_scr[:, g * W:(g + 1) * W] for g in range(G)]
    units = [(t, g) for t in range(len(tiles)) for g in range(G)]

    def scores(t, g):
        k_of, _, elem_of, _ = tiles[t]
        s = _dot_nt(k_of(g), q_groups[g])
        for term in elem_of(g):
            s = s + term
        return s

    def apply(g, alpha, pv):
        acc_scrs[g][...] = alpha * acc_scrs[g][...] + pv

    s_next = scores(*units[0])
    pending = None
    for n, (t, g) in enumerate(units):
        s = s_next
        if n + 1 < len(units):
            s_next = scores(*units[n + 1])
        rows = tiles[t][3](g)
        seg = s.shape[0] // len(rows)
        parts = [s[i * seg:(i + 1) * seg] for i in range(len(rows))]
        m_new = m[g]
        for part, row in zip(parts, rows):
            m_new = jnp.maximum(m_new, jnp.max(part, axis=0, keepdims=True) + row)
        alpha = jnp.exp2(m[g] - m_new)
        p = jnp.concatenate([jnp.exp2(part - (m_new - row)) for part, row in zip(parts, rows)], axis=0)
        m[g] = m_new
        pv = _dot(tiles[t][1](g), p.astype(BF16))
        if pending is not None:
            apply(*pending)
        pending = (g, alpha, pv)
    apply(*pending)
    m_scr[...] = jnp.concatenate(m, axis=1)


PICKED = -3e38
NO_INDEX = 1e9


def _top_picks(score, n_pick):
    idx = _iota(score.shape, 0).astype(F32)

    def pick(_, g):
        mx = jnp.max(g, axis=0, keepdims=True)
        first = jnp.min(jnp.where(g == mx, idx, NO_INDEX), axis=0, keepdims=True)
        return jnp.where(idx == first, PICKED, g)

    return lax.fori_loop(0, n_pick, pick, score) == PICKED


def _moba_kernel(q_ref, k_ref, v_ref, bias_ref, o_ref, km_scr, vt_scr, sel_scr, m_scr, *acc_scrs,
                 dmax, n_blocks):
    c = pl.program_id(2)
    BLK = MOBA_BLOCK
    W = GROUP_W
    CT = MOBA_CHUNK * BLK
    spb = BLK // W
    per_head = CT // W
    per_pair = 2 * per_head
    G = MOBA_PAIRS * per_pair
    first = MOBA_CHUNK * c

    @pl.when(c == 0)
    def _():
        def body(j, _):
            rows = pl.ds(pl.multiple_of(j * BLK, BLK), BLK)
            km_scr[pl.ds(j, 1), :] = jnp.mean(k_ref[0, rows, :].astype(F32), axis=0, keepdims=True)
            vt_scr[j] = _transpose_bf16(v_ref[0, rows, :])
            return 0
        lax.fori_loop(0, n_blocks, body, 0)

    def lanes(pi):
        return slice(pi * LANES, (pi + 1) * LANES)

    def group(g):
        pi, gp = divmod(g, per_pair)
        h, sl = divmod(gp, per_head)
        return pi, h, sl // spb, sl % spb

    q = q_ref[0]
    km = km_scr[...].astype(BF16)
    q_groups, gates = [], []
    for pi in range(MOBA_PAIRS):
        qs = _stack_pair(q[:, lanes(pi)])
        gates.append(_dot_nt(km[:, lanes(pi)], qs))
        q_groups.extend(qs[g * W:(g + 1) * W] for g in range(per_pair))
    gate = jnp.concatenate(gates, axis=1)
    blk = _iota(gate.shape, 0)
    own_blk = first + (_iota(gate.shape, 1) % CT) // BLK
    gate = jnp.where(blk < own_blk, gate, NEG)
    sel = (blk < own_blk) & _top_picks(gate, min(MOBA_TOPK, n_blocks))
    sel_scr[...] = jnp.where(sel, 0.0, NEG)

    _flash_init(m_scr, acc_scrs)

    def keys_of(kj):
        return lambda g: kj[:, lanes(group(g)[0])]

    def values_of(vt):
        opts = [[_with_ones(vt[lanes(pi)], (1 - h) * HEAD_DIM) for h in range(2)] for pi in range(MOBA_PAIRS)]
        return lambda g: opts[group(g)[0]][group(g)[1]]

    def bias_of(tile, g, rows=slice(None)):
        pi, h, _, sl = group(g)
        return bias_ref[tile, pi, rows, pl.ds(h * BLK + sl * W, W)]

    def past_tile(j, far):
        jc = jnp.minimum(j, n_blocks - 1)
        kj = k_ref[0, pl.ds(pl.multiple_of(jc * BLK, BLK), BLK), :]
        selrow = jnp.where(j < first, sel_scr[pl.ds(jc, 1), :], NEG)

        def elem(g):
            return () if far else (bias_of(jnp.clip(first + group(g)[2] - j, 0, dmax), g),)

        def rows(g):
            row = selrow[:, g * W:(g + 1) * W]
            return (row + bias_of(dmax, g, pl.ds(0, 1)),) if far else (row,)

        return keys_of(kj), values_of(vt_scr[jc]), elem, rows

    def body(far):
        def run(it, _):
            _flash_tiles([past_tile(MOBA_UNROLL * it + u, far) for u in range(MOBA_UNROLL)],
                         q_groups, m_scr, acc_scrs)
            return 0
        return run

    n_far_iters = jnp.maximum(first - _far_delta(BLK) + 1, 0) // MOBA_UNROLL
    lax.fori_loop(0, n_far_iters, body(True), 0)
    lax.fori_loop(n_far_iters, (first + MOBA_UNROLL - 1) // MOBA_UNROLL, body(False), 0)

    key_row = _iota((BLK, W), 0)
    zero_row = jnp.zeros((1, W), F32)
    hidden = jnp.full((BLK, W), NEG, F32)

    def own_tile(t):
        kt = k_ref[0, pl.ds(pl.multiple_of((first + t) * BLK, BLK), BLK), :]
        selrow = sel_scr[pl.ds(first + t, 1), :]

        def elem(g):
            _, _, gb, sl = group(g)
            if gb < t:
                return (hidden,)
            if gb == t:
                return bias_of(0, g), jnp.where(key_row <= sl * W + _iota((BLK, W), 1), 0.0, NEG)
            return (bias_of(min(gb - t, dmax), g),)

        def rows(g):
            return (selrow[:, g * W:(g + 1) * W],) if group(g)[2] > t else (zero_row,)

        return keys_of(kt), values_of(vt_scr[first + t]), elem, rows

    _flash_tiles([own_tile(t) for t in range(MOBA_CHUNK)], q_groups, m_scr, acc_scrs)
    pairs_out = []
    for pi in range(MOBA_PAIRS):
        heads_out = []
        for h in range(2):
            parts = []
            for sl in range(per_head):
                a = acc_scrs[pi * per_pair + h * per_head + sl][...]
                den = a[(1 - h) * HEAD_DIM:(1 - h) * HEAD_DIM + 1]
                parts.append(a[h * HEAD_DIM:(h + 1) * HEAD_DIM] / den)
            heads_out.append(jnp.concatenate(parts, axis=1))
        pairs_out.append(jnp.concatenate(heads_out, axis=0).T)
    o_ref[0] = jnp.concatenate(pairs_out, axis=1).astype(o_ref.dtype)


def _moba(proj, bias_b, dmax):
    B, S, _ = proj.shape
    NB = S // MOBA_BLOCK
    assert NB % MOBA_CHUNK == 0
    CT = MOBA_CHUNK * MOBA_BLOCK
    n_tiles = bias_b.shape[0]
    PW = MOBA_PAIRS * LANES
    qb, kb, vb = (BLK_B * LANES // PW, (BLK_B + 4) * LANES // PW, (BLK_B + 8) * LANES // PW)
    kern = functools.partial(_moba_kernel, dmax=dmax, n_blocks=NB)
    Q = MOBA_PAIRS * 2 * CT
    return pl.pallas_call(
        kern,
        grid=(B, B_HEADS // (2 * MOBA_PAIRS), NB // MOBA_CHUNK),
        in_specs=[pl.BlockSpec((1, CT, PW), lambda b, p, c: (b, c, qb + p)),
                  pl.BlockSpec((1, S, PW), lambda b, p, c: (b, 0, kb + p)),
                  pl.BlockSpec((1, S, PW), lambda b, p, c: (b, 0, vb + p)),
                  pl.BlockSpec((n_tiles, MOBA_PAIRS, MOBA_BLOCK, 2 * MOBA_BLOCK), lambda b, p, c: (0, p, 0, 0))],
        out_specs=pl.BlockSpec((1, CT, PW), lambda b, p, c: (b, c, p)),
        out_shape=jax.ShapeDtypeStruct((B, S, B_HEADS * HEAD_DIM), BF16),
        scratch_shapes=[pltpu.VMEM((NB, PW), F32), pltpu.VMEM((NB, PW, MOBA_BLOCK), BF16),
                        pltpu.VMEM((NB, Q), F32), pltpu.VMEM((1, Q), F32)]
        + [pltpu.VMEM((ACC_ROWS, GROUP_W), F32)] * (Q // GROUP_W),
        compiler_params=_cparams("arbitrary", "arbitrary", "arbitrary"),
        name="moba",
    )(proj, proj, proj, bias_b)


def _gelu_tanh(x):
    return 0.5 * x * (1.0 + jnp.tanh(math.sqrt(2.0 / math.pi) * (x + 0.044715 * (x * x * x))))


def _compress_kernel(sub_ref, pe_ref, w1_ref, w2_ref, o_ref, ot_ref, v_scr):
    n16 = sub_ref.shape[2]
    sub = sub_ref[0, 0].astype(F32)
    top = (sub + pe_ref[0:1, :]).astype(BF16)
    bot = (sub + pe_ref[1:2, :]).astype(BF16)
    u = _dot(top, w1_ref[0])
    v_scr[pl.ds(0, n16), :] = _dot(bot, w1_ref[1])
    v_scr[pl.ds(n16, 8), :] = jnp.zeros((8, v_scr.shape[1]), F32)
    hidden = u + v_scr[pl.ds(1, n16), :]
    out = _dot(_gelu_tanh(hidden).astype(BF16), w2_ref[...])
    o_ref[0, 0] = out.astype(o_ref.dtype)
    ot_ref[0, 0] = out.T.astype(ot_ref.dtype)


def _compress(sub, pe_rows, w1_blk, w2_blk, l):
    B, KV, n16, W = sub.shape
    return pl.pallas_call(
        _compress_kernel,
        grid=(B, KV),
        in_specs=[pl.BlockSpec((1, 1, n16, W), lambda b, n: (b, n, 0, 0)),
                  pl.BlockSpec((None, 2, W), lambda b, n: (l, 0, 0)),
                  pl.BlockSpec((None, 2, W, 2 * CMP_HIDDEN), lambda b, n: (l, 0, 0, 0)),
                  pl.BlockSpec((None, 2 * CMP_HIDDEN, LANES), lambda b, n: (l, 0, 0))],
        out_specs=[pl.BlockSpec((1, 1, n16, LANES), lambda b, n: (b, n, 0, 0)),
                   pl.BlockSpec((1, 1, LANES, n16), lambda b, n: (b, n, 0, 0))],
        out_shape=[jax.ShapeDtypeStruct((B, KV, n16, LANES), BF16),
                   jax.ShapeDtypeStruct((B, KV, LANES, n16), BF16)],
        scratch_shapes=[pltpu.VMEM((n16 + 8, 2 * CMP_HIDDEN), F32)],
        compiler_params=_cparams("arbitrary", "arbitrary"),
        name="nsa_compress",
    )(sub, pe_rows, w1_blk, w2_blk)


def _nsa_kernel(q_ref, kc_ref, kct_ref, slc_ref, win_ref, gate_ref, bias_ref, ovt_ref,
                o_ref, slct_scr, wint_scr, ocmp_scr, imp_scr, sel_scr, g_scr, m_slc, m_win, *acc_scrs, dmax, n_sel):
    n = pl.program_id(1)
    c = pl.program_id(2)
    CQ = C_CHUNK
    W = GROUP_W
    nq = CQ // W
    G = C_GROUP * nq
    halves = NSA_KT // LANES
    per128 = CQ // LANES

    @pl.when(c == 0)
    def _():
        def body(t, _):
            rows = pl.ds(pl.multiple_of(t * NSA_KT, NSA_KT), NSA_KT)
            slct_scr[t] = _transpose_bf16(slc_ref[0, rows, :])
            wint_scr[t] = _transpose_bf16(win_ref[0, rows, :])
            return 0
        lax.fori_loop(0, slct_scr.shape[0], body, 0)

    lane = _iota((CQ, LANES), 1)
    q = q_ref[0].astype(F32)
    q_groups = []
    for pr in range(2):
        qp = q[:, pr * LANES:(pr + 1) * LANES]
        for head in (jnp.where(lane < HEAD_DIM, qp, 0.0), jnp.where(lane < HEAD_DIM, pltpu.roll(qp, HEAD_DIM, 1), 0.0)):
            hb = head.astype(BF16)
            q_groups.extend(hb[j * W:(j + 1) * W] for j in range(nq))

    def tok(j, shape):
        return c * CQ + j * W + _iota(shape, 1)

    NC = kc_ref.shape[2]
    NS = ovt_ref.shape[0]

    def cmp_branch(nk):
        kc = kc_ref[0, 0, pl.ds(0, nk), :]
        kct = kct_ref[0, 0, :, pl.ds(0, nk)]
        ovt = ovt_ref[:, pl.ds(0, nk)]
        key_end = _iota((nk, W), 0) * CMP_STRIDE + (CMP_LEN - 1)
        visible = [key_end <= tok(j, (nk, W)) for j in range(nq)]
        psum = [jnp.zeros((nk, W), F32) for _ in range(nq)]
        s_next = _dot_nt(kc, q_groups[0])
        for g in range(G):
            j = g % nq
            s = jnp.where(visible[j], s_next, NEG)
            if g + 1 < G:
                s_next = _dot_nt(kc, q_groups[g + 1])
            m = jnp.max(s, axis=0, keepdims=True)
            e = jnp.where(visible[j], jnp.exp2(s - m), 0.0)
            den = jnp.sum(e, axis=0, keepdims=True)
            p = e / jnp.maximum(den, 1e-30)
            ocmp_scr[g] = _dot(kct, p.astype(BF16))[HEAD_DIM:]
            psum[j] = psum[j] + p
        for j in range(nq):
            imp = jnp.zeros((NS, W), F32)
            rest = psum[j]
            for _ in range(IMP_SPLITS):
                piece = rest.astype(BF16)
                imp = imp + _dot(ovt, piece)
                rest = rest - piece.astype(F32)
            imp_scr[:, j * W:(j + 1) * W] = imp

    half_chunks = (NC // 2) // (CQ // CMP_STRIDE)

    @pl.when(c < half_chunks)
    def _():
        cmp_branch(NC // 2)

    @pl.when(c >= half_chunks)
    def _():
        cmp_branch(NC)

    imp = imp_scr[...]
    jj = _iota((NS, CQ), 0)
    own = (c * CQ + _iota((NS, CQ), 1)) // SLC_BLOCK
    forced = (jj == 0) | (jj == own) | (jj == own - 1)
    score = jnp.where(forced, FORCED_SCORE, jnp.where(jj <= own, imp, INVALID_SCORE))
    picked = _top_picks(score, max(n_sel - N_FORCED, 0))
    sel_scr[...] = jnp.where(forced | picked, 0.0, NEG)

    acc_slc = acc_scrs[:G]
    acc_win = acc_scrs[G:]
    key_row = _iota((NSA_KT, W), 0)
    zero_row = jnp.zeros((1, W), F32)

    def bias_tile(g, delta0, dynamic):
        gi, j = divmod(g, nq)
        cols = pl.ds(gi * LANES, LANES)
        idx = [delta0 + j - h for h in range(halves)]
        idx = [jnp.clip(d, 0, dmax) for d in idx] if dynamic else [max(d, 0) for d in idx]
        return jnp.concatenate([bias_ref[d, 0, :, cols] for d in idx], axis=0)

    _flash_init(m_slc, acc_slc)
    per_tile = NSA_KT // SLC_BLOCK
    n_kt = slct_scr.shape[0]

    def slc_tile(kt, kind):
        ktc = jnp.minimum(kt, n_kt - 1)
        k = slc_ref[0, pl.ds(pl.multiple_of(ktc * NSA_KT, NSA_KT), NSA_KT), :]
        pair = sel_scr[pl.ds(pl.multiple_of((ktc // 2) * 2 * per_tile, 2 * per_tile), 2 * per_tile), :]
        r = jnp.where(ktc % 2 == 1, pair[per_tile:], pair[:per_tile])
        delta0 = c * per128 - halves * kt
        causal = [jnp.where(kt * NSA_KT + key_row <= tok(j, (NSA_KT, W)), 0.0, NEG) for j in range(nq)]

        def elem(g):
            if kind == "far":
                return ()
            bias = bias_tile(g, delta0, True)
            return (bias, causal[g % nq]) if kind == "diag" else (bias,)

        def rows(g):
            gi, j = divmod(g, nq)
            out = []
            for i in range(per_tile):
                row = r[i:i + 1, j * W:(j + 1) * W]
                if kind == "far":
                    row = row + bias_ref[dmax, 0, pl.ds(0, 1), pl.ds(gi * LANES, LANES)]
                out.append(row)
            return out

        vt = _with_ones(slct_scr[ktc], 0)
        return lambda g: k, lambda g: vt, elem, rows

    def slc_iter(it, kind):
        _flash_tiles([slc_tile(NSA_UNROLL * it + u, kind) for u in range(NSA_UNROLL)], q_groups, m_slc, acc_slc)

    def slc_body(kind):
        def run(it, _):
            slc_iter(it, kind)
            return 0
        return run

    keys_per_iter = NSA_UNROLL * NSA_KT
    last_iter = (c * CQ) // keys_per_iter
    n_far_tiles = jnp.maximum(c * per128 - (halves - 1) - _far_delta(LANES) + halves, 0) // halves
    n_far_iters = n_far_tiles // NSA_UNROLL
    lax.fori_loop(0, n_far_iters, slc_body("far"), 0)
    lax.fori_loop(n_far_iters, last_iter, slc_body("near"), 0)
    slc_iter(last_iter, "diag")

    _flash_init(m_win, acc_win)
    back = WIN // NSA_KT

    def win_tile(w):
        kt = c * (CQ // NSA_KT) - back + w
        ktc = jnp.clip(kt, 0, n_kt - 1)
        k = win_ref[0, pl.ds(pl.multiple_of(ktc * NSA_KT, NSA_KT), NSA_KT), :]
        dist0 = (back - w) * NSA_KT + _iota((NSA_KT, W), 1) - key_row
        negs = []
        for j in range(nq):
            dist = dist0 + j * W
            negs.append(jnp.where((dist >= 0) & (dist < WIN) & (kt >= 0), 0.0, NEG))

        def elem(g):
            return bias_tile(g, (back - w) * halves, False), negs[g % nq]

        vt = _with_ones(wint_scr[ktc], 0)
        return lambda g: k, lambda g: vt, elem, lambda g: (zero_row,)

    _flash_tiles([win_tile(w) for w in range((WIN + CQ) // NSA_KT)], q_groups, m_win, acc_win)

    g_scr[...] = jax.nn.sigmoid(gate_ref[0].astype(F32)).T
    res = []
    for gi in range(C_GROUP):
        col0 = (n * C_GROUP + gi) * 3
        gates = [g_scr[pl.ds(col0 + br, 1), :] for br in range(3)]
        parts = []
        for j in range(nq):
            g = gi * nq + j
            toks = slice(j * W, (j + 1) * W)
            a_s = acc_slc[g][...]
            a_w = acc_win[g][...]
            parts.append(gates[0][:, toks] * ocmp_scr[g]
                         + gates[1][:, toks] * (a_s[HEAD_DIM:] / a_s[0:1])
                         + gates[2][:, toks] * (a_w[HEAD_DIM:] / a_w[0:1]))
        res.append(jnp.concatenate(parts, axis=1))
    o_ref[0] = jnp.concatenate(res, axis=0).T.astype(o_ref.dtype)


def _nsa(proj, tail, kcvc, kcvct, bias_c, overlap_t, dmax):
    B, S, _ = proj.shape
    CQ = C_CHUNK
    assert CQ % NSA_KT == 0 and WIN % NSA_KT == 0 and CQ % GROUP_W == 0
    n_chunks = S // CQ
    NC = kcvc.shape[2]
    NS = S // SLC_BLOCK
    n_tiles = bias_c.shape[0]
    G = C_GROUP * (CQ // GROUP_W)
    kern = functools.partial(_nsa_kernel, dmax=dmax, n_sel=min(SLC_TOPN, NS))
    return pl.pallas_call(
        kern,
        grid=(B, C_KV_HEADS, n_chunks),
        in_specs=[pl.BlockSpec((1, CQ, 2 * LANES), lambda b, n, c: (b, c, BLK_CQ // 2 + n)),
                  pl.BlockSpec((1, 1, NC, LANES), lambda b, n, c: (b, n, 0, 0)),
                  pl.BlockSpec((1, 1, LANES, NC), lambda b, n, c: (b, n, 0, 0)),
                  pl.BlockSpec((1, S, LANES), lambda b, n, c: (b, 0, BLK_CKV + 3 * n + 1)),
                  pl.BlockSpec((1, S, LANES), lambda b, n, c: (b, 0, BLK_CKV + 3 * n + 2)),
                  pl.BlockSpec((1, CQ, LANES), lambda b, n, c: (b, c, BLK_GATE)),
                  pl.BlockSpec((n_tiles, 1, LANES, C_GROUP * LANES), lambda b, n, c: (0, n, 0, 0)),
                  pl.BlockSpec((NS, NC), lambda b, n, c: (0, 0))],
        out_specs=pl.BlockSpec((1, CQ, 2 * LANES), lambda b, n, c: (b, c, n)),
        out_shape=jax.ShapeDtypeStruct((B, S, C_HEADS * HEAD_DIM), BF16),
        scratch_shapes=[pltpu.VMEM((S // NSA_KT, LANES, NSA_KT), BF16), pltpu.VMEM((S // NSA_KT, LANES, NSA_KT), BF16),
                        pltpu.VMEM((G, HEAD_DIM, GROUP_W), F32), pltpu.VMEM((NS, CQ), F32),
                        pltpu.VMEM((NS, CQ), F32), pltpu.VMEM((LANES, CQ), F32),
                        pltpu.VMEM((1, G * GROUP_W), F32), pltpu.VMEM((1, G * GROUP_W), F32)]
        + [pltpu.VMEM((ACC_ROWS, GROUP_W), F32)] * (2 * G),
        compiler_params=_cparams("arbitrary", "arbitrary", "arbitrary"),
        name="nsa",
    )(proj, kcvc, kcvct, tail, tail, tail, bias_c, overlap_t)


def _head_scale():
    scale = np.ones((HEAD_WIDTH,), np.float32)
    for g in range(len(A_CONFIGS)):
        scale[g * A_GROUP_WIDTH:g * A_GROUP_WIDTH + A_HEADS_PER_GROUP * HEAD_DIM] = Q_SCALE
    scale[A_WIDTH:A_WIDTH + B_HEADS * HEAD_DIM] = Q_SCALE * LOG2E
    scale[A_WIDTH + B_WIDTH:] = Q_SCALE * LOG2E
    return scale


def _split_w_in(w_in):
    depth, D, _ = w_in.shape
    scale = np.concatenate([_head_scale(), np.ones((IN_WIDTH - HEAD_WIDTH,), np.float32)])
    w = (w_in * scale).astype(BF16)
    kv = w[:, :, HEAD_WIDTH:HEAD_WIDTH + C_KV_WIDTH].reshape(depth, D, 3, 2, C_KV_HEADS, HEAD_DIM)
    kv = kv.transpose(0, 1, 4, 2, 3, 5).reshape(depth, D, C_KV_WIDTH)
    pad = jnp.zeros((depth, D, TAIL_WIDTH - C_KV_WIDTH - C_GATE_WIDTH), BF16)
    tail = jnp.concatenate([kv, w[:, :, HEAD_WIDTH + C_KV_WIDTH:], pad], axis=2)
    return w, tail


def _compress_weights(cmp_w1, cmp_w2, cmp_pe):
    depth = cmp_w1.shape[0]
    half = CMP_LEN // 2
    w1 = cmp_w1.reshape(depth, 2, 2, half, HEAD_DIM, CMP_HIDDEN)
    w1_blk = jnp.zeros((depth, 2, half, 2, HEAD_DIM, 2, CMP_HIDDEN), F32)
    w1_blk = w1_blk.at[:, :, :, 0, :, 0, :].set(w1[:, 0])
    w1_blk = w1_blk.at[:, :, :, 1, :, 1, :].set(w1[:, 1])
    w1_blk = w1_blk.reshape(depth, 2, half * 2 * HEAD_DIM, 2 * CMP_HIDDEN).astype(BF16)
    w2_blk = jnp.zeros((depth, 2, CMP_HIDDEN, 2, HEAD_DIM), F32)
    w2_blk = w2_blk.at[:, 0, :, 0, :].set(cmp_w2[:, 0])
    w2_blk = w2_blk.at[:, 1, :, 1, :].set(cmp_w2[:, 1])
    w2_blk = w2_blk.reshape(depth, 2 * CMP_HIDDEN, 2 * HEAD_DIM).astype(BF16)
    pe = cmp_pe.reshape(depth, 2, 2, half, HEAD_DIM)
    pe_rows = pe.transpose(0, 2, 3, 1, 4).reshape(depth, 2, half * 2 * HEAD_DIM)
    return w1_blk, w2_blk, pe_rows


def kernel(x, rel_table, w_in, w_out, cmp_w1, cmp_w2, cmp_pe, norm_attn, norm_mlp,
           w_up, conv_w, conv_b, w_down, norm_final):
    B, S, D = x.shape
    depth = w_in.shape[0]
    T = B * S

    a = jnp.arange(A_BLOCK)[:, None]
    bk = jnp.arange(2 * A_BLOCK)[None, :]
    rel = a + A_BLOCK - bk
    idx_a = jnp.stack([_t5_bucket(rel * d) for _, d in A_CONFIGS]).astype(jnp.int32)
    bias_a = _bias_tiles(rel_table[:A_HEADS], idx_a)
    dmax_b = _n_far_tiles(S // MOBA_BLOCK, MOBA_BLOCK)
    bias_b = _bias_tiles(rel_table[A_HEADS:A_HEADS + B_HEADS] * LOG2E,
                         _toeplitz_bucket_tiles_t(dmax_b + 1, MOBA_BLOCK), 2)
    dmax_c = _n_far_tiles(S // LANES, LANES)
    bias_c = _bias_tiles(rel_table[A_HEADS + B_HEADS:] * LOG2E,
                         _toeplitz_bucket_tiles_t(dmax_c + 1, LANES), C_GROUP)

    n16 = S // CMP_STRIDE
    ci = jnp.arange(n16)[None, :] * CMP_STRIDE
    sj = jnp.arange(S // SLC_BLOCK)[:, None] * SLC_BLOCK
    overlap_t = ((ci < sj + SLC_BLOCK) & (ci + CMP_LEN > sj)).astype(BF16)

    w_head, w_tail = _split_w_in(w_in)
    w_out_b = w_out.astype(BF16)
    w_up_b = w_up.astype(BF16)
    w_down_b = w_down.astype(BF16)
    gains_attn = norm_attn.reshape(depth, 1, D)
    gains_mlp = norm_mlp.reshape(depth, 1, D)
    conv_b3 = conv_b.reshape(depth, 1, -1)
    w1_blk, w2_blk, pe_rows = _compress_weights(cmp_w1, cmp_w2, cmp_pe)

    x2 = x.reshape(T, D)
    h = _norm(x2, gains_attn, 0)
    for l in range(depth):
        proj = _inproj(h, w_head, l, HEAD_WIDTH, tn=512).reshape(B, S, HEAD_WIDTH)
        tail = _inproj(h, w_tail, l, TAIL_WIDTH, tn=TAIL_WIDTH).reshape(B, S, TAIL_WIDTH)
        os_, lses = [], []
        for g, (_, d) in enumerate(A_CONFIGS):
            o, lse = _dilated_group(proj, bias_a, g, d)
            os_.append(o)
            lses.append(lse)
        o_b = _moba(proj, bias_b, dmax_b).reshape(T, B_HEADS * HEAD_DIM)
        cmp_cols = tail[:, :, :BLK_GATE * LANES].reshape(B, S, C_KV_HEADS, 3, LANES)[:, :, :, 0]
        sub = cmp_cols.transpose(0, 2, 1, 3).reshape(B, C_KV_HEADS, n16, CMP_STRIDE * LANES)
        kcvc, kcvct = _compress(sub, pe_rows, w1_blk, w2_blk, l)
        o_c = _nsa(proj, tail, kcvc, kcvct, bias_c, overlap_t, dmax_c).reshape(T, C_HEADS * HEAD_DIM)
        x2, h = _outproj(x2, os_, lses, o_b, o_c, w_out_b, gains_mlp, l)
        last = l == depth - 1
        next_gain = norm_final.reshape(1, D) if last else norm_attn[l + 1].reshape(1, D)
        x2, h = _ffn(x2, h, w_up_b, conv_w, conv_b3, w_down_b, l, next_gain, F32 if last else BF16, S)
    return h.reshape(B, S, D)
```

```python
import functools
import math

import numpy as np
import jax
import jax.numpy as jnp
from jax import lax
from jax.experimental import pallas as pl
from jax.experimental.pallas import tpu as pltpu

F32 = jnp.float32
BF16 = jnp.bfloat16

HEAD_DIM = 64
LANES = 128
A_CONFIGS = ((128, 1), (512, 4), (2048, 16))
A_HEADS_PER_GROUP = 4
A_HEADS = 12
A_BLOCK = 128
A_BLOCKS_PER_STEP = 4
B_HEADS = 8
MOBA_BLOCK = 256
MOBA_TOPK = 3
MOBA_CHUNK = 2
MOBA_PAIRS = 2
MOBA_UNROLL = 4
C_KV_HEADS = 3
C_GROUP = 4
C_HEADS = 12
CMP_STRIDE = 16
CMP_LEN = 32
CMP_HIDDEN = 128
SLC_BLOCK = 64
SLC_TOPN = 16
N_FORCED = 3
IMP_SPLITS = 3
WIN = 512
INVALID_SCORE = -1.0
FORCED_SCORE = -2.0
N_BUCKETS = 32
T5_MAX_DIST = 2048
EPS = 1e-6
NEG = -1e30
M_FLOOR = -1e20
Q_SCALE = HEAD_DIM ** -0.5
LOG2E = math.log2(math.e)
C_CHUNK = 512
ACC_ROWS = 128
GROUP_W = 128
NSA_KT = 256
NSA_UNROLL = 4

A_WIDTH = 3 * A_HEADS * HEAD_DIM
A_GROUP_WIDTH = 3 * A_HEADS_PER_GROUP * HEAD_DIM
B_WIDTH = 3 * B_HEADS * HEAD_DIM
C_Q_WIDTH = C_HEADS * HEAD_DIM
C_KV_WIDTH = 6 * C_KV_HEADS * HEAD_DIM
C_GATE_WIDTH = 3 * C_HEADS
IN_WIDTH = A_WIDTH + B_WIDTH + C_Q_WIDTH + C_KV_WIDTH + C_GATE_WIDTH
BLK_B = A_WIDTH // LANES
BLK_CQ = BLK_B + B_WIDTH // LANES
HEAD_WIDTH = A_WIDTH + B_WIDTH + C_Q_WIDTH
BLK_CKV = 0
BLK_GATE = 3 * C_KV_HEADS
TAIL_WIDTH = (BLK_GATE + 1) * LANES

VMEM_LIMIT = 56 * 1024 * 1024


def _cparams(*sem, fuse=None):
    return pltpu.CompilerParams(dimension_semantics=sem, vmem_limit_bytes=VMEM_LIMIT, allow_input_fusion=fuse)


def _dot_nt(a, b):
    return lax.dot_general(a, b, (((1,), (1,)), ((), ())), preferred_element_type=F32)


def _dot(a, b, **kw):
    return jnp.dot(a, b, preferred_element_type=F32, **kw)


def _transpose_bf16(x):
    return x.astype(F32).T.astype(BF16)


def _iota(shape, dim):
    return lax.broadcasted_iota(jnp.int32, shape, dim)


def _t5_bucket(dist):
    n = jnp.maximum(dist, 0)
    max_exact = N_BUCKETS // 2
    nf = jnp.maximum(n, 1).astype(F32)
    large = max_exact + (jnp.log(nf / max_exact) / math.log(T5_MAX_DIST / max_exact)
                         * (N_BUCKETS - max_exact)).astype(jnp.int32)
    large = jnp.minimum(large, N_BUCKETS - 1)
    return jnp.where(n < max_exact, n, large)


def _bias_lookup_kernel(tbl_ref, idx_ref, o_ref, *, hpr):
    idx = idx_ref[0]
    C = idx.shape[1]
    for h in range(tbl_ref.shape[0]):
        acc = jnp.zeros(idx.shape, F32)
        for b in range(N_BUCKETS):
            acc = jnp.where(idx == b, tbl_ref[h, b], acc)
        o_ref[0, h // hpr, :, (h % hpr) * C:(h % hpr + 1) * C] = acc


def _bias_tiles(tbl, idx, heads_per_row=1):
    n, R, C = idx.shape
    H = tbl.shape[0]
    hpr = heads_per_row
    return pl.pallas_call(
        functools.partial(_bias_lookup_kernel, hpr=hpr),
        grid=(n,),
        in_specs=[pl.BlockSpec(memory_space=pltpu.SMEM),
                  pl.BlockSpec((1, R, C), lambda t: (t, 0, 0))],
        out_specs=pl.BlockSpec((1, H // hpr, R, hpr * C), lambda t: (t, 0, 0, 0)),
        out_shape=jax.ShapeDtypeStruct((n, H // hpr, R, hpr * C), F32),
        compiler_params=_cparams("arbitrary"),
        name="bias_tiles",
    )(tbl, idx)


def _toeplitz_bucket_tiles_t(n_tiles, size):
    d = jnp.arange(n_tiles)[:, None, None] * size
    key = jnp.arange(size)[None, :, None]
    qry = jnp.arange(size)[None, None, :]
    return _t5_bucket(d + qry - key).astype(jnp.int32)


def _far_delta(size):
    return -(-(T5_MAX_DIST + size - 1) // size)


def _n_far_tiles(n_chunks, size):
    return min(_far_delta(size), n_chunks - 1)


def _norm_rows(x, g):
    ms = jnp.mean(x * x, axis=-1, keepdims=True)
    return x * lax.rsqrt(ms + EPS) * g


def _norm_kernel(x_ref, g_ref, o_ref):
    o_ref[...] = _norm_rows(x_ref[...], g_ref[...]).astype(o_ref.dtype)


def _norm(x2, gains, l, tm=1024):
    T, D = x2.shape
    return pl.pallas_call(
        _norm_kernel,
        grid=(T // tm,),
        in_specs=[pl.BlockSpec((tm, D), lambda i: (i, 0)),
                  pl.BlockSpec((None, 1, D), lambda i: (l, 0, 0))],
        out_specs=pl.BlockSpec((tm, D), lambda i: (i, 0)),
        out_shape=jax.ShapeDtypeStruct((T, D), BF16),
        compiler_params=_cparams("arbitrary"),
        name="first_norm",
    )(x2, gains)


def _inproj_kernel(h_ref, w_ref, o_ref):
    o_ref[...] = _dot(h_ref[...], w_ref[...]).astype(o_ref.dtype)


def _inproj(h, w_all, l, N, tn, tm=1024):
    T, D = h.shape
    return pl.pallas_call(
        _inproj_kernel,
        grid=(T // tm, N // tn),
        in_specs=[pl.BlockSpec((tm, D), lambda i, j: (i, 0)),
                  pl.BlockSpec((None, D, tn), lambda i, j: (l, 0, j))],
        out_specs=pl.BlockSpec((tm, tn), lambda i, j: (i, j)),
        out_shape=jax.ShapeDtypeStruct((T, N), BF16),
        compiler_params=_cparams("arbitrary", "arbitrary"),
        name="inproj",
    )(h, w_all)


def _outproj_kernel(x_ref, o0_ref, o1_ref, o2_ref, l0_ref, l1_ref, l2_ref, b_ref, c_ref, w_ref, g_ref, o_ref, h_ref):
    ls = [l0_ref[...], l1_ref[...], l2_ref[...]]
    m = jnp.maximum(jnp.maximum(ls[0], ls[1]), ls[2])
    es = [jnp.exp(l - m) for l in ls]
    inv = 1.0 / (es[0] + es[1] + es[2])
    gw = o0_ref.shape[1]
    acc = jnp.zeros(o_ref.shape, F32)
    for g, og_ref in enumerate((o0_ref, o1_ref, o2_ref)):
        acc += _dot((og_ref[...] * (es[g] * inv)).astype(BF16), w_ref[pl.ds(g * gw, gw), :])
    ra, rb = 3 * gw, b_ref.shape[1]
    acc += _dot(b_ref[...], w_ref[pl.ds(ra, rb), :])
    acc += _dot(c_ref[...], w_ref[pl.ds(ra + rb, c_ref.shape[1]), :])
    x_new = x_ref[...] + acc
    o_ref[...] = x_new
    h_ref[...] = _norm_rows(x_new, g_ref[...]).astype(h_ref.dtype)


def _outproj(x2, os_, lses, ob, oc, w_all, gains, l, tm=512):
    T, D = x2.shape
    row = lambda i: (i, 0)
    gspec = pl.BlockSpec((tm, os_[0].shape[1]), row)
    return pl.pallas_call(
        _outproj_kernel,
        grid=(T // tm,),
        in_specs=[pl.BlockSpec((tm, D), row)] + [gspec] * 6 + [
                  pl.BlockSpec((tm, ob.shape[1]), row),
                  pl.BlockSpec((tm, oc.shape[1]), row),
                  pl.BlockSpec((None, w_all.shape[1], D), lambda i: (l, 0, 0)),
                  pl.BlockSpec((None, 1, D), lambda i: (l, 0, 0))],
        out_specs=[pl.BlockSpec((tm, D), row), pl.BlockSpec((tm, D), row)],
        out_shape=[jax.ShapeDtypeStruct((T, D), F32), jax.ShapeDtypeStruct((T, D), BF16)],
        compiler_params=_cparams("arbitrary", fuse=[False] + [True] * 6 + [False] * 4),
        name="outproj",
    )(x2, *os_, *lses, ob, oc, w_all, gains)


FFN_PARTS = 2
FFN_HALO = 16


def _ffn_kernel(x_ref, h_ref, halo_ref, wa_ref, wg_ref, cwa_ref, cwg_ref, cba_ref, cbg_ref, wd_ref, g_ref,
                o_ref, n_ref, h_scr, *u_scrs, tm, blocks_per_seq):
    i = pl.program_id(0)
    f = pl.program_id(1)

    @pl.when(f == 0)
    def _():
        o_ref[...] = x_ref[...]
        keep = jnp.where(i % blocks_per_seq == 0, 0.0, 1.0).astype(BF16)
        h_scr[pl.ds(0, FFN_HALO), :] = halo_ref[0] * keep
        h_scr[pl.ds(FFN_HALO, tm), :] = h_ref[...]

    h = h_scr[...]
    n_parts = len(u_scrs) // 2
    part = wa_ref.shape[1] // n_parts

    def up(pt):
        cols = pl.ds(pt * part, part)
        u_scrs[2 * pt][...] = _dot(h, wa_ref[:, cols])
        u_scrs[2 * pt + 1][...] = _dot(h, wg_ref[:, cols])

    def conv(u_scr, pt, cw_ref, cb_ref):
        cols = pl.ds(pt * part, part)
        acc = cb_ref[:, cols]
        for j in range(3):
            acc = acc + cw_ref[pl.ds(j, 1), cols] * u_scr[pl.ds(FFN_HALO - j, tm), :]
        return acc

    up(0)
    for pt in range(n_parts):
        if pt + 1 < n_parts:
            up(pt + 1)
        ya = conv(u_scrs[2 * pt], pt, cwa_ref, cba_ref)
        yg = conv(u_scrs[2 * pt + 1], pt, cwg_ref, cbg_ref)
        act = (yg * jax.nn.sigmoid(yg) * ya).astype(BF16)
        o_ref[...] += _dot(act, wd_ref[pl.ds(pt * part, part), :])

    @pl.when(f == pl.num_programs(1) - 1)
    def _():
        n_ref[...] = _norm_rows(o_ref[...], g_ref[...]).astype(n_ref.dtype)


def _ffn(x2, h, w_up, conv_w, conv_b, w_down, l, next_gain, next_dtype, seq, tm=512, tf=512):
    T, D = x2.shape
    Fh = w_down.shape[1]
    nf = Fh // tf
    halo_view = h.reshape(T // FFN_HALO, FFN_HALO, D)
    rb = tm // FFN_HALO
    kern = functools.partial(_ffn_kernel, tm=tm, blocks_per_seq=seq // tm)
    row = lambda i, f: (i, 0)
    return pl.pallas_call(
        kern,
        grid=(T // tm, nf),
        in_specs=[pl.BlockSpec((tm, D), row),
                  pl.BlockSpec((tm, D), row),
                  pl.BlockSpec((1, FFN_HALO, D), lambda i, f: (jnp.maximum(i * rb - 1, 0), 0, 0)),
                  pl.BlockSpec((None, D, tf), lambda i, f: (l, 0, f)),
                  pl.BlockSpec((None, D, tf), lambda i, f: (l, 0, f + nf)),
                  pl.BlockSpec((None, 3, tf), lambda i, f: (l, 0, f)),
                  pl.BlockSpec((None, 3, tf), lambda i, f: (l, 0, f + nf)),
                  pl.BlockSpec((None, 1, tf), lambda i, f: (l, 0, f)),
                  pl.BlockSpec((None, 1, tf), lambda i, f: (l, 0, f + nf)),
                  pl.BlockSpec((None, tf, D), lambda i, f: (l, f, 0)),
                  pl.BlockSpec((1, D), lambda i, f: (0, 0))],
        out_specs=[pl.BlockSpec((tm, D), row), pl.BlockSpec((tm, D), row)],
        out_shape=[jax.ShapeDtypeStruct((T, D), F32), jax.ShapeDtypeStruct((T, D), next_dtype)],
        scratch_shapes=[pltpu.VMEM((tm + FFN_HALO, D), BF16)]
        + [pltpu.VMEM((tm + FFN_HALO, tf // FFN_PARTS), F32)] * (2 * FFN_PARTS),
        compiler_params=_cparams("arbitrary", "arbitrary"),
        name="convffn",
    )(x2, h, halo_view, w_up, w_up, conv_w, conv_w, conv_b, conv_b, w_down, next_gain)


def _stack_pair(qp):
    lane = _iota(qp.shape, 1)
    zero = jnp.zeros_like(qp)
    return jnp.concatenate([jnp.where(lane < HEAD_DIM, qp, zero),
                            jnp.where(lane >= HEAD_DIM, qp, zero)], axis=0)


def _unstack_pair(o):
    R = o.shape[0] // 2
    lane = _iota((R, LANES), 1)
    return jnp.where(lane < HEAD_DIM, o[:R], o[R:])


def _dilated_kernel(q_ref, kc_ref, kp_ref, vc_ref, vp_ref, bias_ref, o_ref, lse_ref):
    i = pl.program_id(2)
    R = 2 * A_BLOCK
    n_blocks = q_ref.shape[1] // A_BLOCK
    a = _iota((R, 2 * A_BLOCK), 0) % A_BLOCK
    bk = _iota((R, 2 * A_BLOCK), 1)
    rel = a + A_BLOCK - bk
    band = (rel >= 0) & (rel <= A_BLOCK)
    first = band & ((bk >= A_BLOCK) | (i > 0))
    n_pairs = q_ref.shape[2] // LANES
    units = [(j, p) for j in range(n_blocks) for p in range(n_pairs)]

    def rows(j):
        return slice(j * A_BLOCK, (j + 1) * A_BLOCK)

    def with_prev(cur_ref, prev_ref, j, cols):
        prev = prev_ref[0, :, cols] if j == 0 else cur_ref[0, rows(j - 1), cols]
        return jnp.concatenate([prev, cur_ref[0, rows(j), cols]], axis=0)

    def scores(j, p):
        cols = slice(p * LANES, (p + 1) * LANES)
        qs = _stack_pair(q_ref[0, rows(j), cols])
        s = _dot_nt(qs, with_prev(kc_ref, kp_ref, j, cols)) + bias_ref[0, 2 * p:2 * p + 2].reshape(R, 2 * A_BLOCK)
        return jnp.where(first if j == 0 else band, s, NEG)

    s_all = [scores(j, p) for j, p in units]
    for (j, p), s in zip(units, s_all):
        cols = slice(p * LANES, (p + 1) * LANES)
        mask = first if j == 0 else band
        m = jnp.max(s, axis=1, keepdims=True)
        e = jnp.where(mask, jnp.exp(s - m), 0.0)
        den = jnp.sum(e, axis=1, keepdims=True)
        o = _dot(e.astype(BF16), with_prev(vc_ref, vp_ref, j, cols)) / den
        lse = m + jnp.log(den)
        o_ref[0, rows(j), cols] = _unstack_pair(o)
        lse_ref[0, rows(j), cols] = _unstack_pair(jnp.broadcast_to(lse, (R, LANES)))


def _dilated_group(proj, bias_a, g, dilation):
    B, S, _ = proj.shape
    L = S // dilation
    nq = L // A_BLOCK
    per_step = A_BLOCKS_PER_STEP if nq % A_BLOCKS_PER_STEP == 0 else 1
    W = A_HEADS_PER_GROUP * HEAD_DIM
    if dilation == 1:
        view, base = proj, g * A_GROUP_WIDTH // W
    else:
        cols = proj[:, :, g * A_GROUP_WIDTH:(g + 1) * A_GROUP_WIDTH]
        view, base = cols.reshape(B, L, dilation * A_GROUP_WIDTH), 0

    def cur(part):
        return pl.BlockSpec((1, per_step * A_BLOCK, W), lambda b, r, i: (b, i, base + r * 3 + part))

    def prev(part):
        return pl.BlockSpec((1, A_BLOCK, W),
                            lambda b, r, i: (b, jnp.maximum(i * per_step - 1, 0), base + r * 3 + part))

    out_spec = pl.BlockSpec((1, per_step * A_BLOCK, W), lambda b, r, i: (b, i, r))
    o, lse = pl.pallas_call(
        _dilated_kernel,
        grid=(B, dilation, nq // per_step),
        in_specs=[cur(0), cur(1), prev(1), cur(2), prev(2),
                  pl.BlockSpec((1, A_HEADS_PER_GROUP, A_BLOCK, 2 * A_BLOCK), lambda b, r, i: (g, g, 0, 0))],
        out_specs=[out_spec, out_spec],
        out_shape=[jax.ShapeDtypeStruct((B, L, dilation * W), F32)] * 2,
        compiler_params=_cparams("arbitrary", "arbitrary", "arbitrary",
                                 fuse=None if dilation == 1 else [True] * 5 + [False]),
        name=f"dilated_g{g}",
    )(view, view, view, view, view, bias_a)
    return o.reshape(B * S, W), lse.reshape(B * S, W)


def _flash_init(m_scr, acc_scrs):
    m_scr[...] = jnp.full(m_scr.shape, M_FLOOR, F32)
    for acc in acc_scrs:
        acc[...] = jnp.zeros(acc.shape, F32)


def _with_ones(vt, row):
    return jnp.where(_iota(vt.shape, 0) == row, jnp.ones_like(vt), vt)


def _flash_tiles(tiles, q_groups, m_scr, acc_scrs):
    G = len(q_groups)
    W = q_groups[0].shape[0]
    m = [m_scr[:, g * W:(g + 1) * W] for g in range(G)]
    units = [(t, g) for t in range(len(tiles)) for g in range(G)]

    def scores(t, g):
        k_of, _, elem_of, _ = tiles[t]
        s = _dot_nt(k_of(g), q_groups[g])
        for term in elem_of(g):
            s = s + term
        return s

    def apply(g, alpha, pv):
        acc_scrs[g][...] = alpha * acc_scrs[g][...] + pv

    s_next = scores(*units[0])
    pending = None
    for n, (t, g) in enumerate(units):
        s = s_next
        if n + 1 < len(units):
            s_next = scores(*units[n + 1])
        rows = tiles[t][3](g)
        seg = s.shape[0] // len(rows)
        parts = [s[i * seg:(i + 1) * seg] for i in range(len(rows))]
        m_new = m[g]
        for part, row in zip(parts, rows):
            m_new = jnp.maximum(m_new, jnp.max(part, axis=0, keepdims=True) + row)
        alpha = jnp.exp2(m[g] - m_new)
        p = jnp.concatenate([jnp.exp2(part - (m_new - row)) for part, row in zip(parts, rows)], axis=0)
        m[g] = m_new
        pv = _dot(tiles[t][1](g), p.astype(BF16))
        if pending is not None:
            apply(*pending)
        pending = (g, alpha, pv)
    apply(*pending)
    m_scr[...] = jnp.concatenate(m, axis=1)


PICKED = -3e38
NO_INDEX = 1e9


def _top_picks(score, n_pick):
    idx = _iota(score.shape, 0).astype(F32)

    def pick(_, g):
        mx = jnp.max(g, axis=0, keepdims=True)
        first = jnp.min(jnp.where(g == mx, idx, NO_INDEX), axis=0, keepdims=True)
        return jnp.where(idx == first, PICKED, g)

    return lax.fori_loop(0, n_pick, pick, score) == PICKED


def _moba_kernel(q_ref, k_ref, v_ref, bias_ref, o_ref, km_scr, vt_scr, sel_scr, m_scr, *acc_scrs,
                 dmax, n_blocks):
    c = pl.program_id(2)
    BLK = MOBA_BLOCK
    W = GROUP_W
    CT = MOBA_CHUNK * BLK
    spb = BLK // W
    per_head = CT // W
    per_pair = 2 * per_head
    G = MOBA_PAIRS * per_pair
    first = MOBA_CHUNK * c

    @pl.when(c == 0)
    def _():
        def body(j, _):
            rows = pl.ds(pl.multiple_of(j * BLK, BLK), BLK)
            km_scr[pl.ds(j, 1), :] = jnp.mean(k_ref[0, rows, :].astype(F32), axis=0, keepdims=True)
            vt_scr[j] = _transpose_bf16(v_ref[0, rows, :])
            return 0
        lax.fori_loop(0, n_blocks, body, 0)

    def lanes(pi):
        return slice(pi * LANES, (pi + 1) * LANES)

    def group(g):
        pi, gp = divmod(g, per_pair)
        h, sl = divmod(gp, per_head)
        return pi, h, sl // spb, sl % spb

    q = q_ref[0]
    km = km_scr[...].astype(BF16)
    q_groups, gates = [], []
    for pi in range(MOBA_PAIRS):
        qs = _stack_pair(q[:, lanes(pi)])
        gates.append(_dot_nt(km[:, lanes(pi)], qs))
        q_groups.extend(qs[g * W:(g + 1) * W] for g in range(per_pair))
    gate = jnp.concatenate(gates, axis=1)
    blk = _iota(gate.shape, 0)
    own_blk = first + (_iota(gate.shape, 1) % CT) // BLK
    gate = jnp.where(blk < own_blk, gate, NEG)
    sel = (blk < own_blk) & _top_picks(gate, min(MOBA_TOPK, n_blocks))
    sel_scr[...] = jnp.where(sel, 0.0, NEG)

    _flash_init(m_scr, acc_scrs)

    def keys_of(kj):
        return lambda g: kj[:, lanes(group(g)[0])]

    def values_of(vt):
        opts = [[_with_ones(vt[lanes(pi)], (1 - h) * HEAD_DIM) for h in range(2)] for pi in range(MOBA_PAIRS)]
        return lambda g: opts[group(g)[0]][group(g)[1]]

    def bias_of(tile, g, rows=slice(None)):
        pi, h, _, sl = group(g)
        return bias_ref[tile, pi, rows, pl.ds(h * BLK + sl * W, W)]

    def past_tile(j, far):
        jc = jnp.minimum(j, n_blocks - 1)
        kj = k_ref[0, pl.ds(pl.multiple_of(jc * BLK, BLK), BLK), :]
        selrow = jnp.where(j < first, sel_scr[pl.ds(jc, 1), :], NEG)

        def elem(g):
            return () if far else (bias_of(jnp.clip(first + group(g)[2] - j, 0, dmax), g),)

        def rows(g):
            row = selrow[:, g * W:(g + 1) * W]
            return (row + bias_of(dmax, g, pl.ds(0, 1)),) if far else (row,)

        return keys_of(kj), values_of(vt_scr[jc]), elem, rows

    def body(far):
        def run(it, _):
            _flash_tiles([past_tile(MOBA_UNROLL * it + u, far) for u in range(MOBA_UNROLL)],
                         q_groups, m_scr, acc_scrs)
            return 0
        return run

    n_far_iters = jnp.maximum(first - _far_delta(BLK) + 1, 0) // MOBA_UNROLL
    lax.fori_loop(0, n_far_iters, body(True), 0)
    lax.fori_loop(n_far_iters, (first + MOBA_UNROLL - 1) // MOBA_UNROLL, body(False), 0)

    key_row = _iota((BLK, W), 0)
    zero_row = jnp.zeros((1, W), F32)
    hidden = jnp.full((BLK, W), NEG, F32)

    def own_tile(t):
        kt = k_ref[0, pl.ds(pl.multiple_of((first + t) * BLK, BLK), BLK), :]
        selrow = sel_scr[pl.ds(first + t, 1), :]

        def elem(g):
            _, _, gb, sl = group(g)
            if gb < t:
                return (hidden,)
            if gb == t:
                return bias_of(0, g), jnp.where(key_row <= sl * W + _iota((BLK, W), 1), 0.0, NEG)
            return (bias_of(min(gb - t, dmax), g),)

        def rows(g):
            return (selrow[:, g * W:(g + 1) * W],) if group(g)[2] > t else (zero_row,)

        return keys_of(kt), values_of(vt_scr[first + t]), elem, rows

    _flash_tiles([own_tile(t) for t in range(MOBA_CHUNK)], q_groups, m_scr, acc_scrs)
    pairs_out = []
    for pi in range(MOBA_PAIRS):
        heads_out = []
        for h in range(2):
            parts = []
            for sl in range(per_head):
                a = acc_scrs[pi * per_pair + h * per_head + sl][...]
                den = a[(1 - h) * HEAD_DIM:(1 - h) * HEAD_DIM + 1]
                parts.append(a[h * HEAD_DIM:(h + 1) * HEAD_DIM] / den)
            heads_out.append(jnp.concatenate(parts, axis=1))
        pairs_out.append(jnp.concatenate(heads_out, axis=0).T)
    o_ref[0] = jnp.concatenate(pairs_out, axis=1).astype(o_ref.dtype)


def _moba(proj, bias_b, dmax):
    B, S, _ = proj.shape
    NB = S // MOBA_BLOCK
    assert NB % MOBA_CHUNK == 0
    CT = MOBA_CHUNK * MOBA_BLOCK
    n_tiles = bias_b.shape[0]
    PW = MOBA_PAIRS * LANES
    qb, kb, vb = (BLK_B * LANES // PW, (BLK_B + 4) * LANES // PW, (BLK_B + 8) * LANES // PW)
    kern = functools.partial(_moba_kernel, dmax=dmax, n_blocks=NB)
    Q = MOBA_PAIRS * 2 * CT
    return pl.pallas_call(
        kern,
        grid=(B, B_HEADS // (2 * MOBA_PAIRS), NB // MOBA_CHUNK),
        in_specs=[pl.BlockSpec((1, CT, PW), lambda b, p, c: (b, c, qb + p)),
                  pl.BlockSpec((1, S, PW), lambda b, p, c: (b, 0, kb + p)),
                  pl.BlockSpec((1, S, PW), lambda b, p, c: (b, 0, vb + p)),
                  pl.BlockSpec((n_tiles, MOBA_PAIRS, MOBA_BLOCK, 2 * MOBA_BLOCK), lambda b, p, c: (0, p, 0, 0))],
        out_specs=pl.BlockSpec((1, CT, PW), lambda b, p, c: (b, c, p)),
        out_shape=jax.ShapeDtypeStruct((B, S, B_HEADS * HEAD_DIM), BF16),
        scratch_shapes=[pltpu.VMEM((NB, PW), F32), pltpu.VMEM((NB, PW, MOBA_BLOCK), BF16),
                        pltpu.VMEM((NB, Q), F32), pltpu.VMEM((1, Q), F32)]
        + [pltpu.VMEM((ACC_ROWS, GROUP_W), F32)] * (Q // GROUP_W),
        compiler_params=_cparams("arbitrary", "arbitrary", "arbitrary"),
        name="moba",
    )(proj, proj, proj, bias_b)


def _gelu_tanh(x):
    return 0.5 * x * (1.0 + jnp.tanh(math.sqrt(2.0 / math.pi) * (x + 0.044715 * (x * x * x))))


def _compress_kernel(sub_ref, pe_ref, w1_ref, w2_ref, o_ref, ot_ref, v_scr):
    n16 = sub_ref.shape[2]
    sub = sub_ref[0, 0].astype(F32)
    top = (sub + pe_ref[0:1, :]).astype(BF16)
    bot = (sub + pe_ref[1:2, :]).astype(BF16)
    u = _dot(top, w1_ref[0])
    v_scr[pl.ds(0, n16), :] = _dot(bot, w1_ref[1])
    v_scr[pl.ds(n16, 8), :] = jnp.zeros((8, v_scr.shape[1]), F32)
    hidden = u + v_scr[pl.ds(1, n16), :]
    out = _dot(_gelu_tanh(hidden).astype(BF16), w2_ref[...])
    o_ref[0, 0] = out.astype(o_ref.dtype)
    ot_ref[0, 0] = out.T.astype(ot_ref.dtype)


def _compress(sub, pe_rows, w1_blk, w2_blk, l):
    B, KV, n16, W = sub.shape
    return pl.pallas_call(
        _compress_kernel,
        grid=(B, KV),
        in_specs=[pl.BlockSpec((1, 1, n16, W), lambda b, n: (b, n, 0, 0)),
                  pl.BlockSpec((None, 2, W), lambda b, n: (l, 0, 0)),
                  pl.BlockSpec((None, 2, W, 2 * CMP_HIDDEN), lambda b, n: (l, 0, 0, 0)),
                  pl.BlockSpec((None, 2 * CMP_HIDDEN, LANES), lambda b, n: (l, 0, 0))],
        out_specs=[pl.BlockSpec((1, 1, n16, LANES), lambda b, n: (b, n, 0, 0)),
                   pl.BlockSpec((1, 1, LANES, n16), lambda b, n: (b, n, 0, 0))],
        out_shape=[jax.ShapeDtypeStruct((B, KV, n16, LANES), BF16),
                   jax.ShapeDtypeStruct((B, KV, LANES, n16), BF16)],
        scratch_shapes=[pltpu.VMEM((n16 + 8, 2 * CMP_HIDDEN), F32)],
        compiler_params=_cparams("arbitrary", "arbitrary", fuse=[True, False, False, False]),
        name="nsa_compress",
    )(sub, pe_rows, w1_blk, w2_blk)


def _nsa_kernel(q_ref, kc_ref, kct_ref, slc_ref, win_ref, gate_ref, bias_ref, ovt_ref,
                o_ref, slct_scr, wint_scr, ocmp_scr, imp_scr, sel_scr, g_scr, m_slc, m_win, *acc_scrs, dmax, n_sel):
    n = pl.program_id(1)
    c = pl.program_id(2)
    CQ = C_CHUNK
    W = GROUP_W
    nq = CQ // W
    G = C_GROUP * nq
    halves = NSA_KT // LANES
    per128 = CQ // LANES

    @pl.when(c == 0)
    def _():
        def body(t, _):
            rows = pl.ds(pl.multiple_of(t * NSA_KT, NSA_KT), NSA_KT)
            slct_scr[t] = _transpose_bf16(slc_ref[0, rows, :])
            wint_scr[t] = _transpose_bf16(win_ref[0, rows, :])
            return 0
        lax.fori_loop(0, slct_scr.shape[0], body, 0)

    lane = _iota((CQ, LANES), 1)
    q = q_ref[0].astype(F32)
    q_groups = []
    for pr in range(2):
        qp = q[:, pr * LANES:(pr + 1) * LANES]
        for head in (jnp.where(lane < HEAD_DIM, qp, 0.0), jnp.where(lane < HEAD_DIM, pltpu.roll(qp, HEAD_DIM, 1), 0.0)):
            hb = head.astype(BF16)
            q_groups.extend(hb[j * W:(j + 1) * W] for j in range(nq))

    def tok(j, shape):
        return c * CQ + j * W + _iota(shape, 1)

    NC = kc_ref.shape[2]
    NS = ovt_ref.shape[0]

    def cmp_branch(nk):
        kc = kc_ref[0, 0, pl.ds(0, nk), :]
        kct = kct_ref[0, 0, :, pl.ds(0, nk)]
        ovt = ovt_ref[:, pl.ds(0, nk)]
        key_end = _iota((nk, W), 0) * CMP_STRIDE + (CMP_LEN - 1)
        visible = [key_end <= tok(j, (nk, W)) for j in range(nq)]
        psum = [jnp.zeros((nk, W), F32) for _ in range(nq)]
        s_next = _dot_nt(kc, q_groups[0])
        for g in range(G):
            j = g % nq
            s = jnp.where(visible[j], s_next, NEG)
            if g + 1 < G:
                s_next = _dot_nt(kc, q_groups[g + 1])
            m = jnp.max(s, axis=0, keepdims=True)
            e = jnp.where(visible[j], jnp.exp2(s - m), 0.0)
            den = jnp.sum(e, axis=0, keepdims=True)
            p = e / jnp.maximum(den, 1e-30)
            ocmp_scr[g] = _dot(kct, p.astype(BF16))[HEAD_DIM:]
            psum[j] = psum[j] + p
        for j in range(nq):
            imp = jnp.zeros((NS, W), F32)
            rest = psum[j]
            for _ in range(IMP_SPLITS):
                piece = rest.astype(BF16)
                imp = imp + _dot(ovt, piece)
                rest = rest - piece.astype(F32)
            imp_scr[:, j * W:(j + 1) * W] = imp

    half_chunks = (NC // 2) // (CQ // CMP_STRIDE)

    @pl.when(c < half_chunks)
    def _():
        cmp_branch(NC // 2)

    @pl.when(c >= half_chunks)
    def _():
        cmp_branch(NC)

    imp = imp_scr[...]
    jj = _iota((NS, CQ), 0)
    own = (c * CQ + _iota((NS, CQ), 1)) // SLC_BLOCK
    forced = (jj == 0) | (jj == own) | (jj == own - 1)
    score = jnp.where(forced, FORCED_SCORE, jnp.where(jj <= own, imp, INVALID_SCORE))
    picked = _top_picks(score, max(n_sel - N_FORCED, 0))
    sel_scr[...] = jnp.where(forced | picked, 0.0, NEG)

    acc_slc = acc_scrs[:G]
    acc_win = acc_scrs[G:]
    key_row = _iota((NSA_KT, W), 0)
    zero_row = jnp.zeros((1, W), F32)

    def bias_tile(g, delta0, dynamic):
        gi, j = divmod(g, nq)
        cols = pl.ds(gi * LANES, LANES)
        idx = [delta0 + j - h for h in range(halves)]
        idx = [jnp.clip(d, 0, dmax) for d in idx] if dynamic else [max(d, 0) for d in idx]
        return jnp.concatenate([bias_ref[d, 0, :, cols] for d in idx], axis=0)

    _flash_init(m_slc, acc_slc)
    per_tile = NSA_KT // SLC_BLOCK
    n_kt = slct_scr.shape[0]

    def slc_tile(kt, kind):
        ktc = jnp.minimum(kt, n_kt - 1)
        k = slc_ref[0, pl.ds(pl.multiple_of(ktc * NSA_KT, NSA_KT), NSA_KT), :]
        pair = sel_scr[pl.ds(pl.multiple_of((ktc // 2) * 2 * per_tile, 2 * per_tile), 2 * per_tile), :]
        r = jnp.where(ktc % 2 == 1, pair[per_tile:], pair[:per_tile])
        delta0 = c * per128 - halves * kt
        causal = [jnp.where(kt * NSA_KT + key_row <= tok(j, (NSA_KT, W)), 0.0, NEG) for j in range(nq)]

        def elem(g):
            if kind == "far":
                return ()
            bias = bias_tile(g, delta0, True)
            return (bias, causal[g % nq]) if kind == "diag" else (bias,)

        def rows(g):
            gi, j = divmod(g, nq)
            out = []
            for i in range(per_tile):
                row = r[i:i + 1, j * W:(j + 1) * W]
                if kind == "far":
                    row = row + bias_ref[dmax, 0, pl.ds(0, 1), pl.ds(gi * LANES, LANES)]
                out.append(row)
            return out

        vt = _with_ones(slct_scr[ktc], 0)
        return lambda g: k, lambda g: vt, elem, rows

    def slc_iter(it, kind):
        _flash_tiles([slc_tile(NSA_UNROLL * it + u, kind) for u in range(NSA_UNROLL)], q_groups, m_slc, acc_slc)

    def slc_body(kind):
        def run(it, _):
            slc_iter(it, kind)
            return 0
        return run

    keys_per_iter = NSA_UNROLL * NSA_KT
    last_iter = (c * CQ) // keys_per_iter
    n_far_tiles = jnp.maximum(c * per128 - (halves - 1) - _far_delta(LANES) + halves, 0) // halves
    n_far_iters = n_far_tiles // NSA_UNROLL
    lax.fori_loop(0, n_far_iters, slc_body("far"), 0)
    lax.fori_loop(n_far_iters, last_iter, slc_body("near"), 0)
    slc_iter(last_iter, "diag")

    _flash_init(m_win, acc_win)
    back = WIN // NSA_KT

    def win_tile(w):
        kt = c * (CQ // NSA_KT) - back + w
        ktc = jnp.clip(kt, 0, n_kt - 1)
        k = win_ref[0, pl.ds(pl.multiple_of(ktc * NSA_KT, NSA_KT), NSA_KT), :]
        dist0 = (back - w) * NSA_KT + _iota((NSA_KT, W), 1) - key_row
        negs = []
        for j in range(nq):
            dist = dist0 + j * W
            negs.append(jnp.where((dist >= 0) & (dist < WIN) & (kt >= 0), 0.0, NEG))

        def elem(g):
            return bias_tile(g, (back - w) * halves, False), negs[g % nq]

        vt = _with_ones(wint_scr[ktc], 0)
        return lambda g: k, lambda g: vt, elem, lambda g: (zero_row,)

    _flash_tiles([win_tile(w) for w in range((WIN + CQ) // NSA_KT)], q_groups, m_win, acc_win)

    g_scr[...] = jax.nn.sigmoid(gate_ref[0].astype(F32)).T
    res = []
    for gi in range(C_GROUP):
        col0 = (n * C_GROUP + gi) * 3
        gates = [g_scr[pl.ds(col0 + br, 1), :] for br in range(3)]
        parts = []
        for j in range(nq):
            g = gi * nq + j
            toks = slice(j * W, (j + 1) * W)
            a_s = acc_slc[g][...]
            a_w = acc_win[g][...]
            parts.append(gates[0][:, toks] * ocmp_scr[g]
                         + gates[1][:, toks] * (a_s[HEAD_DIM:] / a_s[0:1])
                         + gates[2][:, toks] * (a_w[HEAD_DIM:] / a_w[0:1]))
        res.append(jnp.concatenate(parts, axis=1))
    o_ref[0] = jnp.concatenate(res, axis=0).T.astype(o_ref.dtype)


def _nsa(proj, tail, kcvc, kcvct, bias_c, overlap_t, dmax):
    B, S, _ = proj.shape
    CQ = C_CHUNK
    assert CQ % NSA_KT == 0 and WIN % NSA_KT == 0 and CQ % GROUP_W == 0
    n_chunks = S // CQ
    NC = kcvc.shape[2]
    NS = S // SLC_BLOCK
    n_tiles = bias_c.shape[0]
    G = C_GROUP * (CQ // GROUP_W)
    kern = functools.partial(_nsa_kernel, dmax=dmax, n_sel=min(SLC_TOPN, NS))
    return pl.pallas_call(
        kern,
        grid=(B, C_KV_HEADS, n_chunks),
        in_specs=[pl.BlockSpec((1, CQ, 2 * LANES), lambda b, n, c: (b, c, BLK_CQ // 2 + n)),
                  pl.BlockSpec((1, 1, NC, LANES), lambda b, n, c: (b, n, 0, 0)),
                  pl.BlockSpec((1, 1, LANES, NC), lambda b, n, c: (b, n, 0, 0)),
                  pl.BlockSpec((1, S, LANES), lambda b, n, c: (b, 0, BLK_CKV + 3 * n + 1)),
                  pl.BlockSpec((1, S, LANES), lambda b, n, c: (b, 0, BLK_CKV + 3 * n + 2)),
                  pl.BlockSpec((1, CQ, LANES), lambda b, n, c: (b, c, BLK_GATE)),
                  pl.BlockSpec((n_tiles, 1, LANES, C_GROUP * LANES), lambda b, n, c: (0, n, 0, 0)),
                  pl.BlockSpec((NS, NC), lambda b, n, c: (0, 0))],
        out_specs=pl.BlockSpec((1, CQ, 2 * LANES), lambda b, n, c: (b, c, n)),
        out_shape=jax.ShapeDtypeStruct((B, S, C_HEADS * HEAD_DIM), BF16),
        scratch_shapes=[pltpu.VMEM((S // NSA_KT, LANES, NSA_KT), BF16), pltpu.VMEM((S // NSA_KT, LANES, NSA_KT), BF16),
                        pltpu.VMEM((G, HEAD_DIM, GROUP_W), F32), pltpu.VMEM((NS, CQ), F32),
                        pltpu.VMEM((NS, CQ), F32), pltpu.VMEM((LANES, CQ), F32),
                        pltpu.VMEM((1, G * GROUP_W), F32), pltpu.VMEM((1, G * GROUP_W), F32)]
        + [pltpu.VMEM((ACC_ROWS, GROUP_W), F32)] * (2 * G),
        compiler_params=_cparams("arbitrary", "arbitrary", "arbitrary"),
        name="nsa",
    )(proj, kcvc, kcvct, tail, tail, tail, bias_c, overlap_t)


def _head_scale():
    scale = np.ones((HEAD_WIDTH,), np.float32)
    for g in range(len(A_CONFIGS)):
        scale[g * A_GROUP_WIDTH:g * A_GROUP_WIDTH + A_HEADS_PER_GROUP * HEAD_DIM] = Q_SCALE
    scale[A_WIDTH:A_WIDTH + B_HEADS * HEAD_DIM] = Q_SCALE * LOG2E
    scale[A_WIDTH + B_WIDTH:] = Q_SCALE * LOG2E
    return scale


def _split_w_in(w_in):
    depth, D, _ = w_in.shape
    scale = np.concatenate([_head_scale(), np.ones((IN_WIDTH - HEAD_WIDTH,), np.float32)])
    w = (w_in * scale).astype(BF16)
    kv = w[:, :, HEAD_WIDTH:HEAD_WIDTH + C_KV_WIDTH].reshape(depth, D, 3, 2, C_KV_HEADS, HEAD_DIM)
    kv = kv.transpose(0, 1, 4, 2, 3, 5).reshape(depth, D, C_KV_WIDTH)
    pad = jnp.zeros((depth, D, TAIL_WIDTH - C_KV_WIDTH - C_GATE_WIDTH), BF16)
    tail = jnp.concatenate([kv, w[:, :, HEAD_WIDTH + C_KV_WIDTH:], pad], axis=2)
    return w, tail


def _compress_weights(cmp_w1, cmp_w2, cmp_pe):
    depth = cmp_w1.shape[0]
    half = CMP_LEN // 2
    w1 = cmp_w1.reshape(depth, 2, 2, half, HEAD_DIM, CMP_HIDDEN)
    w1_blk = jnp.zeros((depth, 2, half, 2, HEAD_DIM, 2, CMP_HIDDEN), F32)
    w1_blk = w1_blk.at[:, :, :, 0, :, 0, :].set(w1[:, 0])
    w1_blk = w1_blk.at[:, :, :, 1, :, 1, :].set(w1[:, 1])
    w1_blk = w1_blk.reshape(depth, 2, half * 2 * HEAD_DIM, 2 * CMP_HIDDEN).astype(BF16)
    w2_blk = jnp.zeros((depth, 2, CMP_HIDDEN, 2, HEAD_DIM), F32)
    w2_blk = w2_blk.at[:, 0, :, 0, :].set(cmp_w2[:, 0])
    w2_blk = w2_blk.at[:, 1, :, 1, :].set(cmp_w2[:, 1])
    w2_blk = w2_blk.reshape(depth, 2 * CMP_HIDDEN, 2 * HEAD_DIM).astype(BF16)
    pe = cmp_pe.reshape(depth, 2, 2, half, HEAD_DIM)
    pe_rows = pe.transpose(0, 2, 3, 1, 4).reshape(depth, 2, half * 2 * HEAD_DIM)
    return w1_blk, w2_blk, pe_rows


def kernel(x, rel_table, w_in, w_out, cmp_w1, cmp_w2, cmp_pe, norm_attn, norm_mlp,
           w_up, conv_w, conv_b, w_down, norm_final):
    B, S, D = x.shape
    depth = w_in.shape[0]
    T = B * S

    a = jnp.arange(A_BLOCK)[:, None]
    bk = jnp.arange(2 * A_BLOCK)[None, :]
    rel = a + A_BLOCK - bk
    idx_a = jnp.stack([_t5_bucket(rel * d) for _, d in A_CONFIGS]).astype(jnp.int32)
    bias_a = _bias_tiles(rel_table[:A_HEADS], idx_a)
    dmax_b = _n_far_tiles(S // MOBA_BLOCK, MOBA_BLOCK)
    bias_b = _bias_tiles(rel_table[A_HEADS:A_HEADS + B_HEADS] * LOG2E,
                         _toeplitz_bucket_tiles_t(dmax_b + 1, MOBA_BLOCK), 2)
    dmax_c = _n_far_tiles(S // LANES, LANES)
    bias_c = _bias_tiles(rel_table[A_HEADS + B_HEADS:] * LOG2E,
                         _toeplitz_bucket_tiles_t(dmax_c + 1, LANES), C_GROUP)

    n16 = S // CMP_STRIDE
    ci = jnp.arange(n16)[None, :] * CMP_STRIDE
    sj = jnp.arange(S // SLC_BLOCK)[:, None] * SLC_BLOCK
    overlap_t = ((ci < sj + SLC_BLOCK) & (ci + CMP_LEN > sj)).astype(BF16)

    w_head, w_tail = _split_w_in(w_in)
    w_out_b = w_out.astype(BF16)
    w_up_b = w_up.astype(BF16)
    w_down_b = w_down.astype(BF16)
    gains_attn = norm_attn.reshape(depth, 1, D)
    gains_mlp = norm_mlp.reshape(depth, 1, D)
    conv_b3 = conv_b.reshape(depth, 1, -1)
    w1_blk, w2_blk, pe_rows = _compress_weights(cmp_w1, cmp_w2, cmp_pe)

    x2 = x.reshape(T, D)
    h = _norm(x2, gains_attn, 0)
    for l in range(depth):
        proj = _inproj(h, w_head, l, HEAD_WIDTH, tn=512).reshape(B, S, HEAD_WIDTH)
        tail = _inproj(h, w_tail, l, TAIL_WIDTH, tn=TAIL_WIDTH).reshape(B, S, TAIL_WIDTH)
        os_, lses = [], []
        for g, (_, d) in enumerate(A_CONFIGS):
            o, lse = _dilated_group(proj, bias_a, g, d)
            os_.append(o)
            lses.append(lse)
        o_b = _moba(proj, bias_b, dmax_b).reshape(T, B_HEADS * HEAD_DIM)
        cmp_cols = tail[:, :, :BLK_GATE * LANES].reshape(B, S, C_KV_HEADS, 3, LANES)[:, :, :, 0]
        sub = cmp_cols.transpose(0, 2, 1, 3).reshape(B, C_KV_HEADS, n16, CMP_STRIDE * LANES)
        kcvc, kcvct = _compress(sub, pe_rows, w1_blk, w2_blk, l)
        o_c = _nsa(proj, tail, kcvc, kcvct, bias_c, overlap_t, dmax_c).reshape(T, C_HEADS * HEAD_DIM)
        x2, h = _outproj(x2, os_, lses, o_b, o_c, w_out_b, gains_mlp, l)
        last = l == depth - 1
        next_gain = norm_final.reshape(1, D) if last else norm_attn[l + 1].reshape(1, D)
        x2, h = _ffn(x2, h, w_up_b, conv_w, conv_b3, w_down_b, l, next_gain, F32 if last else BF16, S)
    return h.reshape(B, S, D)
```
